```python
import functools
import jax
import jax.numpy as jnp
from jax import lax
import numpy as np

D_MODEL = 1024
BATCH = 8
SEQ = 2048
DEPTH = 2

GRID_W = 64
CTX_LEN = 256
N_MIXERS = 2
RECURRENT_MIXER = 0
HEAD_SIZE = 64
N_HEADS = D_MODEL // HEAD_SIZE
LORA_DECAY = 64
LORA_ICLR = 64
LORA_GATE = 160
CONV_WIDTH = 3
D_FF = 3584
N_EXPERTS = 8
TOP_K = 2
MOE_BLOCK = 128
NORM_EPS = 1e-6
GN_EPS = 64e-5
N_RWKV = (DEPTH + 1) // 2
N_CONV = DEPTH // 2
N_DENSE = (DEPTH + 1) // 2
N_MOE = DEPTH // 2

kernel_name = "hybrid_rwkv7_shortconv_moe_dit"


def rms_norm(x, g):
    xf = x.astype(jnp.float32)
    y = xf * lax.rsqrt(jnp.mean(xf * xf, axis=-1, keepdims=True) + NORM_EPS)
    return (y * g.astype(jnp.float32)).astype(x.dtype)


def modulate(h, shift, scale):
    return h * (1.0 + scale) + shift


def _heads(z):
    return z.reshape(z.shape[:-1] + (N_HEADS, HEAD_SIZE))


def grid_shift(x):
    b, t, d = x.shape
    rows = t // GRID_W
    g = x.reshape(b, rows, GRID_W, d)
    q = d // 4
    left = jnp.pad(g[:, :, :-1, :q], ((0, 0), (0, 0), (1, 0), (0, 0)))
    right = jnp.pad(g[:, :, 1:, q:2 * q], ((0, 0), (0, 0), (0, 1), (0, 0)))
    up = jnp.pad(g[:, :-1, :, 2 * q:3 * q], ((0, 0), (1, 0), (0, 0), (0, 0)))
    down = jnp.pad(g[:, 1:, :, 3 * q:], ((0, 0), (0, 1), (0, 0), (0, 0)))
    return jnp.concatenate([left, right, up, down], axis=-1).reshape(b, t, d)


def seq_shift(x):
    h = x.shape[-1] // 2
    prev = jnp.pad(x[:, :-1, :h], ((0, 0), (1, 0), (0, 0)))
    nxt = jnp.pad(x[:, 1:, h:], ((0, 0), (0, 1), (0, 0)))
    return jnp.concatenate([prev, nxt], axis=-1)


def rwkv7_tokenwise(h, h_shift, p):
    mu, w_rkv, w0, w1, w2, a0, a1, a2, g1, g2, k_k, k_a = p
    f32 = jnp.float32
    dx = h_shift - h
    xr, xw, xk, xv, xa, xg = (h + dx * mu[n] for n in range(6))
    r = xr @ w_rkv[0]
    k = xk @ w_rkv[1]
    v = xv @ w_rkv[2]
    w_pre = jnp.einsum("zbtr,zrd->zbtd", jnp.tanh(jnp.einsum("btd,zdr->zbtr", xw, w1)), w2) + w0[:, None, None, :]
    decay = jnp.exp(-jnp.exp(-jax.nn.softplus(-w_pre.astype(f32)) - 0.5))
    iclr = jax.nn.sigmoid((jnp.einsum("zbtr,zrd->zbtd", jnp.einsum("btd,zdr->zbtr", xa, a1), a2)
                           + a0[:, None, None, :]).astype(f32))
    gate = jax.nn.sigmoid(xg @ g1) @ g2
    kk = _heads((k * k_k).astype(f32))
    kk = kk * lax.rsqrt(jnp.maximum(jnp.sum(kk * kk, axis=-1, keepdims=True), 1e-24))
    k_dir = k.astype(f32)[None] * (1.0 + (iclr - 1.0) * k_a.astype(f32))
    return (_heads(r.astype(f32)), _heads(k_dir), _heads(v.astype(f32)), _heads(decay),
            -kk, kk[None] * _heads(iclr), gate)


def rwkv7_scan(s0, r, w, k, v, a, b, reverse):
    def step(s, inp):
        r_t, w_t, k_t, v_t, a_t, b_t = inp
        sa = jnp.einsum("bhij,bhj->bhi", s, a_t)
        s = s * w_t[:, :, None, :] + sa[..., None] * b_t[:, :, None, :] + v_t[..., None] * k_t[:, :, None, :]
        return s, jnp.einsum("bhij,bhj->bhi", s, r_t)
    xs = tuple(jnp.moveaxis(z, 1, 0) for z in (r, w, k, v, a, b))
    s_final, ys = lax.scan(step, s0, xs, reverse=reverse)
    return jnp.moveaxis(ys, 0, 1), s_final


def rwkv7_bidir(p, s_fwd, s_bwd):
    r, k2, v, w2, a, b2 = p[:6]
    y_f, s_f = rwkv7_scan(s_fwd, r, w2[0], k2[0], v, a, b2[0], reverse=False)
    y_b, s_b = rwkv7_scan(s_bwd, r, w2[1], k2[1], v, a, b2[1], reverse=True)
    return y_f + y_b, s_f, s_b


def rwkv7_readout(y, p, r_k, ln_w, ln_b, w_out, dtype):
    r, k2, v, gate = p[0], p[1], p[2], p[6]
    mean = jnp.mean(y, axis=-1, keepdims=True)
    var = jnp.mean(jnp.square(y - mean), axis=-1, keepdims=True)
    yn = (y - mean) * lax.rsqrt(var + GN_EPS)
    bonus = jnp.sum(r * (0.5 * (k2[0] + k2[1])) * r_k.astype(jnp.float32), axis=-1, keepdims=True) * v
    b, t = y.shape[:2]
    o = yn.reshape(b, t, -1) * ln_w + ln_b + bonus.reshape(b, t, -1)
    return (o.astype(dtype) * gate) @ w_out


def short_conv(h, w_in, w_conv, w_out):
    bg, cg, u = jnp.split(h @ w_in, 3, axis=-1)
    z = cg * u
    t = z.shape[1]
    half = CONV_WIDTH // 2
    zp = jnp.pad(z, ((0, 0), (half, half), (0, 0)))
    conv = sum(zp[:, n:n + t] * w_conv[n] for n in range(CONV_WIDTH))
    return (bg * conv) @ w_out


def swiglu(h, w_gu, w_down):
    g, u = jnp.split(h @ w_gu, 2, axis=-1)
    return (jax.nn.silu(g) * u) @ w_down


def moe_swiglu(h, w_router, w_gu, w_down):
    shp = h.shape
    tok = h.reshape(-1, shp[-1])
    n = tok.shape[0]
    nk = n * TOP_K
    logits = (tok @ w_router).astype(jnp.float32)
    top_v, top_e = lax.top_k(logits, TOP_K)
    gates = jax.nn.softmax(top_v, axis=-1)
    flat_e = top_e.reshape(-1)
    flat_t = jnp.arange(nk, dtype=jnp.int32) // TOP_K
    flat_g = gates.reshape(-1)
    order = jnp.argsort(flat_e)
    se = flat_e[order]
    counts = jax.ops.segment_sum(jnp.ones_like(flat_e), flat_e, num_segments=N_EXPERTS)
    padded = (counts + MOE_BLOCK - 1) // MOE_BLOCK * MOE_BLOCK
    start = jnp.cumsum(counts) - counts
    pend = jnp.cumsum(padded)
    pstart = pend - padded
    dest = pstart[se] + jnp.arange(nk, dtype=jnp.int32) - start[se]
    n_slots = -(-nk // MOE_BLOCK) * MOE_BLOCK + N_EXPERTS * MOE_BLOCK
    n_blocks = n_slots // MOE_BLOCK
    slot_tok = jnp.zeros((n_slots,), jnp.int32).at[dest].set(flat_t[order])
    slot_gate = jnp.zeros((n_slots,), jnp.float32).at[dest].set(flat_g[order])
    blk_start = jnp.arange(n_blocks, dtype=jnp.int32) * MOE_BLOCK
    blk_e = jnp.minimum(jnp.searchsorted(pend, blk_start, side="right"), N_EXPERTS - 1)
    xs = tok[slot_tok].reshape(n_blocks, MOE_BLOCK, -1)

    def expert_block(args):
        xb, e = args
        return swiglu(xb, w_gu[e], w_down[e])

    ys = lax.map(expert_block, (xs, blk_e)).reshape(n_slots, -1)
    ys = ys * slot_gate[:, None].astype(ys.dtype)
    out = jnp.zeros_like(tok).at[slot_tok].add(ys)
    return out.reshape(shp)


def setup_inputs(seed: int = 0) -> dict:
    key = jax.random.key(seed)

    def nrm(n, shape, scale=1.0):
        return jax.random.normal(jax.random.fold_in(key, n), shape, jnp.float32) * scale

    def unif(n, shape, lo, hi):
        return jax.random.uniform(jax.random.fold_in(key, n), shape, jnp.float32, lo, hi)

    D, F, E = D_MODEL, D_FF, N_EXPERTS
    return {
        "x": nrm(0, (BATCH, SEQ, D)),
        "c": nrm(1, (BATCH, D)),
        "ctx": nrm(2, (BATCH, CTX_LEN, D)),
        "c_ctx": nrm(3, (D,)),
        "mod_w": nrm(4, (DEPTH, D, 6 * D), 0.5 * D ** -0.5),
        "mod_b": nrm(5, (DEPTH, 6 * D), 0.02),
        "norm_g": 1.0 + nrm(6, (DEPTH, 4, D), 0.02),
        "rwkv_mu": unif(7, (N_RWKV, 6, D), 0.0, 1.0),
        "rwkv_w_rkv": nrm(8, (N_RWKV, 3, D, D), D ** -0.5),
        "rwkv_w0": unif(9, (N_RWKV, 2, D), -6.0, -1.0),
        "rwkv_w1": nrm(10, (N_RWKV, 2, D, LORA_DECAY), D ** -0.5),
        "rwkv_w2": nrm(11, (N_RWKV, 2, LORA_DECAY, D), 0.1 * LORA_DECAY ** -0.5),
        "rwkv_a0": nrm(12, (N_RWKV, 2, D), 0.2),
        "rwkv_a1": nrm(13, (N_RWKV, 2, D, LORA_ICLR), D ** -0.5),
        "rwkv_a2": nrm(14, (N_RWKV, 2, LORA_ICLR, D), 0.1 * LORA_ICLR ** -0.5),
        "rwkv_g1": nrm(15, (N_RWKV, D, LORA_GATE), D ** -0.5),
        "rwkv_g2": nrm(16, (N_RWKV, LORA_GATE, D), LORA_GATE ** -0.5),
        "rwkv_k_k": 0.85 + nrm(17, (N_RWKV, D), 0.05),
        "rwkv_k_a": 1.0 + nrm(18, (N_RWKV, D), 0.05),
        "rwkv_r_k": nrm(19, (N_RWKV, N_HEADS, HEAD_SIZE), 0.1),
        "rwkv_ln_w": 1.0 + nrm(20, (N_RWKV, D), 0.02),
        "rwkv_ln_b": nrm(21, (N_RWKV, D), 0.02),
        "rwkv_w_out": nrm(22, (N_RWKV, D, D), D ** -0.5),
        "conv_w_in": nrm(23, (N_CONV, D, 3 * D), D ** -0.5),
        "conv_w": nrm(24, (N_CONV, CONV_WIDTH, D), CONV_WIDTH ** -0.5),
        "conv_w_out": nrm(25, (N_CONV, D, D), D ** -0.5),
        "ffn_w_gu": nrm(26, (N_DENSE, D, 2 * F), D ** -0.5),
        "ffn_w_down": nrm(27, (N_DENSE, F, D), F ** -0.5),
        "moe_router": nrm(28, (N_MOE, D, E), D ** -0.5),
        "moe_w_gu": nrm(29, (N_MOE, E, D, 2 * F), D ** -0.5),
        "moe_w_down": nrm(30, (N_MOE, E, F, D), F ** -0.5),
    }


def reference(x, c, ctx, c_ctx, mod_w, mod_b, norm_g, rwkv_mu, rwkv_w_rkv, rwkv_w0, rwkv_w1, rwkv_w2,
              rwkv_a0, rwkv_a1, rwkv_a2, rwkv_g1, rwkv_g2, rwkv_k_k, rwkv_k_a, rwkv_r_k, rwkv_ln_w,
              rwkv_ln_b, rwkv_w_out, conv_w_in, conv_w, conv_w_out, ffn_w_gu, ffn_w_down,
              moe_router, moe_w_gu, moe_w_down):
    silu_c = jax.nn.silu(c)
    silu_cc = jax.nn.silu(c_ctx)
    for i in range(DEPTH):
        kind = i % N_MIXERS
        jm = i // N_MIXERS
        jf = i // 2
        ctx_live = any(l % N_MIXERS == RECURRENT_MIXER for l in range(i + 1, DEPTH))
        m = jnp.split(silu_c @ mod_w[i] + mod_b[i], 6, axis=-1)
        sh_a, sc_a, gt_a, sh_f, sc_f, gt_f = (z[:, None, :] for z in m)
        h = modulate(rms_norm(x, norm_g[i, 0]), sh_a, sc_a)
        if kind == RECURRENT_MIXER or ctx_live:
            cm = jnp.split(silu_cc @ mod_w[i] + mod_b[i], 6, axis=-1)
            hc = modulate(rms_norm(ctx, norm_g[i, 0]), cm[0], cm[1])
        if kind == RECURRENT_MIXER:
            p_tok = (rwkv_mu[jm], rwkv_w_rkv[jm], rwkv_w0[jm], rwkv_w1[jm], rwkv_w2[jm], rwkv_a0[jm],
                     rwkv_a1[jm], rwkv_a2[jm], rwkv_g1[jm], rwkv_g2[jm], rwkv_k_k[jm], rwkv_k_a[jm])
            readout = functools.partial(rwkv7_readout, r_k=rwkv_r_k[jm], ln_w=rwkv_ln_w[jm],
                                        ln_b=rwkv_ln_b[jm], w_out=rwkv_w_out[jm])
            pc = rwkv7_tokenwise(hc, seq_shift(hc), p_tok)
            pl = rwkv7_tokenwise(h, grid_shift(h), p_tok)
            s0 = jnp.zeros((ctx.shape[0], N_HEADS, HEAD_SIZE, HEAD_SIZE), jnp.float32)
            yc, s_f, s_b = rwkv7_bidir(pc, s0, s0)
            yl, _, _ = rwkv7_bidir(pl, s_f, s_b)
            y = readout(yl, pl, dtype=x.dtype)
            if ctx_live:
                y_ctx = readout(yc, pc, dtype=ctx.dtype)
        else:
            y = short_conv(h, conv_w_in[jm], conv_w[jm], conv_w_out[jm])
            if ctx_live:
                y_ctx = short_conv(hc, conv_w_in[jm], conv_w[jm], conv_w_out[jm])
        x = x + gt_a * rms_norm(y, norm_g[i, 1])
        if ctx_live:
            ctx = ctx + cm[2] * rms_norm(y_ctx, norm_g[i, 1])
        if i % 2 == 0:
            ffn = functools.partial(swiglu, w_gu=ffn_w_gu[jf], w_down=ffn_w_down[jf])
        else:
            ffn = functools.partial(moe_swiglu, w_router=moe_router[jf], w_gu=moe_w_gu[jf],
                                    w_down=moe_w_down[jf])
        h = modulate(rms_norm(x, norm_g[i, 2]), sh_f, sc_f)
        x = x + gt_f * rms_norm(ffn(h), norm_g[i, 3])
        if ctx_live:
            hcf = modulate(rms_norm(ctx, norm_g[i, 2]), cm[3], cm[4])
            ctx = ctx + cm[5] * rms_norm(ffn(hcf), norm_g[i, 3])
    return x
```

```python
import functools
import math

import jax
import jax.numpy as jnp
from jax import lax
from jax.experimental import pallas as pl
from jax.experimental.pallas import tpu as pltpu

F32 = jnp.float32
BF16 = jnp.bfloat16

HEAD = 64
LANES = 128
GRID_W = 64
CHUNK = 64
SUB = 16
NORM_EPS = 1e-6
GN_EPS = 64e-5
DECAY_SCALE = math.exp(-0.5)
TOP_K = 2
VMEM_LIMIT = 56 * 1024 * 1024


ROW_TILE = 256
FFN_ROWS = 1024
FFN_COLS = 512


def _cparams(sem):
    return pltpu.CompilerParams(dimension_semantics=sem, vmem_limit_bytes=VMEM_LIMIT)


def _dot(a, b):
    return jnp.dot(a, b, preferred_element_type=F32)


def _dot_nt(a, b):
    return lax.dot_general(a, b, (((1,), (1,)), ((), ())), preferred_element_type=F32)


def _bdot(a, b):
    return _dot(a.astype(BF16), b.astype(BF16))


def _iota(shape, axis):
    return lax.broadcasted_iota(jnp.int32, shape, axis)


def _rms(x, g):
    return x * lax.rsqrt(jnp.mean(x * x, axis=-1, keepdims=True) + NORM_EPS) * g


def _silu(x):
    return x * jax.nn.sigmoid(x)


def _split_bf16(x):
    hi = x.astype(BF16)
    lo = (x - hi.astype(F32)).astype(BF16)
    return hi, lo


def _head_sum(x):
    ones = (_iota((LANES, LANES), 0) // HEAD == _iota((LANES, LANES), 1) // HEAD).astype(BF16)
    hi, lo = _split_bf16(x)
    outs = []
    for p in range(x.shape[-1] // LANES):
        sl = slice(p * LANES, (p + 1) * LANES)
        outs.append(_dot(hi[:, sl], ones) + _dot(lo[:, sl], ones))
    return jnp.concatenate(outs, axis=1)


def _mod_kernel(c_ref, w_ref, b_ref, o_ref):
    s = _silu(c_ref[...])
    o_ref[0] = jnp.dot(s, w_ref[0], preferred_element_type=F32,
                       precision=lax.Precision.HIGHEST) + b_ref[0]


def _modulation(cs, mod_w, mod_b):
    depth, d, n = mod_w.shape
    rows = cs.shape[0]
    tn = 1536
    return pl.pallas_call(
        _mod_kernel,
        grid=(depth, n // tn),
        in_specs=[pl.BlockSpec((rows, d), lambda i, j: (0, 0)),
                  pl.BlockSpec((1, d, tn), lambda i, j: (i, 0, j)),
                  pl.BlockSpec((1, 1, tn), lambda i, j: (i, 0, j))],
        out_specs=pl.BlockSpec((1, rows, tn), lambda i, j: (i, 0, j)),
        out_shape=jax.ShapeDtypeStruct((depth, rows, n), F32),
        compiler_params=_cparams(("arbitrary", "arbitrary")),
        name="modulation",
    )(cs, mod_w, mod_b.reshape(depth, 1, n))


def _tok_kernel(mode, readout, tT, nT, *refs):
    refs = list(refs)
    x_ref = refs.pop(0)
    if mode == "grid":
        xp_ref = refs.pop(0)
        xn_ref = refs.pop(0)
    (mod_ref, ng_ref, mu_ref, wr_ref, wk_ref, wv_ref, w1_ref, w2_ref, w0_ref,
     a1_ref, a2_ref, a0_ref, kkp_ref, ka_ref) = refs[:14]
    refs = refs[14:]
    if readout:
        g1_ref, g2_ref, rk_ref = refs[:3]
        refs = refs[3:]
    r_o, k_o, v_o, kk_o, ic0_o, ic1_o, lw0_o, lw1_o = refs[:8]
    refs = refs[8:]

    d = x_ref.shape[-1]
    sh = mod_ref[0, :, 0:d]
    sc = mod_ref[0, :, d:2 * d]
    g = ng_ref[...]

    def norm_mod(xx):
        return _rms(xx, g) * (1.0 + sc) + sh

    h = norm_mod(x_ref[0])
    if mode == "grid":
        i = pl.program_id(1)
        q = d // 4
        hp = jnp.where(i > 0, norm_mod(xp_ref[0]), 0.0)
        hn = jnp.where(i < nT - 1, norm_mod(xn_ref[0]), 0.0)
        col = _iota((tT, q), 0) % GRID_W
        left = jnp.where(col == 0, 0.0, pltpu.roll(h[:, 0:q], 1, 0))
        right = jnp.where(col == GRID_W - 1, 0.0, pltpu.roll(h[:, q:2 * q], tT - 1, 0))
        up = jnp.concatenate([hp[:, 2 * q:3 * q], h[:tT - GRID_W, 2 * q:3 * q]], axis=0)
        down = jnp.concatenate([h[GRID_W:, 3 * q:], hn[:, 3 * q:]], axis=0)
        hs = jnp.concatenate([left, right, up, down], axis=1)
    else:
        half = d // 2
        row = _iota((tT, half), 0)
        prev = jnp.where(row == 0, 0.0, pltpu.roll(h[:, :half], 1, 0))
        nxt = jnp.where(row == tT - 1, 0.0, pltpu.roll(h[:, half:], tT - 1, 0))
        hs = jnp.concatenate([prev, nxt], axis=1)

    dx = hs - h

    def mix(n):
        return (h + dx * mu_ref[n:n + 1, :]).astype(BF16)

    r = _dot(mix(0), wr_ref[...])
    k = _dot(mix(2), wk_ref[...])
    v = _dot(mix(3), wv_ref[...])
    w1o = jnp.tanh(_dot(mix(1), w1_ref[...])).astype(BF16)
    a1o = _dot(mix(4), a1_ref[...]).astype(BF16)
    ics = []
    for z, (lw_o, ic_o) in enumerate(((lw0_o, ic0_o), (lw1_o, ic1_o))):
        w_pre = _dot(w1o, w2_ref[z]) + w0_ref[z:z + 1, :]
        lw_o[0] = -DECAY_SCALE * jax.nn.sigmoid(w_pre)
        ic = jax.nn.sigmoid(_dot(a1o, a2_ref[z]) + a0_ref[z:z + 1, :])
        ic_o[0] = ic
        ics.append(ic)
    kk = k * kkp_ref[...]
    kk = kk * lax.rsqrt(jnp.maximum(_head_sum(kk * kk), 1e-24))
    r_o[0] = r
    k_o[0] = k
    v_o[0] = v
    kk_o[0] = kk
    if readout:
        gate_o, bon_o = refs
        gate_o[0] = _dot(jax.nn.sigmoid(_dot(mix(5), g1_ref[...])).astype(BF16), g2_ref[...])
        k_avg = k * (1.0 + (0.5 * (ics[0] + ics[1]) - 1.0) * ka_ref[...])
        bon_o[0] = _head_sum(r * k_avg * rk_ref[...]) * v


def _tokenwise(x, mod, mode, readout, p, tT):
    b, t, d = x.shape
    nT = t // tT
    row_spec = pl.BlockSpec((1, tT, d), lambda bi, i: (bi, i, 0))
    in_specs = [row_spec]
    args = [x]
    if mode == "grid":
        hb = tT // GRID_W
        in_specs += [
            pl.BlockSpec((1, GRID_W, d), lambda bi, i: (bi, jnp.maximum(i * hb - 1, 0), 0)),
            pl.BlockSpec((1, GRID_W, d), lambda bi, i: (bi, jnp.minimum((i + 1) * hb, t // GRID_W - 1), 0)),
        ]
        args += [x, x]
    if mod.shape[0] == 1:
        in_specs.append(pl.BlockSpec((1, 1, mod.shape[-1]), lambda bi, i: (0, 0, 0)))
    else:
        in_specs.append(pl.BlockSpec((1, 1, mod.shape[-1]), lambda bi, i: (bi, 0, 0)))
    args.append(mod)
    consts = [p["ng"], p["mu"], p["wr"], p["wk"], p["wv"], p["w1"], p["w2"], p["w0"],
              p["a1"], p["a2"], p["a0"], p["k_k"], p["k_a"]]
    if readout:
        consts += [p["g1"], p["g2"], p["r_k"]]
    for a in consts:
        in_specs.append(pl.BlockSpec(a.shape, lambda bi, i, _n=a.ndim: (0,) * _n))
        args.append(a)
    n_out = 10 if readout else 8
    return pl.pallas_call(
        functools.partial(_tok_kernel, mode, readout, tT, nT),
        grid=(b, nT),
        in_specs=in_specs,
        out_specs=[row_spec] * n_out,
        out_shape=[jax.ShapeDtypeStruct((b, t, d), F32)] * n_out,
        compiler_params=_cparams(("arbitrary", "arbitrary")),
        name="rwkv_tokenwise_" + mode,
    )(*args)


def _scan_chunk(r, kd, v, lw, kk, ic, s, reverse):
    c = r.shape[0]
    n = 2 * c
    ti = _iota((c, c), 0)
    si = _iota((c, c), 1)
    tri = ((si >= ti) if reverse else (si <= ti)).astype(BF16)
    lw_hi, lw_lo = _split_bf16(lw)
    cum = _dot(tri, lw_hi) + _dot(tri, lw_lo)
    pc_log = cum[0:1] if reverse else cum[c - 1:c]
    p_in = jnp.exp(cum)
    p_inv = jnp.exp(-cum)
    p_prev = jnp.exp(cum - lw)
    p_rest = jnp.exp(pc_log - cum)
    b = kk * ic
    rt = r * p_in
    at = -(kk * p_prev)
    bt = b * p_inv
    kt = kd * p_inv
    bp = b * p_rest
    kp = kd * p_rest

    first = _iota((c, LANES), 1) < HEAD

    def ms(x):
        return jnp.concatenate([jnp.where(first, x, 0.0), jnp.where(first, 0.0, x)], axis=0)

    lhs = jnp.concatenate([ms(at), ms(rt)], axis=0).astype(BF16)
    rhs = jnp.concatenate([ms(bt), ms(kt)], axis=0).astype(BF16)
    gmat = _dot_nt(lhs, rhs)
    row = _iota((n, n), 0)
    col = _iota((n, n), 1)
    strict = (col > row) if reverse else (col < row)
    incl = (col >= row) if reverse else (col <= row)
    nmat = jnp.where(strict, gmat[:n, :n], 0.0)
    a_ak = jnp.where(strict, gmat[:n, n:], 0.0)
    a_rb = jnp.where(incl, gmat[n:, :n], 0.0)
    a_rk = jnp.where(incl, gmat[n:, n:], 0.0)

    same = (row // SUB) == (col // SUB)
    nd = jnp.where(same, nmat, 0.0)
    no = jnp.where(same, 0.0, nmat)
    pw = nd
    tp = nd
    span = 2
    while span < SUB:
        pw = _bdot(pw, pw)
        tp = tp + pw + _bdot(tp, pw)
        span *= 2
    m = no + _bdot(tp, no)
    qp = m
    pw = m
    span = 2
    while span < c // SUB:
        pw = _bdot(pw, pw)
        qp = qp + pw + _bdot(qp, pw)
        span *= 2
    tp = tp + qp + _bdot(qp, tp)

    v_ms = ms(v)
    s_b = s.astype(BF16)
    rhs_z = _dot_nt(lhs[:n], s_b) + _bdot(a_ak, v_ms)
    z = rhs_z + _bdot(tp, rhs_z)
    y_ms = _dot_nt(lhs[n:], s_b) + _bdot(a_rb, z) + _bdot(a_rk, v_ms)
    y = y_ms[:c] + y_ms[c:]
    zv_t = jnp.concatenate([z, v_ms], axis=0).T
    upd = _bdot(zv_t, jnp.concatenate([ms(bp), ms(kp)], axis=0))
    s_new = s * jnp.exp(pc_log) + upd
    return y, s_new


def _scan_kernel(pp, nc, has_s0, want_y, want_state, *refs):
    refs = list(refs)
    fwd = refs[:6]
    bwd = refs[6:12]
    ka_ref = refs[12]
    refs = refs[13:]
    if has_s0:
        s0_ref = refs.pop(0)
    if want_y:
        y_refs = (refs.pop(0), refs.pop(0))
    if want_state:
        st_ref = refs.pop(0)
    s_scr = refs.pop(0)
    c = pl.program_id(2)

    @pl.when(c == 0)
    def _():
        if has_s0:
            s_scr[...] = s0_ref[0]
        else:
            s_scr[...] = jnp.zeros_like(s_scr)

    for dr, ins in enumerate((fwd, bwd)):
        r_ref, k_ref, v_ref, kk_ref, ic_ref, lw_ref = ins
        for p in range(pp):
            sl = slice(p * LANES, (p + 1) * LANES)
            ic = ic_ref[0, :, sl]
            kd = k_ref[0, :, sl] * (1.0 + (ic - 1.0) * ka_ref[:, sl])
            y, s_new = _scan_chunk(r_ref[0, :, sl], kd, v_ref[0, :, sl], lw_ref[0, :, sl],
                                   kk_ref[0, :, sl], ic, s_scr[dr, p], reverse=(dr == 1))
            s_scr[dr, p] = s_new
            if want_y:
                y_refs[dr][0, :, sl] = y

    if want_state:
        @pl.when(c == nc - 1)
        def _():
            st_ref[0] = s_scr[...]


def _scan(tok, ka, s0, want_y, want_state, pp=2):
    r, k, v, kk, ic0, ic1, lw0, lw1 = tok
    b, t, d = r.shape
    nc = t // CHUNK
    w = LANES * pp
    npair = d // LANES
    f_spec = pl.BlockSpec((1, CHUNK, w), lambda bi, hi, ci: (bi, ci, hi))
    b_spec = pl.BlockSpec((1, CHUNK, w), lambda bi, hi, ci: (bi, nc - 1 - ci, hi))
    st_spec = pl.BlockSpec((1, 2, pp, LANES, LANES), lambda bi, hi, ci: (bi, 0, hi, 0, 0))
    in_specs = [f_spec] * 6 + [b_spec] * 6 + [pl.BlockSpec((1, w), lambda bi, hi, ci: (0, hi))]
    args = [r, k, v, kk, ic0, lw0, r, k, v, kk, ic1, lw1, ka]
    if s0 is not None:
        in_specs.append(st_spec)
        args.append(s0)
    out_specs, out_shape = [], []
    if want_y:
        out_specs += [f_spec, b_spec]
        out_shape += [jax.ShapeDtypeStruct((b, t, d), F32)] * 2
    if want_state:
        out_specs.append(st_spec)
        out_shape.append(jax.ShapeDtypeStruct((b, 2, npair, LANES, LANES), F32))
    return pl.pallas_call(
        functools.partial(_scan_kernel, pp, nc, s0 is not None, want_y, want_state),
        grid=(b, npair // pp, nc),
        in_specs=in_specs,
        out_specs=out_specs,
        out_shape=out_shape,
        scratch_shapes=[pltpu.VMEM((2, pp, LANES, LANES), F32)],
        compiler_params=_cparams(("arbitrary", "arbitrary", "arbitrary")),
        name="rwkv_scan_ctx" if s0 is None else "rwkv_scan_latent",
    )(*args)


def _readout_kernel(yf_ref, yb_ref, bon_ref, gate_ref, x_ref, mod_ref, lnw_ref, lnb_ref,
                    wo_ref, g1_ref, g2_ref, x_o, h_o):
    d = x_ref.shape[-1]
    y = yf_ref[0] + yb_ref[0]
    mean = _head_sum(y) * (1.0 / HEAD)
    yc = y - mean
    var = _head_sum(yc * yc) * (1.0 / HEAD)
    o = yc * lax.rsqrt(var + GN_EPS) * lnw_ref[...] + lnb_ref[...] + bon_ref[0]
    att = _dot((o * gate_ref[0]).astype(BF16), wo_ref[...])
    gt_a = mod_ref[0, :, 2 * d:3 * d]
    sh_f = mod_ref[0, :, 3 * d:4 * d]
    sc_f = mod_ref[0, :, 4 * d:5 * d]
    x1 = x_ref[0] + gt_a * _rms(att, g1_ref[...])
    x_o[0] = x1
    h_o[0] = (_rms(x1, g2_ref[...]) * (1.0 + sc_f) + sh_f).astype(BF16)


def _readout(yf, yb, bon, gate, x, mod, lnw, lnb, wo, g1, g2, tT):
    b, t, d = x.shape
    row_spec = pl.BlockSpec((1, tT, d), lambda bi, i: (bi, i, 0))
    vec = pl.BlockSpec((1, d), lambda bi, i: (0, 0))
    return pl.pallas_call(
        _readout_kernel,
        grid=(b, t // tT),
        in_specs=[row_spec] * 5 + [pl.BlockSpec((1, 1, 6 * d), lambda bi, i: (bi, 0, 0)),
                                   vec, vec, pl.BlockSpec((d, d), lambda bi, i: (0, 0)), vec, vec],
        out_specs=[row_spec, row_spec],
        out_shape=[jax.ShapeDtypeStruct((b, t, d), F32), jax.ShapeDtypeStruct((b, t, d), BF16)],
        compiler_params=_cparams(("arbitrary", "arbitrary")),
        name="rwkv_readout",
    )(yf, yb, bon, gate, x, mod, lnw, lnb, wo, g1, g2)


def _ffn_kernel(nf, h_ref, wg_ref, wu_ref, wd_ref, x_ref, mod_ref, g3_ref, gn_ref, mod2_ref,
                x_o, h_o, acc):
    f = pl.program_id(1)
    d = x_ref.shape[-1]

    @pl.when(f == 0)
    def _():
        acc[...] = jnp.zeros_like(acc)

    h = h_ref[...]
    act = _silu(_dot(h, wg_ref[...])) * _dot(h, wu_ref[...])
    acc[...] += _dot(act.astype(BF16), wd_ref[...])

    @pl.when(f == nf - 1)
    def _():
        gt_f = mod_ref[0, :, 5 * d:6 * d]
        x2 = x_ref[...] + gt_f * _rms(acc[...], g3_ref[...])
        x_o[...] = x2
        sh = mod2_ref[0, :, 0:d]
        sc = mod2_ref[0, :, d:2 * d]
        h_o[...] = (_rms(x2, gn_ref[...]) * (1.0 + sc) + sh).astype(BF16)


def _ffn(h, w_gu, w_down, x, mod, g3, gn, mod2, rows_per_batch, tm, tf):
    n, d = x.shape
    ff = w_down.shape[0]
    nf = ff // tf
    per = rows_per_batch // tm
    row = pl.BlockSpec((tm, d), lambda i, f: (i, 0))
    vec = pl.BlockSpec((1, d), lambda i, f: (0, 0))
    modspec = pl.BlockSpec((1, 1, 6 * d), lambda i, f: (i // per, 0, 0))
    return pl.pallas_call(
        functools.partial(_ffn_kernel, nf),
        grid=(n // tm, nf),
        in_specs=[row,
                  pl.BlockSpec((d, tf), lambda i, f: (0, f)),
                  pl.BlockSpec((d, tf), lambda i, f: (0, nf + f)),
                  pl.BlockSpec((tf, d), lambda i, f: (f, 0)),
                  row, modspec, vec, vec, modspec],
        out_specs=[row, row],
        out_shape=[jax.ShapeDtypeStruct((n, d), F32), jax.ShapeDtypeStruct((n, d), BF16)],
        scratch_shapes=[pltpu.VMEM((tm, d), F32)],
        compiler_params=_cparams(("arbitrary", "arbitrary")),
        name="dense_swiglu",
    )(h, w_gu, w_gu, w_down, x, mod, g3, gn, mod2)


HALO = 16


def _conv_kernel(tT, nT, h_ref, hp_ref, hn_ref, x_ref, mod_ref, win_ref, cw_ref, wo_ref,
                 g1_ref, g2_ref, wr_ref, x_o, h_o, lg_o):
    i = pl.program_id(1)
    d = x_ref.shape[-1]
    n = tT + 2 * HALO
    h_ext = jnp.concatenate([hp_ref[0], h_ref[0], hn_ref[0]], axis=0)
    proj = _dot(h_ext, win_ref[...])
    z = proj[:, d:2 * d] * proj[:, 2 * d:]
    row = _iota((n, d), 0)
    dead = ((row < HALO) & (i == 0)) | ((row >= HALO + tT) & (i == nT - 1))
    z = jnp.where(dead, 0.0, z)
    conv = (pltpu.roll(z, 1, 0) * cw_ref[0:1, :] + z * cw_ref[1:2, :]
            + pltpu.roll(z, n - 1, 0) * cw_ref[2:3, :])
    gated = (proj[HALO:HALO + tT, 0:d] * conv[HALO:HALO + tT]).astype(BF16)
    y = _dot(gated, wo_ref[...])
    gt_a = mod_ref[0, :, 2 * d:3 * d]
    sh_f = mod_ref[0, :, 3 * d:4 * d]
    sc_f = mod_ref[0, :, 4 * d:5 * d]
    x3 = x_ref[0] + gt_a * _rms(y, g1_ref[...])
    x_o[0] = x3
    h4 = _rms(x3, g2_ref[...]) * (1.0 + sc_f) + sh_f
    h_o[0] = h4.astype(BF16)
    lg_o[0] = jnp.dot(h4, wr_ref[...], preferred_element_type=F32,
                      precision=lax.Precision.HIGHEST)


def _conv_layer(h, x, mod, w_in, conv_w, w_out, g1, g2, w_router, tT):
    b, t, d = x.shape
    nT = t // tT
    hb = tT // HALO
    row_spec = pl.BlockSpec((1, tT, d), lambda bi, i: (bi, i, 0))
    vec = pl.BlockSpec((1, d), lambda bi, i: (0, 0))
    full = lambda a: pl.BlockSpec(a.shape, lambda bi, i, _n=a.ndim: (0,) * _n)
    return pl.pallas_call(
        functools.partial(_conv_kernel, tT, nT),
        grid=(b, nT),
        in_specs=[row_spec,
                  pl.BlockSpec((1, HALO, d), lambda bi, i: (bi, jnp.maximum(i * hb - 1, 0), 0)),
                  pl.BlockSpec((1, HALO, d), lambda bi, i: (bi, jnp.minimum((i + 1) * hb, t // HALO - 1), 0)),
                  row_spec, pl.BlockSpec((1, 1, 6 * d), lambda bi, i: (bi, 0, 0)),
                  full(w_in), full(conv_w), full(w_out), vec, vec, full(w_router)],
        out_specs=[row_spec, row_spec, pl.BlockSpec((1, tT, LANES), lambda bi, i: (bi, i, 0))],
        out_shape=[jax.ShapeDtypeStruct((b, t, d), F32), jax.ShapeDtypeStruct((b, t, d), BF16),
                   jax.ShapeDtypeStruct((b, t, LANES), F32)],
        compiler_params=_cparams(("arbitrary", "arbitrary")),
        name="short_conv",
    )(h, h, h, x, mod, w_in, conv_w, w_out, g1, g2, w_router)


MOE_SUB = 256


def _moe_kernel(nf, tm, e_ref, nv_ref, x_ref, wg_ref, wu_ref, wd_ref, y_o, acc):
    i = pl.program_id(0)
    f = pl.program_id(1)
    nv = nv_ref[i]

    for s in range(tm // MOE_SUB):
        rows = slice(s * MOE_SUB, (s + 1) * MOE_SUB)
        live = nv > s * MOE_SUB

        @pl.when(live)
        def _():
            h = x_ref[rows, :]
            act = _silu(_dot(h, wg_ref[0])) * _dot(h, wu_ref[0])
            part = _dot(act.astype(BF16), wd_ref[0])

            @pl.when(f == 0)
            def _():
                acc[rows, :] = part

            @pl.when(f > 0)
            def _():
                acc[rows, :] += part

            @pl.when(f == nf - 1)
            def _():
                y_o[rows, :] = acc[rows, :]

        @pl.when(jnp.logical_not(live) & (f == nf - 1))
        def _():
            y_o[rows, :] = jnp.zeros((MOE_SUB, y_o.shape[-1]), F32)


def _moe_experts(xs, blk_e, blk_nv, w_gu, w_down, tm, tf):
    n, d = xs.shape
    ff = w_down.shape[1]
    nf = ff // tf
    nblk = n // tm

    def f_eff(i, f, nv_ref):
        return jnp.where(nv_ref[i] > 0, f, nf - 1)

    grid_spec = pltpu.PrefetchScalarGridSpec(
        num_scalar_prefetch=2,
        grid=(nblk, nf),
        in_specs=[pl.BlockSpec((tm, d), lambda i, f, e, nv: (i, 0)),
                  pl.BlockSpec((1, d, tf), lambda i, f, e, nv: (e[i], 0, f_eff(i, f, nv))),
                  pl.BlockSpec((1, d, tf), lambda i, f, e, nv: (e[i], 0, nf + f_eff(i, f, nv))),
                  pl.BlockSpec((1, tf, d), lambda i, f, e, nv: (e[i], f_eff(i, f, nv), 0))],
        out_specs=pl.BlockSpec((tm, d), lambda i, f, e, nv: (i, 0)),
        scratch_shapes=[pltpu.VMEM((tm, d), F32)],
    )
    return pl.pallas_call(
        functools.partial(_moe_kernel, nf, tm),
        grid_spec=grid_spec,
        out_shape=jax.ShapeDtypeStruct((n, d), F32),
        compiler_params=_cparams(("arbitrary", "arbitrary")),
        name="moe_experts",
    )(blk_e, blk_nv, xs, w_gu, w_gu, w_down)


def _combine_kernel(y0_ref, y1_ref, gt_ref, x_ref, mod_ref, g_ref, o_ref):
    d = x_ref.shape[-1]
    gates = gt_ref[0]
    y = y0_ref[0] * gates[:, 0:1] + y1_ref[0] * gates[:, 1:2]
    gt_f = mod_ref[0, :, 5 * d:6 * d]
    o_ref[0] = x_ref[0] + gt_f * _rms(y, g_ref[...])


def _combine(y0, y1, gates, x, mod, g, tT):
    b, t, d = x.shape
    row_spec = pl.BlockSpec((1, tT, d), lambda bi, i: (bi, i, 0))
    return pl.pallas_call(
        _combine_kernel,
        grid=(b, t // tT),
        in_specs=[row_spec, row_spec, pl.BlockSpec((1, tT, LANES), lambda bi, i: (bi, i, 0)),
                  row_spec, pl.BlockSpec((1, 1, 6 * d), lambda bi, i: (bi, 0, 0)),
                  pl.BlockSpec((1, d), lambda bi, i: (0, 0))],
        out_specs=row_spec,
        out_shape=jax.ShapeDtypeStruct((b, t, d), F32),
        compiler_params=_cparams(("arbitrary", "arbitrary")),
        name="moe_combine",
    )(y0, y1, gates, x, mod, g)


def _route(logits, n_experts, tm):
    n = logits.shape[0]
    nk = n * TOP_K
    top_v, top_e = lax.top_k(logits, TOP_K)
    gates = jax.nn.softmax(top_v, axis=-1)
    flat_e = top_e.reshape(-1).astype(jnp.int32)
    order = jnp.argsort(flat_e, stable=True)
    se = flat_e[order]
    counts = jnp.zeros((n_experts,), jnp.int32).at[flat_e].add(1)
    nblk_e = (counts + tm - 1) // tm
    start = jnp.cumsum(counts) - counts
    bend = jnp.cumsum(nblk_e)
    bstart = bend - nblk_e
    dest = bstart[se] * tm + jnp.arange(nk, dtype=jnp.int32) - start[se]
    nblk = nk // tm + n_experts
    slot_tok = jnp.zeros((nblk * tm,), jnp.int32).at[dest].set(order // TOP_K)
    pos = jnp.zeros((nk,), jnp.int32).at[order].set(dest).reshape(n, TOP_K)
    blk = jnp.arange(nblk, dtype=jnp.int32)
    blk_e = jnp.minimum(jnp.searchsorted(bend, blk, side="right"), n_experts - 1).astype(jnp.int32)
    blk_nv = jnp.clip(counts[blk_e] - (blk - bstart[blk_e]) * tm, 0, tm)
    blk_nv = jnp.where(blk < bend[-1], blk_nv, 0).astype(jnp.int32)
    return gates, slot_tok, pos, blk_e, blk_nv


def kernel(x, c, ctx, c_ctx, mod_w, mod_b, norm_g, rwkv_mu, rwkv_w_rkv, rwkv_w0, rwkv_w1, rwkv_w2,
           rwkv_a0, rwkv_a1, rwkv_a2, rwkv_g1, rwkv_g2, rwkv_k_k, rwkv_k_a, rwkv_r_k, rwkv_ln_w,
           rwkv_ln_b, rwkv_w_out, conv_w_in, conv_w, conv_w_out, ffn_w_gu, ffn_w_down,
           moe_router, moe_w_gu, moe_w_down):
    b, t, d = x.shape
    n_experts = moe_router.shape[-1]
    rows = 16
    cs = jnp.zeros((rows, d), F32).at[:b].set(c).at[b].set(c_ctx)
    mods = _modulation(cs, mod_w, mod_b)
    mod0 = mods[0, :b].reshape(b, 1, 6 * d)
    mod0c = mods[0, b].reshape(1, 1, 6 * d)
    mod1 = mods[1, :b].reshape(b, 1, 6 * d)

    def pad_lora(w):
        zr = jnp.zeros_like(w[0])
        return jnp.stack([jnp.concatenate([w[0], zr], 0), jnp.concatenate([zr, w[1]], 0)]).astype(BF16)

    lg = rwkv_g1.shape[-1]
    lgp = -(-lg // LANES) * LANES
    p = {
        "ng": norm_g[0, 0].reshape(1, d),
        "mu": rwkv_mu[0],
        "wr": rwkv_w_rkv[0, 0].astype(BF16), "wk": rwkv_w_rkv[0, 1].astype(BF16),
        "wv": rwkv_w_rkv[0, 2].astype(BF16),
        "w1": jnp.concatenate([rwkv_w1[0, 0], rwkv_w1[0, 1]], axis=1).astype(BF16),
        "w2": pad_lora(rwkv_w2[0]), "w0": rwkv_w0[0],
        "a1": jnp.concatenate([rwkv_a1[0, 0], rwkv_a1[0, 1]], axis=1).astype(BF16),
        "a2": pad_lora(rwkv_a2[0]), "a0": rwkv_a0[0],
        "k_k": rwkv_k_k[0].reshape(1, d), "k_a": rwkv_k_a[0].reshape(1, d),
        "g1": jnp.pad(rwkv_g1[0], ((0, 0), (0, lgp - lg))).astype(BF16),
        "g2": jnp.pad(rwkv_g2[0], ((0, lgp - lg), (0, 0))).astype(BF16),
        "r_k": rwkv_r_k[0].reshape(1, d),
    }

    tok_c = _tokenwise(ctx, mod0c, "seq", False, p, ctx.shape[1])
    s_ctx = _scan(tok_c, p["k_a"], None, want_y=False, want_state=True)[0]
    tt = min(ROW_TILE, t)
    tm = min(FFN_ROWS, t)
    tf = min(FFN_COLS, ffn_w_down.shape[1])
    tok_l = _tokenwise(x, mod0, "grid", True, p, tt)
    yf, yb = _scan(tok_l[:8], p["k_a"], s_ctx, want_y=True, want_state=False)
    x1, h2 = _readout(yf, yb, tok_l[9], tok_l[8], x, mod0,
                      rwkv_ln_w[0].reshape(1, d), rwkv_ln_b[0].reshape(1, d),
                      rwkv_w_out[0].astype(BF16), norm_g[0, 1].reshape(1, d),
                      norm_g[0, 2].reshape(1, d), tt)

    x2, h3 = _ffn(h2.reshape(b * t, d), ffn_w_gu[0].astype(BF16), ffn_w_down[0].astype(BF16),
                  x1.reshape(b * t, d), mod0, norm_g[0, 3].reshape(1, d),
                  norm_g[1, 0].reshape(1, d), mod1, t, tm, tf)

    w_router = jnp.pad(moe_router[0], ((0, 0), (0, LANES - n_experts)))
    x3, h4, logits = _conv_layer(h3.reshape(b, t, d), x2.reshape(b, t, d), mod1,
                                 conv_w_in[0].astype(BF16), conv_w[0], conv_w_out[0].astype(BF16),
                                 norm_g[1, 1].reshape(1, d), norm_g[1, 2].reshape(1, d), w_router, tt)

    gates, slot_tok, pos, blk_e, blk_nv = _route(logits.reshape(b * t, LANES)[:, :n_experts],
                                                 n_experts, tm)
    xs = jnp.take(h4.reshape(b * t, d), slot_tok, axis=0)
    ys = _moe_experts(xs, blk_e, blk_nv, moe_w_gu[0].astype(BF16), moe_w_down[0].astype(BF16),
                      tm, tf)
    y0 = jnp.take(ys, pos[:, 0], axis=0).reshape(b, t, d)
    y1 = jnp.take(ys, pos[:, 1], axis=0).reshape(b, t, d)
    gates_p = jnp.pad(gates, ((0, 0), (0, LANES - TOP_K))).reshape(b, t, LANES)
    return _combine(y0, y1, gates_p, x3, mod1, norm_g[1, 3].reshape(1, d), tt)
```

```python
import functools
import math

import jax
import jax.numpy as jnp
from jax import lax
from jax.experimental import pallas as pl
from jax.experimental.pallas import tpu as pltpu

F32 = jnp.float32
BF16 = jnp.bfloat16

HEAD = 64
LANES = 128
GRID_W = 64
CHUNK = 64
SUB = 16
SCAN_PAIRS = 4
NORM_EPS = 1e-6
GN_EPS = 64e-5
DECAY_SCALE = math.exp(-0.5)
TOP_K = 2
VMEM_LIMIT = 56 * 1024 * 1024


ROW_TILE = 256
FFN_ROWS = 1024
FFN_COLS = 512


def _cparams(sem):
    return pltpu.CompilerParams(dimension_semantics=sem, vmem_limit_bytes=VMEM_LIMIT)


def _dot(a, b):
    return jnp.dot(a, b, preferred_element_type=F32)


def _dot_nt(a, b):
    return lax.dot_general(a, b, (((1,), (1,)), ((), ())), preferred_element_type=F32)


def _bdot(a, b):
    return _dot(a.astype(BF16), b.astype(BF16))


def _iota(shape, axis):
    return lax.broadcasted_iota(jnp.int32, shape, axis)


def _rms(x, g):
    return x * lax.rsqrt(jnp.mean(x * x, axis=-1, keepdims=True) + NORM_EPS) * g


def _silu(x):
    return x * jax.nn.sigmoid(x)


def _split_bf16(x):
    hi = x.astype(BF16)
    lo = (x - hi.astype(F32)).astype(BF16)
    return hi, lo


def _head_sum(x):
    ones = (_iota((LANES, LANES), 0) // HEAD == _iota((LANES, LANES), 1) // HEAD).astype(BF16)
    hi, lo = _split_bf16(x)
    outs = []
    for p in range(x.shape[-1] // LANES):
        sl = slice(p * LANES, (p + 1) * LANES)
        outs.append(_dot(hi[:, sl], ones) + _dot(lo[:, sl], ones))
    return jnp.concatenate(outs, axis=1)


def _mod_kernel(c_ref, w_ref, b_ref, o_ref):
    s = _silu(c_ref[...])
    o_ref[0] = jnp.dot(s, w_ref[0], preferred_element_type=F32,
                       precision=lax.Precision.HIGHEST) + b_ref[0]


def _modulation(cs, mod_w, mod_b):
    depth, d, n = mod_w.shape
    rows = cs.shape[0]
    tn = 1536
    return pl.pallas_call(
        _mod_kernel,
        grid=(depth, n // tn),
        in_specs=[pl.BlockSpec((rows, d), lambda i, j: (0, 0)),
                  pl.BlockSpec((1, d, tn), lambda i, j: (i, 0, j)),
                  pl.BlockSpec((1, 1, tn), lambda i, j: (i, 0, j))],
        out_specs=pl.BlockSpec((1, rows, tn), lambda i, j: (i, 0, j)),
        out_shape=jax.ShapeDtypeStruct((depth, rows, n), F32),
        compiler_params=_cparams(("arbitrary", "arbitrary")),
        name="modulation",
    )(cs, mod_w, mod_b.reshape(depth, 1, n))


def _tok_kernel(mode, readout, tT, nT, *refs):
    refs = list(refs)
    x_ref = refs.pop(0)
    if mode == "grid":
        xp_ref = refs.pop(0)
        xn_ref = refs.pop(0)
    (mod_ref, ng_ref, mu_ref, wr_ref, wk_ref, wv_ref, w1_ref, w2_ref, w0_ref,
     a1_ref, a2_ref, a0_ref, kkp_ref, ka_ref) = refs[:14]
    refs = refs[14:]
    if readout:
        g1_ref, g2_ref, rk_ref = refs[:3]
        refs = refs[3:]
    r_o, k_o, v_o, kk_o, ic0_o, ic1_o, lw0_o, lw1_o = refs[:8]
    refs = refs[8:]

    d = x_ref.shape[-1]
    sh = mod_ref[0, :, 0:d]
    sc = mod_ref[0, :, d:2 * d]
    g = ng_ref[...]

    def norm_mod(xx):
        return _rms(xx, g) * (1.0 + sc) + sh

    h = norm_mod(x_ref[0])
    if mode == "grid":
        i = pl.program_id(1)
        q = d // 4
        hp = jnp.where(i > 0, norm_mod(xp_ref[0]), 0.0)
        hn = jnp.where(i < nT - 1, norm_mod(xn_ref[0]), 0.0)
        col = _iota((tT, q), 0) % GRID_W
        left = jnp.where(col == 0, 0.0, pltpu.roll(h[:, 0:q], 1, 0))
        right = jnp.where(col == GRID_W - 1, 0.0, pltpu.roll(h[:, q:2 * q], tT - 1, 0))
        up = jnp.concatenate([hp[:, 2 * q:3 * q], h[:tT - GRID_W, 2 * q:3 * q]], axis=0)
        down = jnp.concatenate([h[GRID_W:, 3 * q:], hn[:, 3 * q:]], axis=0)
        hs = jnp.concatenate([left, right, up, down], axis=1)
    else:
        half = d // 2
        row = _iota((tT, half), 0)
        prev = jnp.where(row == 0, 0.0, pltpu.roll(h[:, :half], 1, 0))
        nxt = jnp.where(row == tT - 1, 0.0, pltpu.roll(h[:, half:], tT - 1, 0))
        hs = jnp.concatenate([prev, nxt], axis=1)

    dx = hs - h

    def mix(n):
        return (h + dx * mu_ref[n:n + 1, :]).astype(BF16)

    r = _dot(mix(0), wr_ref[...])
    k = _dot(mix(2), wk_ref[...])
    v = _dot(mix(3), wv_ref[...])
    w1o = jnp.tanh(_dot(mix(1), w1_ref[...])).astype(BF16)
    a1o = _dot(mix(4), a1_ref[...]).astype(BF16)
    ics = []
    for z, (lw_o, ic_o) in enumerate(((lw0_o, ic0_o), (lw1_o, ic1_o))):
        w_pre = _dot(w1o, w2_ref[z]) + w0_ref[z:z + 1, :]
        lw_o[0] = -DECAY_SCALE * jax.nn.sigmoid(w_pre)
        ic = jax.nn.sigmoid(_dot(a1o, a2_ref[z]) + a0_ref[z:z + 1, :])
        ic_o[0] = ic
        ics.append(ic)
    kk = k * kkp_ref[...]
    kk = kk * lax.rsqrt(jnp.maximum(_head_sum(kk * kk), 1e-24))
    r_o[0] = r
    k_o[0] = k
    v_o[0] = v
    kk_o[0] = kk
    if readout:
        gate_o, bon_o = refs
        gate_o[0] = _dot(jax.nn.sigmoid(_dot(mix(5), g1_ref[...])).astype(BF16), g2_ref[...])
        k_avg = k * (1.0 + (0.5 * (ics[0] + ics[1]) - 1.0) * ka_ref[...])
        bon_o[0] = _head_sum(r * k_avg * rk_ref[...]) * v


def _tokenwise(x, mod, mode, readout, p, tT):
    b, t, d = x.shape
    nT = t // tT
    row_spec = pl.BlockSpec((1, tT, d), lambda bi, i: (bi, i, 0))
    in_specs = [row_spec]
    args = [x]
    if mode == "grid":
        hb = tT // GRID_W
        in_specs += [
            pl.BlockSpec((1, GRID_W, d), lambda bi, i: (bi, jnp.maximum(i * hb - 1, 0), 0)),
            pl.BlockSpec((1, GRID_W, d), lambda bi, i: (bi, jnp.minimum((i + 1) * hb, t // GRID_W - 1), 0)),
        ]
        args += [x, x]
    if mod.shape[0] == 1:
        in_specs.append(pl.BlockSpec((1, 1, mod.shape[-1]), lambda bi, i: (0, 0, 0)))
    else:
        in_specs.append(pl.BlockSpec((1, 1, mod.shape[-1]), lambda bi, i: (bi, 0, 0)))
    args.append(mod)
    consts = [p["ng"], p["mu"], p["wr"], p["wk"], p["wv"], p["w1"], p["w2"], p["w0"],
              p["a1"], p["a2"], p["a0"], p["k_k"], p["k_a"]]
    if readout:
        consts += [p["g1"], p["g2"], p["r_k"]]
    for a in consts:
        in_specs.append(pl.BlockSpec(a.shape, lambda bi, i, _n=a.ndim: (0,) * _n))
        args.append(a)
    n_out = 10 if readout else 8
    return pl.pallas_call(
        functools.partial(_tok_kernel, mode, readout, tT, nT),
        grid=(b, nT),
        in_specs=in_specs,
        out_specs=[row_spec] * n_out,
        out_shape=[jax.ShapeDtypeStruct((b, t, d), F32)] * n_out,
        compiler_params=_cparams(("arbitrary", "arbitrary")),
        name="rwkv_tokenwise_" + mode,
    )(*args)


def _scan_prep(r, kd, v, lw, kk, ic, reverse):
    c = r.shape[0]
    ti = _iota((c, c), 0)
    si = _iota((c, c), 1)
    tri = ((si >= ti) if reverse else (si <= ti)).astype(BF16)
    lw_hi, lw_lo = _split_bf16(lw)
    cum = _dot(tri, lw_hi) + _dot(tri, lw_lo)
    pc_log = cum[0:1] if reverse else cum[c - 1:c]
    p_inv = jnp.exp(-cum)
    p_rest = jnp.exp(pc_log - cum)
    b = kk * ic
    return {"rt": r * jnp.exp(cum), "at": -(kk * jnp.exp(cum - lw)),
            "bt": b * p_inv, "kt": kd * p_inv, "bp": b * p_rest, "kp": kd * p_rest,
            "v": v, "pc": jnp.exp(pc_log)}


def _scan_step(prep, states):
    c = prep[0]["v"].shape[0]
    n = 2 * c
    pp = len(states[0])
    chains = [(dr, p) for dr in range(2) for p in range(pp)]
    first = _iota((c, LANES), 1) < HEAD
    row = _iota((n, n), 0)
    col = _iota((n, n), 1)
    strict = (col < row, col > row)
    incl = (col <= row, col >= row)
    same = (row // SUB) == (col // SUB)

    def ms(name, dr, p):
        x = prep[dr][name][:, p * LANES:(p + 1) * LANES]
        return jnp.concatenate([jnp.where(first, x, 0.0), jnp.where(first, 0.0, x)], axis=0)

    def each(fn, *lists):
        return [fn(*xs) for xs in zip(*lists)]

    lhs = [jnp.concatenate([ms("at", dr, p), ms("rt", dr, p)], axis=0).astype(BF16) for dr, p in chains]
    rhs = [jnp.concatenate([ms("bt", dr, p), ms("kt", dr, p)], axis=0).astype(BF16) for dr, p in chains]
    v_ms = [ms("v", dr, p).astype(BF16) for dr, p in chains]
    gmat = each(_dot_nt, lhs, rhs)
    nmat = [jnp.where(strict[dr], g[:n, :n], 0.0) for (dr, _), g in zip(chains, gmat)]
    a_ak = [jnp.where(strict[dr], g[:n, n:], 0.0).astype(BF16) for (dr, _), g in zip(chains, gmat)]
    a_rb = [jnp.where(incl[dr], g[n:, :n], 0.0).astype(BF16) for (dr, _), g in zip(chains, gmat)]
    a_rk = [jnp.where(incl[dr], g[n:, n:], 0.0).astype(BF16) for (dr, _), g in zip(chains, gmat)]

    nd = [jnp.where(same, x, 0.0) for x in nmat]
    no = [jnp.where(same, 0.0, x) for x in nmat]
    pw = nd
    tp = nd
    span = 2
    while span < SUB:
        pw = each(_bdot, pw, pw)
        tp = each(lambda t, q: t + q + _bdot(t, q), tp, pw)
        span *= 2
    m = each(lambda t, o: o + _bdot(t, o), tp, no)
    qp = m
    pw = m
    span = 2
    while span < c // SUB:
        pw = each(_bdot, pw, pw)
        qp = each(lambda t, q: t + q + _bdot(t, q), qp, pw)
        span *= 2
    tp = each(lambda t, q: t + q + _bdot(q, t), tp, qp)

    s_b = [states[dr][p].astype(BF16) for dr, p in chains]
    rhs_z = each(lambda l, s, a, v: _dot_nt(l[:n], s) + _dot(a, v), lhs, s_b, a_ak, v_ms)
    z = each(lambda t, x: x + _bdot(t, x), tp, rhs_z)
    y_ms = each(lambda l, s, ab, zz, ak, v: _dot_nt(l[n:], s) + _bdot(ab, zz) + _dot(ak, v),
                lhs, s_b, a_rb, z, a_rk, v_ms)
    upd = [_bdot(jnp.concatenate([zz, v.astype(F32)], axis=0).T,
                 jnp.concatenate([ms("bp", dr, p), ms("kp", dr, p)], axis=0))
           for (dr, p), zz, v in zip(chains, z, v_ms)]
    y = [jnp.concatenate([ym[:c] + ym[c:] for (d2, _), ym in zip(chains, y_ms) if d2 == dr], axis=1)
         for dr in range(2)]
    new_states = [[None] * pp for _ in range(2)]
    for (dr, p), u in zip(chains, upd):
        new_states[dr][p] = states[dr][p] * prep[dr]["pc"][:, p * LANES:(p + 1) * LANES] + u
    return y, new_states


def _scan_kernel(pp, nc, has_s0, want_y, want_state, *refs):
    refs = list(refs)
    ka_ref = refs[12]
    dirs = (refs[:6], refs[6:12])
    refs = refs[13:]
    if has_s0:
        s0_ref = refs.pop(0)
    if want_y:
        y_refs = (refs.pop(0), refs.pop(0))
    if want_state:
        st_ref = refs.pop(0)
    s_scr = refs.pop(0)
    c = pl.program_id(2)

    @pl.when(c == 0)
    def _():
        if has_s0:
            s_scr[...] = s0_ref[0]
        else:
            s_scr[...] = jnp.zeros_like(s_scr)

    prep = []
    for dr, (r_ref, k_ref, v_ref, kk_ref, ic_ref, lw_ref) in enumerate(dirs):
        ic = ic_ref[0]
        kd = k_ref[0] * (1.0 + (ic - 1.0) * ka_ref[...])
        prep.append(_scan_prep(r_ref[0], kd, v_ref[0], lw_ref[0], kk_ref[0], ic, reverse=(dr == 1)))
    states = [[s_scr[dr, p] for p in range(pp)] for dr in range(2)]
    y, new_states = _scan_step(prep, states)
    for dr in range(2):
        for p in range(pp):
            s_scr[dr, p] = new_states[dr][p]
        if want_y:
            y_refs[dr][0] = y[dr]

    if want_state:
        @pl.when(c == nc - 1)
        def _():
            st_ref[0] = s_scr[...]


def _scan(tok, ka, s0, want_y, want_state):
    r, k, v, kk, ic0, ic1, lw0, lw1 = tok
    b, t, d = r.shape
    nc = t // CHUNK
    npair = d // LANES
    pp = min(SCAN_PAIRS, npair)
    w = LANES * pp
    f_spec = pl.BlockSpec((1, CHUNK, w), lambda bi, hi, ci: (bi, ci, hi))
    b_spec = pl.BlockSpec((1, CHUNK, w), lambda bi, hi, ci: (bi, nc - 1 - ci, hi))
    st_spec = pl.BlockSpec((1, 2, pp, LANES, LANES), lambda bi, hi, ci: (bi, 0, hi, 0, 0))
    in_specs = [f_spec] * 6 + [b_spec] * 6 + [pl.BlockSpec((1, w), lambda bi, hi, ci: (0, hi))]
    args = [r, k, v, kk, ic0, lw0, r, k, v, kk, ic1, lw1, ka]
    if s0 is not None:
        in_specs.append(st_spec)
        args.append(s0)
    out_specs, out_shape = [], []
    if want_y:
        out_specs += [f_spec, b_spec]
        out_shape += [jax.ShapeDtypeStruct((b, t, d), F32)] * 2
    if want_state:
        out_specs.append(st_spec)
        out_shape.append(jax.ShapeDtypeStruct((b, 2, npair, LANES, LANES), F32))
    return pl.pallas_call(
        functools.partial(_scan_kernel, pp, nc, s0 is not None, want_y, want_state),
        grid=(b, npair // pp, nc),
        in_specs=in_specs,
        out_specs=out_specs,
        out_shape=out_shape,
        scratch_shapes=[pltpu.VMEM((2, pp, LANES, LANES), F32)],
        compiler_params=_cparams(("arbitrary", "arbitrary", "arbitrary")),
        name="rwkv_scan_ctx" if s0 is None else "rwkv_scan_latent",
    )(*args)


def _readout_kernel(yf_ref, yb_ref, bon_ref, gate_ref, x_ref, mod_ref, lnw_ref, lnb_ref,
                    wo_ref, g1_ref, g2_ref, x_o, h_o):
    d = x_ref.shape[-1]
    y = yf_ref[0] + yb_ref[0]
    mean = _head_sum(y) * (1.0 / HEAD)
    yc = y - mean
    var = _head_sum(yc * yc) * (1.0 / HEAD)
    o = yc * lax.rsqrt(var + GN_EPS) * lnw_ref[...] + lnb_ref[...] + bon_ref[0]
    att = _dot((o * gate_ref[0]).astype(BF16), wo_ref[...])
    gt_a = mod_ref[0, :, 2 * d:3 * d]
    sh_f = mod_ref[0, :, 3 * d:4 * d]
    sc_f = mod_ref[0, :, 4 * d:5 * d]
    x1 = x_ref[0] + gt_a * _rms(att, g1_ref[...])
    x_o[0] = x1
    h_o[0] = (_rms(x1, g2_ref[...]) * (1.0 + sc_f) + sh_f).astype(BF16)


def _readout(yf, yb, bon, gate, x, mod, lnw, lnb, wo, g1, g2, tT):
    b, t, d = x.shape
    row_spec = pl.BlockSpec((1, tT, d), lambda bi, i: (bi, i, 0))
    vec = pl.BlockSpec((1, d), lambda bi, i: (0, 0))
    return pl.pallas_call(
        _readout_kernel,
        grid=(b, t // tT),
        in_specs=[row_spec] * 5 + [pl.BlockSpec((1, 1, 6 * d), lambda bi, i: (bi, 0, 0)),
                                   vec, vec, pl.BlockSpec((d, d), lambda bi, i: (0, 0)), vec, vec],
        out_specs=[row_spec, row_spec],
        out_shape=[jax.ShapeDtypeStruct((b, t, d), F32), jax.ShapeDtypeStruct((b, t, d), BF16)],
        compiler_params=_cparams(("arbitrary", "arbitrary")),
        name="rwkv_readout",
    )(yf, yb, bon, gate, x, mod, lnw, lnb, wo, g1, g2)


def _ffn_kernel(nf, h_ref, wg_ref, wu_ref, wd_ref, x_ref, mod_ref, g3_ref, gn_ref, mod2_ref,
                x_o, h_o, acc):
    f = pl.program_id(1)
    d = x_ref.shape[-1]

    @pl.when(f == 0)
    def _():
        acc[...] = jnp.zeros_like(acc)

    h = h_ref[...]
    act = _silu(_dot(h, wg_ref[...])) * _dot(h, wu_ref[...])
    acc[...] += _dot(act.astype(BF16), wd_ref[...])

    @pl.when(f == nf - 1)
    def _():
        gt_f = mod_ref[0, :, 5 * d:6 * d]
        x2 = x_ref[...] + gt_f * _rms(acc[...], g3_ref[...])
        x_o[...] = x2
        sh = mod2_ref[0, :, 0:d]
        sc = mod2_ref[0, :, d:2 * d]
        h_o[...] = (_rms(x2, gn_ref[...]) * (1.0 + sc) + sh).astype(BF16)


def _ffn(h, w_gu, w_down, x, mod, g3, gn, mod2, rows_per_batch, tm, tf):
    n, d = x.shape
    ff = w_down.shape[0]
    nf = ff // tf
    per = rows_per_batch // tm
    row = pl.BlockSpec((tm, d), lambda i, f: (i, 0))
    vec = pl.BlockSpec((1, d), lambda i, f: (0, 0))
    modspec = pl.BlockSpec((1, 1, 6 * d), lambda i, f: (i // per, 0, 0))
    return pl.pallas_call(
        functools.partial(_ffn_kernel, nf),
        grid=(n // tm, nf),
        in_specs=[row,
                  pl.BlockSpec((d, tf), lambda i, f: (0, f)),
                  pl.BlockSpec((d, tf), lambda i, f: (0, nf + f)),
                  pl.BlockSpec((tf, d), lambda i, f: (f, 0)),
                  row, modspec, vec, vec, modspec],
        out_specs=[row, row],
        out_shape=[jax.ShapeDtypeStruct((n, d), F32), jax.ShapeDtypeStruct((n, d), BF16)],
        scratch_shapes=[pltpu.VMEM((tm, d), F32)],
        compiler_params=_cparams(("arbitrary", "arbitrary")),
        name="dense_swiglu",
    )(h, w_gu, w_gu, w_down, x, mod, g3, gn, mod2)


HALO = 16


def _conv_kernel(tT, nT, h_ref, hp_ref, hn_ref, x_ref, mod_ref, win_ref, cw_ref, wo_ref,
                 g1_ref, g2_ref, wr_ref, x_o, h_o, lg_o):
    i = pl.program_id(1)
    d = x_ref.shape[-1]
    n = tT + 2 * HALO
    h_ext = jnp.concatenate([hp_ref[0], h_ref[0], hn_ref[0]], axis=0)
    proj = _dot(h_ext, win_ref[...])
    z = proj[:, d:2 * d] * proj[:, 2 * d:]
    row = _iota((n, d), 0)
    dead = ((row < HALO) & (i == 0)) | ((row >= HALO + tT) & (i == nT - 1))
    z = jnp.where(dead, 0.0, z)
    conv = (pltpu.roll(z, 1, 0) * cw_ref[0:1, :] + z * cw_ref[1:2, :]
            + pltpu.roll(z, n - 1, 0) * cw_ref[2:3, :])
    gated = (proj[HALO:HALO + tT, 0:d] * conv[HALO:HALO + tT]).astype(BF16)
    y = _dot(gated, wo_ref[...])
    gt_a = mod_ref[0, :, 2 * d:3 * d]
    sh_f = mod_ref[0, :, 3 * d:4 * d]
    sc_f = mod_ref[0, :, 4 * d:5 * d]
    x3 = x_ref[0] + gt_a * _rms(y, g1_ref[...])
    x_o[0] = x3
    h4 = _rms(x3, g2_ref[...]) * (1.0 + sc_f) + sh_f
    h_o[0] = h4
    lg_o[0] = jnp.dot(h4, wr_ref[...], preferred_element_type=F32,
                      precision=lax.Precision.HIGHEST)


def _conv_layer(h, x, mod, w_in, conv_w, w_out, g1, g2, w_router, tT):
    b, t, d = x.shape
    nT = t // tT
    hb = tT // HALO
    row_spec = pl.BlockSpec((1, tT, d), lambda bi, i: (bi, i, 0))
    vec = pl.BlockSpec((1, d), lambda bi, i: (0, 0))
    full = lambda a: pl.BlockSpec(a.shape, lambda bi, i, _n=a.ndim: (0,) * _n)
    return pl.pallas_call(
        functools.partial(_conv_kernel, tT, nT),
        grid=(b, nT),
        in_specs=[row_spec,
                  pl.BlockSpec((1, HALO, d), lambda bi, i: (bi, jnp.maximum(i * hb - 1, 0), 0)),
                  pl.BlockSpec((1, HALO, d), lambda bi, i: (bi, jnp.minimum((i + 1) * hb, t // HALO - 1), 0)),
                  row_spec, pl.BlockSpec((1, 1, 6 * d), lambda bi, i: (bi, 0, 0)),
                  full(w_in), full(conv_w), full(w_out), vec, vec, full(w_router)],
        out_specs=[row_spec, row_spec, pl.BlockSpec((1, tT, LANES), lambda bi, i: (bi, i, 0))],
        out_shape=[jax.ShapeDtypeStruct((b, t, d), F32), jax.ShapeDtypeStruct((b, t, d), F32),
                   jax.ShapeDtypeStruct((b, t, LANES), F32)],
        compiler_params=_cparams(("arbitrary", "arbitrary")),
        name="short_conv",
    )(h, h, h, x, mod, w_in, conv_w, w_out, g1, g2, w_router)


MOE_SUB = 256


def _moe_kernel(nf, tm, e_ref, nv_ref, x_ref, wg_ref, wu_ref, wd_ref, y_o, acc):
    i = pl.program_id(0)
    f = pl.program_id(1)
    nv = nv_ref[i]

    for s in range(tm // MOE_SUB):
        rows = slice(s * MOE_SUB, (s + 1) * MOE_SUB)
        live = nv > s * MOE_SUB

        @pl.when(live)
        def _():
            h = x_ref[rows, :].astype(BF16)
            act = _silu(_dot(h, wg_ref[0])) * _dot(h, wu_ref[0])
            part = _dot(act.astype(BF16), wd_ref[0])

            @pl.when(f == 0)
            def _():
                acc[rows, :] = part

            @pl.when(f > 0)
            def _():
                acc[rows, :] += part

            @pl.when(f == nf - 1)
            def _():
                y_o[rows, :] = acc[rows, :]

        @pl.when(jnp.logical_not(live) & (f == nf - 1))
        def _():
            y_o[rows, :] = jnp.zeros((MOE_SUB, y_o.shape[-1]), F32)


def _moe_experts(xs, blk_e, blk_nv, w_gu, w_down, tm, tf):
    n, d = xs.shape
    ff = w_down.shape[1]
    nf = ff // tf
    nblk = n // tm

    def f_eff(i, f, nv_ref):
        return jnp.where(nv_ref[i] > 0, f, nf - 1)

    grid_spec = pltpu.PrefetchScalarGridSpec(
        num_scalar_prefetch=2,
        grid=(nblk, nf),
        in_specs=[pl.BlockSpec((tm, d), lambda i, f, e, nv: (i, 0)),
                  pl.BlockSpec((1, d, tf), lambda i, f, e, nv: (e[i], 0, f_eff(i, f, nv))),
                  pl.BlockSpec((1, d, tf), lambda i, f, e, nv: (e[i], 0, nf + f_eff(i, f, nv))),
                  pl.BlockSpec((1, tf, d), lambda i, f, e, nv: (e[i], f_eff(i, f, nv), 0))],
        out_specs=pl.BlockSpec((tm, d), lambda i, f, e, nv: (i, 0)),
        scratch_shapes=[pltpu.VMEM((tm, d), F32)],
    )
    return pl.pallas_call(
        functools.partial(_moe_kernel, nf, tm),
        grid_spec=grid_spec,
        out_shape=jax.ShapeDtypeStruct((n, d), F32),
        compiler_params=_cparams(("arbitrary", "arbitrary")),
        name="moe_experts",
    )(blk_e, blk_nv, xs, w_gu, w_gu, w_down)


def _combine_kernel(y0_ref, y1_ref, gt_ref, x_ref, mod_ref, g_ref, o_ref):
    d = x_ref.shape[-1]
    gates = gt_ref[0]
    y = y0_ref[0] * gates[:, 0:1] + y1_ref[0] * gates[:, 1:2]
    gt_f = mod_ref[0, :, 5 * d:6 * d]
    o_ref[0] = x_ref[0] + gt_f * _rms(y, g_ref[...])


def _combine(y0, y1, gates, x, mod, g, tT):
    b, t, d = x.shape
    row_spec = pl.BlockSpec((1, tT, d), lambda bi, i: (bi, i, 0))
    return pl.pallas_call(
        _combine_kernel,
        grid=(b, t // tT),
        in_specs=[row_spec, row_spec, pl.BlockSpec((1, tT, LANES), lambda bi, i: (bi, i, 0)),
                  row_spec, pl.BlockSpec((1, 1, 6 * d), lambda bi, i: (bi, 0, 0)),
                  pl.BlockSpec((1, d), lambda bi, i: (0, 0))],
        out_specs=row_spec,
        out_shape=jax.ShapeDtypeStruct((b, t, d), F32),
        compiler_params=_cparams(("arbitrary", "arbitrary")),
        name="moe_combine",
    )(y0, y1, gates, x, mod, g)


def _route(logits, n_experts, tm):
    n = logits.shape[0]
    nk = n * TOP_K
    top_v, top_e = lax.top_k(logits, TOP_K)
    gates = jax.nn.softmax(top_v, axis=-1)
    flat_e = top_e.reshape(-1).astype(jnp.int32)
    order = jnp.argsort(flat_e, stable=True)
    se = flat_e[order]
    counts = jnp.zeros((n_experts,), jnp.int32).at[flat_e].add(1)
    nblk_e = (counts + tm - 1) // tm
    start = jnp.cumsum(counts) - counts
    bend = jnp.cumsum(nblk_e)
    bstart = bend - nblk_e
    dest = bstart[se] * tm + jnp.arange(nk, dtype=jnp.int32) - start[se]
    nblk = nk // tm + n_experts
    slot_tok = jnp.zeros((nblk * tm,), jnp.int32).at[dest].set(order // TOP_K)
    pos = jnp.zeros((nk,), jnp.int32).at[order].set(dest).reshape(n, TOP_K)
    blk = jnp.arange(nblk, dtype=jnp.int32)
    blk_e = jnp.minimum(jnp.searchsorted(bend, blk, side="right"), n_experts - 1).astype(jnp.int32)
    blk_nv = jnp.clip(counts[blk_e] - (blk - bstart[blk_e]) * tm, 0, tm)
    blk_nv = jnp.where(blk < bend[-1], blk_nv, 0).astype(jnp.int32)
    return gates, slot_tok, pos, blk_e, blk_nv


def kernel(x, c, ctx, c_ctx, mod_w, mod_b, norm_g, rwkv_mu, rwkv_w_rkv, rwkv_w0, rwkv_w1, rwkv_w2,
           rwkv_a0, rwkv_a1, rwkv_a2, rwkv_g1, rwkv_g2, rwkv_k_k, rwkv_k_a, rwkv_r_k, rwkv_ln_w,
           rwkv_ln_b, rwkv_w_out, conv_w_in, conv_w, conv_w_out, ffn_w_gu, ffn_w_down,
           moe_router, moe_w_gu, moe_w_down):
    b, t, d = x.shape
    n_experts = moe_router.shape[-1]
    rows = 16
    cs = jnp.zeros((rows, d), F32).at[:b].set(c).at[b].set(c_ctx)
    mods = _modulation(cs, mod_w, mod_b)
    mod0 = mods[0, :b].reshape(b, 1, 6 * d)
    mod0c = mods[0, b].reshape(1, 1, 6 * d)
    mod1 = mods[1, :b].reshape(b, 1, 6 * d)

    def pad_lora(w):
        zr = jnp.zeros_like(w[0])
        return jnp.stack([jnp.concatenate([w[0], zr], 0), jnp.concatenate([zr, w[1]], 0)]).astype(BF16)

    lg = rwkv_g1.shape[-1]
    lgp = -(-lg // LANES) * LANES
    p = {
        "ng": norm_g[0, 0].reshape(1, d),
        "mu": rwkv_mu[0],
        "wr": rwkv_w_rkv[0, 0].astype(BF16), "wk": rwkv_w_rkv[0, 1].astype(BF16),
        "wv": rwkv_w_rkv[0, 2].astype(BF16),
        "w1": jnp.concatenate([rwkv_w1[0, 0], rwkv_w1[0, 1]], axis=1).astype(BF16),
        "w2": pad_lora(rwkv_w2[0]), "w0": rwkv_w0[0],
        "a1": jnp.concatenate([rwkv_a1[0, 0], rwkv_a1[0, 1]], axis=1).astype(BF16),
        "a2": pad_lora(rwkv_a2[0]), "a0": rwkv_a0[0],
        "k_k": rwkv_k_k[0].reshape(1, d), "k_a": rwkv_k_a[0].reshape(1, d),
        "g1": jnp.pad(rwkv_g1[0], ((0, 0), (0, lgp - lg))).astype(BF16),
        "g2": jnp.pad(rwkv_g2[0], ((0, lgp - lg), (0, 0))).astype(BF16),
        "r_k": rwkv_r_k[0].reshape(1, d),
    }

    tok_c = _tokenwise(ctx, mod0c, "seq", False, p, ctx.shape[1])
    s_ctx = _scan(tok_c, p["k_a"], None, want_y=False, want_state=True)[0]
    tt = min(ROW_TILE, t)
    tm = min(FFN_ROWS, t)
    tf = min(FFN_COLS, ffn_w_down.shape[1])
    tok_l = _tokenwise(x, mod0, "grid", True, p, tt)
    yf, yb = _scan(tok_l[:8], p["k_a"], s_ctx, want_y=True, want_state=False)
    x1, h2 = _readout(yf, yb, tok_l[9], tok_l[8], x, mod0,
                      rwkv_ln_w[0].reshape(1, d), rwkv_ln_b[0].reshape(1, d),
                      rwkv_w_out[0].astype(BF16), norm_g[0, 1].reshape(1, d),
                      norm_g[0, 2].reshape(1, d), tt)

    x2, h3 = _ffn(h2.reshape(b * t, d), ffn_w_gu[0].astype(BF16), ffn_w_down[0].astype(BF16),
                  x1.reshape(b * t, d), mod0, norm_g[0, 3].reshape(1, d),
                  norm_g[1, 0].reshape(1, d), mod1, t, tm, tf)

    w_router = jnp.pad(moe_router[0], ((0, 0), (0, LANES - n_experts)))
    x3, h4, logits = _conv_layer(h3.reshape(b, t, d), x2.reshape(b, t, d), mod1,
                                 conv_w_in[0].astype(BF16), conv_w[0], conv_w_out[0].astype(BF16),
                                 norm_g[1, 1].reshape(1, d), norm_g[1, 2].reshape(1, d), w_router, tt)

    gates, slot_tok, pos, blk_e, blk_nv = _route(logits.reshape(b * t, LANES)[:, :n_experts],
                                                 n_experts, tm)
    xs = jnp.take(h4.reshape(b * t, d), slot_tok, axis=0)
    ys = _moe_experts(xs, blk_e, blk_nv, moe_w_gu[0].astype(BF16), moe_w_down[0].astype(BF16),
                      tm, tf)
    y0 = jnp.take(ys, pos[:, 0], axis=0).reshape(b, t, d)
    y1 = jnp.take(ys, pos[:, 1], axis=0).reshape(b, t, d)
    gates_p = jnp.pad(gates, ((0, 0), (0, LANES - TOP_K))).reshape(b, t, LANES)
    return _combine(y0, y1, gates_p, x3, mod1, norm_g[1, 3].reshape(1, d), tt)
```

```python
import functools
import math

import jax
import jax.numpy as jnp
from jax import lax
from jax.experimental import pallas as pl
from jax.experimental.pallas import tpu as pltpu

F32 = jnp.float32
BF16 = jnp.bfloat16

HEAD = 64
LANES = 128
GRID_W = 64
CHUNK = 64
SUB = 16
SCAN_PAIRS = 4
NORM_EPS = 1e-6
GN_EPS = 64e-5
DECAY_SCALE = math.exp(-0.5)
TOP_K = 2
VMEM_LIMIT = 56 * 1024 * 1024


ROW_TILE = 256
FFN_ROWS = 1024
FFN_COLS = 512
MOE_ROWS = 2048
MOE_COLS = 256


def _cparams(sem):
    return pltpu.CompilerParams(dimension_semantics=sem, vmem_limit_bytes=VMEM_LIMIT)


def _dot(a, b):
    return jnp.dot(a, b, preferred_element_type=F32)


def _dot_nt(a, b):
    return lax.dot_general(a, b, (((1,), (1,)), ((), ())), preferred_element_type=F32)


def _bdot(a, b):
    return _dot(a.astype(BF16), b.astype(BF16))


def _iota(shape, axis):
    return lax.broadcasted_iota(jnp.int32, shape, axis)


def _rms(x, g):
    return x * lax.rsqrt(jnp.mean(x * x, axis=-1, keepdims=True) + NORM_EPS) * g


def _silu(x):
    return x * jax.nn.sigmoid(x)


def _split_bf16(x):
    hi = x.astype(BF16)
    lo = (x - hi.astype(F32)).astype(BF16)
    return hi, lo


def _head_sum(x):
    ones = (_iota((LANES, LANES), 0) // HEAD == _iota((LANES, LANES), 1) // HEAD).astype(BF16)
    hi, lo = _split_bf16(x)
    outs = []
    for p in range(x.shape[-1] // LANES):
        sl = slice(p * LANES, (p + 1) * LANES)
        outs.append(_dot(hi[:, sl], ones) + _dot(lo[:, sl], ones))
    return jnp.concatenate(outs, axis=1)


def _mod_kernel(c_ref, w_ref, b_ref, o_ref):
    s = _silu(c_ref[...])
    o_ref[0] = jnp.dot(s, w_ref[0], preferred_element_type=F32,
                       precision=lax.Precision.HIGHEST) + b_ref[0]


def _modulation(cs, mod_w, mod_b):
    depth, d, n = mod_w.shape
    rows = cs.shape[0]
    tn = 1536
    return pl.pallas_call(
        _mod_kernel,
        grid=(depth, n // tn),
        in_specs=[pl.BlockSpec((rows, d), lambda i, j: (0, 0)),
                  pl.BlockSpec((1, d, tn), lambda i, j: (i, 0, j)),
                  pl.BlockSpec((1, 1, tn), lambda i, j: (i, 0, j))],
        out_specs=pl.BlockSpec((1, rows, tn), lambda i, j: (i, 0, j)),
        out_shape=jax.ShapeDtypeStruct((depth, rows, n), F32),
        compiler_params=_cparams(("arbitrary", "arbitrary")),
        name="modulation",
    )(cs, mod_w, mod_b.reshape(depth, 1, n))


def _tok_kernel(mode, readout, tT, nT, *refs):
    refs = list(refs)
    x_ref = refs.pop(0)
    if mode == "grid":
        xp_ref = refs.pop(0)
        xn_ref = refs.pop(0)
    (mod_ref, ng_ref, mu_ref, wr_ref, wk_ref, wv_ref, w1_ref, w2_ref, w0_ref,
     a1_ref, a2_ref, a0_ref, kkp_ref, ka_ref) = refs[:14]
    refs = refs[14:]
    if readout:
        g1_ref, g2_ref, rk_ref = refs[:3]
        refs = refs[3:]
    r_o, k_o, v_o, kk_o, ic0_o, ic1_o, lw0_o, lw1_o = refs[:8]
    refs = refs[8:]

    d = x_ref.shape[-1]
    sh = mod_ref[0, :, 0:d]
    sc = mod_ref[0, :, d:2 * d]
    g = ng_ref[...]

    def norm_mod(xx):
        return _rms(xx, g) * (1.0 + sc) + sh

    h = norm_mod(x_ref[0])
    if mode == "grid":
        i = pl.program_id(1)
        q = d // 4
        hp = jnp.where(i > 0, norm_mod(xp_ref[0]), 0.0)
        hn = jnp.where(i < nT - 1, norm_mod(xn_ref[0]), 0.0)
        col = _iota((tT, q), 0) % GRID_W
        left = jnp.where(col == 0, 0.0, pltpu.roll(h[:, 0:q], 1, 0))
        right = jnp.where(col == GRID_W - 1, 0.0, pltpu.roll(h[:, q:2 * q], tT - 1, 0))
        up = jnp.concatenate([hp[:, 2 * q:3 * q], h[:tT - GRID_W, 2 * q:3 * q]], axis=0)
        down = jnp.concatenate([h[GRID_W:, 3 * q:], hn[:, 3 * q:]], axis=0)
        hs = jnp.concatenate([left, right, up, down], axis=1)
    else:
        half = d // 2
        row = _iota((tT, half), 0)
        prev = jnp.where(row == 0, 0.0, pltpu.roll(h[:, :half], 1, 0))
        nxt = jnp.where(row == tT - 1, 0.0, pltpu.roll(h[:, half:], tT - 1, 0))
        hs = jnp.concatenate([prev, nxt], axis=1)

    dx = hs - h

    def mix(n):
        return (h + dx * mu_ref[n:n + 1, :]).astype(BF16)

    r = _dot(mix(0), wr_ref[...])
    k = _dot(mix(2), wk_ref[...])
    v = _dot(mix(3), wv_ref[...])
    w1o = jnp.tanh(_dot(mix(1), w1_ref[...])).astype(BF16)
    a1o = _dot(mix(4), a1_ref[...]).astype(BF16)
    ics = []
    for z, (lw_o, ic_o) in enumerate(((lw0_o, ic0_o), (lw1_o, ic1_o))):
        w_pre = _dot(w1o, w2_ref[z]) + w0_ref[z:z + 1, :]
        lw_o[0] = -DECAY_SCALE * jax.nn.sigmoid(w_pre)
        ic = jax.nn.sigmoid(_dot(a1o, a2_ref[z]) + a0_ref[z:z + 1, :])
        ic_o[0] = ic
        ics.append(ic)
    kk = k * kkp_ref[...]
    kk = kk * lax.rsqrt(jnp.maximum(_head_sum(kk * kk), 1e-24))
    r_o[0] = r
    k_o[0] = k
    v_o[0] = v
    kk_o[0] = kk
    if readout:
        gate_o, bon_o = refs
        gate_o[0] = _dot(jax.nn.sigmoid(_dot(mix(5), g1_ref[...])).astype(BF16), g2_ref[...])
        k_avg = k * (1.0 + (0.5 * (ics[0] + ics[1]) - 1.0) * ka_ref[...])
        bon_o[0] = _head_sum(r * k_avg * rk_ref[...]) * v


def _tokenwise(x, mod, mode, readout, p, tT):
    b, t, d = x.shape
    nT = t // tT
    row_spec = pl.BlockSpec((1, tT, d), lambda bi, i: (bi, i, 0))
    in_specs = [row_spec]
    args = [x]
    if mode == "grid":
        hb = tT // GRID_W
        in_specs += [
            pl.BlockSpec((1, GRID_W, d), lambda bi, i: (bi, jnp.maximum(i * hb - 1, 0), 0)),
            pl.BlockSpec((1, GRID_W, d), lambda bi, i: (bi, jnp.minimum((i + 1) * hb, t // GRID_W - 1), 0)),
        ]
        args += [x, x]
    if mod.shape[0] == 1:
        in_specs.append(pl.BlockSpec((1, 1, mod.shape[-1]), lambda bi, i: (0, 0, 0)))
    else:
        in_specs.append(pl.BlockSpec((1, 1, mod.shape[-1]), lambda bi, i: (bi, 0, 0)))
    args.append(mod)
    consts = [p["ng"], p["mu"], p["wr"], p["wk"], p["wv"], p["w1"], p["w2"], p["w0"],
              p["a1"], p["a2"], p["a0"], p["k_k"], p["k_a"]]
    if readout:
        consts += [p["g1"], p["g2"], p["r_k"]]
    for a in consts:
        in_specs.append(pl.BlockSpec(a.shape, lambda bi, i, _n=a.ndim: (0,) * _n))
        args.append(a)
    n_out = 10 if readout else 8
    return pl.pallas_call(
        functools.partial(_tok_kernel, mode, readout, tT, nT),
        grid=(b, nT),
        in_specs=in_specs,
        out_specs=[row_spec] * n_out,
        out_shape=[jax.ShapeDtypeStruct((b, t, d), F32)] * n_out,
        compiler_params=_cparams(("arbitrary", "arbitrary")),
        name="rwkv_tokenwise_" + mode,
    )(*args)


def _scan_prep(r, kd, v, lw, kk, ic, reverse):
    c = r.shape[0]
    ti = _iota((c, c), 0)
    si = _iota((c, c), 1)
    tri = ((si >= ti) if reverse else (si <= ti)).astype(BF16)
    lw_hi, lw_lo = _split_bf16(lw)
    cum = _dot(tri, lw_hi) + _dot(tri, lw_lo)
    pc_log = cum[0:1] if reverse else cum[c - 1:c]
    p_inv = jnp.exp(-cum)
    p_rest = jnp.exp(pc_log - cum)
    b = kk * ic
    return {"rt": r * jnp.exp(cum), "at": -(kk * jnp.exp(cum - lw)),
            "bt": b * p_inv, "kt": kd * p_inv, "bp": b * p_rest, "kp": kd * p_rest,
            "v": v, "pc": jnp.exp(pc_log)}


def _scan_step(prep, states):
    c = prep[0]["v"].shape[0]
    n = 2 * c
    pp = len(states[0])
    chains = [(dr, p) for dr in range(2) for p in range(pp)]
    first = _iota((c, LANES), 1) < HEAD
    row = _iota((n, n), 0)
    col = _iota((n, n), 1)
    strict = (col < row, col > row)
    incl = (col <= row, col >= row)
    same = (row // SUB) == (col // SUB)

    def ms(name, dr, p):
        x = prep[dr][name][:, p * LANES:(p + 1) * LANES]
        return jnp.concatenate([jnp.where(first, x, 0.0), jnp.where(first, 0.0, x)], axis=0)

    def each(fn, *lists):
        return [fn(*xs) for xs in zip(*lists)]

    lhs = [jnp.concatenate([ms("at", dr, p), ms("rt", dr, p)], axis=0).astype(BF16) for dr, p in chains]
    rhs = [jnp.concatenate([ms("bt", dr, p), ms("kt", dr, p)], axis=0).astype(BF16) for dr, p in chains]
    v_ms = [ms("v", dr, p).astype(BF16) for dr, p in chains]
    gmat = each(_dot_nt, lhs, rhs)
    nmat = [jnp.where(strict[dr], g[:n, :n], 0.0) for (dr, _), g in zip(chains, gmat)]
    a_ak = [jnp.where(strict[dr], g[:n, n:], 0.0).astype(BF16) for (dr, _), g in zip(chains, gmat)]
    a_rb = [jnp.where(incl[dr], g[n:, :n], 0.0).astype(BF16) for (dr, _), g in zip(chains, gmat)]
    a_rk = [jnp.where(incl[dr], g[n:, n:], 0.0).astype(BF16) for (dr, _), g in zip(chains, gmat)]

    nd = [jnp.where(same, x, 0.0) for x in nmat]
    no = [jnp.where(same, 0.0, x) for x in nmat]
    pw = nd
    tp = nd
    span = 2
    while span < SUB:
        pw = each(_bdot, pw, pw)
        tp = each(lambda t, q: t + q + _bdot(t, q), tp, pw)
        span *= 2
    m = each(lambda t, o: o + _bdot(t, o), tp, no)
    qp = m
    pw = m
    span = 2
    while span < c // SUB:
        pw = each(_bdot, pw, pw)
        qp = each(lambda t, q: t + q + _bdot(t, q), qp, pw)
        span *= 2
    tp = each(lambda t, q: t + q + _bdot(q, t), tp, qp)

    s_b = [states[dr][p].astype(BF16) for dr, p in chains]
    rhs_z = each(lambda l, s, a, v: _dot_nt(l[:n], s) + _dot(a, v), lhs, s_b, a_ak, v_ms)
    z = each(lambda t, x: x + _bdot(t, x), tp, rhs_z)
    y_ms = each(lambda l, s, ab, zz, ak, v: _dot_nt(l[n:], s) + _bdot(ab, zz) + _dot(ak, v),
                lhs, s_b, a_rb, z, a_rk, v_ms)
    upd = [_bdot(jnp.concatenate([zz, v.astype(F32)], axis=0).T,
                 jnp.concatenate([ms("bp", dr, p), ms("kp", dr, p)], axis=0))
           for (dr, p), zz, v in zip(chains, z, v_ms)]
    y = [jnp.concatenate([ym[:c] + ym[c:] for (d2, _), ym in zip(chains, y_ms) if d2 == dr], axis=1)
         for dr in range(2)]
    new_states = [[None] * pp for _ in range(2)]
    for (dr, p), u in zip(chains, upd):
        new_states[dr][p] = states[dr][p] * prep[dr]["pc"][:, p * LANES:(p + 1) * LANES] + u
    return y, new_states


def _scan_kernel(pp, nc, has_s0, want_y, want_state, *refs):
    refs = list(refs)
    ka_ref = refs[12]
    dirs = (refs[:6], refs[6:12])
    refs = refs[13:]
    if has_s0:
        s0_ref = refs.pop(0)
    if want_y:
        y_refs = (refs.pop(0), refs.pop(0))
    if want_state:
        st_ref = refs.pop(0)
    s_scr = refs.pop(0)
    c = pl.program_id(2)

    @pl.when(c == 0)
    def _():
        if has_s0:
            s_scr[...] = s0_ref[0]
        else:
            s_scr[...] = jnp.zeros_like(s_scr)

    prep = []
    for dr, (r_ref, k_ref, v_ref, kk_ref, ic_ref, lw_ref) in enumerate(dirs):
        ic = ic_ref[0]
        kd = k_ref[0] * (1.0 + (ic - 1.0) * ka_ref[...])
        prep.append(_scan_prep(r_ref[0], kd, v_ref[0], lw_ref[0], kk_ref[0], ic, reverse=(dr == 1)))
    states = [[s_scr[dr, p] for p in range(pp)] for dr in range(2)]
    y, new_states = _scan_step(prep, states)
    for dr in range(2):
        for p in range(pp):
            s_scr[dr, p] = new_states[dr][p]
        if want_y:
            y_refs[dr][0] = y[dr]

    if want_state:
        @pl.when(c == nc - 1)
        def _():
            st_ref[0] = s_scr[...]


def _scan(tok, ka, s0, want_y, want_state):
    r, k, v, kk, ic0, ic1, lw0, lw1 = tok
    b, t, d = r.shape
    nc = t // CHUNK
    npair = d // LANES
    pp = min(SCAN_PAIRS, npair)
    w = LANES * pp
    f_spec = pl.BlockSpec((1, CHUNK, w), lambda bi, hi, ci: (bi, ci, hi))
    b_spec = pl.BlockSpec((1, CHUNK, w), lambda bi, hi, ci: (bi, nc - 1 - ci, hi))
    st_spec = pl.BlockSpec((1, 2, pp, LANES, LANES), lambda bi, hi, ci: (bi, 0, hi, 0, 0))
    in_specs = [f_spec] * 6 + [b_spec] * 6 + [pl.BlockSpec((1, w), lambda bi, hi, ci: (0, hi))]
    args = [r, k, v, kk, ic0, lw0, r, k, v, kk, ic1, lw1, ka]
    if s0 is not None:
        in_specs.append(st_spec)
        args.append(s0)
    out_specs, out_shape = [], []
    if want_y:
        out_specs += [f_spec, b_spec]
        out_shape += [jax.ShapeDtypeStruct((b, t, d), F32)] * 2
    if want_state:
        out_specs.append(st_spec)
        out_shape.append(jax.ShapeDtypeStruct((b, 2, npair, LANES, LANES), F32))
    return pl.pallas_call(
        functools.partial(_scan_kernel, pp, nc, s0 is not None, want_y, want_state),
        grid=(b, npair // pp, nc),
        in_specs=in_specs,
        out_specs=out_specs,
        out_shape=out_shape,
        scratch_shapes=[pltpu.VMEM((2, pp, LANES, LANES), F32)],
        compiler_params=_cparams(("arbitrary", "arbitrary", "arbitrary")),
        name="rwkv_scan_ctx" if s0 is None else "rwkv_scan_latent",
    )(*args)


def _readout_kernel(yf_ref, yb_ref, bon_ref, gate_ref, x_ref, mod_ref, lnw_ref, lnb_ref,
                    wo_ref, g1_ref, g2_ref, x_o, h_o):
    d = x_ref.shape[-1]
    y = yf_ref[0] + yb_ref[0]
    mean = _head_sum(y) * (1.0 / HEAD)
    yc = y - mean
    var = _head_sum(yc * yc) * (1.0 / HEAD)
    o = yc * lax.rsqrt(var + GN_EPS) * lnw_ref[...] + lnb_ref[...] + bon_ref[0]
    att = _dot((o * gate_ref[0]).astype(BF16), wo_ref[...])
    gt_a = mod_ref[0, :, 2 * d:3 * d]
    sh_f = mod_ref[0, :, 3 * d:4 * d]
    sc_f = mod_ref[0, :, 4 * d:5 * d]
    x1 = x_ref[0] + gt_a * _rms(att, g1_ref[...])
    x_o[0] = x1
    h_o[0] = (_rms(x1, g2_ref[...]) * (1.0 + sc_f) + sh_f).astype(BF16)


def _readout(yf, yb, bon, gate, x, mod, lnw, lnb, wo, g1, g2, tT):
    b, t, d = x.shape
    row_spec = pl.BlockSpec((1, tT, d), lambda bi, i: (bi, i, 0))
    vec = pl.BlockSpec((1, d), lambda bi, i: (0, 0))
    return pl.pallas_call(
        _readout_kernel,
        grid=(b, t // tT),
        in_specs=[row_spec] * 5 + [pl.BlockSpec((1, 1, 6 * d), lambda bi, i: (bi, 0, 0)),
                                   vec, vec, pl.BlockSpec((d, d), lambda bi, i: (0, 0)), vec, vec],
        out_specs=[row_spec, row_spec],
        out_shape=[jax.ShapeDtypeStruct((b, t, d), F32), jax.ShapeDtypeStruct((b, t, d), BF16)],
        compiler_params=_cparams(("arbitrary", "arbitrary")),
        name="rwkv_readout",
    )(yf, yb, bon, gate, x, mod, lnw, lnb, wo, g1, g2)


def _ffn_kernel(nf, h_ref, wg_ref, wu_ref, wd_ref, x_ref, mod_ref, g3_ref, gn_ref, mod2_ref,
                x_o, h_o, acc):
    f = pl.program_id(1)
    d = x_ref.shape[-1]

    @pl.when(f == 0)
    def _():
        acc[...] = jnp.zeros_like(acc)

    h = h_ref[...]
    act = _silu(_dot(h, wg_ref[...])) * _dot(h, wu_ref[...])
    acc[...] += _dot(act.astype(BF16), wd_ref[...])

    @pl.when(f == nf - 1)
    def _():
        gt_f = mod_ref[0, :, 5 * d:6 * d]
        x2 = x_ref[...] + gt_f * _rms(acc[...], g3_ref[...])
        x_o[...] = x2
        sh = mod2_ref[0, :, 0:d]
        sc = mod2_ref[0, :, d:2 * d]
        h_o[...] = (_rms(x2, gn_ref[...]) * (1.0 + sc) + sh).astype(BF16)


def _ffn(h, w_gu, w_down, x, mod, g3, gn, mod2, rows_per_batch, tm, tf):
    n, d = x.shape
    ff = w_down.shape[0]
    nf = ff // tf
    per = rows_per_batch // tm
    row = pl.BlockSpec((tm, d), lambda i, f: (i, 0))
    vec = pl.BlockSpec((1, d), lambda i, f: (0, 0))
    modspec = pl.BlockSpec((1, 1, 6 * d), lambda i, f: (i // per, 0, 0))
    return pl.pallas_call(
        functools.partial(_ffn_kernel, nf),
        grid=(n // tm, nf),
        in_specs=[row,
                  pl.BlockSpec((d, tf), lambda i, f: (0, f)),
                  pl.BlockSpec((d, tf), lambda i, f: (0, nf + f)),
                  pl.BlockSpec((tf, d), lambda i, f: (f, 0)),
                  row, modspec, vec, vec, modspec],
        out_specs=[row, row],
        out_shape=[jax.ShapeDtypeStruct((n, d), F32), jax.ShapeDtypeStruct((n, d), BF16)],
        scratch_shapes=[pltpu.VMEM((tm, d), F32)],
        compiler_params=_cparams(("arbitrary", "arbitrary")),
        name="dense_swiglu",
    )(h, w_gu, w_gu, w_down, x, mod, g3, gn, mod2)


HALO = 16


def _conv_kernel(tT, nT, h_ref, hp_ref, hn_ref, x_ref, mod_ref, win_ref, cw_ref, wo_ref,
                 g1_ref, g2_ref, wr_ref, x_o, h_o, lg_o):
    i = pl.program_id(1)
    d = x_ref.shape[-1]
    n = tT + 2 * HALO
    h_ext = jnp.concatenate([hp_ref[0], h_ref[0], hn_ref[0]], axis=0)
    proj = _dot(h_ext, win_ref[...])
    z = proj[:, d:2 * d] * proj[:, 2 * d:]
    row = _iota((n, d), 0)
    dead = ((row < HALO) & (i == 0)) | ((row >= HALO + tT) & (i == nT - 1))
    z = jnp.where(dead, 0.0, z)
    conv = (pltpu.roll(z, 1, 0) * cw_ref[0:1, :] + z * cw_ref[1:2, :]
            + pltpu.roll(z, n - 1, 0) * cw_ref[2:3, :])
    gated = (proj[HALO:HALO + tT, 0:d] * conv[HALO:HALO + tT]).astype(BF16)
    y = _dot(gated, wo_ref[...])
    gt_a = mod_ref[0, :, 2 * d:3 * d]
    sh_f = mod_ref[0, :, 3 * d:4 * d]
    sc_f = mod_ref[0, :, 4 * d:5 * d]
    x3 = x_ref[0] + gt_a * _rms(y, g1_ref[...])
    x_o[0] = x3
    h4 = _rms(x3, g2_ref[...]) * (1.0 + sc_f) + sh_f
    h_o[0] = h4
    lg_o[0] = jnp.dot(h4, wr_ref[...], preferred_element_type=F32,
                      precision=lax.Precision.HIGHEST)


def _conv_layer(h, x, mod, w_in, conv_w, w_out, g1, g2, w_router, tT):
    b, t, d = x.shape
    nT = t // tT
    hb = tT // HALO
    row_spec = pl.BlockSpec((1, tT, d), lambda bi, i: (bi, i, 0))
    vec = pl.BlockSpec((1, d), lambda bi, i: (0, 0))
    full = lambda a: pl.BlockSpec(a.shape, lambda bi, i, _n=a.ndim: (0,) * _n)
    return pl.pallas_call(
        functools.partial(_conv_kernel, tT, nT),
        grid=(b, nT),
        in_specs=[row_spec,
                  pl.BlockSpec((1, HALO, d), lambda bi, i: (bi, jnp.maximum(i * hb - 1, 0), 0)),
                  pl.BlockSpec((1, HALO, d), lambda bi, i: (bi, jnp.minimum((i + 1) * hb, t // HALO - 1), 0)),
                  row_spec, pl.BlockSpec((1, 1, 6 * d), lambda bi, i: (bi, 0, 0)),
                  full(w_in), full(conv_w), full(w_out), vec, vec, full(w_router)],
        out_specs=[row_spec, row_spec, pl.BlockSpec((1, tT, LANES), lambda bi, i: (bi, i, 0))],
        out_shape=[jax.ShapeDtypeStruct((b, t, d), F32), jax.ShapeDtypeStruct((b, t, d), F32),
                   jax.ShapeDtypeStruct((b, t, LANES), F32)],
        compiler_params=_cparams(("arbitrary", "arbitrary")),
        name="short_conv",
    )(h, h, h, x, mod, w_in, conv_w, w_out, g1, g2, w_router)


MOE_SUB = 256


def _moe_kernel(nf, tm, e_ref, nv_ref, x_ref, wg_ref, wu_ref, wd_ref, y_o):
    i = pl.program_id(0)
    f = pl.program_id(1)
    nv = nv_ref[i]

    @pl.when(nv > 0)
    def _():
        wg = wg_ref[0].astype(BF16)
        wu = wu_ref[0].astype(BF16)
        wd = wd_ref[0].astype(BF16)
        for s in range(tm // MOE_SUB):
            rows = slice(s * MOE_SUB, (s + 1) * MOE_SUB)

            @pl.when(nv > s * MOE_SUB)
            def _():
                h = x_ref[rows, :].astype(BF16)
                act = _silu(_dot(h, wg)) * _dot(h, wu)
                part = _dot(act.astype(BF16), wd)

                @pl.when(f == 0)
                def _():
                    y_o[rows, :] = part

                @pl.when(f > 0)
                def _():
                    y_o[rows, :] += part

    @pl.when(f == nf - 1)
    def _():
        for s in range(tm // MOE_SUB):
            @pl.when(nv <= s * MOE_SUB)
            def _():
                y_o[s * MOE_SUB:(s + 1) * MOE_SUB, :] = jnp.zeros((MOE_SUB, y_o.shape[-1]), F32)


def _moe_experts(xs, blk_e, blk_nv, w_gu, w_down, tm, tf):
    n, d = xs.shape
    ff = w_down.shape[1]
    nf = ff // tf
    nblk = n // tm

    def f_eff(i, f, nv_ref):
        return jnp.where(nv_ref[i] > 0, f, nf - 1)

    grid_spec = pltpu.PrefetchScalarGridSpec(
        num_scalar_prefetch=2,
        grid=(nblk, nf),
        in_specs=[pl.BlockSpec((tm, d), lambda i, f, e, nv: (i, 0)),
                  pl.BlockSpec((1, d, tf), lambda i, f, e, nv: (e[i], 0, f_eff(i, f, nv))),
                  pl.BlockSpec((1, d, tf), lambda i, f, e, nv: (e[i], 0, nf + f_eff(i, f, nv))),
                  pl.BlockSpec((1, tf, d), lambda i, f, e, nv: (e[i], f_eff(i, f, nv), 0))],
        out_specs=pl.BlockSpec((tm, d), lambda i, f, e, nv: (i, 0)),
    )
    return pl.pallas_call(
        functools.partial(_moe_kernel, nf, tm),
        grid_spec=grid_spec,
        out_shape=jax.ShapeDtypeStruct((n, d), F32),
        compiler_params=_cparams(("arbitrary", "arbitrary")),
        name="moe_experts",
    )(blk_e, blk_nv, xs, w_gu, w_gu, w_down)


def _combine_kernel(y0_ref, y1_ref, gt_ref, x_ref, mod_ref, g_ref, o_ref):
    d = x_ref.shape[-1]
    gates = gt_ref[0]
    y = y0_ref[0] * gates[:, 0:1] + y1_ref[0] * gates[:, 1:2]
    gt_f = mod_ref[0, :, 5 * d:6 * d]
    o_ref[0] = x_ref[0] + gt_f * _rms(y, g_ref[...])


def _combine(y0, y1, gates, x, mod, g, tT):
    b, t, d = x.shape
    row_spec = pl.BlockSpec((1, tT, d), lambda bi, i: (bi, i, 0))
    return pl.pallas_call(
        _combine_kernel,
        grid=(b, t // tT),
        in_specs=[row_spec, row_spec, pl.BlockSpec((1, tT, LANES), lambda bi, i: (bi, i, 0)),
                  row_spec, pl.BlockSpec((1, 1, 6 * d), lambda bi, i: (bi, 0, 0)),
                  pl.BlockSpec((1, d), lambda bi, i: (0, 0))],
        out_specs=row_spec,
        out_shape=jax.ShapeDtypeStruct((b, t, d), F32),
        compiler_params=_cparams(("arbitrary", "arbitrary")),
        name="moe_combine",
    )(y0, y1, gates, x, mod, g)


def _route(logits, n_experts, tm):
    n = logits.shape[0]
    nk = n * TOP_K
    top_v, top_e = lax.top_k(logits, TOP_K)
    gates = jax.nn.softmax(top_v, axis=-1)
    flat_e = top_e.reshape(-1).astype(jnp.int32)
    onehot = (flat_e[None, :] == jnp.arange(n_experts, dtype=jnp.int32)[:, None]).astype(jnp.int32)
    csum = jnp.cumsum(onehot, axis=1)
    counts = csum[:, -1]
    rank = jnp.sum((csum - onehot) * onehot, axis=0)
    nblk_e = (counts + tm - 1) // tm
    start = jnp.cumsum(counts) - counts
    bend = jnp.cumsum(nblk_e)
    bstart = bend - nblk_e
    pos = (jnp.sum(onehot * (bstart * tm)[:, None], axis=0) + rank).reshape(n, TOP_K)
    nblk = nk // tm + n_experts
    blk = jnp.arange(nblk, dtype=jnp.int32)
    blk_e = jnp.minimum(jnp.searchsorted(bend, blk, side="right"), n_experts - 1).astype(jnp.int32)
    blk_nv = jnp.clip(counts[blk_e] - (blk - bstart[blk_e]) * tm, 0, tm)
    blk_nv = jnp.where(blk < bend[-1], blk_nv, 0).astype(jnp.int32)
    order = jnp.argsort(flat_e, stable=True).astype(jnp.int32)
    in_blk = jnp.arange(tm, dtype=jnp.int32)[None, :]
    src = (start[blk_e] + (blk - bstart[blk_e]) * tm)[:, None] + in_blk
    src = jnp.where(in_blk < blk_nv[:, None], src, 0).reshape(-1)
    slot_tok = order.at[src].get(mode="promise_in_bounds") // TOP_K
    return gates, slot_tok, pos, blk_e, blk_nv


def kernel(x, c, ctx, c_ctx, mod_w, mod_b, norm_g, rwkv_mu, rwkv_w_rkv, rwkv_w0, rwkv_w1, rwkv_w2,
           rwkv_a0, rwkv_a1, rwkv_a2, rwkv_g1, rwkv_g2, rwkv_k_k, rwkv_k_a, rwkv_r_k, rwkv_ln_w,
           rwkv_ln_b, rwkv_w_out, conv_w_in, conv_w, conv_w_out, ffn_w_gu, ffn_w_down,
           moe_router, moe_w_gu, moe_w_down):
    b, t, d = x.shape
    n_experts = moe_router.shape[-1]
    rows = 16
    cs = jnp.zeros((rows, d), F32).at[:b].set(c).at[b].set(c_ctx)
    mods = _modulation(cs, mod_w, mod_b)
    mod0 = mods[0, :b].reshape(b, 1, 6 * d)
    mod0c = mods[0, b].reshape(1, 1, 6 * d)
    mod1 = mods[1, :b].reshape(b, 1, 6 * d)

    def pad_lora(w):
        zr = jnp.zeros_like(w[0])
        return jnp.stack([jnp.concatenate([w[0], zr], 0), jnp.concatenate([zr, w[1]], 0)]).astype(BF16)

    lg = rwkv_g1.shape[-1]
    lgp = -(-lg // LANES) * LANES
    p = {
        "ng": norm_g[0, 0].reshape(1, d),
        "mu": rwkv_mu[0],
        "wr": rwkv_w_rkv[0, 0].astype(BF16), "wk": rwkv_w_rkv[0, 1].astype(BF16),
        "wv": rwkv_w_rkv[0, 2].astype(BF16),
        "w1": jnp.concatenate([rwkv_w1[0, 0], rwkv_w1[0, 1]], axis=1).astype(BF16),
        "w2": pad_lora(rwkv_w2[0]), "w0": rwkv_w0[0],
        "a1": jnp.concatenate([rwkv_a1[0, 0], rwkv_a1[0, 1]], axis=1).astype(BF16),
        "a2": pad_lora(rwkv_a2[0]), "a0": rwkv_a0[0],
        "k_k": rwkv_k_k[0].reshape(1, d), "k_a": rwkv_k_a[0].reshape(1, d),
        "g1": jnp.pad(rwkv_g1[0], ((0, 0), (0, lgp - lg))).astype(BF16),
        "g2": jnp.pad(rwkv_g2[0], ((0, lgp - lg), (0, 0))).astype(BF16),
        "r_k": rwkv_r_k[0].reshape(1, d),
    }

    tok_c = _tokenwise(ctx, mod0c, "seq", False, p, ctx.shape[1])
    s_ctx = _scan(tok_c, p["k_a"], None, want_y=False, want_state=True)[0]
    tt = min(ROW_TILE, t)
    tm = min(FFN_ROWS, t)
    tf = min(FFN_COLS, ffn_w_down.shape[1])
    tok_l = _tokenwise(x, mod0, "grid", True, p, tt)
    yf, yb = _scan(tok_l[:8], p["k_a"], s_ctx, want_y=True, want_state=False)
    x1, h2 = _readout(yf, yb, tok_l[9], tok_l[8], x, mod0,
                      rwkv_ln_w[0].reshape(1, d), rwkv_ln_b[0].reshape(1, d),
                      rwkv_w_out[0].astype(BF16), norm_g[0, 1].reshape(1, d),
                      norm_g[0, 2].reshape(1, d), tt)

    x2, h3 = _ffn(h2.reshape(b * t, d), ffn_w_gu[0].astype(BF16), ffn_w_down[0].astype(BF16),
                  x1.reshape(b * t, d), mod0, norm_g[0, 3].reshape(1, d),
                  norm_g[1, 0].reshape(1, d), mod1, t, tm, tf)

    w_router = jnp.pad(moe_router[0], ((0, 0), (0, LANES - n_experts)))
    x3, h4, logits = _conv_layer(h3.reshape(b, t, d), x2.reshape(b, t, d), mod1,
                                 conv_w_in[0].astype(BF16), conv_w[0], conv_w_out[0].astype(BF16),
                                 norm_g[1, 1].reshape(1, d), norm_g[1, 2].reshape(1, d), w_router, tt)

    tme = min(MOE_ROWS, b * t * TOP_K)
    gates, slot_tok, pos, blk_e, blk_nv = _route(logits.reshape(b * t, LANES)[:, :n_experts],
                                                 n_experts, tme)
    xs = h4.reshape(b * t, d).at[slot_tok].get(mode="promise_in_bounds")
    ys = _moe_experts(xs, blk_e, blk_nv, moe_w_gu[0], moe_w_down[0], tme,
                      min(MOE_COLS, moe_w_down.shape[2]))
    y0 = ys.at[pos[:, 0]].get(mode="promise_in_bounds").reshape(b, t, d)
    y1 = ys.at[pos[:, 1]].get(mode="promise_in_bounds").reshape(b, t, d)
    gates_p = jnp.pad(gates, ((0, 0), (0, LANES - TOP_K))).reshape(b, t, LANES)
    return _combine(y0, y1, gates_p, x3, mod1, norm_g[1, 3].reshape(1, d), tt)
```

```python
import functools
import math

import jax
import jax.numpy as jnp
from jax import lax
from jax.experimental import pallas as pl
from jax.experimental.pallas import tpu as pltpu

F32 = jnp.float32
BF16 = jnp.bfloat16

HEAD = 64
LANES = 128
GRID_W = 64
CHUNK = 64
SUB = 16
SCAN_PAIRS = 4
NORM_EPS = 1e-6
GN_EPS = 64e-5
DECAY_SCALE = math.exp(-0.5)
TOP_K = 2
VMEM_LIMIT = 56 * 1024 * 1024


ROW_TILE = 256
FFN_ROWS = 1024
FFN_COLS = 512
MOE_ROWS = 2048
MOE_COLS = 512


def _cparams(sem):
    return pltpu.CompilerParams(dimension_semantics=sem, vmem_limit_bytes=VMEM_LIMIT)


def _dot(a, b):
    return jnp.dot(a, b, preferred_element_type=F32)


def _dot_nt(a, b):
    return lax.dot_general(a, b, (((1,), (1,)), ((), ())), preferred_element_type=F32)


def _bdot(a, b):
    return _dot(a.astype(BF16), b.astype(BF16))


def _iota(shape, axis):
    return lax.broadcasted_iota(jnp.int32, shape, axis)


def _rms(x, g):
    return x * lax.rsqrt(jnp.mean(x * x, axis=-1, keepdims=True) + NORM_EPS) * g


def _silu(x):
    return x * jax.nn.sigmoid(x)


def _split_bf16(x):
    hi = x.astype(BF16)
    lo = (x - hi.astype(F32)).astype(BF16)
    return hi, lo


def _head_sum(x):
    ones = (_iota((LANES, LANES), 0) // HEAD == _iota((LANES, LANES), 1) // HEAD).astype(BF16)
    hi, lo = _split_bf16(x)
    outs = []
    for p in range(x.shape[-1] // LANES):
        sl = slice(p * LANES, (p + 1) * LANES)
        outs.append(_dot(hi[:, sl], ones) + _dot(lo[:, sl], ones))
    return jnp.concatenate(outs, axis=1)


def _mod_kernel(c_ref, w_ref, b_ref, o_ref):
    s = _silu(c_ref[...])
    o_ref[0] = jnp.dot(s, w_ref[0], preferred_element_type=F32,
                       precision=lax.Precision.HIGHEST) + b_ref[0]


def _modulation(cs, mod_w, mod_b):
    depth, d, n = mod_w.shape
    rows = cs.shape[0]
    tn = 1536
    return pl.pallas_call(
        _mod_kernel,
        grid=(depth, n // tn),
        in_specs=[pl.BlockSpec((rows, d), lambda i, j: (0, 0)),
                  pl.BlockSpec((1, d, tn), lambda i, j: (i, 0, j)),
                  pl.BlockSpec((1, 1, tn), lambda i, j: (i, 0, j))],
        out_specs=pl.BlockSpec((1, rows, tn), lambda i, j: (i, 0, j)),
        out_shape=jax.ShapeDtypeStruct((depth, rows, n), F32),
        compiler_params=_cparams(("arbitrary", "arbitrary")),
        name="modulation",
    )(cs, mod_w, mod_b.reshape(depth, 1, n))


def _tok_kernel(mode, readout, tT, nT, *refs):
    refs = list(refs)
    x_ref = refs.pop(0)
    if mode == "grid":
        xp_ref = refs.pop(0)
        xn_ref = refs.pop(0)
    (mod_ref, ng_ref, mu_ref, wr_ref, wk_ref, wv_ref, w1_ref, w2_ref, w0_ref,
     a1_ref, a2_ref, a0_ref, kkp_ref, ka_ref) = refs[:14]
    refs = refs[14:]
    if readout:
        g1_ref, g2_ref, rk_ref = refs[:3]
        refs = refs[3:]
    r_o, k_o, v_o, kk_o, ic0_o, ic1_o, lw0_o, lw1_o = refs[:8]
    refs = refs[8:]

    d = x_ref.shape[-1]
    sh = mod_ref[0, :, 0:d]
    sc = mod_ref[0, :, d:2 * d]
    g = ng_ref[...]

    def norm_mod(xx):
        return _rms(xx, g) * (1.0 + sc) + sh

    h = norm_mod(x_ref[0])
    if mode == "grid":
        i = pl.program_id(1)
        q = d // 4
        hp = jnp.where(i > 0, norm_mod(xp_ref[0]), 0.0)
        hn = jnp.where(i < nT - 1, norm_mod(xn_ref[0]), 0.0)
        col = _iota((tT, q), 0) % GRID_W
        left = jnp.where(col == 0, 0.0, pltpu.roll(h[:, 0:q], 1, 0))
        right = jnp.where(col == GRID_W - 1, 0.0, pltpu.roll(h[:, q:2 * q], tT - 1, 0))
        up = jnp.concatenate([hp[:, 2 * q:3 * q], h[:tT - GRID_W, 2 * q:3 * q]], axis=0)
        down = jnp.concatenate([h[GRID_W:, 3 * q:], hn[:, 3 * q:]], axis=0)
        hs = jnp.concatenate([left, right, up, down], axis=1)
    else:
        half = d // 2
        row = _iota((tT, half), 0)
        prev = jnp.where(row == 0, 0.0, pltpu.roll(h[:, :half], 1, 0))
        nxt = jnp.where(row == tT - 1, 0.0, pltpu.roll(h[:, half:], tT - 1, 0))
        hs = jnp.concatenate([prev, nxt], axis=1)

    dx = hs - h

    def mix(n):
        return (h + dx * mu_ref[n:n + 1, :]).astype(BF16)

    r = _dot(mix(0), wr_ref[...])
    k = _dot(mix(2), wk_ref[...])
    v = _dot(mix(3), wv_ref[...])
    w1o = jnp.tanh(_dot(mix(1), w1_ref[...])).astype(BF16)
    a1o = _dot(mix(4), a1_ref[...]).astype(BF16)
    ics = []
    for z, (lw_o, ic_o) in enumerate(((lw0_o, ic0_o), (lw1_o, ic1_o))):
        w_pre = _dot(w1o, w2_ref[z]) + w0_ref[z:z + 1, :]
        lw_o[0] = -DECAY_SCALE * jax.nn.sigmoid(w_pre)
        ic = jax.nn.sigmoid(_dot(a1o, a2_ref[z]) + a0_ref[z:z + 1, :])
        ic_o[0] = ic
        ics.append(ic)
    kk = k * kkp_ref[...]
    kk = kk * lax.rsqrt(jnp.maximum(_head_sum(kk * kk), 1e-24))
    r_o[0] = r
    k_o[0] = k
    v_o[0] = v
    kk_o[0] = kk
    if readout:
        gate_o, bon_o = refs
        gate_o[0] = _dot(jax.nn.sigmoid(_dot(mix(5), g1_ref[...])).astype(BF16), g2_ref[...])
        k_avg = k * (1.0 + (0.5 * (ics[0] + ics[1]) - 1.0) * ka_ref[...])
        bon_o[0] = _head_sum(r * k_avg * rk_ref[...]) * v


def _tokenwise(x, mod, mode, readout, p, tT):
    b, t, d = x.shape
    nT = t // tT
    row_spec = pl.BlockSpec((1, tT, d), lambda bi, i: (bi, i, 0))
    in_specs = [row_spec]
    args = [x]
    if mode == "grid":
        hb = tT // GRID_W
        in_specs += [
            pl.BlockSpec((1, GRID_W, d), lambda bi, i: (bi, jnp.maximum(i * hb - 1, 0), 0)),
            pl.BlockSpec((1, GRID_W, d), lambda bi, i: (bi, jnp.minimum((i + 1) * hb, t // GRID_W - 1), 0)),
        ]
        args += [x, x]
    if mod.shape[0] == 1:
        in_specs.append(pl.BlockSpec((1, 1, mod.shape[-1]), lambda bi, i: (0, 0, 0)))
    else:
        in_specs.append(pl.BlockSpec((1, 1, mod.shape[-1]), lambda bi, i: (bi, 0, 0)))
    args.append(mod)
    consts = [p["ng"], p["mu"], p["wr"], p["wk"], p["wv"], p["w1"], p["w2"], p["w0"],
              p["a1"], p["a2"], p["a0"], p["k_k"], p["k_a"]]
    if readout:
        consts += [p["g1"], p["g2"], p["r_k"]]
    for a in consts:
        in_specs.append(pl.BlockSpec(a.shape, lambda bi, i, _n=a.ndim: (0,) * _n))
        args.append(a)
    n_out = 10 if readout else 8
    return pl.pallas_call(
        functools.partial(_tok_kernel, mode, readout, tT, nT),
        grid=(b, nT),
        in_specs=in_specs,
        out_specs=[row_spec] * n_out,
        out_shape=[jax.ShapeDtypeStruct((b, t, d), F32)] * n_out,
        compiler_params=_cparams(("arbitrary", "arbitrary")),
        name="rwkv_tokenwise_" + mode,
    )(*args)


def _scan_prep(r, kd, v, lw, kk, ic, reverse):
    c = r.shape[0]
    ti = _iota((c, c), 0)
    si = _iota((c, c), 1)
    tri = ((si >= ti) if reverse else (si <= ti)).astype(BF16)
    lw_hi, lw_lo = _split_bf16(lw)
    cum = _dot(tri, lw_hi) + _dot(tri, lw_lo)
    pc_log = cum[0:1] if reverse else cum[c - 1:c]
    p_inv = jnp.exp(-cum)
    p_rest = jnp.exp(pc_log - cum)
    b = kk * ic
    return {"rt": r * jnp.exp(cum), "at": -(kk * jnp.exp(cum - lw)),
            "bt": b * p_inv, "kt": kd * p_inv, "bp": b * p_rest, "kp": kd * p_rest,
            "v": v, "pc": jnp.exp(pc_log)}


def _scan_step(prep, states):
    c = prep[0]["v"].shape[0]
    n = 2 * c
    pp = len(states[0])
    chains = [(dr, p) for dr in range(2) for p in range(pp)]
    first = _iota((c, LANES), 1) < HEAD
    row = _iota((n, n), 0)
    col = _iota((n, n), 1)
    strict = (col < row, col > row)
    incl = (col <= row, col >= row)
    same = (row // SUB) == (col // SUB)

    def ms(name, dr, p):
        x = prep[dr][name][:, p * LANES:(p + 1) * LANES]
        return jnp.concatenate([jnp.where(first, x, 0.0), jnp.where(first, 0.0, x)], axis=0)

    def each(fn, *lists):
        return [fn(*xs) for xs in zip(*lists)]

    lhs = [jnp.concatenate([ms("at", dr, p), ms("rt", dr, p)], axis=0).astype(BF16) for dr, p in chains]
    rhs = [jnp.concatenate([ms("bt", dr, p), ms("kt", dr, p)], axis=0).astype(BF16) for dr, p in chains]
    v_ms = [ms("v", dr, p).astype(BF16) for dr, p in chains]
    gmat = each(_dot_nt, lhs, rhs)
    nmat = [jnp.where(strict[dr], g[:n, :n], 0.0) for (dr, _), g in zip(chains, gmat)]
    a_ak = [jnp.where(strict[dr], g[:n, n:], 0.0).astype(BF16) for (dr, _), g in zip(chains, gmat)]
    a_rb = [jnp.where(incl[dr], g[n:, :n], 0.0).astype(BF16) for (dr, _), g in zip(chains, gmat)]
    a_rk = [jnp.where(incl[dr], g[n:, n:], 0.0).astype(BF16) for (dr, _), g in zip(chains, gmat)]

    nd = [jnp.where(same, x, 0.0) for x in nmat]
    no = [jnp.where(same, 0.0, x) for x in nmat]
    pw = nd
    tp = nd
    span = 2
    while span < SUB:
        pw = each(_bdot, pw, pw)
        tp = each(lambda t, q: t + q + _bdot(t, q), tp, pw)
        span *= 2
    m = each(lambda t, o: o + _bdot(t, o), tp, no)
    qp = m
    pw = m
    span = 2
    while span < c // SUB:
        pw = each(_bdot, pw, pw)
        qp = each(lambda t, q: t + q + _bdot(t, q), qp, pw)
        span *= 2
    tp = each(lambda t, q: t + q + _bdot(q, t), tp, qp)

    s_b = [states[dr][p].astype(BF16) for dr, p in chains]
    rhs_z = each(lambda l, s, a, v: _dot_nt(l[:n], s) + _dot(a, v), lhs, s_b, a_ak, v_ms)
    z = each(lambda t, x: x + _bdot(t, x), tp, rhs_z)
    y_ms = each(lambda l, s, ab, zz, ak, v: _dot_nt(l[n:], s) + _bdot(ab, zz) + _dot(ak, v),
                lhs, s_b, a_rb, z, a_rk, v_ms)
    upd = [_bdot(jnp.concatenate([zz, v.astype(F32)], axis=0).T,
                 jnp.concatenate([ms("bp", dr, p), ms("kp", dr, p)], axis=0))
           for (dr, p), zz, v in zip(chains, z, v_ms)]
    y = [jnp.concatenate([ym[:c] + ym[c:] for (d2, _), ym in zip(chains, y_ms) if d2 == dr], axis=1)
         for dr in range(2)]
    new_states = [[None] * pp for _ in range(2)]
    for (dr, p), u in zip(chains, upd):
        new_states[dr][p] = states[dr][p] * prep[dr]["pc"][:, p * LANES:(p + 1) * LANES] + u
    return y, new_states


def _scan_kernel(pp, nc, has_s0, want_y, want_state, *refs):
    refs = list(refs)
    ka_ref = refs[12]
    dirs = (refs[:6], refs[6:12])
    refs = refs[13:]
    if has_s0:
        s0_ref = refs.pop(0)
    if want_y:
        y_refs = (refs.pop(0), refs.pop(0))
    if want_state:
        st_ref = refs.pop(0)
    s_scr = refs.pop(0)
    c = pl.program_id(2)

    @pl.when(c == 0)
    def _():
        if has_s0:
            s_scr[...] = s0_ref[0]
        else:
            s_scr[...] = jnp.zeros_like(s_scr)

    prep = []
    for dr, (r_ref, k_ref, v_ref, kk_ref, ic_ref, lw_ref) in enumerate(dirs):
        ic = ic_ref[0]
        kd = k_ref[0] * (1.0 + (ic - 1.0) * ka_ref[...])
        prep.append(_scan_prep(r_ref[0], kd, v_ref[0], lw_ref[0], kk_ref[0], ic, reverse=(dr == 1)))
    states = [[s_scr[dr, p] for p in range(pp)] for dr in range(2)]
    y, new_states = _scan_step(prep, states)
    for dr in range(2):
        for p in range(pp):
            s_scr[dr, p] = new_states[dr][p]
        if want_y:
            y_refs[dr][0] = y[dr]

    if want_state:
        @pl.when(c == nc - 1)
        def _():
            st_ref[0] = s_scr[...]


def _scan(tok, ka, s0, want_y, want_state):
    r, k, v, kk, ic0, ic1, lw0, lw1 = tok
    b, t, d = r.shape
    nc = t // CHUNK
    npair = d // LANES
    pp = min(SCAN_PAIRS, npair)
    w = LANES * pp
    f_spec = pl.BlockSpec((1, CHUNK, w), lambda bi, hi, ci: (bi, ci, hi))
    b_spec = pl.BlockSpec((1, CHUNK, w), lambda bi, hi, ci: (bi, nc - 1 - ci, hi))
    st_spec = pl.BlockSpec((1, 2, pp, LANES, LANES), lambda bi, hi, ci: (bi, 0, hi, 0, 0))
    in_specs = [f_spec] * 6 + [b_spec] * 6 + [pl.BlockSpec((1, w), lambda bi, hi, ci: (0, hi))]
    args = [r, k, v, kk, ic0, lw0, r, k, v, kk, ic1, lw1, ka]
    if s0 is not None:
        in_specs.append(st_spec)
        args.append(s0)
    out_specs, out_shape = [], []
    if want_y:
        out_specs += [f_spec, b_spec]
        out_shape += [jax.ShapeDtypeStruct((b, t, d), F32)] * 2
    if want_state:
        out_specs.append(st_spec)
        out_shape.append(jax.ShapeDtypeStruct((b, 2, npair, LANES, LANES), F32))
    return pl.pallas_call(
        functools.partial(_scan_kernel, pp, nc, s0 is not None, want_y, want_state),
        grid=(b, npair // pp, nc),
        in_specs=in_specs,
        out_specs=out_specs,
        out_shape=out_shape,
        scratch_shapes=[pltpu.VMEM((2, pp, LANES, LANES), F32)],
        compiler_params=_cparams(("arbitrary", "arbitrary", "arbitrary")),
        name="rwkv_scan_ctx" if s0 is None else "rwkv_scan_latent",
    )(*args)


def _readout_kernel(yf_ref, yb_ref, bon_ref, gate_ref, x_ref, mod_ref, lnw_ref, lnb_ref,
                    wo_ref, g1_ref, g2_ref, x_o, h_o):
    d = x_ref.shape[-1]
    y = yf_ref[0] + yb_ref[0]
    mean = _head_sum(y) * (1.0 / HEAD)
    yc = y - mean
    var = _head_sum(yc * yc) * (1.0 / HEAD)
    o = yc * lax.rsqrt(var + GN_EPS) * lnw_ref[...] + lnb_ref[...] + bon_ref[0]
    att = _dot((o * gate_ref[0]).astype(BF16), wo_ref[...])
    gt_a = mod_ref[0, :, 2 * d:3 * d]
    sh_f = mod_ref[0, :, 3 * d:4 * d]
    sc_f = mod_ref[0, :, 4 * d:5 * d]
    x1 = x_ref[0] + gt_a * _rms(att, g1_ref[...])
    x_o[0] = x1
    h_o[0] = (_rms(x1, g2_ref[...]) * (1.0 + sc_f) + sh_f).astype(BF16)


def _readout(yf, yb, bon, gate, x, mod, lnw, lnb, wo, g1, g2, tT):
    b, t, d = x.shape
    row_spec = pl.BlockSpec((1, tT, d), lambda bi, i: (bi, i, 0))
    vec = pl.BlockSpec((1, d), lambda bi, i: (0, 0))
    return pl.pallas_call(
        _readout_kernel,
        grid=(b, t // tT),
        in_specs=[row_spec] * 5 + [pl.BlockSpec((1, 1, 6 * d), lambda bi, i: (bi, 0, 0)),
                                   vec, vec, pl.BlockSpec((d, d), lambda bi, i: (0, 0)), vec, vec],
        out_specs=[row_spec, row_spec],
        out_shape=[jax.ShapeDtypeStruct((b, t, d), F32), jax.ShapeDtypeStruct((b, t, d), BF16)],
        compiler_params=_cparams(("arbitrary", "arbitrary")),
        name="rwkv_readout",
    )(yf, yb, bon, gate, x, mod, lnw, lnb, wo, g1, g2)


def _ffn_kernel(nf, h_ref, wg_ref, wu_ref, wd_ref, x_ref, mod_ref, g3_ref, gn_ref, mod2_ref,
                x_o, h_o, acc):
    f = pl.program_id(1)
    d = x_ref.shape[-1]

    @pl.when(f == 0)
    def _():
        acc[...] = jnp.zeros_like(acc)

    h = h_ref[...]
    act = _silu(_dot(h, wg_ref[...])) * _dot(h, wu_ref[...])
    acc[...] += _dot(act.astype(BF16), wd_ref[...])

    @pl.when(f == nf - 1)
    def _():
        gt_f = mod_ref[0, :, 5 * d:6 * d]
        x2 = x_ref[...] + gt_f * _rms(acc[...], g3_ref[...])
        x_o[...] = x2
        sh = mod2_ref[0, :, 0:d]
        sc = mod2_ref[0, :, d:2 * d]
        h_o[...] = (_rms(x2, gn_ref[...]) * (1.0 + sc) + sh).astype(BF16)


def _ffn(h, w_gu, w_down, x, mod, g3, gn, mod2, rows_per_batch, tm, tf):
    n, d = x.shape
    ff = w_down.shape[0]
    nf = ff // tf
    per = rows_per_batch // tm
    row = pl.BlockSpec((tm, d), lambda i, f: (i, 0))
    vec = pl.BlockSpec((1, d), lambda i, f: (0, 0))
    modspec = pl.BlockSpec((1, 1, 6 * d), lambda i, f: (i // per, 0, 0))
    return pl.pallas_call(
        functools.partial(_ffn_kernel, nf),
        grid=(n // tm, nf),
        in_specs=[row,
                  pl.BlockSpec((d, tf), lambda i, f: (0, f)),
                  pl.BlockSpec((d, tf), lambda i, f: (0, nf + f)),
                  pl.BlockSpec((tf, d), lambda i, f: (f, 0)),
                  row, modspec, vec, vec, modspec],
        out_specs=[row, row],
        out_shape=[jax.ShapeDtypeStruct((n, d), F32), jax.ShapeDtypeStruct((n, d), BF16)],
        scratch_shapes=[pltpu.VMEM((tm, d), F32)],
        compiler_params=_cparams(("arbitrary", "arbitrary")),
        name="dense_swiglu",
    )(h, w_gu, w_gu, w_down, x, mod, g3, gn, mod2)


HALO = 16


def _conv_kernel(tT, nT, h_ref, hp_ref, hn_ref, x_ref, mod_ref, win_ref, cw_ref, wo_ref,
                 g1_ref, g2_ref, wr_ref, x_o, h_o, lg_o):
    i = pl.program_id(1)
    d = x_ref.shape[-1]
    n = tT + 2 * HALO
    h_ext = jnp.concatenate([hp_ref[0], h_ref[0], hn_ref[0]], axis=0)
    proj = _dot(h_ext, win_ref[...])
    z = proj[:, d:2 * d] * proj[:, 2 * d:]
    row = _iota((n, d), 0)
    dead = ((row < HALO) & (i == 0)) | ((row >= HALO + tT) & (i == nT - 1))
    z = jnp.where(dead, 0.0, z)
    conv = (pltpu.roll(z, 1, 0) * cw_ref[0:1, :] + z * cw_ref[1:2, :]
            + pltpu.roll(z, n - 1, 0) * cw_ref[2:3, :])
    gated = (proj[HALO:HALO + tT, 0:d] * conv[HALO:HALO + tT]).astype(BF16)
    y = _dot(gated, wo_ref[...])
    gt_a = mod_ref[0, :, 2 * d:3 * d]
    sh_f = mod_ref[0, :, 3 * d:4 * d]
    sc_f = mod_ref[0, :, 4 * d:5 * d]
    x3 = x_ref[0] + gt_a * _rms(y, g1_ref[...])
    x_o[0] = x3
    h4 = _rms(x3, g2_ref[...]) * (1.0 + sc_f) + sh_f
    h_o[0] = h4
    lg_o[0] = jnp.dot(h4, wr_ref[...], preferred_element_type=F32,
                      precision=lax.Precision.HIGHEST)


def _conv_layer(h, x, mod, w_in, conv_w, w_out, g1, g2, w_router, tT):
    b, t, d = x.shape
    nT = t // tT
    hb = tT // HALO
    row_spec = pl.BlockSpec((1, tT, d), lambda bi, i: (bi, i, 0))
    vec = pl.BlockSpec((1, d), lambda bi, i: (0, 0))
    full = lambda a: pl.BlockSpec(a.shape, lambda bi, i, _n=a.ndim: (0,) * _n)
    return pl.pallas_call(
        functools.partial(_conv_kernel, tT, nT),
        grid=(b, nT),
        in_specs=[row_spec,
                  pl.BlockSpec((1, HALO, d), lambda bi, i: (bi, jnp.maximum(i * hb - 1, 0), 0)),
                  pl.BlockSpec((1, HALO, d), lambda bi, i: (bi, jnp.minimum((i + 1) * hb, t // HALO - 1), 0)),
                  row_spec, pl.BlockSpec((1, 1, 6 * d), lambda bi, i: (bi, 0, 0)),
                  full(w_in), full(conv_w), full(w_out), vec, vec, full(w_router)],
        out_specs=[row_spec, row_spec, pl.BlockSpec((1, tT, LANES), lambda bi, i: (bi, i, 0))],
        out_shape=[jax.ShapeDtypeStruct((b, t, d), F32), jax.ShapeDtypeStruct((b, t, d), F32),
                   jax.ShapeDtypeStruct((b, t, LANES), F32)],
        compiler_params=_cparams(("arbitrary", "arbitrary")),
        name="short_conv",
    )(h, h, h, x, mod, w_in, conv_w, w_out, g1, g2, w_router)


GATHER_ROWS = 512


def _gather_kernel(tg, nsteps, live_ref, tok_cur, tok_nxt, h_hbm, o_ref, buf, sem):
    j = pl.program_id(0)
    slot = j % 2

    def row_copy(tok_ref, s, r):
        return pltpu.make_async_copy(h_hbm.at[pl.ds(tok_ref[0, 0, r], 1)],
                                     buf.at[s, pl.ds(r, 1)], sem.at[s])

    def issue(tok_ref, s):
        def body(r, carry):
            row_copy(tok_ref, s, r).start()
            return carry
        lax.fori_loop(0, tg, body, 0, unroll=8)

    @pl.when((j == 0) & (live_ref[0] > 0))
    def _():
        issue(tok_cur, 0)

    @pl.when(j + 1 < nsteps)
    def _():
        @pl.when(live_ref[j + 1] > 0)
        def _():
            issue(tok_nxt, 1 - slot)

    @pl.when(live_ref[j] > 0)
    def _():
        pltpu.make_async_copy(h_hbm.at[pl.ds(0, tg)], buf.at[slot], sem.at[slot]).wait()
        o_ref[...] = buf[slot].astype(BF16)

    @pl.when(live_ref[j] == 0)
    def _():
        o_ref[...] = jnp.zeros(o_ref.shape, BF16)


def _gather_rows(h, slot_tok, live, tg):
    n, d = h.shape
    nsteps = slot_tok.shape[0] // tg
    tok3 = slot_tok.reshape(nsteps, 1, tg)
    grid_spec = pltpu.PrefetchScalarGridSpec(
        num_scalar_prefetch=1,
        grid=(nsteps,),
        in_specs=[pl.BlockSpec((1, 1, tg), lambda j, lv: (j, 0, 0), memory_space=pltpu.SMEM),
                  pl.BlockSpec((1, 1, tg), lambda j, lv: (jnp.minimum(j + 1, nsteps - 1), 0, 0),
                               memory_space=pltpu.SMEM),
                  pl.BlockSpec(memory_space=pl.ANY)],
        out_specs=pl.BlockSpec((tg, d), lambda j, lv: (j, 0)),
        scratch_shapes=[pltpu.VMEM((2, tg, d), F32), pltpu.SemaphoreType.DMA((2,))],
    )
    return pl.pallas_call(
        functools.partial(_gather_kernel, tg, nsteps),
        grid_spec=grid_spec,
        out_shape=jax.ShapeDtypeStruct((nsteps * tg, d), BF16),
        compiler_params=_cparams(("arbitrary",)),
        name="moe_dispatch",
    )(live, tok3, tok3, h)


MOE_SUB = 512


def _moe_kernel(nf, tm, e_ref, nv_ref, x_ref, wg_ref, wu_ref, wd_ref, y_o):
    i = pl.program_id(0)
    f = pl.program_id(1)
    nv = nv_ref[i]

    @pl.when(nv > 0)
    def _():
        wg = wg_ref[0].astype(BF16)
        wu = wu_ref[0].astype(BF16)
        wd = wd_ref[0].astype(BF16)
        for s in range(tm // MOE_SUB):
            rows = slice(s * MOE_SUB, (s + 1) * MOE_SUB)

            @pl.when(nv > s * MOE_SUB)
            def _():
                h = x_ref[rows, :]
                act = _silu(_dot(h, wg)) * _dot(h, wu)
                part = _dot(act.astype(BF16), wd)

                @pl.when(f == 0)
                def _():
                    y_o[rows, :] = part

                @pl.when(f > 0)
                def _():
                    y_o[rows, :] += part

    @pl.when(f == nf - 1)
    def _():
        for s in range(tm // MOE_SUB):
            @pl.when(nv <= s * MOE_SUB)
            def _():
                y_o[s * MOE_SUB:(s + 1) * MOE_SUB, :] = jnp.zeros((MOE_SUB, y_o.shape[-1]), F32)


def _moe_experts(xs, blk_e, blk_nv, w_gu, w_down, tm, tf):
    n, d = xs.shape
    ff = w_down.shape[1]
    nf = ff // tf
    nblk = n // tm

    def f_eff(i, f, nv_ref):
        return jnp.where(nv_ref[i] > 0, f, nf - 1)

    grid_spec = pltpu.PrefetchScalarGridSpec(
        num_scalar_prefetch=2,
        grid=(nblk, nf),
        in_specs=[pl.BlockSpec((tm, d), lambda i, f, e, nv: (i, 0)),
                  pl.BlockSpec((1, d, tf), lambda i, f, e, nv: (e[i], 0, f_eff(i, f, nv))),
                  pl.BlockSpec((1, d, tf), lambda i, f, e, nv: (e[i], 0, nf + f_eff(i, f, nv))),
                  pl.BlockSpec((1, tf, d), lambda i, f, e, nv: (e[i], f_eff(i, f, nv), 0))],
        out_specs=pl.BlockSpec((tm, d), lambda i, f, e, nv: (i, 0)),
    )
    return pl.pallas_call(
        functools.partial(_moe_kernel, nf, tm),
        grid_spec=grid_spec,
        out_shape=jax.ShapeDtypeStruct((n, d), F32),
        compiler_params=_cparams(("arbitrary", "arbitrary")),
        name="moe_experts",
    )(blk_e, blk_nv, xs, w_gu, w_gu, w_down)


def _combine_kernel(y0_ref, y1_ref, gt_ref, x_ref, mod_ref, g_ref, o_ref):
    d = x_ref.shape[-1]
    gates = gt_ref[0]
    y = y0_ref[0] * gates[:, 0:1] + y1_ref[0] * gates[:, 1:2]
    gt_f = mod_ref[0, :, 5 * d:6 * d]
    o_ref[0] = x_ref[0] + gt_f * _rms(y, g_ref[...])


def _combine(y0, y1, gates, x, mod, g, tT):
    b, t, d = x.shape
    row_spec = pl.BlockSpec((1, tT, d), lambda bi, i: (bi, i, 0))
    return pl.pallas_call(
        _combine_kernel,
        grid=(b, t // tT),
        in_specs=[row_spec, row_spec, pl.BlockSpec((1, tT, LANES), lambda bi, i: (bi, i, 0)),
                  row_spec, pl.BlockSpec((1, 1, 6 * d), lambda bi, i: (bi, 0, 0)),
                  pl.BlockSpec((1, d), lambda bi, i: (0, 0))],
        out_specs=row_spec,
        out_shape=jax.ShapeDtypeStruct((b, t, d), F32),
        compiler_params=_cparams(("arbitrary", "arbitrary")),
        name="moe_combine",
    )(y0, y1, gates, x, mod, g)


def _route(logits, n_experts, tm):
    n = logits.shape[0]
    nk = n * TOP_K
    top_v, top_e = lax.top_k(logits, TOP_K)
    gates = jax.nn.softmax(top_v, axis=-1)
    flat_e = top_e.reshape(-1).astype(jnp.int32)
    onehot = (flat_e[None, :] == jnp.arange(n_experts, dtype=jnp.int32)[:, None]).astype(jnp.int32)
    csum = jnp.cumsum(onehot, axis=1)
    counts = csum[:, -1]
    rank = jnp.sum((csum - onehot) * onehot, axis=0)
    nblk_e = (counts + tm - 1) // tm
    start = jnp.cumsum(counts) - counts
    bend = jnp.cumsum(nblk_e)
    bstart = bend - nblk_e
    pos = (jnp.sum(onehot * (bstart * tm)[:, None], axis=0) + rank).reshape(n, TOP_K)
    nblk = nk // tm + n_experts
    blk = jnp.arange(nblk, dtype=jnp.int32)
    blk_e = jnp.minimum(jnp.searchsorted(bend, blk, side="right"), n_experts - 1).astype(jnp.int32)
    blk_nv = jnp.clip(counts[blk_e] - (blk - bstart[blk_e]) * tm, 0, tm)
    blk_nv = jnp.where(blk < bend[-1], blk_nv, 0).astype(jnp.int32)
    order = jnp.argsort(flat_e, stable=True).astype(jnp.int32)
    in_blk = jnp.arange(tm, dtype=jnp.int32)[None, :]
    src = (start[blk_e] + (blk - bstart[blk_e]) * tm)[:, None] + in_blk
    src = jnp.where(in_blk < blk_nv[:, None], src, 0).reshape(-1)
    slot_tok = order.at[src].get(mode="promise_in_bounds") // TOP_K
    return gates, slot_tok, pos, blk_e, blk_nv


def kernel(x, c, ctx, c_ctx, mod_w, mod_b, norm_g, rwkv_mu, rwkv_w_rkv, rwkv_w0, rwkv_w1, rwkv_w2,
           rwkv_a0, rwkv_a1, rwkv_a2, rwkv_g1, rwkv_g2, rwkv_k_k, rwkv_k_a, rwkv_r_k, rwkv_ln_w,
           rwkv_ln_b, rwkv_w_out, conv_w_in, conv_w, conv_w_out, ffn_w_gu, ffn_w_down,
           moe_router, moe_w_gu, moe_w_down):
    b, t, d = x.shape
    n_experts = moe_router.shape[-1]
    rows = 16
    cs = jnp.zeros((rows, d), F32).at[:b].set(c).at[b].set(c_ctx)
    mods = _modulation(cs, mod_w, mod_b)
    mod0 = mods[0, :b].reshape(b, 1, 6 * d)
    mod0c = mods[0, b].reshape(1, 1, 6 * d)
    mod1 = mods[1, :b].reshape(b, 1, 6 * d)

    def pad_lora(w):
        zr = jnp.zeros_like(w[0])
        return jnp.stack([jnp.concatenate([w[0], zr], 0), jnp.concatenate([zr, w[1]], 0)]).astype(BF16)

    lg = rwkv_g1.shape[-1]
    lgp = -(-lg // LANES) * LANES
    p = {
        "ng": norm_g[0, 0].reshape(1, d),
        "mu": rwkv_mu[0],
        "wr": rwkv_w_rkv[0, 0].astype(BF16), "wk": rwkv_w_rkv[0, 1].astype(BF16),
        "wv": rwkv_w_rkv[0, 2].astype(BF16),
        "w1": jnp.concatenate([rwkv_w1[0, 0], rwkv_w1[0, 1]], axis=1).astype(BF16),
        "w2": pad_lora(rwkv_w2[0]), "w0": rwkv_w0[0],
        "a1": jnp.concatenate([rwkv_a1[0, 0], rwkv_a1[0, 1]], axis=1).astype(BF16),
        "a2": pad_lora(rwkv_a2[0]), "a0": rwkv_a0[0],
        "k_k": rwkv_k_k[0].reshape(1, d), "k_a": rwkv_k_a[0].reshape(1, d),
        "g1": jnp.pad(rwkv_g1[0], ((0, 0), (0, lgp - lg))).astype(BF16),
        "g2": jnp.pad(rwkv_g2[0], ((0, lgp - lg), (0, 0))).astype(BF16),
        "r_k": rwkv_r_k[0].reshape(1, d),
    }

    tok_c = _tokenwise(ctx, mod0c, "seq", False, p, ctx.shape[1])
    s_ctx = _scan(tok_c, p["k_a"], None, want_y=False, want_state=True)[0]
    tt = min(ROW_TILE, t)
    tm = min(FFN_ROWS, t)
    tf = min(FFN_COLS, ffn_w_down.shape[1])
    tok_l = _tokenwise(x, mod0, "grid", True, p, tt)
    yf, yb = _scan(tok_l[:8], p["k_a"], s_ctx, want_y=True, want_state=False)
    x1, h2 = _readout(yf, yb, tok_l[9], tok_l[8], x, mod0,
                      rwkv_ln_w[0].reshape(1, d), rwkv_ln_b[0].reshape(1, d),
                      rwkv_w_out[0].astype(BF16), norm_g[0, 1].reshape(1, d),
                      norm_g[0, 2].reshape(1, d), tt)

    x2, h3 = _ffn(h2.reshape(b * t, d), ffn_w_gu[0].astype(BF16), ffn_w_down[0].astype(BF16),
                  x1.reshape(b * t, d), mod0, norm_g[0, 3].reshape(1, d),
                  norm_g[1, 0].reshape(1, d), mod1, t, tm, tf)

    w_router = jnp.pad(moe_router[0], ((0, 0), (0, LANES - n_experts)))
    x3, h4, logits = _conv_layer(h3.reshape(b, t, d), x2.reshape(b, t, d), mod1,
                                 conv_w_in[0].astype(BF16), conv_w[0], conv_w_out[0].astype(BF16),
                                 norm_g[1, 1].reshape(1, d), norm_g[1, 2].reshape(1, d), w_router, tt)

    tme = min(MOE_ROWS, b * t * TOP_K)
    gates, slot_tok, pos, blk_e, blk_nv = _route(logits.reshape(b * t, LANES)[:, :n_experts],
                                                 n_experts, tme)
    tg = min(GATHER_ROWS, tme)
    live = (blk_nv[:, None] > jnp.arange(0, tme, tg, dtype=jnp.int32)[None, :]).astype(jnp.int32)
    xs = _gather_rows(h4.reshape(b * t, d), slot_tok, live.reshape(-1), tg)
    ys = _moe_experts(xs, blk_e, blk_nv, moe_w_gu[0], moe_w_down[0], tme,
                      min(MOE_COLS, moe_w_down.shape[2]))
    y0 = ys.at[pos[:, 0]].get(mode="promise_in_bounds").reshape(b, t, d)
    y1 = ys.at[pos[:, 1]].get(mode="promise_in_bounds").reshape(b, t, d)
    gates_p = jnp.pad(gates, ((0, 0), (0, LANES - TOP_K))).reshape(b, t, LANES)
    return _combine(y0, y1, gates_p, x3, mod1, norm_g[1, 3].reshape(1, d), tt)
```

```python
import functools
import math

import jax
import jax.numpy as jnp
from jax import lax
from jax.experimental import pallas as pl
from jax.experimental.pallas import tpu as pltpu

F32 = jnp.float32
BF16 = jnp.bfloat16

HEAD = 64
LANES = 128
GRID_W = 64
CHUNK = 64
SUB = 16
SCAN_PAIRS = 8
NORM_EPS = 1e-6
GN_EPS = 64e-5
DECAY_SCALE = math.exp(-0.5)
TOP_K = 2
VMEM_LIMIT = 56 * 1024 * 1024


ROW_TILE = 256
FFN_ROWS = 1024
FFN_COLS = 512
MOE_ROWS = 2048
MOE_COLS = 512


def _cparams(sem):
    return pltpu.CompilerParams(dimension_semantics=sem, vmem_limit_bytes=VMEM_LIMIT)


def _dot(a, b):
    return jnp.dot(a, b, preferred_element_type=F32)


def _dot_nt(a, b):
    return lax.dot_general(a, b, (((1,), (1,)), ((), ())), preferred_element_type=F32)


def _bdot(a, b):
    return _dot(a.astype(BF16), b.astype(BF16))


def _iota(shape, axis):
    return lax.broadcasted_iota(jnp.int32, shape, axis)


def _rms(x, g):
    return x * lax.rsqrt(jnp.mean(x * x, axis=-1, keepdims=True) + NORM_EPS) * g


def _silu(x):
    return x * jax.nn.sigmoid(x)


def _split_bf16(x):
    hi = x.astype(BF16)
    lo = (x - hi.astype(F32)).astype(BF16)
    return hi, lo


def _head_sum(x):
    ones = (_iota((LANES, LANES), 0) // HEAD == _iota((LANES, LANES), 1) // HEAD).astype(BF16)
    hi, lo = _split_bf16(x)
    outs = []
    for p in range(x.shape[-1] // LANES):
        sl = slice(p * LANES, (p + 1) * LANES)
        outs.append(_dot(hi[:, sl], ones) + _dot(lo[:, sl], ones))
    return jnp.concatenate(outs, axis=1)


def _mod_kernel(c_ref, w_ref, b_ref, o_ref):
    s = _silu(c_ref[...])
    o_ref[0] = jnp.dot(s, w_ref[0], preferred_element_type=F32,
                       precision=lax.Precision.HIGHEST) + b_ref[0]


def _modulation(cs, mod_w, mod_b):
    depth, d, n = mod_w.shape
    rows = cs.shape[0]
    tn = 1536
    return pl.pallas_call(
        _mod_kernel,
        grid=(depth, n // tn),
        in_specs=[pl.BlockSpec((rows, d), lambda i, j: (0, 0)),
                  pl.BlockSpec((1, d, tn), lambda i, j: (i, 0, j)),
                  pl.BlockSpec((1, 1, tn), lambda i, j: (i, 0, j))],
        out_specs=pl.BlockSpec((1, rows, tn), lambda i, j: (i, 0, j)),
        out_shape=jax.ShapeDtypeStruct((depth, rows, n), F32),
        compiler_params=_cparams(("arbitrary", "arbitrary")),
        name="modulation",
    )(cs, mod_w, mod_b.reshape(depth, 1, n))


def _tok_kernel(mode, readout, tT, nT, *refs):
    refs = list(refs)
    x_ref = refs.pop(0)
    if mode == "grid":
        xp_ref = refs.pop(0)
        xn_ref = refs.pop(0)
    (mod_ref, ng_ref, mu_ref, wr_ref, wk_ref, wv_ref, w1_ref, w2_ref, w0_ref,
     a1_ref, a2_ref, a0_ref, kkp_ref, ka_ref) = refs[:14]
    refs = refs[14:]
    if readout:
        g1_ref, g2_ref, rk_ref = refs[:3]
        refs = refs[3:]
    r_o, k_o, v_o, kk_o, ic0_o, ic1_o, lw0_o, lw1_o = refs[:8]
    refs = refs[8:]

    d = x_ref.shape[-1]
    sh = mod_ref[0, :, 0:d]
    sc = mod_ref[0, :, d:2 * d]
    g = ng_ref[...]

    def norm_mod(xx):
        return _rms(xx, g) * (1.0 + sc) + sh

    h = norm_mod(x_ref[0])
    if mode == "grid":
        i = pl.program_id(1)
        q = d // 4
        hp = jnp.where(i > 0, norm_mod(xp_ref[0]), 0.0)
        hn = jnp.where(i < nT - 1, norm_mod(xn_ref[0]), 0.0)
        col = _iota((tT, q), 0) % GRID_W
        left = jnp.where(col == 0, 0.0, pltpu.roll(h[:, 0:q], 1, 0))
        right = jnp.where(col == GRID_W - 1, 0.0, pltpu.roll(h[:, q:2 * q], tT - 1, 0))
        up = jnp.concatenate([hp[:, 2 * q:3 * q], h[:tT - GRID_W, 2 * q:3 * q]], axis=0)
        down = jnp.concatenate([h[GRID_W:, 3 * q:], hn[:, 3 * q:]], axis=0)
        hs = jnp.concatenate([left, right, up, down], axis=1)
    else:
        half = d // 2
        row = _iota((tT, half), 0)
        prev = jnp.where(row == 0, 0.0, pltpu.roll(h[:, :half], 1, 0))
        nxt = jnp.where(row == tT - 1, 0.0, pltpu.roll(h[:, half:], tT - 1, 0))
        hs = jnp.concatenate([prev, nxt], axis=1)

    dx = hs - h

    def mix(n):
        return (h + dx * mu_ref[n:n + 1, :]).astype(BF16)

    r = _dot(mix(0), wr_ref[...])
    k = _dot(mix(2), wk_ref[...])
    v = _dot(mix(3), wv_ref[...])
    w1o = jnp.tanh(_dot(mix(1), w1_ref[...])).astype(BF16)
    a1o = _dot(mix(4), a1_ref[...]).astype(BF16)
    ics = []
    for z, (lw_o, ic_o) in enumerate(((lw0_o, ic0_o), (lw1_o, ic1_o))):
        w_pre = _dot(w1o, w2_ref[z]) + w0_ref[z:z + 1, :]
        lw_o[0] = -DECAY_SCALE * jax.nn.sigmoid(w_pre)
        ic = jax.nn.sigmoid(_dot(a1o, a2_ref[z]) + a0_ref[z:z + 1, :])
        ic_o[0] = ic.astype(BF16)
        ics.append(ic)
    kk = k * kkp_ref[...]
    kk = kk * lax.rsqrt(jnp.maximum(_head_sum(kk * kk), 1e-24))
    r_o[0] = r.astype(BF16)
    k_o[0] = k.astype(BF16)
    v_o[0] = v.astype(BF16)
    kk_o[0] = kk.astype(BF16)
    if readout:
        gate_o, bon_o = refs
        gate = _dot(jax.nn.sigmoid(_dot(mix(5), g1_ref[...])).astype(BF16), g2_ref[...])
        gate_o[0] = gate.astype(BF16)
        k_avg = k * (1.0 + (0.5 * (ics[0] + ics[1]) - 1.0) * ka_ref[...])
        bon_o[0] = (_head_sum(r * k_avg * rk_ref[...]) * v).astype(BF16)


def _tokenwise(x, mod, mode, readout, p, tT):
    b, t, d = x.shape
    nT = t // tT
    row_spec = pl.BlockSpec((1, tT, d), lambda bi, i: (bi, i, 0))
    in_specs = [row_spec]
    args = [x]
    if mode == "grid":
        hb = tT // GRID_W
        in_specs += [
            pl.BlockSpec((1, GRID_W, d), lambda bi, i: (bi, jnp.maximum(i * hb - 1, 0), 0)),
            pl.BlockSpec((1, GRID_W, d), lambda bi, i: (bi, jnp.minimum((i + 1) * hb, t // GRID_W - 1), 0)),
        ]
        args += [x, x]
    if mod.shape[0] == 1:
        in_specs.append(pl.BlockSpec((1, 1, mod.shape[-1]), lambda bi, i: (0, 0, 0)))
    else:
        in_specs.append(pl.BlockSpec((1, 1, mod.shape[-1]), lambda bi, i: (bi, 0, 0)))
    args.append(mod)
    consts = [p["ng"], p["mu"], p["wr"], p["wk"], p["wv"], p["w1"], p["w2"], p["w0"],
              p["a1"], p["a2"], p["a0"], p["k_k"], p["k_a"]]
    if readout:
        consts += [p["g1"], p["g2"], p["r_k"]]
    for a in consts:
        in_specs.append(pl.BlockSpec(a.shape, lambda bi, i, _n=a.ndim: (0,) * _n))
        args.append(a)
    dtypes = [BF16] * 6 + [F32] * 2 + ([BF16] * 2 if readout else [])
    return pl.pallas_call(
        functools.partial(_tok_kernel, mode, readout, tT, nT),
        grid=(b, nT),
        in_specs=in_specs,
        out_specs=[row_spec] * len(dtypes),
        out_shape=[jax.ShapeDtypeStruct((b, t, d), dt) for dt in dtypes],
        compiler_params=_cparams(("arbitrary", "arbitrary")),
        name="rwkv_tokenwise_" + mode,
    )(*args)


def _scan_prep(r, kd, v, lw, kk, ic, reverse):
    c = r.shape[0]
    ti = _iota((c, c), 0)
    si = _iota((c, c), 1)
    tri = ((si >= ti) if reverse else (si <= ti)).astype(BF16)
    lw_hi, lw_lo = _split_bf16(lw)
    cum = _dot(tri, lw_hi) + _dot(tri, lw_lo)
    pc_log = cum[0:1] if reverse else cum[c - 1:c]
    p_inv = jnp.exp(-cum)
    p_rest = jnp.exp(pc_log - cum)
    b = kk * ic
    return {"rt": r * jnp.exp(cum), "at": -(kk * jnp.exp(cum - lw)),
            "bt": b * p_inv, "kt": kd * p_inv, "bp": b * p_rest, "kp": kd * p_rest,
            "v": v, "pc": jnp.exp(pc_log)}


def _scan_step(prep, states):
    c = prep[0]["v"].shape[0]
    n = 2 * c
    pp = len(states[0])
    chains = [(dr, p) for dr in range(2) for p in range(pp)]
    first = _iota((c, LANES), 1) < HEAD
    row = _iota((n, n), 0)
    col = _iota((n, n), 1)
    strict = (col < row, col > row)
    incl = (col <= row, col >= row)
    same = (row // SUB) == (col // SUB)

    def ms(name, dr, p):
        x = prep[dr][name][:, p * LANES:(p + 1) * LANES]
        return jnp.concatenate([jnp.where(first, x, 0.0), jnp.where(first, 0.0, x)], axis=0)

    def each(fn, *lists):
        return [fn(*xs) for xs in zip(*lists)]

    lhs = [jnp.concatenate([ms("at", dr, p), ms("rt", dr, p)], axis=0).astype(BF16) for dr, p in chains]
    rhs = [jnp.concatenate([ms("bt", dr, p), ms("kt", dr, p)], axis=0).astype(BF16) for dr, p in chains]
    v_ms = [ms("v", dr, p).astype(BF16) for dr, p in chains]
    gmat = each(_dot_nt, lhs, rhs)
    nmat = [jnp.where(strict[dr], g[:n, :n], 0.0) for (dr, _), g in zip(chains, gmat)]
    a_ak = [jnp.where(strict[dr], g[:n, n:], 0.0).astype(BF16) for (dr, _), g in zip(chains, gmat)]
    a_r = [jnp.where(jnp.concatenate([incl[dr], incl[dr]], axis=1), g[n:], 0.0).astype(BF16)
           for (dr, _), g in zip(chains, gmat)]

    nd = [jnp.where(same, x, 0.0) for x in nmat]
    no = [jnp.where(same, 0.0, x) for x in nmat]
    pw = nd
    tp = nd
    span = 2
    while span < SUB:
        pw = each(_bdot, pw, pw)
        tp = each(lambda t, q: t + q + _bdot(t, q), tp, pw)
        span *= 2
    m = each(lambda t, o: o + _bdot(t, o), tp, no)
    qp = m
    pw = m
    span = 2
    while span < c // SUB:
        pw = each(_bdot, pw, pw)
        qp = each(lambda t, q: t + q + _bdot(t, q), qp, pw)
        span *= 2
    tp = each(lambda t, q: t + q + _bdot(q, t), tp, qp)

    bk_t = [jnp.concatenate([ms("bp", dr, p), ms("kp", dr, p)], axis=0).T.astype(BF16)
            for dr, p in chains]
    diag = row == col
    pc_col = [jnp.sum(jnp.where(diag, prep[dr]["pc"][:, p * LANES:(p + 1) * LANES], 0.0),
                      axis=1, keepdims=True) for dr, p in chains]

    s_t = [states[dr][p].astype(BF16) for dr, p in chains]
    rhs_z = each(lambda l, a, st, v: _dot(jnp.concatenate([l[:n], a], axis=1),
                                          jnp.concatenate([st, v], axis=0)), lhs, a_ak, s_t, v_ms)
    z = each(lambda t, x: x + _bdot(t, x), tp, rhs_z)
    zv = each(lambda zz, v: jnp.concatenate([zz.astype(BF16), v], axis=0), z, v_ms)
    y_ms = each(lambda l, st, ar, w: _dot(l[n:], st) + _dot(ar, w), lhs, s_t, a_r, zv)
    y = [jnp.concatenate([ym[:c] + ym[c:] for (d2, _), ym in zip(chains, y_ms) if d2 == dr], axis=1)
         for dr in range(2)]
    new_states = [[None] * pp for _ in range(2)]
    for (dr, p), bt_, w, pc in zip(chains, bk_t, zv, pc_col):
        new_states[dr][p] = states[dr][p] * pc + _dot(bt_, w)
    return y, new_states


def _scan_kernel(pp, nc, has_s0, want_y, want_state, *refs):
    refs = list(refs)
    ka_ref = refs[12]
    dirs = (refs[:6], refs[6:12])
    refs = refs[13:]
    if has_s0:
        s0_ref = refs.pop(0)
    if want_y:
        y_refs = (refs.pop(0), refs.pop(0))
    if want_state:
        st_ref = refs.pop(0)
    s_scr = refs.pop(0)
    c = pl.program_id(2)

    @pl.when(c == 0)
    def _():
        if has_s0:
            s_scr[...] = s0_ref[0]
        else:
            s_scr[...] = jnp.zeros_like(s_scr)

    prep = []
    for dr, (r_ref, k_ref, v_ref, kk_ref, ic_ref, lw_ref) in enumerate(dirs):
        ic = ic_ref[0].astype(F32)
        kd = k_ref[0].astype(F32) * (1.0 + (ic - 1.0) * ka_ref[...])
        prep.append(_scan_prep(r_ref[0].astype(F32), kd, v_ref[0].astype(F32), lw_ref[0],
                               kk_ref[0].astype(F32), ic, reverse=(dr == 1)))
    states = [[s_scr[dr, p] for p in range(pp)] for dr in range(2)]
    y, new_states = _scan_step(prep, states)
    for dr in range(2):
        for p in range(pp):
            s_scr[dr, p] = new_states[dr][p]
        if want_y:
            y_refs[dr][0] = y[dr]

    if want_state:
        @pl.when(c == nc - 1)
        def _():
            st_ref[0] = s_scr[...]


def _scan(tok, ka, s0, want_y, want_state):
    r, k, v, kk, ic0, ic1, lw0, lw1 = tok
    b, t, d = r.shape
    nc = t // CHUNK
    npair = d // LANES
    pp = min(SCAN_PAIRS, npair)
    w = LANES * pp
    f_spec = pl.BlockSpec((1, CHUNK, w), lambda bi, hi, ci: (bi, ci, hi))
    b_spec = pl.BlockSpec((1, CHUNK, w), lambda bi, hi, ci: (bi, nc - 1 - ci, hi))
    st_spec = pl.BlockSpec((1, 2, pp, LANES, LANES), lambda bi, hi, ci: (bi, 0, hi, 0, 0))
    in_specs = [f_spec] * 6 + [b_spec] * 6 + [pl.BlockSpec((1, w), lambda bi, hi, ci: (0, hi))]
    args = [r, k, v, kk, ic0, lw0, r, k, v, kk, ic1, lw1, ka]
    if s0 is not None:
        in_specs.append(st_spec)
        args.append(s0)
    out_specs, out_shape = [], []
    if want_y:
        out_specs += [f_spec, b_spec]
        out_shape += [jax.ShapeDtypeStruct((b, t, d), F32)] * 2
    if want_state:
        out_specs.append(st_spec)
        out_shape.append(jax.ShapeDtypeStruct((b, 2, npair, LANES, LANES), F32))
    return pl.pallas_call(
        functools.partial(_scan_kernel, pp, nc, s0 is not None, want_y, want_state),
        grid=(b, npair // pp, nc),
        in_specs=in_specs,
        out_specs=out_specs,
        out_shape=out_shape,
        scratch_shapes=[pltpu.VMEM((2, pp, LANES, LANES), F32)],
        compiler_params=_cparams(("arbitrary", "arbitrary", "arbitrary")),
        name="rwkv_scan_ctx" if s0 is None else "rwkv_scan_latent",
    )(*args)


def _readout_kernel(yf_ref, yb_ref, bon_ref, gate_ref, x_ref, mod_ref, lnw_ref, lnb_ref,
                    wo_ref, g1_ref, g2_ref, x_o, h_o):
    d = x_ref.shape[-1]
    y = yf_ref[0] + yb_ref[0]
    mean = _head_sum(y) * (1.0 / HEAD)
    yc = y - mean
    var = _head_sum(yc * yc) * (1.0 / HEAD)
    o = yc * lax.rsqrt(var + GN_EPS) * lnw_ref[...] + lnb_ref[...] + bon_ref[0]
    att = _dot((o * gate_ref[0]).astype(BF16), wo_ref[...])
    gt_a = mod_ref[0, :, 2 * d:3 * d]
    sh_f = mod_ref[0, :, 3 * d:4 * d]
    sc_f = mod_ref[0, :, 4 * d:5 * d]
    x1 = x_ref[0] + gt_a * _rms(att, g1_ref[...])
    x_o[0] = x1
    h_o[0] = (_rms(x1, g2_ref[...]) * (1.0 + sc_f) + sh_f).astype(BF16)


def _readout(yf, yb, bon, gate, x, mod, lnw, lnb, wo, g1, g2, tT):
    b, t, d = x.shape
    row_spec = pl.BlockSpec((1, tT, d), lambda bi, i: (bi, i, 0))
    vec = pl.BlockSpec((1, d), lambda bi, i: (0, 0))
    return pl.pallas_call(
        _readout_kernel,
        grid=(b, t // tT),
        in_specs=[row_spec] * 5 + [pl.BlockSpec((1, 1, 6 * d), lambda bi, i: (bi, 0, 0)),
                                   vec, vec, pl.BlockSpec((d, d), lambda bi, i: (0, 0)), vec, vec],
        out_specs=[row_spec, row_spec],
        out_shape=[jax.ShapeDtypeStruct((b, t, d), F32), jax.ShapeDtypeStruct((b, t, d), BF16)],
        compiler_params=_cparams(("arbitrary", "arbitrary")),
        name="rwkv_readout",
    )(yf, yb, bon, gate, x, mod, lnw, lnb, wo, g1, g2)


def _ffn_kernel(nf, h_ref, wg_ref, wu_ref, wd_ref, x_ref, mod_ref, g3_ref, gn_ref, mod2_ref,
                x_o, h_o, acc):
    f = pl.program_id(1)
    d = x_ref.shape[-1]

    @pl.when(f == 0)
    def _():
        acc[...] = jnp.zeros_like(acc)

    h = h_ref[...]
    act = _silu(_dot(h, wg_ref[...])) * _dot(h, wu_ref[...])
    acc[...] += _dot(act.astype(BF16), wd_ref[...])

    @pl.when(f == nf - 1)
    def _():
        gt_f = mod_ref[0, :, 5 * d:6 * d]
        x2 = x_ref[...] + gt_f * _rms(acc[...], g3_ref[...])
        x_o[...] = x2
        sh = mod2_ref[0, :, 0:d]
        sc = mod2_ref[0, :, d:2 * d]
        h_o[...] = (_rms(x2, gn_ref[...]) * (1.0 + sc) + sh).astype(BF16)


def _ffn(h, w_gu, w_down, x, mod, g3, gn, mod2, rows_per_batch, tm, tf):
    n, d = x.shape
    ff = w_down.shape[0]
    nf = ff // tf
    per = rows_per_batch // tm
    row = pl.BlockSpec((tm, d), lambda i, f: (i, 0))
    vec = pl.BlockSpec((1, d), lambda i, f: (0, 0))
    modspec = pl.BlockSpec((1, 1, 6 * d), lambda i, f: (i // per, 0, 0))
    return pl.pallas_call(
        functools.partial(_ffn_kernel, nf),
        grid=(n // tm, nf),
        in_specs=[row,
                  pl.BlockSpec((d, tf), lambda i, f: (0, f)),
                  pl.BlockSpec((d, tf), lambda i, f: (0, nf + f)),
                  pl.BlockSpec((tf, d), lambda i, f: (f, 0)),
                  row, modspec, vec, vec, modspec],
        out_specs=[row, row],
        out_shape=[jax.ShapeDtypeStruct((n, d), F32), jax.ShapeDtypeStruct((n, d), BF16)],
        scratch_shapes=[pltpu.VMEM((tm, d), F32)],
        compiler_params=_cparams(("arbitrary", "arbitrary")),
        name="dense_swiglu",
    )(h, w_gu, w_gu, w_down, x, mod, g3, gn, mod2)


HALO = 16


def _conv_kernel(tT, nT, h_ref, hp_ref, hn_ref, x_ref, mod_ref, win_ref, cw_ref, wo_ref,
                 g1_ref, g2_ref, wr_ref, x_o, h_o, lg_o):
    i = pl.program_id(1)
    d = x_ref.shape[-1]
    n = tT + 2 * HALO
    h_ext = jnp.concatenate([hp_ref[0], h_ref[0], hn_ref[0]], axis=0)
    proj = _dot(h_ext, win_ref[...])
    z = proj[:, d:2 * d] * proj[:, 2 * d:]
    row = _iota((n, d), 0)
    dead = ((row < HALO) & (i == 0)) | ((row >= HALO + tT) & (i == nT - 1))
    z = jnp.where(dead, 0.0, z)
    conv = (pltpu.roll(z, 1, 0) * cw_ref[0:1, :] + z * cw_ref[1:2, :]
            + pltpu.roll(z, n - 1, 0) * cw_ref[2:3, :])
    gated = (proj[HALO:HALO + tT, 0:d] * conv[HALO:HALO + tT]).astype(BF16)
    y = _dot(gated, wo_ref[...])
    gt_a = mod_ref[0, :, 2 * d:3 * d]
    sh_f = mod_ref[0, :, 3 * d:4 * d]
    sc_f = mod_ref[0, :, 4 * d:5 * d]
    x3 = x_ref[0] + gt_a * _rms(y, g1_ref[...])
    x_o[0] = x3
    h4 = _rms(x3, g2_ref[...]) * (1.0 + sc_f) + sh_f
    h_o[0] = h4
    lg_o[0] = jnp.dot(h4, wr_ref[...], preferred_element_type=F32,
                      precision=lax.Precision.HIGHEST)


def _conv_layer(h, x, mod, w_in, conv_w, w_out, g1, g2, w_router, tT):
    b, t, d = x.shape
    nT = t // tT
    hb = tT // HALO
    row_spec = pl.BlockSpec((1, tT, d), lambda bi, i: (bi, i, 0))
    vec = pl.BlockSpec((1, d), lambda bi, i: (0, 0))
    full = lambda a: pl.BlockSpec(a.shape, lambda bi, i, _n=a.ndim: (0,) * _n)
    return pl.pallas_call(
        functools.partial(_conv_kernel, tT, nT),
        grid=(b, nT),
        in_specs=[row_spec,
                  pl.BlockSpec((1, HALO, d), lambda bi, i: (bi, jnp.maximum(i * hb - 1, 0), 0)),
                  pl.BlockSpec((1, HALO, d), lambda bi, i: (bi, jnp.minimum((i + 1) * hb, t // HALO - 1), 0)),
                  row_spec, pl.BlockSpec((1, 1, 6 * d), lambda bi, i: (bi, 0, 0)),
                  full(w_in), full(conv_w), full(w_out), vec, vec, full(w_router)],
        out_specs=[row_spec, row_spec, pl.BlockSpec((1, tT, LANES), lambda bi, i: (bi, i, 0))],
        out_shape=[jax.ShapeDtypeStruct((b, t, d), F32), jax.ShapeDtypeStruct((b, t, d), F32),
                   jax.ShapeDtypeStruct((b, t, LANES), F32)],
        compiler_params=_cparams(("arbitrary", "arbitrary")),
        name="short_conv",
    )(h, h, h, x, mod, w_in, conv_w, w_out, g1, g2, w_router)


GATHER_ROWS = 512


def _gather_kernel(tg, nsteps, live_ref, tok_cur, tok_nxt, h_hbm, o_ref, buf, sem):
    j = pl.program_id(0)
    slot = j % 2

    def row_copy(tok_ref, s, r):
        return pltpu.make_async_copy(h_hbm.at[pl.ds(tok_ref[0, 0, r], 1)],
                                     buf.at[s, pl.ds(r, 1)], sem.at[s])

    def issue(tok_ref, s):
        def body(r, carry):
            row_copy(tok_ref, s, r).start()
            return carry
        lax.fori_loop(0, tg, body, 0, unroll=8)

    @pl.when((j == 0) & (live_ref[0] > 0))
    def _():
        issue(tok_cur, 0)

    @pl.when(j + 1 < nsteps)
    def _():
        @pl.when(live_ref[j + 1] > 0)
        def _():
            issue(tok_nxt, 1 - slot)

    @pl.when(live_ref[j] > 0)
    def _():
        pltpu.make_async_copy(h_hbm.at[pl.ds(0, tg)], buf.at[slot], sem.at[slot]).wait()
        o_ref[...] = buf[slot].astype(BF16)

    @pl.when(live_ref[j] == 0)
    def _():
        o_ref[...] = jnp.zeros(o_ref.shape, BF16)


def _gather_rows(h, slot_tok, live, tg):
    n, d = h.shape
    nsteps = slot_tok.shape[0] // tg
    tok3 = slot_tok.reshape(nsteps, 1, tg)
    grid_spec = pltpu.PrefetchScalarGridSpec(
        num_scalar_prefetch=1,
        grid=(nsteps,),
        in_specs=[pl.BlockSpec((1, 1, tg), lambda j, lv: (j, 0, 0), memory_space=pltpu.SMEM),
                  pl.BlockSpec((1, 1, tg), lambda j, lv: (jnp.minimum(j + 1, nsteps - 1), 0, 0),
                               memory_space=pltpu.SMEM),
                  pl.BlockSpec(memory_space=pl.ANY)],
        out_specs=pl.BlockSpec((tg, d), lambda j, lv: (j, 0)),
        scratch_shapes=[pltpu.VMEM((2, tg, d), F32), pltpu.SemaphoreType.DMA((2,))],
    )
    return pl.pallas_call(
        functools.partial(_gather_kernel, tg, nsteps),
        grid_spec=grid_spec,
        out_shape=jax.ShapeDtypeStruct((nsteps * tg, d), BF16),
        compiler_params=_cparams(("arbitrary",)),
        name="moe_dispatch",
    )(live, tok3, tok3, h)


MOE_SUB = 512


def _moe_kernel(nf, tm, e_ref, nv_ref, x_ref, wg_ref, wu_ref, wd_ref, y_o):
    i = pl.program_id(0)
    f = pl.program_id(1)
    nv = nv_ref[i]

    @pl.when(nv > 0)
    def _():
        wg = wg_ref[0].astype(BF16)
        wu = wu_ref[0].astype(BF16)
        wd = wd_ref[0].astype(BF16)
        for s in range(tm // MOE_SUB):
            rows = slice(s * MOE_SUB, (s + 1) * MOE_SUB)

            @pl.when(nv > s * MOE_SUB)
            def _():
                h = x_ref[rows, :]
                act = _silu(_dot(h, wg)) * _dot(h, wu)
                part = _dot(act.astype(BF16), wd)

                @pl.when(f == 0)
                def _():
                    y_o[rows, :] = part

                @pl.when(f > 0)
                def _():
                    y_o[rows, :] += part

    @pl.when(f == nf - 1)
    def _():
        for s in range(tm // MOE_SUB):
            @pl.when(nv <= s * MOE_SUB)
            def _():
                y_o[s * MOE_SUB:(s + 1) * MOE_SUB, :] = jnp.zeros((MOE_SUB, y_o.shape[-1]), F32)


def _moe_experts(xs, blk_e, blk_nv, w_gu, w_down, tm, tf):
    n, d = xs.shape
    ff = w_down.shape[1]
    nf = ff // tf
    nblk = n // tm

    def f_eff(i, f, nv_ref):
        return jnp.where(nv_ref[i] > 0, f, nf - 1)

    grid_spec = pltpu.PrefetchScalarGridSpec(
        num_scalar_prefetch=2,
        grid=(nblk, nf),
        in_specs=[pl.BlockSpec((tm, d), lambda i, f, e, nv: (i, 0)),
                  pl.BlockSpec((1, d, tf), lambda i, f, e, nv: (e[i], 0, f_eff(i, f, nv))),
                  pl.BlockSpec((1, d, tf), lambda i, f, e, nv: (e[i], 0, nf + f_eff(i, f, nv))),
                  pl.BlockSpec((1, tf, d), lambda i, f, e, nv: (e[i], f_eff(i, f, nv), 0))],
        out_specs=pl.BlockSpec((tm, d), lambda i, f, e, nv: (i, 0)),
    )
    return pl.pallas_call(
        functools.partial(_moe_kernel, nf, tm),
        grid_spec=grid_spec,
        out_shape=jax.ShapeDtypeStruct((n, d), F32),
        compiler_params=_cparams(("arbitrary", "arbitrary")),
        name="moe_experts",
    )(blk_e, blk_nv, xs, w_gu, w_gu, w_down)


def _combine_kernel(y0_ref, y1_ref, gt_ref, x_ref, mod_ref, g_ref, o_ref):
    d = x_ref.shape[-1]
    gates = gt_ref[0]
    y = y0_ref[0] * gates[:, 0:1] + y1_ref[0] * gates[:, 1:2]
    gt_f = mod_ref[0, :, 5 * d:6 * d]
    o_ref[0] = x_ref[0] + gt_f * _rms(y, g_ref[...])


def _combine(y0, y1, gates, x, mod, g, tT):
    b, t, d = x.shape
    row_spec = pl.BlockSpec((1, tT, d), lambda bi, i: (bi, i, 0))
    return pl.pallas_call(
        _combine_kernel,
        grid=(b, t // tT),
        in_specs=[row_spec, row_spec, pl.BlockSpec((1, tT, LANES), lambda bi, i: (bi, i, 0)),
                  row_spec, pl.BlockSpec((1, 1, 6 * d), lambda bi, i: (bi, 0, 0)),
                  pl.BlockSpec((1, d), lambda bi, i: (0, 0))],
        out_specs=row_spec,
        out_shape=jax.ShapeDtypeStruct((b, t, d), F32),
        compiler_params=_cparams(("arbitrary", "arbitrary")),
        name="moe_combine",
    )(y0, y1, gates, x, mod, g)


def _route(logits, n_experts, tm):
    n = logits.shape[0]
    nk = n * TOP_K
    top_v, top_e = lax.top_k(logits, TOP_K)
    gates = jax.nn.softmax(top_v, axis=-1)
    flat_e = top_e.reshape(-1).astype(jnp.int32)
    onehot = (flat_e[None, :] == jnp.arange(n_experts, dtype=jnp.int32)[:, None]).astype(jnp.int32)
    csum = jnp.cumsum(onehot, axis=1)
    counts = csum[:, -1]
    rank = jnp.sum((csum - onehot) * onehot, axis=0)
    nblk_e = (counts + tm - 1) // tm
    start = jnp.cumsum(counts) - counts
    bend = jnp.cumsum(nblk_e)
    bstart = bend - nblk_e
    pos = (jnp.sum(onehot * (bstart * tm)[:, None], axis=0) + rank).reshape(n, TOP_K)
    nblk = nk // tm + n_experts
    blk = jnp.arange(nblk, dtype=jnp.int32)
    blk_e = jnp.minimum(jnp.searchsorted(bend, blk, side="right"), n_experts - 1).astype(jnp.int32)
    blk_nv = jnp.clip(counts[blk_e] - (blk - bstart[blk_e]) * tm, 0, tm)
    blk_nv = jnp.where(blk < bend[-1], blk_nv, 0).astype(jnp.int32)
    order = jnp.argsort(flat_e, stable=True).astype(jnp.int32)
    in_blk = jnp.arange(tm, dtype=jnp.int32)[None, :]
    src = (start[blk_e] + (blk - bstart[blk_e]) * tm)[:, None] + in_blk
    src = jnp.where(in_blk < blk_nv[:, None], src, 0).reshape(-1)
    slot_tok = order.at[src].get(mode="promise_in_bounds") // TOP_K
    return gates, slot_tok, pos, blk_e, blk_nv


def kernel(x, c, ctx, c_ctx, mod_w, mod_b, norm_g, rwkv_mu, rwkv_w_rkv, rwkv_w0, rwkv_w1, rwkv_w2,
           rwkv_a0, rwkv_a1, rwkv_a2, rwkv_g1, rwkv_g2, rwkv_k_k, rwkv_k_a, rwkv_r_k, rwkv_ln_w,
           rwkv_ln_b, rwkv_w_out, conv_w_in, conv_w, conv_w_out, ffn_w_gu, ffn_w_down,
           moe_router, moe_w_gu, moe_w_down):
    b, t, d = x.shape
    n_experts = moe_router.shape[-1]
    rows = 16
    cs = jnp.zeros((rows, d), F32).at[:b].set(c).at[b].set(c_ctx)
    mods = _modulation(cs, mod_w, mod_b)
    mod0 = mods[0, :b].reshape(b, 1, 6 * d)
    mod0c = mods[0, b].reshape(1, 1, 6 * d)
    mod1 = mods[1, :b].reshape(b, 1, 6 * d)

    def pad_lora(w):
        zr = jnp.zeros_like(w[0])
        return jnp.stack([jnp.concatenate([w[0], zr], 0), jnp.concatenate([zr, w[1]], 0)]).astype(BF16)

    lg = rwkv_g1.shape[-1]
    lgp = -(-lg // LANES) * LANES
    p = {
        "ng": norm_g[0, 0].reshape(1, d),
        "mu": rwkv_mu[0],
        "wr": rwkv_w_rkv[0, 0].astype(BF16), "wk": rwkv_w_rkv[0, 1].astype(BF16),
        "wv": rwkv_w_rkv[0, 2].astype(BF16),
        "w1": jnp.concatenate([rwkv_w1[0, 0], rwkv_w1[0, 1]], axis=1).astype(BF16),
        "w2": pad_lora(rwkv_w2[0]), "w0": rwkv_w0[0],
        "a1": jnp.concatenate([rwkv_a1[0, 0], rwkv_a1[0, 1]], axis=1).astype(BF16),
        "a2": pad_lora(rwkv_a2[0]), "a0": rwkv_a0[0],
        "k_k": rwkv_k_k[0].reshape(1, d), "k_a": rwkv_k_a[0].reshape(1, d),
        "g1": jnp.pad(rwkv_g1[0], ((0, 0), (0, lgp - lg))).astype(BF16),
        "g2": jnp.pad(rwkv_g2[0], ((0, lgp - lg), (0, 0))).astype(BF16),
        "r_k": rwkv_r_k[0].reshape(1, d),
    }

    tok_c = _tokenwise(ctx, mod0c, "seq", False, p, ctx.shape[1])
    s_ctx = _scan(tok_c, p["k_a"], None, want_y=False, want_state=True)[0]
    tt = min(ROW_TILE, t)
    tm = min(FFN_ROWS, t)
    tf = min(FFN_COLS, ffn_w_down.shape[1])
    tok_l = _tokenwise(x, mod0, "grid", True, p, tt)
    yf, yb = _scan(tok_l[:8], p["k_a"], s_ctx, want_y=True, want_state=False)
    x1, h2 = _readout(yf, yb, tok_l[9], tok_l[8], x, mod0,
                      rwkv_ln_w[0].reshape(1, d), rwkv_ln_b[0].reshape(1, d),
                      rwkv_w_out[0].astype(BF16), norm_g[0, 1].reshape(1, d),
                      norm_g[0, 2].reshape(1, d), tt)

    x2, h3 = _ffn(h2.reshape(b * t, d), ffn_w_gu[0].astype(BF16), ffn_w_down[0].astype(BF16),
                  x1.reshape(b * t, d), mod0, norm_g[0, 3].reshape(1, d),
                  norm_g[1, 0].reshape(1, d), mod1, t, tm, tf)

    w_router = jnp.pad(moe_router[0], ((0, 0), (0, LANES - n_experts)))
    x3, h4, logits = _conv_layer(h3.reshape(b, t, d), x2.reshape(b, t, d), mod1,
                                 conv_w_in[0].astype(BF16), conv_w[0], conv_w_out[0].astype(BF16),
                                 norm_g[1, 1].reshape(1, d), norm_g[1, 2].reshape(1, d), w_router, tt)

    tme = min(MOE_ROWS, b * t * TOP_K)
    gates, slot_tok, pos, blk_e, blk_nv = _route(logits.reshape(b * t, LANES)[:, :n_experts],
                                                 n_experts, tme)
    tg = min(GATHER_ROWS, tme)
    live = (blk_nv[:, None] > jnp.arange(0, tme, tg, dtype=jnp.int32)[None, :]).astype(jnp.int32)
    xs = _gather_rows(h4.reshape(b * t, d), slot_tok, live.reshape(-1), tg)
    ys = _moe_experts(xs, blk_e, blk_nv, moe_w_gu[0], moe_w_down[0], tme,
                      min(MOE_COLS, moe_w_down.shape[2]))
    y0 = ys.at[pos[:, 0]].get(mode="promise_in_bounds").reshape(b, t, d)
    y1 = ys.at[pos[:, 1]].get(mode="promise_in_bounds").reshape(b, t, d)
    gates_p = jnp.pad(gates, ((0, 0), (0, LANES - TOP_K))).reshape(b, t, LANES)
    return _combine(y0, y1, gates_p, x3, mod1, norm_g[1, 3].reshape(1, d), tt)
```

```python
import functools
import math

import jax
import jax.numpy as jnp
from jax import lax
from jax.experimental import pallas as pl
from jax.experimental.pallas import tpu as pltpu

F32 = jnp.float32
BF16 = jnp.bfloat16

HEAD = 64
LANES = 128
GRID_W = 64
CHUNK = 64
SUB = 16
SCAN_PAIRS = 8
NORM_EPS = 1e-6
GN_EPS = 64e-5
DECAY_SCALE = math.exp(-0.5)
TOP_K = 2
VMEM_LIMIT = 56 * 1024 * 1024


ROW_TILE = 256
FFN_ROWS = 1024
FFN_COLS = 512
MOE_ROWS = 2560
MOE_COLS = 512


def _cparams(sem):
    return pltpu.CompilerParams(dimension_semantics=sem, vmem_limit_bytes=VMEM_LIMIT)


def _dot(a, b):
    return jnp.dot(a, b, preferred_element_type=F32)


def _dot_nt(a, b):
    return lax.dot_general(a, b, (((1,), (1,)), ((), ())), preferred_element_type=F32)


def _bdot(a, b):
    return _dot(a.astype(BF16), b.astype(BF16))


def _iota(shape, axis):
    return lax.broadcasted_iota(jnp.int32, shape, axis)


def _rms(x, g):
    return x * lax.rsqrt(jnp.mean(x * x, axis=-1, keepdims=True) + NORM_EPS) * g


def _silu(x):
    return x * jax.nn.sigmoid(x)


def _split_bf16(x):
    hi = x.astype(BF16)
    lo = (x - hi.astype(F32)).astype(BF16)
    return hi, lo


def _head_sum(x):
    ones = (_iota((LANES, LANES), 0) // HEAD == _iota((LANES, LANES), 1) // HEAD).astype(BF16)
    hi, lo = _split_bf16(x)
    outs = []
    for p in range(x.shape[-1] // LANES):
        sl = slice(p * LANES, (p + 1) * LANES)
        outs.append(_dot(hi[:, sl], ones) + _dot(lo[:, sl], ones))
    return jnp.concatenate(outs, axis=1)


def _mod_kernel(c_ref, w_ref, b_ref, o_ref):
    s = _silu(c_ref[...])
    o_ref[0] = jnp.dot(s, w_ref[0], preferred_element_type=F32,
                       precision=lax.Precision.HIGHEST) + b_ref[0]


def _modulation(cs, mod_w, mod_b):
    depth, d, n = mod_w.shape
    rows = cs.shape[0]
    tn = 1536
    return pl.pallas_call(
        _mod_kernel,
        grid=(depth, n // tn),
        in_specs=[pl.BlockSpec((rows, d), lambda i, j: (0, 0)),
                  pl.BlockSpec((1, d, tn), lambda i, j: (i, 0, j)),
                  pl.BlockSpec((1, 1, tn), lambda i, j: (i, 0, j))],
        out_specs=pl.BlockSpec((1, rows, tn), lambda i, j: (i, 0, j)),
        out_shape=jax.ShapeDtypeStruct((depth, rows, n), F32),
        compiler_params=_cparams(("arbitrary", "arbitrary")),
        name="modulation",
    )(cs, mod_w, mod_b.reshape(depth, 1, n))


def _tok_kernel(mode, readout, tT, nT, *refs):
    refs = list(refs)
    x_ref = refs.pop(0)
    if mode == "grid":
        xp_ref = refs.pop(0)
        xn_ref = refs.pop(0)
    (mod_ref, ng_ref, mu_ref, wr_ref, wk_ref, wv_ref, w1_ref, w2_ref, w0_ref,
     a1_ref, a2_ref, a0_ref, kkp_ref, ka_ref) = refs[:14]
    refs = refs[14:]
    if readout:
        g1_ref, g2_ref, rk_ref = refs[:3]
        refs = refs[3:]
    r_o, k_o, v_o, kk_o, ic0_o, ic1_o, lw0_o, lw1_o = refs[:8]
    refs = refs[8:]

    d = x_ref.shape[-1]
    sh = mod_ref[0, :, 0:d]
    sc = mod_ref[0, :, d:2 * d]
    g = ng_ref[...]

    def norm_mod(xx):
        return _rms(xx, g) * (1.0 + sc) + sh

    h = norm_mod(x_ref[0])
    if mode == "grid":
        i = pl.program_id(1)
        q = d // 4
        hp = jnp.where(i > 0, norm_mod(xp_ref[0]), 0.0)
        hn = jnp.where(i < nT - 1, norm_mod(xn_ref[0]), 0.0)
        col = _iota((tT, q), 0) % GRID_W
        left = jnp.where(col == 0, 0.0, pltpu.roll(h[:, 0:q], 1, 0))
        right = jnp.where(col == GRID_W - 1, 0.0, pltpu.roll(h[:, q:2 * q], tT - 1, 0))
        up = jnp.concatenate([hp[:, 2 * q:3 * q], h[:tT - GRID_W, 2 * q:3 * q]], axis=0)
        down = jnp.concatenate([h[GRID_W:, 3 * q:], hn[:, 3 * q:]], axis=0)
        hs = jnp.concatenate([left, right, up, down], axis=1)
    else:
        half = d // 2
        row = _iota((tT, half), 0)
        prev = jnp.where(row == 0, 0.0, pltpu.roll(h[:, :half], 1, 0))
        nxt = jnp.where(row == tT - 1, 0.0, pltpu.roll(h[:, half:], tT - 1, 0))
        hs = jnp.concatenate([prev, nxt], axis=1)

    dx = hs - h

    def mix(n):
        return (h + dx * mu_ref[n:n + 1, :]).astype(BF16)

    r = _dot(mix(0), wr_ref[...])
    k = _dot(mix(2), wk_ref[...])
    v = _dot(mix(3), wv_ref[...])
    w1o = jnp.tanh(_dot(mix(1), w1_ref[...])).astype(BF16)
    a1o = _dot(mix(4), a1_ref[...]).astype(BF16)
    ics = []
    for z, (lw_o, ic_o) in enumerate(((lw0_o, ic0_o), (lw1_o, ic1_o))):
        w_pre = _dot(w1o, w2_ref[z]) + w0_ref[z:z + 1, :]
        lw_o[0] = -DECAY_SCALE * jax.nn.sigmoid(w_pre)
        ic = jax.nn.sigmoid(_dot(a1o, a2_ref[z]) + a0_ref[z:z + 1, :])
        ic_o[0] = ic.astype(BF16)
        ics.append(ic)
    kk = k * kkp_ref[...]
    kk = kk * lax.rsqrt(jnp.maximum(_head_sum(kk * kk), 1e-24))
    r_o[0] = r.astype(BF16)
    k_o[0] = k.astype(BF16)
    v_o[0] = v.astype(BF16)
    kk_o[0] = kk.astype(BF16)
    if readout:
        gate_o, bon_o = refs
        gate = _dot(jax.nn.sigmoid(_dot(mix(5), g1_ref[...])).astype(BF16), g2_ref[...])
        gate_o[0] = gate.astype(BF16)
        k_avg = k * (1.0 + (0.5 * (ics[0] + ics[1]) - 1.0) * ka_ref[...])
        bon_o[0] = (_head_sum(r * k_avg * rk_ref[...]) * v).astype(BF16)


def _tokenwise(x, mod, mode, readout, p, tT):
    b, t, d = x.shape
    nT = t // tT
    row_spec = pl.BlockSpec((1, tT, d), lambda bi, i: (bi, i, 0))
    in_specs = [row_spec]
    args = [x]
    if mode == "grid":
        hb = tT // GRID_W
        in_specs += [
            pl.BlockSpec((1, GRID_W, d), lambda bi, i: (bi, jnp.maximum(i * hb - 1, 0), 0)),
            pl.BlockSpec((1, GRID_W, d), lambda bi, i: (bi, jnp.minimum((i + 1) * hb, t // GRID_W - 1), 0)),
        ]
        args += [x, x]
    if mod.shape[0] == 1:
        in_specs.append(pl.BlockSpec((1, 1, mod.shape[-1]), lambda bi, i: (0, 0, 0)))
    else:
        in_specs.append(pl.BlockSpec((1, 1, mod.shape[-1]), lambda bi, i: (bi, 0, 0)))
    args.append(mod)
    consts = [p["ng"], p["mu"], p["wr"], p["wk"], p["wv"], p["w1"], p["w2"], p["w0"],
              p["a1"], p["a2"], p["a0"], p["k_k"], p["k_a"]]
    if readout:
        consts += [p["g1"], p["g2"], p["r_k"]]
    for a in consts:
        in_specs.append(pl.BlockSpec(a.shape, lambda bi, i, _n=a.ndim: (0,) * _n))
        args.append(a)
    dtypes = [BF16] * 6 + [F32] * 2 + ([BF16] * 2 if readout else [])
    return pl.pallas_call(
        functools.partial(_tok_kernel, mode, readout, tT, nT),
        grid=(b, nT),
        in_specs=in_specs,
        out_specs=[row_spec] * len(dtypes),
        out_shape=[jax.ShapeDtypeStruct((b, t, d), dt) for dt in dtypes],
        compiler_params=_cparams(("arbitrary", "arbitrary")),
        name="rwkv_tokenwise_" + mode,
    )(*args)


def _scan_prep(r, kd, v, lw, kk, ic, reverse):
    c = r.shape[0]
    ti = _iota((c, c), 0)
    si = _iota((c, c), 1)
    tri = ((si >= ti) if reverse else (si <= ti)).astype(BF16)
    lw_hi, lw_lo = _split_bf16(lw)
    cum = _dot(tri, lw_hi) + _dot(tri, lw_lo)
    pc_log = cum[0:1] if reverse else cum[c - 1:c]
    p_inv = jnp.exp(-cum)
    p_rest = jnp.exp(pc_log - cum)
    b = kk * ic
    return {"rt": r * jnp.exp(cum), "at": -(kk * jnp.exp(cum - lw)),
            "bt": b * p_inv, "kt": kd * p_inv, "bp": b * p_rest, "kp": kd * p_rest,
            "v": v, "pc": jnp.exp(pc_log)}


def _scan_step(prep, states):
    c = prep[0]["v"].shape[0]
    n = 2 * c
    pp = len(states[0])
    chains = [(dr, p) for dr in range(2) for p in range(pp)]
    first = _iota((c, LANES), 1) < HEAD
    row = _iota((n, n), 0)
    col = _iota((n, n), 1)
    strict = (col < row, col > row)
    incl = (col <= row, col >= row)
    same = (row // SUB) == (col // SUB)

    def ms(name, dr, p):
        x = prep[dr][name][:, p * LANES:(p + 1) * LANES]
        return jnp.concatenate([jnp.where(first, x, 0.0), jnp.where(first, 0.0, x)], axis=0)

    def each(fn, *lists):
        return [fn(*xs) for xs in zip(*lists)]

    lhs = [jnp.concatenate([ms("at", dr, p), ms("rt", dr, p)], axis=0).astype(BF16) for dr, p in chains]
    rhs = [jnp.concatenate([ms("bt", dr, p), ms("kt", dr, p)], axis=0).astype(BF16) for dr, p in chains]
    v_ms = [ms("v", dr, p).astype(BF16) for dr, p in chains]
    gmat = each(_dot_nt, lhs, rhs)
    nmat = [jnp.where(strict[dr], g[:n, :n], 0.0) for (dr, _), g in zip(chains, gmat)]
    a_ak = [jnp.where(strict[dr], g[:n, n:], 0.0).astype(BF16) for (dr, _), g in zip(chains, gmat)]
    a_r = [jnp.where(jnp.concatenate([incl[dr], incl[dr]], axis=1), g[n:], 0.0).astype(BF16)
           for (dr, _), g in zip(chains, gmat)]

    nd = [jnp.where(same, x, 0.0) for x in nmat]
    no = [jnp.where(same, 0.0, x) for x in nmat]
    pw = nd
    tp = nd
    span = 2
    while span < SUB:
        pw = each(_bdot, pw, pw)
        tp = each(lambda t, q: t + q + _bdot(t, q), tp, pw)
        span *= 2
    m = each(lambda t, o: o + _bdot(t, o), tp, no)
    qp = m
    pw = m
    span = 2
    while span < c // SUB:
        pw = each(_bdot, pw, pw)
        qp = each(lambda t, q: t + q + _bdot(t, q), qp, pw)
        span *= 2
    tp = each(lambda t, q: t + q + _bdot(q, t), tp, qp)

    bk_t = [jnp.concatenate([ms("bp", dr, p), ms("kp", dr, p)], axis=0).T.astype(BF16)
            for dr, p in chains]
    diag = row == col
    pc_col = [jnp.sum(jnp.where(diag, prep[dr]["pc"][:, p * LANES:(p + 1) * LANES], 0.0),
                      axis=1, keepdims=True) for dr, p in chains]

    s_t = [states[dr][p].astype(BF16) for dr, p in chains]
    rhs_z = each(lambda l, a, st, v: _dot(jnp.concatenate([l[:n], a], axis=1),
                                          jnp.concatenate([st, v], axis=0)), lhs, a_ak, s_t, v_ms)
    z = each(lambda t, x: x + _bdot(t, x), tp, rhs_z)
    zv = each(lambda zz, v: jnp.concatenate([zz.astype(BF16), v], axis=0), z, v_ms)
    y_ms = each(lambda l, st, ar, w: _dot(l[n:], st) + _dot(ar, w), lhs, s_t, a_r, zv)
    y = [jnp.concatenate([ym[:c] + ym[c:] for (d2, _), ym in zip(chains, y_ms) if d2 == dr], axis=1)
         for dr in range(2)]
    new_states = [[None] * pp for _ in range(2)]
    for (dr, p), bt_, w, pc in zip(chains, bk_t, zv, pc_col):
        new_states[dr][p] = states[dr][p] * pc + _dot(bt_, w)
    return y, new_states


def _scan_kernel(pp, nc, has_s0, want_y, want_state, *refs):
    refs = list(refs)
    ka_ref = refs[12]
    dirs = (refs[:6], refs[6:12])
    refs = refs[13:]
    if has_s0:
        s0_ref = refs.pop(0)
    if want_y:
        y_refs = (refs.pop(0), refs.pop(0))
    if want_state:
        st_ref = refs.pop(0)
    s_scr = refs.pop(0)
    c = pl.program_id(2)

    @pl.when(c == 0)
    def _():
        if has_s0:
            s_scr[...] = s0_ref[0]
        else:
            s_scr[...] = jnp.zeros_like(s_scr)

    prep = []
    for dr, (r_ref, k_ref, v_ref, kk_ref, ic_ref, lw_ref) in enumerate(dirs):
        ic = ic_ref[0].astype(F32)
        kd = k_ref[0].astype(F32) * (1.0 + (ic - 1.0) * ka_ref[...])
        prep.append(_scan_prep(r_ref[0].astype(F32), kd, v_ref[0].astype(F32), lw_ref[0],
                               kk_ref[0].astype(F32), ic, reverse=(dr == 1)))
    states = [[s_scr[dr, p] for p in range(pp)] for dr in range(2)]
    y, new_states = _scan_step(prep, states)
    for dr in range(2):
        for p in range(pp):
            s_scr[dr, p] = new_states[dr][p]
        if want_y:
            y_refs[dr][0] = y[dr]

    if want_state:
        @pl.when(c == nc - 1)
        def _():
            st_ref[0] = s_scr[...]


def _scan(tok, ka, s0, want_y, want_state):
    r, k, v, kk, ic0, ic1, lw0, lw1 = tok
    b, t, d = r.shape
    nc = t // CHUNK
    npair = d // LANES
    pp = min(SCAN_PAIRS, npair)
    w = LANES * pp
    f_spec = pl.BlockSpec((1, CHUNK, w), lambda bi, hi, ci: (bi, ci, hi))
    b_spec = pl.BlockSpec((1, CHUNK, w), lambda bi, hi, ci: (bi, nc - 1 - ci, hi))
    st_spec = pl.BlockSpec((1, 2, pp, LANES, LANES), lambda bi, hi, ci: (bi, 0, hi, 0, 0))
    in_specs = [f_spec] * 6 + [b_spec] * 6 + [pl.BlockSpec((1, w), lambda bi, hi, ci: (0, hi))]
    args = [r, k, v, kk, ic0, lw0, r, k, v, kk, ic1, lw1, ka]
    if s0 is not None:
        in_specs.append(st_spec)
        args.append(s0)
    out_specs, out_shape = [], []
    if want_y:
        out_specs += [f_spec, b_spec]
        out_shape += [jax.ShapeDtypeStruct((b, t, d), F32)] * 2
    if want_state:
        out_specs.append(st_spec)
        out_shape.append(jax.ShapeDtypeStruct((b, 2, npair, LANES, LANES), F32))
    return pl.pallas_call(
        functools.partial(_scan_kernel, pp, nc, s0 is not None, want_y, want_state),
        grid=(b, npair // pp, nc),
        in_specs=in_specs,
        out_specs=out_specs,
        out_shape=out_shape,
        scratch_shapes=[pltpu.VMEM((2, pp, LANES, LANES), F32)],
        compiler_params=_cparams(("arbitrary", "arbitrary", "arbitrary")),
        name="rwkv_scan_ctx" if s0 is None else "rwkv_scan_latent",
    )(*args)


def _readout_kernel(yf_ref, yb_ref, bon_ref, gate_ref, x_ref, mod_ref, lnw_ref, lnb_ref,
                    wo_ref, g1_ref, g2_ref, x_o, h_o):
    d = x_ref.shape[-1]
    y = yf_ref[0] + yb_ref[0]
    mean = _head_sum(y) * (1.0 / HEAD)
    yc = y - mean
    var = _head_sum(yc * yc) * (1.0 / HEAD)
    o = yc * lax.rsqrt(var + GN_EPS) * lnw_ref[...] + lnb_ref[...] + bon_ref[0]
    att = _dot((o * gate_ref[0]).astype(BF16), wo_ref[...])
    gt_a = mod_ref[0, :, 2 * d:3 * d]
    sh_f = mod_ref[0, :, 3 * d:4 * d]
    sc_f = mod_ref[0, :, 4 * d:5 * d]
    x1 = x_ref[0] + gt_a * _rms(att, g1_ref[...])
    x_o[0] = x1
    h_o[0] = (_rms(x1, g2_ref[...]) * (1.0 + sc_f) + sh_f).astype(BF16)


def _readout(yf, yb, bon, gate, x, mod, lnw, lnb, wo, g1, g2, tT):
    b, t, d = x.shape
    row_spec = pl.BlockSpec((1, tT, d), lambda bi, i: (bi, i, 0))
    vec = pl.BlockSpec((1, d), lambda bi, i: (0, 0))
    return pl.pallas_call(
        _readout_kernel,
        grid=(b, t // tT),
        in_specs=[row_spec] * 5 + [pl.BlockSpec((1, 1, 6 * d), lambda bi, i: (bi, 0, 0)),
                                   vec, vec, pl.BlockSpec((d, d), lambda bi, i: (0, 0)), vec, vec],
        out_specs=[row_spec, row_spec],
        out_shape=[jax.ShapeDtypeStruct((b, t, d), F32), jax.ShapeDtypeStruct((b, t, d), BF16)],
        compiler_params=_cparams(("arbitrary", "arbitrary")),
        name="rwkv_readout",
    )(yf, yb, bon, gate, x, mod, lnw, lnb, wo, g1, g2)


def _ffn_kernel(nf, h_ref, wg_ref, wu_ref, wd_ref, x_ref, mod_ref, g3_ref, gn_ref, mod2_ref,
                x_o, h_o, acc):
    f = pl.program_id(1)
    d = x_ref.shape[-1]

    @pl.when(f == 0)
    def _():
        acc[...] = jnp.zeros_like(acc)

    h = h_ref[...]
    act = _silu(_dot(h, wg_ref[...])) * _dot(h, wu_ref[...])
    acc[...] += _dot(act.astype(BF16), wd_ref[...])

    @pl.when(f == nf - 1)
    def _():
        gt_f = mod_ref[0, :, 5 * d:6 * d]
        x2 = x_ref[...] + gt_f * _rms(acc[...], g3_ref[...])
        x_o[...] = x2
        sh = mod2_ref[0, :, 0:d]
        sc = mod2_ref[0, :, d:2 * d]
        h_o[...] = (_rms(x2, gn_ref[...]) * (1.0 + sc) + sh).astype(BF16)


def _ffn(h, w_gu, w_down, x, mod, g3, gn, mod2, rows_per_batch, tm, tf):
    n, d = x.shape
    ff = w_down.shape[0]
    nf = ff // tf
    per = rows_per_batch // tm
    row = pl.BlockSpec((tm, d), lambda i, f: (i, 0))
    vec = pl.BlockSpec((1, d), lambda i, f: (0, 0))
    modspec = pl.BlockSpec((1, 1, 6 * d), lambda i, f: (i // per, 0, 0))
    return pl.pallas_call(
        functools.partial(_ffn_kernel, nf),
        grid=(n // tm, nf),
        in_specs=[row,
                  pl.BlockSpec((d, tf), lambda i, f: (0, f)),
                  pl.BlockSpec((d, tf), lambda i, f: (0, nf + f)),
                  pl.BlockSpec((tf, d), lambda i, f: (f, 0)),
                  row, modspec, vec, vec, modspec],
        out_specs=[row, row],
        out_shape=[jax.ShapeDtypeStruct((n, d), F32), jax.ShapeDtypeStruct((n, d), BF16)],
        scratch_shapes=[pltpu.VMEM((tm, d), F32)],
        compiler_params=_cparams(("arbitrary", "arbitrary")),
        name="dense_swiglu",
    )(h, w_gu, w_gu, w_down, x, mod, g3, gn, mod2)


HALO = 16


def _conv_kernel(tT, nT, h_ref, hp_ref, hn_ref, x_ref, mod_ref, win_ref, cw_ref, wo_ref,
                 g1_ref, g2_ref, wr_ref, x_o, h_o, lg_o):
    i = pl.program_id(1)
    d = x_ref.shape[-1]
    n = tT + 2 * HALO
    h_ext = jnp.concatenate([hp_ref[0], h_ref[0], hn_ref[0]], axis=0)
    proj = _dot(h_ext, win_ref[...])
    z = proj[:, d:2 * d] * proj[:, 2 * d:]
    row = _iota((n, d), 0)
    dead = ((row < HALO) & (i == 0)) | ((row >= HALO + tT) & (i == nT - 1))
    z = jnp.where(dead, 0.0, z)
    conv = (pltpu.roll(z, 1, 0) * cw_ref[0:1, :] + z * cw_ref[1:2, :]
            + pltpu.roll(z, n - 1, 0) * cw_ref[2:3, :])
    gated = (proj[HALO:HALO + tT, 0:d] * conv[HALO:HALO + tT]).astype(BF16)
    y = _dot(gated, wo_ref[...])
    gt_a = mod_ref[0, :, 2 * d:3 * d]
    sh_f = mod_ref[0, :, 3 * d:4 * d]
    sc_f = mod_ref[0, :, 4 * d:5 * d]
    x3 = x_ref[0] + gt_a * _rms(y, g1_ref[...])
    x_o[0] = x3
    h4 = _rms(x3, g2_ref[...]) * (1.0 + sc_f) + sh_f
    h_o[0] = h4
    h_hi, h_lo = _split_bf16(h4)
    lg_o[0] = _dot(h_hi, wr_ref[0]) + _dot(h_lo, wr_ref[0]) + _dot(h_hi, wr_ref[1])


def _conv_layer(h, x, mod, w_in, conv_w, w_out, g1, g2, w_router, tT):
    b, t, d = x.shape
    nT = t // tT
    hb = tT // HALO
    row_spec = pl.BlockSpec((1, tT, d), lambda bi, i: (bi, i, 0))
    vec = pl.BlockSpec((1, d), lambda bi, i: (0, 0))
    full = lambda a: pl.BlockSpec(a.shape, lambda bi, i, _n=a.ndim: (0,) * _n)
    return pl.pallas_call(
        functools.partial(_conv_kernel, tT, nT),
        grid=(b, nT),
        in_specs=[row_spec,
                  pl.BlockSpec((1, HALO, d), lambda bi, i: (bi, jnp.maximum(i * hb - 1, 0), 0)),
                  pl.BlockSpec((1, HALO, d), lambda bi, i: (bi, jnp.minimum((i + 1) * hb, t // HALO - 1), 0)),
                  row_spec, pl.BlockSpec((1, 1, 6 * d), lambda bi, i: (bi, 0, 0)),
                  full(w_in), full(conv_w), full(w_out), vec, vec, full(w_router)],
        out_specs=[row_spec, row_spec, pl.BlockSpec((1, tT, LANES), lambda bi, i: (bi, i, 0))],
        out_shape=[jax.ShapeDtypeStruct((b, t, d), F32), jax.ShapeDtypeStruct((b, t, d), F32),
                   jax.ShapeDtypeStruct((b, t, LANES), F32)],
        compiler_params=_cparams(("arbitrary", "arbitrary")),
        name="short_conv",
    )(h, h, h, x, mod, w_in, conv_w, w_out, g1, g2, w_router)


GATHER_ROWS = 512


def _gather_kernel(tg, nsteps, live_ref, tok_cur, tok_nxt, h_hbm, o_ref, buf, sem):
    j = pl.program_id(0)
    slot = j % 2

    def row_copy(tok_ref, s, r):
        return pltpu.make_async_copy(h_hbm.at[pl.ds(tok_ref[0, 0, r], 1)],
                                     buf.at[s, pl.ds(r, 1)], sem.at[s])

    def issue(tok_ref, s):
        def body(r, carry):
            row_copy(tok_ref, s, r).start()
            return carry
        lax.fori_loop(0, tg, body, 0, unroll=8)

    @pl.when((j == 0) & (live_ref[0] > 0))
    def _():
        issue(tok_cur, 0)

    @pl.when(j + 1 < nsteps)
    def _():
        @pl.when(live_ref[j + 1] > 0)
        def _():
            issue(tok_nxt, 1 - slot)

    @pl.when(live_ref[j] > 0)
    def _():
        pltpu.make_async_copy(h_hbm.at[pl.ds(0, tg)], buf.at[slot], sem.at[slot]).wait()
        o_ref[...] = buf[slot].astype(BF16)

    @pl.when(live_ref[j] == 0)
    def _():
        o_ref[...] = jnp.zeros(o_ref.shape, BF16)


def _gather_rows(h, slot_tok, live, tg):
    n, d = h.shape
    nsteps = slot_tok.shape[0] // tg
    tok3 = slot_tok.reshape(nsteps, 1, tg)
    grid_spec = pltpu.PrefetchScalarGridSpec(
        num_scalar_prefetch=1,
        grid=(nsteps,),
        in_specs=[pl.BlockSpec((1, 1, tg), lambda j, lv: (j, 0, 0), memory_space=pltpu.SMEM),
                  pl.BlockSpec((1, 1, tg), lambda j, lv: (jnp.minimum(j + 1, nsteps - 1), 0, 0),
                               memory_space=pltpu.SMEM),
                  pl.BlockSpec(memory_space=pl.ANY)],
        out_specs=pl.BlockSpec((tg, d), lambda j, lv: (j, 0)),
        scratch_shapes=[pltpu.VMEM((2, tg, d), F32), pltpu.SemaphoreType.DMA((2,))],
    )
    return pl.pallas_call(
        functools.partial(_gather_kernel, tg, nsteps),
        grid_spec=grid_spec,
        out_shape=jax.ShapeDtypeStruct((nsteps * tg, d), BF16),
        compiler_params=_cparams(("arbitrary",)),
        name="moe_dispatch",
    )(live, tok3, tok3, h)


MOE_SUB = 512


def _moe_kernel(nf, tm, e_ref, nv_ref, x_ref, wg_ref, wu_ref, wd_ref, y_o):
    i = pl.program_id(0)
    f = pl.program_id(1)
    nv = nv_ref[i]

    nsub = tm // MOE_SUB

    @pl.when(f == 0)
    def _():
        y_o[...] = jnp.zeros(y_o.shape, F32)

    def weights():
        return wg_ref[0].astype(BF16), wu_ref[0].astype(BF16), wd_ref[0].astype(BF16)

    def sub_block(s, wg, wu, wd):
        rows = slice(s * MOE_SUB, (s + 1) * MOE_SUB)
        h = x_ref[rows, :]
        act = _silu(_dot(h, wg)) * _dot(h, wu)
        y_o[rows, :] += _dot(act.astype(BF16), wd)

    @pl.when(nv > (nsub - 1) * MOE_SUB)
    def _():
        w = weights()
        for s in range(nsub):
            sub_block(s, *w)

    @pl.when((nv > 0) & (nv <= (nsub - 1) * MOE_SUB))
    def _():
        w = weights()
        for s in range(nsub - 1):
            @pl.when(nv > s * MOE_SUB)
            def _():
                sub_block(s, *w)


def _moe_experts(xs, blk_e, blk_nv, w_gu, w_down, tm, tf):
    n, d = xs.shape
    ff = w_down.shape[1]
    nf = ff // tf
    nblk = n // tm

    def f_eff(i, f, nv_ref):
        return jnp.where(nv_ref[i] > 0, f, nf - 1)

    grid_spec = pltpu.PrefetchScalarGridSpec(
        num_scalar_prefetch=2,
        grid=(nblk, nf),
        in_specs=[pl.BlockSpec((tm, d), lambda i, f, e, nv: (i, 0)),
                  pl.BlockSpec((1, d, tf), lambda i, f, e, nv: (e[i], 0, f_eff(i, f, nv))),
                  pl.BlockSpec((1, d, tf), lambda i, f, e, nv: (e[i], 0, nf + f_eff(i, f, nv))),
                  pl.BlockSpec((1, tf, d), lambda i, f, e, nv: (e[i], f_eff(i, f, nv), 0))],
        out_specs=pl.BlockSpec((tm, d), lambda i, f, e, nv: (i, 0)),
    )
    return pl.pallas_call(
        functools.partial(_moe_kernel, nf, tm),
        grid_spec=grid_spec,
        out_shape=jax.ShapeDtypeStruct((n, d), F32),
        compiler_params=_cparams(("arbitrary", "arbitrary")),
        name="moe_experts",
    )(blk_e, blk_nv, xs, w_gu, w_gu, w_down)


def _combine_kernel(y0_ref, y1_ref, gt_ref, x_ref, mod_ref, g_ref, o_ref):
    d = x_ref.shape[-1]
    gates = gt_ref[0]
    y = y0_ref[0] * gates[:, 0:1] + y1_ref[0] * gates[:, 1:2]
    gt_f = mod_ref[0, :, 5 * d:6 * d]
    o_ref[0] = x_ref[0] + gt_f * _rms(y, g_ref[...])


def _combine(y0, y1, gates, x, mod, g, tT):
    b, t, d = x.shape
    row_spec = pl.BlockSpec((1, tT, d), lambda bi, i: (bi, i, 0))
    return pl.pallas_call(
        _combine_kernel,
        grid=(b, t // tT),
        in_specs=[row_spec, row_spec, pl.BlockSpec((1, tT, LANES), lambda bi, i: (bi, i, 0)),
                  row_spec, pl.BlockSpec((1, 1, 6 * d), lambda bi, i: (bi, 0, 0)),
                  pl.BlockSpec((1, d), lambda bi, i: (0, 0))],
        out_specs=row_spec,
        out_shape=jax.ShapeDtypeStruct((b, t, d), F32),
        compiler_params=_cparams(("arbitrary", "arbitrary")),
        name="moe_combine",
    )(y0, y1, gates, x, mod, g)


def _route(logits, n_experts, tm):
    n = logits.shape[0]
    nk = n * TOP_K
    top_v, top_e = lax.top_k(logits, TOP_K)
    gates = jax.nn.softmax(top_v, axis=-1)
    flat_e = top_e.reshape(-1).astype(jnp.int32)
    onehot = (flat_e[None, :] == jnp.arange(n_experts, dtype=jnp.int32)[:, None]).astype(jnp.int32)
    csum = jnp.cumsum(onehot, axis=1)
    counts = csum[:, -1]
    rank = jnp.sum((csum - onehot) * onehot, axis=0)
    nblk_e = (counts + tm - 1) // tm
    start = jnp.cumsum(counts) - counts
    bend = jnp.cumsum(nblk_e)
    bstart = bend - nblk_e
    pos = (jnp.sum(onehot * (bstart * tm)[:, None], axis=0) + rank).reshape(n, TOP_K)
    nblk = -(-nk // tm) + n_experts
    blk = jnp.arange(nblk, dtype=jnp.int32)
    blk_e = jnp.minimum(jnp.searchsorted(bend, blk, side="right"), n_experts - 1).astype(jnp.int32)
    blk_nv = jnp.clip(counts[blk_e] - (blk - bstart[blk_e]) * tm, 0, tm)
    blk_nv = jnp.where(blk < bend[-1], blk_nv, 0).astype(jnp.int32)
    order = jnp.argsort(flat_e, stable=True).astype(jnp.int32)
    in_blk = jnp.arange(tm, dtype=jnp.int32)[None, :]
    src = (start[blk_e] + (blk - bstart[blk_e]) * tm)[:, None] + in_blk
    src = jnp.where(in_blk < blk_nv[:, None], src, 0).reshape(-1)
    slot_tok = order.at[src].get(mode="promise_in_bounds") // TOP_K
    return gates, slot_tok, pos, blk_e, blk_nv


def kernel(x, c, ctx, c_ctx, mod_w, mod_b, norm_g, rwkv_mu, rwkv_w_rkv, rwkv_w0, rwkv_w1, rwkv_w2,
           rwkv_a0, rwkv_a1, rwkv_a2, rwkv_g1, rwkv_g2, rwkv_k_k, rwkv_k_a, rwkv_r_k, rwkv_ln_w,
           rwkv_ln_b, rwkv_w_out, conv_w_in, conv_w, conv_w_out, ffn_w_gu, ffn_w_down,
           moe_router, moe_w_gu, moe_w_down):
    b, t, d = x.shape
    n_experts = moe_router.shape[-1]
    rows = 16
    cs = jnp.zeros((rows, d), F32).at[:b].set(c).at[b].set(c_ctx)
    mods = _modulation(cs, mod_w, mod_b)
    mod0 = mods[0, :b].reshape(b, 1, 6 * d)
    mod0c = mods[0, b].reshape(1, 1, 6 * d)
    mod1 = mods[1, :b].reshape(b, 1, 6 * d)

    def pad_lora(w):
        zr = jnp.zeros_like(w[0])
        return jnp.stack([jnp.concatenate([w[0], zr], 0), jnp.concatenate([zr, w[1]], 0)]).astype(BF16)

    lg = rwkv_g1.shape[-1]
    lgp = -(-lg // LANES) * LANES
    p = {
        "ng": norm_g[0, 0].reshape(1, d),
        "mu": rwkv_mu[0],
        "wr": rwkv_w_rkv[0, 0].astype(BF16), "wk": rwkv_w_rkv[0, 1].astype(BF16),
        "wv": rwkv_w_rkv[0, 2].astype(BF16),
        "w1": jnp.concatenate([rwkv_w1[0, 0], rwkv_w1[0, 1]], axis=1).astype(BF16),
        "w2": pad_lora(rwkv_w2[0]), "w0": rwkv_w0[0],
        "a1": jnp.concatenate([rwkv_a1[0, 0], rwkv_a1[0, 1]], axis=1).astype(BF16),
        "a2": pad_lora(rwkv_a2[0]), "a0": rwkv_a0[0],
        "k_k": rwkv_k_k[0].reshape(1, d), "k_a": rwkv_k_a[0].reshape(1, d),
        "g1": jnp.pad(rwkv_g1[0], ((0, 0), (0, lgp - lg))).astype(BF16),
        "g2": jnp.pad(rwkv_g2[0], ((0, lgp - lg), (0, 0))).astype(BF16),
        "r_k": rwkv_r_k[0].reshape(1, d),
    }

    tok_c = _tokenwise(ctx, mod0c, "seq", False, p, ctx.shape[1])
    s_ctx = _scan(tok_c, p["k_a"], None, want_y=False, want_state=True)[0]
    tt = min(ROW_TILE, t)
    tm = min(FFN_ROWS, t)
    tf = min(FFN_COLS, ffn_w_down.shape[1])
    tok_l = _tokenwise(x, mod0, "grid", True, p, tt)
    yf, yb = _scan(tok_l[:8], p["k_a"], s_ctx, want_y=True, want_state=False)
    x1, h2 = _readout(yf, yb, tok_l[9], tok_l[8], x, mod0,
                      rwkv_ln_w[0].reshape(1, d), rwkv_ln_b[0].reshape(1, d),
                      rwkv_w_out[0].astype(BF16), norm_g[0, 1].reshape(1, d),
                      norm_g[0, 2].reshape(1, d), tt)

    x2, h3 = _ffn(h2.reshape(b * t, d), ffn_w_gu[0].astype(BF16), ffn_w_down[0].astype(BF16),
                  x1.reshape(b * t, d), mod0, norm_g[0, 3].reshape(1, d),
                  norm_g[1, 0].reshape(1, d), mod1, t, tm, tf)

    w_router = jnp.stack(_split_bf16(jnp.pad(moe_router[0], ((0, 0), (0, LANES - n_experts)))))
    x3, h4, logits = _conv_layer(h3.reshape(b, t, d), x2.reshape(b, t, d), mod1,
                                 conv_w_in[0].astype(BF16), conv_w[0], conv_w_out[0].astype(BF16),
                                 norm_g[1, 1].reshape(1, d), norm_g[1, 2].reshape(1, d), w_router, tt)

    tme = min(MOE_ROWS, b * t * TOP_K)
    gates, slot_tok, pos, blk_e, blk_nv = _route(logits.reshape(b * t, LANES)[:, :n_experts],
                                                 n_experts, tme)
    tg = min(GATHER_ROWS, tme)
    live = (blk_nv[:, None] > jnp.arange(0, tme, tg, dtype=jnp.int32)[None, :]).astype(jnp.int32)
    xs = _gather_rows(h4.reshape(b * t, d), slot_tok, live.reshape(-1), tg)
    ys = _moe_experts(xs, blk_e, blk_nv, moe_w_gu[0], moe_w_down[0], tme,
                      min(MOE_COLS, moe_w_down.shape[2]))
    y0 = ys.at[pos[:, 0]].get(mode="promise_in_bounds").reshape(b, t, d)
    y1 = ys.at[pos[:, 1]].get(mode="promise_in_bounds").reshape(b, t, d)
    gates_p = jnp.pad(gates, ((0, 0), (0, LANES - TOP_K))).reshape(b, t, LANES)
    return _combine(y0, y1, gates_p, x3, mod1, norm_g[1, 3].reshape(1, d), tt)
```

```python
import functools
import math

import jax
import jax.numpy as jnp
from jax import lax
from jax.experimental import pallas as pl
from jax.experimental.pallas import tpu as pltpu

F32 = jnp.float32
BF16 = jnp.bfloat16

HEAD = 64
LANES = 128
GRID_W = 64
CHUNK = 64
SUB = 16
SCAN_PAIRS = 8
NORM_EPS = 1e-6
GN_EPS = 64e-5
DECAY_SCALE = math.exp(-0.5)
TOP_K = 2
VMEM_LIMIT = 56 * 1024 * 1024


ROW_TILE = 256
FFN_ROWS = 1024
FFN_COLS = 512
MOE_ROWS = 2560
MOE_COLS = 512


def _cparams(sem):
    return pltpu.CompilerParams(dimension_semantics=sem, vmem_limit_bytes=VMEM_LIMIT)


def _dot(a, b):
    return jnp.dot(a, b, preferred_element_type=F32)


def _dot_nt(a, b):
    return lax.dot_general(a, b, (((1,), (1,)), ((), ())), preferred_element_type=F32)


def _bdot(a, b):
    return _dot(a.astype(BF16), b.astype(BF16))


def _iota(shape, axis):
    return lax.broadcasted_iota(jnp.int32, shape, axis)


def _rms(x, g):
    return x * lax.rsqrt(jnp.mean(x * x, axis=-1, keepdims=True) + NORM_EPS) * g


def _silu(x):
    return x * jax.nn.sigmoid(x)


def _split_bf16(x):
    hi = x.astype(BF16)
    lo = (x - hi.astype(F32)).astype(BF16)
    return hi, lo


def _head_sum(x):
    ones = (_iota((LANES, LANES), 0) // HEAD == _iota((LANES, LANES), 1) // HEAD).astype(BF16)
    hi, lo = _split_bf16(x)
    outs = []
    for p in range(x.shape[-1] // LANES):
        sl = slice(p * LANES, (p + 1) * LANES)
        outs.append(_dot(hi[:, sl], ones) + _dot(lo[:, sl], ones))
    return jnp.concatenate(outs, axis=1)


def _mod_kernel(c_ref, w_ref, b_ref, o_ref):
    s = _silu(c_ref[...])
    o_ref[0] = jnp.dot(s, w_ref[0], preferred_element_type=F32,
                       precision=lax.Precision.HIGHEST) + b_ref[0]


def _modulation(cs, mod_w, mod_b):
    depth, d, n = mod_w.shape
    rows = cs.shape[0]
    tn = 1536
    return pl.pallas_call(
        _mod_kernel,
        grid=(depth, n // tn),
        in_specs=[pl.BlockSpec((rows, d), lambda i, j: (0, 0)),
                  pl.BlockSpec((1, d, tn), lambda i, j: (i, 0, j)),
                  pl.BlockSpec((1, 1, tn), lambda i, j: (i, 0, j))],
        out_specs=pl.BlockSpec((1, rows, tn), lambda i, j: (i, 0, j)),
        out_shape=jax.ShapeDtypeStruct((depth, rows, n), F32),
        compiler_params=_cparams(("arbitrary", "arbitrary")),
        name="modulation",
    )(cs, mod_w, mod_b.reshape(depth, 1, n))


def _tok_kernel(mode, readout, tT, nT, *refs):
    refs = list(refs)
    x_ref = refs.pop(0)
    if mode == "grid":
        xp_ref = refs.pop(0)
        xn_ref = refs.pop(0)
    (mod_ref, ng_ref, mu_ref, wr_ref, wk_ref, wv_ref, w1_ref, w2_ref, w0_ref,
     a1_ref, a2_ref, a0_ref, kkp_ref, ka_ref) = refs[:14]
    refs = refs[14:]
    if readout:
        g1_ref, g2_ref, rk_ref = refs[:3]
        refs = refs[3:]
    r_o, k_o, v_o, kk_o, ic0_o, ic1_o, lw0_o, lw1_o = refs[:8]
    refs = refs[8:]

    d = x_ref.shape[-1]
    sh = mod_ref[0, :, 0:d]
    sc = mod_ref[0, :, d:2 * d]
    g = ng_ref[...]

    def norm_mod(xx):
        return _rms(xx, g) * (1.0 + sc) + sh

    h = norm_mod(x_ref[0])
    if mode == "grid":
        i = pl.program_id(1)
        q = d // 4
        hp = jnp.where(i > 0, norm_mod(xp_ref[0]), 0.0)
        hn = jnp.where(i < nT - 1, norm_mod(xn_ref[0]), 0.0)
        col = _iota((tT, q), 0) % GRID_W
        left = jnp.where(col == 0, 0.0, pltpu.roll(h[:, 0:q], 1, 0))
        right = jnp.where(col == GRID_W - 1, 0.0, pltpu.roll(h[:, q:2 * q], tT - 1, 0))
        up = jnp.concatenate([hp[:, 2 * q:3 * q], h[:tT - GRID_W, 2 * q:3 * q]], axis=0)
        down = jnp.concatenate([h[GRID_W:, 3 * q:], hn[:, 3 * q:]], axis=0)
        hs = jnp.concatenate([left, right, up, down], axis=1)
    else:
        half = d // 2
        row = _iota((tT, half), 0)
        prev = jnp.where(row == 0, 0.0, pltpu.roll(h[:, :half], 1, 0))
        nxt = jnp.where(row == tT - 1, 0.0, pltpu.roll(h[:, half:], tT - 1, 0))
        hs = jnp.concatenate([prev, nxt], axis=1)

    dx = hs - h

    def mix(n):
        return (h + dx * mu_ref[n:n + 1, :]).astype(BF16)

    r = _dot(mix(0), wr_ref[...])
    k = _dot(mix(2), wk_ref[...])
    v = _dot(mix(3), wv_ref[...])
    w1o = jnp.tanh(_dot(mix(1), w1_ref[...])).astype(BF16)
    a1o = _dot(mix(4), a1_ref[...]).astype(BF16)
    ics = []
    for z, (lw_o, ic_o) in enumerate(((lw0_o, ic0_o), (lw1_o, ic1_o))):
        w_pre = _dot(w1o, w2_ref[z]) + w0_ref[z:z + 1, :]
        lw_o[0] = -DECAY_SCALE * jax.nn.sigmoid(w_pre)
        ic = jax.nn.sigmoid(_dot(a1o, a2_ref[z]) + a0_ref[z:z + 1, :])
        ic_o[0] = ic.astype(BF16)
        ics.append(ic)
    kk = k * kkp_ref[...]
    kk = kk * lax.rsqrt(jnp.maximum(_head_sum(kk * kk), 1e-24))
    r_o[0] = r.astype(BF16)
    k_o[0] = k.astype(BF16)
    v_o[0] = v.astype(BF16)
    kk_o[0] = kk.astype(BF16)
    if readout:
        gate_o, bon_o = refs
        gate = _dot(jax.nn.sigmoid(_dot(mix(5), g1_ref[...])).astype(BF16), g2_ref[...])
        gate_o[0] = gate.astype(BF16)
        k_avg = k * (1.0 + (0.5 * (ics[0] + ics[1]) - 1.0) * ka_ref[...])
        bon_o[0] = (_head_sum(r * k_avg * rk_ref[...]) * v).astype(BF16)


def _tokenwise(x, mod, mode, readout, p, tT):
    b, t, d = x.shape
    nT = t // tT
    row_spec = pl.BlockSpec((1, tT, d), lambda bi, i: (bi, i, 0))
    in_specs = [row_spec]
    args = [x]
    if mode == "grid":
        hb = tT // GRID_W
        in_specs += [
            pl.BlockSpec((1, GRID_W, d), lambda bi, i: (bi, jnp.maximum(i * hb - 1, 0), 0)),
            pl.BlockSpec((1, GRID_W, d), lambda bi, i: (bi, jnp.minimum((i + 1) * hb, t // GRID_W - 1), 0)),
        ]
        args += [x, x]
    if mod.shape[0] == 1:
        in_specs.append(pl.BlockSpec((1, 1, mod.shape[-1]), lambda bi, i: (0, 0, 0)))
    else:
        in_specs.append(pl.BlockSpec((1, 1, mod.shape[-1]), lambda bi, i: (bi, 0, 0)))
    args.append(mod)
    consts = [p["ng"], p["mu"], p["wr"], p["wk"], p["wv"], p["w1"], p["w2"], p["w0"],
              p["a1"], p["a2"], p["a0"], p["k_k"], p["k_a"]]
    if readout:
        consts += [p["g1"], p["g2"], p["r_k"]]
    for a in consts:
        in_specs.append(pl.BlockSpec(a.shape, lambda bi, i, _n=a.ndim: (0,) * _n))
        args.append(a)
    dtypes = [BF16] * 6 + [F32] * 2 + ([BF16] * 2 if readout else [])
    return pl.pallas_call(
        functools.partial(_tok_kernel, mode, readout, tT, nT),
        grid=(b, nT),
        in_specs=in_specs,
        out_specs=[row_spec] * len(dtypes),
        out_shape=[jax.ShapeDtypeStruct((b, t, d), dt) for dt in dtypes],
        compiler_params=_cparams(("arbitrary", "arbitrary")),
        name="rwkv_tokenwise_" + mode,
    )(*args)


def _scan_prep(r, kd, v, lw, kk, ic, reverse):
    c = r.shape[0]
    ti = _iota((c, c), 0)
    si = _iota((c, c), 1)
    tri = ((si >= ti) if reverse else (si <= ti)).astype(BF16)
    lw_hi, lw_lo = _split_bf16(lw)
    cum = _dot(tri, lw_hi) + _dot(tri, lw_lo)
    pc_log = cum[0:1] if reverse else cum[c - 1:c]
    p_inv = jnp.exp(-cum)
    p_rest = jnp.exp(pc_log - cum)
    b = kk * ic
    return {"rt": r * jnp.exp(cum), "at": -(kk * jnp.exp(cum - lw)),
            "bt": b * p_inv, "kt": kd * p_inv, "bp": b * p_rest, "kp": kd * p_rest,
            "v": v, "pc": jnp.exp(pc_log)}


def _scan_step(prep, states):
    c = prep[0]["v"].shape[0]
    n = 2 * c
    pp = len(states[0])
    chains = [(dr, p) for dr in range(2) for p in range(pp)]
    first = _iota((c, LANES), 1) < HEAD
    row = _iota((n, n), 0)
    col = _iota((n, n), 1)
    strict = (col < row, col > row)
    same = (row // SUB) == (col // SUB)

    def pair(name, dr, p):
        return prep[dr][name][:, p * LANES:(p + 1) * LANES]

    def stack(x):
        return jnp.concatenate([jnp.where(first, x, 0.0), jnp.where(first, 0.0, x)], axis=0)

    def ms(name, dr, p):
        return stack(pair(name, dr, p))

    def each(fn, *lists):
        return [fn(*xs) for xs in zip(*lists)]

    lhs = [jnp.concatenate([pair("at", dr, p), pair("rt", dr, p)], axis=0).astype(BF16)
           for dr, p in chains]
    rhs = [jnp.concatenate([ms("bt", dr, p), ms("kt", dr, p)], axis=0).astype(BF16) for dr, p in chains]
    at_ms = [ms("at", dr, p).astype(BF16) for dr, p in chains]
    v_ms = [ms("v", dr, p).astype(BF16) for dr, p in chains]
    gmat = each(_dot_nt, lhs, rhs)
    nmat = [jnp.where(strict[dr], stack(g[:c, :n]), 0.0) for (dr, _), g in zip(chains, gmat)]
    a_ak = [jnp.where(strict[dr], stack(g[:c, n:]), 0.0).astype(BF16) for (dr, _), g in zip(chains, gmat)]
    t_pl = _iota((c, 2 * n), 0)
    s_pl = _iota((c, 2 * n), 1) % c
    incl_pl = (s_pl <= t_pl, s_pl >= t_pl)
    a_r = [jnp.where(incl_pl[dr], g[c:], 0.0).astype(BF16)
           for (dr, _), g in zip(chains, gmat)]

    nd = [jnp.where(same, x, 0.0) for x in nmat]
    no = [jnp.where(same, 0.0, x) for x in nmat]
    pw = nd
    tp = nd
    span = 2
    while span < SUB:
        pw = each(_bdot, pw, pw)
        tp = each(lambda t, q: t + q + _bdot(t, q), tp, pw)
        span *= 2
    m = each(lambda t, o: o + _bdot(t, o), tp, no)
    qp = m
    pw = m
    span = 2
    while span < c // SUB:
        pw = each(_bdot, pw, pw)
        qp = each(lambda t, q: t + q + _bdot(t, q), qp, pw)
        span *= 2
    tp = each(lambda t, q: t + q + _bdot(q, t), tp, qp)

    bk_t = [jnp.concatenate([ms("bp", dr, p), ms("kp", dr, p)], axis=0).T.astype(BF16)
            for dr, p in chains]
    diag = row == col
    pc_col = [jnp.sum(jnp.where(diag, prep[dr]["pc"][:, p * LANES:(p + 1) * LANES], 0.0),
                      axis=1, keepdims=True) for dr, p in chains]

    s_t = [states[dr][p].astype(BF16) for dr, p in chains]
    rhs_z = each(lambda l, a, st, v: _dot(jnp.concatenate([l, a], axis=1),
                                          jnp.concatenate([st, v], axis=0)), at_ms, a_ak, s_t, v_ms)
    z = each(lambda t, x: x + _bdot(t, x), tp, rhs_z)
    zv = each(lambda zz, v: jnp.concatenate([zz.astype(BF16), v], axis=0), z, v_ms)
    y_pl = each(lambda l, st, ar, w: _dot(l[c:], st) + _dot(ar, w), lhs, s_t, a_r, zv)
    y = [jnp.concatenate([yy for (d2, _), yy in zip(chains, y_pl) if d2 == dr], axis=1)
         for dr in range(2)]
    new_states = [[None] * pp for _ in range(2)]
    for (dr, p), bt_, w, pc in zip(chains, bk_t, zv, pc_col):
        new_states[dr][p] = states[dr][p] * pc + _dot(bt_, w)
    return y, new_states


def _scan_kernel(pp, nc, has_s0, want_y, want_state, *refs):
    refs = list(refs)
    ka_ref = refs[12]
    dirs = (refs[:6], refs[6:12])
    refs = refs[13:]
    if has_s0:
        s0_ref = refs.pop(0)
    if want_y:
        y_refs = (refs.pop(0), refs.pop(0))
    if want_state:
        st_ref = refs.pop(0)
    s_scr = refs.pop(0)
    c = pl.program_id(2)

    @pl.when(c == 0)
    def _():
        if has_s0:
            s_scr[...] = s0_ref[0]
        else:
            s_scr[...] = jnp.zeros_like(s_scr)

    prep = []
    for dr, (r_ref, k_ref, v_ref, kk_ref, ic_ref, lw_ref) in enumerate(dirs):
        ic = ic_ref[0].astype(F32)
        kd = k_ref[0].astype(F32) * (1.0 + (ic - 1.0) * ka_ref[...])
        prep.append(_scan_prep(r_ref[0].astype(F32), kd, v_ref[0].astype(F32), lw_ref[0],
                               kk_ref[0].astype(F32), ic, reverse=(dr == 1)))
    states = [[s_scr[dr, p] for p in range(pp)] for dr in range(2)]
    y, new_states = _scan_step(prep, states)
    for dr in range(2):
        for p in range(pp):
            s_scr[dr, p] = new_states[dr][p]
        if want_y:
            y_refs[dr][0] = y[dr]

    if want_state:
        @pl.when(c == nc - 1)
        def _():
            st_ref[0] = s_scr[...]


def _scan(tok, ka, s0, want_y, want_state):
    r, k, v, kk, ic0, ic1, lw0, lw1 = tok
    b, t, d = r.shape
    nc = t // CHUNK
    npair = d // LANES
    pp = min(SCAN_PAIRS, npair)
    w = LANES * pp
    f_spec = pl.BlockSpec((1, CHUNK, w), lambda bi, hi, ci: (bi, ci, hi))
    b_spec = pl.BlockSpec((1, CHUNK, w), lambda bi, hi, ci: (bi, nc - 1 - ci, hi))
    st_spec = pl.BlockSpec((1, 2, pp, LANES, LANES), lambda bi, hi, ci: (bi, 0, hi, 0, 0))
    in_specs = [f_spec] * 6 + [b_spec] * 6 + [pl.BlockSpec((1, w), lambda bi, hi, ci: (0, hi))]
    args = [r, k, v, kk, ic0, lw0, r, k, v, kk, ic1, lw1, ka]
    if s0 is not None:
        in_specs.append(st_spec)
        args.append(s0)
    out_specs, out_shape = [], []
    if want_y:
        out_specs += [f_spec, b_spec]
        out_shape += [jax.ShapeDtypeStruct((b, t, d), F32)] * 2
    if want_state:
        out_specs.append(st_spec)
        out_shape.append(jax.ShapeDtypeStruct((b, 2, npair, LANES, LANES), F32))
    return pl.pallas_call(
        functools.partial(_scan_kernel, pp, nc, s0 is not None, want_y, want_state),
        grid=(b, npair // pp, nc),
        in_specs=in_specs,
        out_specs=out_specs,
        out_shape=out_shape,
        scratch_shapes=[pltpu.VMEM((2, pp, LANES, LANES), F32)],
        compiler_params=_cparams(("arbitrary", "arbitrary", "arbitrary")),
        name="rwkv_scan_ctx" if s0 is None else "rwkv_scan_latent",
    )(*args)


def _readout_kernel(yf_ref, yb_ref, bon_ref, gate_ref, x_ref, mod_ref, lnw_ref, lnb_ref,
                    wo_ref, g1_ref, g2_ref, x_o, h_o):
    d = x_ref.shape[-1]
    y = yf_ref[0] + yb_ref[0]
    mean = _head_sum(y) * (1.0 / HEAD)
    yc = y - mean
    var = _head_sum(yc * yc) * (1.0 / HEAD)
    o = yc * lax.rsqrt(var + GN_EPS) * lnw_ref[...] + lnb_ref[...] + bon_ref[0]
    att = _dot((o * gate_ref[0]).astype(BF16), wo_ref[...])
    gt_a = mod_ref[0, :, 2 * d:3 * d]
    sh_f = mod_ref[0, :, 3 * d:4 * d]
    sc_f = mod_ref[0, :, 4 * d:5 * d]
    x1 = x_ref[0] + gt_a * _rms(att, g1_ref[...])
    x_o[0] = x1
    h_o[0] = (_rms(x1, g2_ref[...]) * (1.0 + sc_f) + sh_f).astype(BF16)


def _readout(yf, yb, bon, gate, x, mod, lnw, lnb, wo, g1, g2, tT):
    b, t, d = x.shape
    row_spec = pl.BlockSpec((1, tT, d), lambda bi, i: (bi, i, 0))
    vec = pl.BlockSpec((1, d), lambda bi, i: (0, 0))
    return pl.pallas_call(
        _readout_kernel,
        grid=(b, t // tT),
        in_specs=[row_spec] * 5 + [pl.BlockSpec((1, 1, 6 * d), lambda bi, i: (bi, 0, 0)),
                                   vec, vec, pl.BlockSpec((d, d), lambda bi, i: (0, 0)), vec, vec],
        out_specs=[row_spec, row_spec],
        out_shape=[jax.ShapeDtypeStruct((b, t, d), F32), jax.ShapeDtypeStruct((b, t, d), BF16)],
        compiler_params=_cparams(("arbitrary", "arbitrary")),
        name="rwkv_readout",
    )(yf, yb, bon, gate, x, mod, lnw, lnb, wo, g1, g2)


def _ffn_kernel(nf, h_ref, wg_ref, wu_ref, wd_ref, x_ref, mod_ref, g3_ref, gn_ref, mod2_ref,
                x_o, h_o, acc):
    f = pl.program_id(1)
    d = x_ref.shape[-1]

    @pl.when(f == 0)
    def _():
        acc[...] = jnp.zeros_like(acc)

    h = h_ref[...]
    act = _silu(_dot(h, wg_ref[...])) * _dot(h, wu_ref[...])
    acc[...] += _dot(act.astype(BF16), wd_ref[...])

    @pl.when(f == nf - 1)
    def _():
        gt_f = mod_ref[0, :, 5 * d:6 * d]
        x2 = x_ref[...] + gt_f * _rms(acc[...], g3_ref[...])
        x_o[...] = x2
        sh = mod2_ref[0, :, 0:d]
        sc = mod2_ref[0, :, d:2 * d]
        h_o[...] = (_rms(x2, gn_ref[...]) * (1.0 + sc) + sh).astype(BF16)


def _ffn(h, w_gu, w_down, x, mod, g3, gn, mod2, rows_per_batch, tm, tf):
    n, d = x.shape
    ff = w_down.shape[0]
    nf = ff // tf
    per = rows_per_batch // tm
    row = pl.BlockSpec((tm, d), lambda i, f: (i, 0))
    vec = pl.BlockSpec((1, d), lambda i, f: (0, 0))
    modspec = pl.BlockSpec((1, 1, 6 * d), lambda i, f: (i // per, 0, 0))
    return pl.pallas_call(
        functools.partial(_ffn_kernel, nf),
        grid=(n // tm, nf),
        in_specs=[row,
                  pl.BlockSpec((d, tf), lambda i, f: (0, f)),
                  pl.BlockSpec((d, tf), lambda i, f: (0, nf + f)),
                  pl.BlockSpec((tf, d), lambda i, f: (f, 0)),
                  row, modspec, vec, vec, modspec],
        out_specs=[row, row],
        out_shape=[jax.ShapeDtypeStruct((n, d), F32), jax.ShapeDtypeStruct((n, d), BF16)],
        scratch_shapes=[pltpu.VMEM((tm, d), F32)],
        compiler_params=_cparams(("arbitrary", "arbitrary")),
        name="dense_swiglu",
    )(h, w_gu, w_gu, w_down, x, mod, g3, gn, mod2)


HALO = 16


def _conv_kernel(tT, nT, h_ref, hp_ref, hn_ref, x_ref, mod_ref, win_ref, cw_ref, wo_ref,
                 g1_ref, g2_ref, wr_ref, x_o, h_o, lg_o):
    i = pl.program_id(1)
    d = x_ref.shape[-1]
    n = tT + 2 * HALO
    h_ext = jnp.concatenate([hp_ref[0], h_ref[0], hn_ref[0]], axis=0)
    proj = _dot(h_ext, win_ref[...])
    z = proj[:, d:2 * d] * proj[:, 2 * d:]
    row = _iota((n, d), 0)
    dead = ((row < HALO) & (i == 0)) | ((row >= HALO + tT) & (i == nT - 1))
    z = jnp.where(dead, 0.0, z)
    conv = (pltpu.roll(z, 1, 0) * cw_ref[0:1, :] + z * cw_ref[1:2, :]
            + pltpu.roll(z, n - 1, 0) * cw_ref[2:3, :])
    gated = (proj[HALO:HALO + tT, 0:d] * conv[HALO:HALO + tT]).astype(BF16)
    y = _dot(gated, wo_ref[...])
    gt_a = mod_ref[0, :, 2 * d:3 * d]
    sh_f = mod_ref[0, :, 3 * d:4 * d]
    sc_f = mod_ref[0, :, 4 * d:5 * d]
    x3 = x_ref[0] + gt_a * _rms(y, g1_ref[...])
    x_o[0] = x3
    h4 = _rms(x3, g2_ref[...]) * (1.0 + sc_f) + sh_f
    s_rows = d // LANES
    for c in range(s_rows):
        h_o[0, pl.ds(c, tT, stride=s_rows), :] = h4[:, c * LANES:(c + 1) * LANES]
    h_hi, h_lo = _split_bf16(h4)
    lg_o[0] = _dot(h_hi, wr_ref[0]) + _dot(h_lo, wr_ref[0]) + _dot(h_hi, wr_ref[1])


def _conv_layer(h, x, mod, w_in, conv_w, w_out, g1, g2, w_router, tT):
    b, t, d = x.shape
    nT = t // tT
    hb = tT // HALO
    row_spec = pl.BlockSpec((1, tT, d), lambda bi, i: (bi, i, 0))
    vec = pl.BlockSpec((1, d), lambda bi, i: (0, 0))
    full = lambda a: pl.BlockSpec(a.shape, lambda bi, i, _n=a.ndim: (0,) * _n)
    return pl.pallas_call(
        functools.partial(_conv_kernel, tT, nT),
        grid=(b, nT),
        in_specs=[row_spec,
                  pl.BlockSpec((1, HALO, d), lambda bi, i: (bi, jnp.maximum(i * hb - 1, 0), 0)),
                  pl.BlockSpec((1, HALO, d), lambda bi, i: (bi, jnp.minimum((i + 1) * hb, t // HALO - 1), 0)),
                  row_spec, pl.BlockSpec((1, 1, 6 * d), lambda bi, i: (bi, 0, 0)),
                  full(w_in), full(conv_w), full(w_out), vec, vec, full(w_router)],
        out_specs=[row_spec, pl.BlockSpec((1, tT * (d // LANES), LANES), lambda bi, i: (bi, i, 0)),
                   pl.BlockSpec((1, tT, LANES), lambda bi, i: (bi, i, 0))],
        out_shape=[jax.ShapeDtypeStruct((b, t, d), F32),
                   jax.ShapeDtypeStruct((b, t * (d // LANES), LANES), F32),
                   jax.ShapeDtypeStruct((b, t, LANES), F32)],
        compiler_params=_cparams(("arbitrary", "arbitrary")),
        name="short_conv",
    )(h, h, h, x, mod, w_in, conv_w, w_out, g1, g2, w_router)


GATHER_ROWS = 512


def _gather_kernel(tg, nsteps, live_ref, tok_cur, tok_nxt, h_hbm, o_ref, buf, sem):
    j = pl.program_id(0)
    slot = j % 2
    s_rows = buf.shape[1] // tg

    def issue(tok_ref, s):
        for r in range(tg):
            src = pl.multiple_of(tok_ref[0, 0, r] * s_rows, s_rows)
            pltpu.make_async_copy(h_hbm.at[pl.ds(src, s_rows)],
                                  buf.at[s, pl.ds(r * s_rows, s_rows)], sem.at[s]).start()

    @pl.when((j == 0) & (live_ref[0] > 0))
    def _():
        issue(tok_cur, 0)

    @pl.when(j + 1 < nsteps)
    def _():
        @pl.when(live_ref[j + 1] > 0)
        def _():
            issue(tok_nxt, 1 - slot)

    @pl.when(live_ref[j] > 0)
    def _():
        pltpu.make_async_copy(h_hbm.at[pl.ds(0, tg * s_rows)], buf.at[slot], sem.at[slot]).wait()
        for c in range(s_rows):
            o_ref[:, c * LANES:(c + 1) * LANES] = buf[slot, pl.ds(c, tg, stride=s_rows), :].astype(BF16)

    @pl.when(live_ref[j] == 0)
    def _():
        o_ref[...] = jnp.zeros(o_ref.shape, BF16)


def _gather_rows(h, d, slot_tok, live, tg):
    s_rows = d // LANES
    nsteps = slot_tok.shape[0] // tg
    tok3 = slot_tok.reshape(nsteps, 1, tg)
    grid_spec = pltpu.PrefetchScalarGridSpec(
        num_scalar_prefetch=1,
        grid=(nsteps,),
        in_specs=[pl.BlockSpec((1, 1, tg), lambda j, lv: (j, 0, 0), memory_space=pltpu.SMEM),
                  pl.BlockSpec((1, 1, tg), lambda j, lv: (jnp.minimum(j + 1, nsteps - 1), 0, 0),
                               memory_space=pltpu.SMEM),
                  pl.BlockSpec(memory_space=pl.ANY)],
        out_specs=pl.BlockSpec((tg, d), lambda j, lv: (j, 0)),
        scratch_shapes=[pltpu.VMEM((2, tg * s_rows, LANES), F32), pltpu.SemaphoreType.DMA((2,))],
    )
    return pl.pallas_call(
        functools.partial(_gather_kernel, tg, nsteps),
        grid_spec=grid_spec,
        out_shape=jax.ShapeDtypeStruct((nsteps * tg, d), BF16),
        compiler_params=_cparams(("arbitrary",)),
        name="moe_dispatch",
    )(live, tok3, tok3, h)


MOE_SUB = 512


def _moe_kernel(nf, tm, e_ref, nv_ref, x_ref, wg_ref, wu_ref, wd_ref, y_o):
    i = pl.program_id(0)
    f = pl.program_id(1)
    nv = nv_ref[i]

    nsub = tm // MOE_SUB

    @pl.when(f == 0)
    def _():
        y_o[...] = jnp.zeros(y_o.shape, F32)

    def weights():
        return wg_ref[0].astype(BF16), wu_ref[0].astype(BF16), wd_ref[0].astype(BF16)

    def sub_block(s, wg, wu, wd):
        rows = slice(s * MOE_SUB, (s + 1) * MOE_SUB)
        h = x_ref[rows, :]
        act = _silu(_dot(h, wg)) * _dot(h, wu)
        y_o[rows, :] += _dot(act.astype(BF16), wd)

    @pl.when(nv > (nsub - 1) * MOE_SUB)
    def _():
        w = weights()
        for s in range(nsub):
            sub_block(s, *w)

    @pl.when((nv > 0) & (nv <= (nsub - 1) * MOE_SUB))
    def _():
        w = weights()
        for s in range(nsub - 1):
            @pl.when(nv > s * MOE_SUB)
            def _():
                sub_block(s, *w)


def _moe_experts(xs, blk_e, blk_nv, w_gu, w_down, tm, tf):
    n, d = xs.shape
    ff = w_down.shape[1]
    nf = ff // tf
    nblk = n // tm

    def f_eff(i, f, nv_ref):
        return jnp.where(nv_ref[i] > 0, f, nf - 1)

    grid_spec = pltpu.PrefetchScalarGridSpec(
        num_scalar_prefetch=2,
        grid=(nblk, nf),
        in_specs=[pl.BlockSpec((tm, d), lambda i, f, e, nv: (i, 0)),
                  pl.BlockSpec((1, d, tf), lambda i, f, e, nv: (e[i], 0, f_eff(i, f, nv))),
                  pl.BlockSpec((1, d, tf), lambda i, f, e, nv: (e[i], 0, nf + f_eff(i, f, nv))),
                  pl.BlockSpec((1, tf, d), lambda i, f, e, nv: (e[i], f_eff(i, f, nv), 0))],
        out_specs=pl.BlockSpec((tm, d), lambda i, f, e, nv: (i, 0)),
    )
    return pl.pallas_call(
        functools.partial(_moe_kernel, nf, tm),
        grid_spec=grid_spec,
        out_shape=jax.ShapeDtypeStruct((n, d), F32),
        compiler_params=_cparams(("arbitrary", "arbitrary")),
        name="moe_experts",
    )(blk_e, blk_nv, xs, w_gu, w_gu, w_down)


def _combine_kernel(y0_ref, y1_ref, gt_ref, x_ref, mod_ref, g_ref, o_ref):
    d = x_ref.shape[-1]
    gates = gt_ref[0]
    y = y0_ref[0] * gates[:, 0:1] + y1_ref[0] * gates[:, 1:2]
    gt_f = mod_ref[0, :, 5 * d:6 * d]
    o_ref[0] = x_ref[0] + gt_f * _rms(y, g_ref[...])


def _combine(y0, y1, gates, x, mod, g, tT):
    b, t, d = x.shape
    row_spec = pl.BlockSpec((1, tT, d), lambda bi, i: (bi, i, 0))
    return pl.pallas_call(
        _combine_kernel,
        grid=(b, t // tT),
        in_specs=[row_spec, row_spec, pl.BlockSpec((1, tT, LANES), lambda bi, i: (bi, i, 0)),
                  row_spec, pl.BlockSpec((1, 1, 6 * d), lambda bi, i: (bi, 0, 0)),
                  pl.BlockSpec((1, d), lambda bi, i: (0, 0))],
        out_specs=row_spec,
        out_shape=jax.ShapeDtypeStruct((b, t, d), F32),
        compiler_params=_cparams(("arbitrary", "arbitrary")),
        name="moe_combine",
    )(y0, y1, gates, x, mod, g)


def _route(logits, n_experts, tm):
    n = logits.shape[0]
    nk = n * TOP_K
    top_v, top_e = lax.top_k(logits, TOP_K)
    gates = jax.nn.softmax(top_v, axis=-1)
    flat_e = top_e.reshape(-1).astype(jnp.int32)
    onehot = (flat_e[None, :] == jnp.arange(n_experts, dtype=jnp.int32)[:, None]).astype(jnp.int32)
    csum = jnp.cumsum(onehot, axis=1)
    counts = csum[:, -1]
    rank = jnp.sum((csum - onehot) * onehot, axis=0)
    nblk_e = (counts + tm - 1) // tm
    start = jnp.cumsum(counts) - counts
    bend = jnp.cumsum(nblk_e)
    bstart = bend - nblk_e
    pos = (jnp.sum(onehot * (bstart * tm)[:, None], axis=0) + rank).reshape(n, TOP_K)
    nblk = -(-nk // tm) + n_experts
    blk = jnp.arange(nblk, dtype=jnp.int32)
    blk_e = jnp.minimum(jnp.searchsorted(bend, blk, side="right"), n_experts - 1).astype(jnp.int32)
    blk_nv = jnp.clip(counts[blk_e] - (blk - bstart[blk_e]) * tm, 0, tm)
    blk_nv = jnp.where(blk < bend[-1], blk_nv, 0).astype(jnp.int32)
    order = jnp.argsort(flat_e, stable=True).astype(jnp.int32)
    in_blk = jnp.arange(tm, dtype=jnp.int32)[None, :]
    src = (start[blk_e] + (blk - bstart[blk_e]) * tm)[:, None] + in_blk
    src = jnp.where(in_blk < blk_nv[:, None], src, 0).reshape(-1)
    slot_tok = order.at[src].get(mode="promise_in_bounds") // TOP_K
    return gates, slot_tok, pos, blk_e, blk_nv


def kernel(x, c, ctx, c_ctx, mod_w, mod_b, norm_g, rwkv_mu, rwkv_w_rkv, rwkv_w0, rwkv_w1, rwkv_w2,
           rwkv_a0, rwkv_a1, rwkv_a2, rwkv_g1, rwkv_g2, rwkv_k_k, rwkv_k_a, rwkv_r_k, rwkv_ln_w,
           rwkv_ln_b, rwkv_w_out, conv_w_in, conv_w, conv_w_out, ffn_w_gu, ffn_w_down,
           moe_router, moe_w_gu, moe_w_down):
    b, t, d = x.shape
    n_experts = moe_router.shape[-1]
    rows = 16
    cs = jnp.zeros((rows, d), F32).at[:b].set(c).at[b].set(c_ctx)
    mods = _modulation(cs, mod_w, mod_b)
    mod0 = mods[0, :b].reshape(b, 1, 6 * d)
    mod0c = mods[0, b].reshape(1, 1, 6 * d)
    mod1 = mods[1, :b].reshape(b, 1, 6 * d)

    def pad_lora(w):
        zr = jnp.zeros_like(w[0])
        return jnp.stack([jnp.concatenate([w[0], zr], 0), jnp.concatenate([zr, w[1]], 0)]).astype(BF16)

    lg = rwkv_g1.shape[-1]
    lgp = -(-lg // LANES) * LANES
    p = {
        "ng": norm_g[0, 0].reshape(1, d),
        "mu": rwkv_mu[0],
        "wr": rwkv_w_rkv[0, 0].astype(BF16), "wk": rwkv_w_rkv[0, 1].astype(BF16),
        "wv": rwkv_w_rkv[0, 2].astype(BF16),
        "w1": jnp.concatenate([rwkv_w1[0, 0], rwkv_w1[0, 1]], axis=1).astype(BF16),
        "w2": pad_lora(rwkv_w2[0]), "w0": rwkv_w0[0],
        "a1": jnp.concatenate([rwkv_a1[0, 0], rwkv_a1[0, 1]], axis=1).astype(BF16),
        "a2": pad_lora(rwkv_a2[0]), "a0": rwkv_a0[0],
        "k_k": rwkv_k_k[0].reshape(1, d), "k_a": rwkv_k_a[0].reshape(1, d),
        "g1": jnp.pad(rwkv_g1[0], ((0, 0), (0, lgp - lg))).astype(BF16),
        "g2": jnp.pad(rwkv_g2[0], ((0, lgp - lg), (0, 0))).astype(BF16),
        "r_k": rwkv_r_k[0].reshape(1, d),
    }

    tok_c = _tokenwise(ctx, mod0c, "seq", False, p, ctx.shape[1])
    s_ctx = _scan(tok_c, p["k_a"], None, want_y=False, want_state=True)[0]
    tt = min(ROW_TILE, t)
    tm = min(FFN_ROWS, t)
    tf = min(FFN_COLS, ffn_w_down.shape[1])
    tok_l = _tokenwise(x, mod0, "grid", True, p, tt)
    yf, yb = _scan(tok_l[:8], p["k_a"], s_ctx, want_y=True, want_state=False)
    x1, h2 = _readout(yf, yb, tok_l[9], tok_l[8], x, mod0,
                      rwkv_ln_w[0].reshape(1, d), rwkv_ln_b[0].reshape(1, d),
                      rwkv_w_out[0].astype(BF16), norm_g[0, 1].reshape(1, d),
                      norm_g[0, 2].reshape(1, d), tt)

    x2, h3 = _ffn(h2.reshape(b * t, d), ffn_w_gu[0].astype(BF16), ffn_w_down[0].astype(BF16),
                  x1.reshape(b * t, d), mod0, norm_g[0, 3].reshape(1, d),
                  norm_g[1, 0].reshape(1, d), mod1, t, tm, tf)

    w_router = jnp.stack(_split_bf16(jnp.pad(moe_router[0], ((0, 0), (0, LANES - n_experts)))))
    x3, h4, logits = _conv_layer(h3.reshape(b, t, d), x2.reshape(b, t, d), mod1,
                                 conv_w_in[0].astype(BF16), conv_w[0], conv_w_out[0].astype(BF16),
                                 norm_g[1, 1].reshape(1, d), norm_g[1, 2].reshape(1, d), w_router, tt)

    tme = min(MOE_ROWS, b * t * TOP_K)
    gates, slot_tok, pos, blk_e, blk_nv = _route(logits.reshape(b * t, LANES)[:, :n_experts],
                                                 n_experts, tme)
    tg = min(GATHER_ROWS, tme)
    live = (blk_nv[:, None] > jnp.arange(0, tme, tg, dtype=jnp.int32)[None, :]).astype(jnp.int32)
    xs = _gather_rows(h4.reshape(-1, LANES), d, slot_tok, live.reshape(-1), tg)
    ys = _moe_experts(xs, blk_e, blk_nv, moe_w_gu[0], moe_w_down[0], tme,
                      min(MOE_COLS, moe_w_down.shape[2]))
    y0 = ys.at[pos[:, 0]].get(mode="promise_in_bounds").reshape(b, t, d)
    y1 = ys.at[pos[:, 1]].get(mode="promise_in_bounds").reshape(b, t, d)
    gates_p = jnp.pad(gates, ((0, 0), (0, LANES - TOP_K))).reshape(b, t, LANES)
    return _combine(y0, y1, gates_p, x3, mod1, norm_g[1, 3].reshape(1, d), tt)
```

```python
import functools
import math

import jax
import jax.numpy as jnp
from jax import lax
from jax.experimental import pallas as pl
from jax.experimental.pallas import tpu as pltpu

F32 = jnp.float32
BF16 = jnp.bfloat16

HEAD = 64
LANES = 128
GRID_W = 64
CHUNK = 64
SUB = 16
SCAN_PAIRS = 8
NORM_EPS = 1e-6
GN_EPS = 64e-5
DECAY_SCALE = math.exp(-0.5)
TOP_K = 2
VMEM_LIMIT = 56 * 1024 * 1024


ROW_TILE = 256
FFN_ROWS = 1024
FFN_COLS = 512
MOE_ROWS = 2560
MOE_COLS = 512


def _cparams(sem):
    return pltpu.CompilerParams(dimension_semantics=sem, vmem_limit_bytes=VMEM_LIMIT)


def _dot(a, b):
    return jnp.dot(a, b, preferred_element_type=F32)


def _dot_nt(a, b):
    return lax.dot_general(a, b, (((1,), (1,)), ((), ())), preferred_element_type=F32)


def _bdot(a, b):
    return _dot(a.astype(BF16), b.astype(BF16))


def _iota(shape, axis):
    return lax.broadcasted_iota(jnp.int32, shape, axis)


def _rms(x, g):
    return x * lax.rsqrt(jnp.mean(x * x, axis=-1, keepdims=True) + NORM_EPS) * g


def _silu(x):
    return x * jax.nn.sigmoid(x)


def _split_bf16(x):
    hi = x.astype(BF16)
    lo = (x - hi.astype(F32)).astype(BF16)
    return hi, lo


def _head_sum(x):
    ones = (_iota((LANES, LANES), 0) // HEAD == _iota((LANES, LANES), 1) // HEAD).astype(BF16)
    hi, lo = _split_bf16(x)
    outs = []
    for p in range(x.shape[-1] // LANES):
        sl = slice(p * LANES, (p + 1) * LANES)
        outs.append(_dot(hi[:, sl], ones) + _dot(lo[:, sl], ones))
    return jnp.concatenate(outs, axis=1)


def _mod_kernel(c_ref, w_ref, b_ref, o_ref):
    s = _silu(c_ref[...])
    o_ref[0] = jnp.dot(s, w_ref[0], preferred_element_type=F32,
                       precision=lax.Precision.HIGHEST) + b_ref[0]


def _modulation(cs, mod_w, mod_b):
    depth, d, n = mod_w.shape
    rows = cs.shape[0]
    tn = 1536
    return pl.pallas_call(
        _mod_kernel,
        grid=(depth, n // tn),
        in_specs=[pl.BlockSpec((rows, d), lambda i, j: (0, 0)),
                  pl.BlockSpec((1, d, tn), lambda i, j: (i, 0, j)),
                  pl.BlockSpec((1, 1, tn), lambda i, j: (i, 0, j))],
        out_specs=pl.BlockSpec((1, rows, tn), lambda i, j: (i, 0, j)),
        out_shape=jax.ShapeDtypeStruct((depth, rows, n), F32),
        compiler_params=_cparams(("arbitrary", "arbitrary")),
        name="modulation",
    )(cs, mod_w, mod_b.reshape(depth, 1, n))


def _tok_kernel(mode, readout, tT, nT, *refs):
    refs = list(refs)
    x_ref = refs.pop(0)
    if mode == "grid":
        xp_ref = refs.pop(0)
        xn_ref = refs.pop(0)
    (mod_ref, ng_ref, mu_ref, wr_ref, wk_ref, wv_ref, w1_ref, w2_ref, w0_ref,
     a1_ref, a2_ref, a0_ref, kkp_ref, ka_ref) = refs[:14]
    refs = refs[14:]
    if readout:
        g1_ref, g2_ref, rk_ref = refs[:3]
        refs = refs[3:]
    r_o, k_o, v_o, kk_o, ic0_o, ic1_o, lw0_o, lw1_o = refs[:8]
    refs = refs[8:]

    d = x_ref.shape[-1]
    sh = mod_ref[0, :, 0:d]
    sc = mod_ref[0, :, d:2 * d]
    g = ng_ref[...]

    def norm_mod(xx):
        return _rms(xx, g) * (1.0 + sc) + sh

    h = norm_mod(x_ref[0])
    if mode == "grid":
        i = pl.program_id(1)
        q = d // 4
        hp = jnp.where(i > 0, norm_mod(xp_ref[0]), 0.0)
        hn = jnp.where(i < nT - 1, norm_mod(xn_ref[0]), 0.0)
        col = _iota((tT, q), 0) % GRID_W
        left = jnp.where(col == 0, 0.0, pltpu.roll(h[:, 0:q], 1, 0))
        right = jnp.where(col == GRID_W - 1, 0.0, pltpu.roll(h[:, q:2 * q], tT - 1, 0))
        up = jnp.concatenate([hp[:, 2 * q:3 * q], h[:tT - GRID_W, 2 * q:3 * q]], axis=0)
        down = jnp.concatenate([h[GRID_W:, 3 * q:], hn[:, 3 * q:]], axis=0)
        hs = jnp.concatenate([left, right, up, down], axis=1)
    else:
        half = d // 2
        row = _iota((tT, half), 0)
        prev = jnp.where(row == 0, 0.0, pltpu.roll(h[:, :half], 1, 0))
        nxt = jnp.where(row == tT - 1, 0.0, pltpu.roll(h[:, half:], tT - 1, 0))
        hs = jnp.concatenate([prev, nxt], axis=1)

    dx = hs - h

    def mix(n):
        return (h + dx * mu_ref[n:n + 1, :]).astype(BF16)

    r = _dot(mix(0), wr_ref[...])
    k = _dot(mix(2), wk_ref[...])
    v = _dot(mix(3), wv_ref[...])
    w1o = jnp.tanh(_dot(mix(1), w1_ref[...])).astype(BF16)
    a1o = _dot(mix(4), a1_ref[...]).astype(BF16)
    ics = []
    for z, (lw_o, ic_o) in enumerate(((lw0_o, ic0_o), (lw1_o, ic1_o))):
        w_pre = _dot(w1o, w2_ref[z]) + w0_ref[z:z + 1, :]
        lw_o[0] = -DECAY_SCALE * jax.nn.sigmoid(w_pre)
        ic = jax.nn.sigmoid(_dot(a1o, a2_ref[z]) + a0_ref[z:z + 1, :])
        ic_o[0] = ic.astype(BF16)
        ics.append(ic)
    kk = k * kkp_ref[...]
    kk = kk * lax.rsqrt(jnp.maximum(_head_sum(kk * kk), 1e-24))
    r_o[0] = r.astype(BF16)
    k_o[0] = k.astype(BF16)
    v_o[0] = v.astype(BF16)
    kk_o[0] = kk.astype(BF16)
    if readout:
        gate_o, bon_o = refs
        gate = _dot(jax.nn.sigmoid(_dot(mix(5), g1_ref[...])).astype(BF16), g2_ref[...])
        gate_o[0] = gate.astype(BF16)
        k_avg = k * (1.0 + (0.5 * (ics[0] + ics[1]) - 1.0) * ka_ref[...])
        bon_o[0] = (_head_sum(r * k_avg * rk_ref[...]) * v).astype(BF16)


def _tokenwise(x, mod, mode, readout, p, tT):
    b, t, d = x.shape
    nT = t // tT
    row_spec = pl.BlockSpec((1, tT, d), lambda bi, i: (bi, i, 0))
    in_specs = [row_spec]
    args = [x]
    if mode == "grid":
        hb = tT // GRID_W
        in_specs += [
            pl.BlockSpec((1, GRID_W, d), lambda bi, i: (bi, jnp.maximum(i * hb - 1, 0), 0)),
            pl.BlockSpec((1, GRID_W, d), lambda bi, i: (bi, jnp.minimum((i + 1) * hb, t // GRID_W - 1), 0)),
        ]
        args += [x, x]
    if mod.shape[0] == 1:
        in_specs.append(pl.BlockSpec((1, 1, mod.shape[-1]), lambda bi, i: (0, 0, 0)))
    else:
        in_specs.append(pl.BlockSpec((1, 1, mod.shape[-1]), lambda bi, i: (bi, 0, 0)))
    args.append(mod)
    consts = [p["ng"], p["mu"], p["wr"], p["wk"], p["wv"], p["w1"], p["w2"], p["w0"],
              p["a1"], p["a2"], p["a0"], p["k_k"], p["k_a"]]
    if readout:
        consts += [p["g1"], p["g2"], p["r_k"]]
    for a in consts:
        in_specs.append(pl.BlockSpec(a.shape, lambda bi, i, _n=a.ndim: (0,) * _n))
        args.append(a)
    dtypes = [BF16] * 6 + [F32] * 2 + ([BF16] * 2 if readout else [])
    return pl.pallas_call(
        functools.partial(_tok_kernel, mode, readout, tT, nT),
        grid=(b, nT),
        in_specs=in_specs,
        out_specs=[row_spec] * len(dtypes),
        out_shape=[jax.ShapeDtypeStruct((b, t, d), dt) for dt in dtypes],
        compiler_params=_cparams(("arbitrary", "arbitrary")),
        name="rwkv_tokenwise_" + mode,
    )(*args)


def _scan_prep(r, kd, v, lw, kk, ic, reverse):
    c = r.shape[0]
    ti = _iota((c, c), 0)
    si = _iota((c, c), 1)
    tri = ((si >= ti) if reverse else (si <= ti)).astype(BF16)
    lw_hi, lw_lo = _split_bf16(lw)
    cum = _dot(tri, lw_hi) + _dot(tri, lw_lo)
    pc_log = cum[0:1] if reverse else cum[c - 1:c]
    p_inv = jnp.exp(-cum)
    p_rest = jnp.exp(pc_log - cum)
    b = kk * ic
    return {"rt": r * jnp.exp(cum), "at": -(kk * jnp.exp(cum - lw)),
            "bt": b * p_inv, "kt": kd * p_inv, "bp": b * p_rest, "kp": kd * p_rest,
            "v": v, "pc": jnp.exp(pc_log)}


def _scan_step(prep, states):
    c = prep[0]["v"].shape[0]
    n = 2 * c
    pp = len(states[0])
    chains = [(dr, p) for dr in range(2) for p in range(pp)]
    first = _iota((c, LANES), 1) < HEAD
    row = _iota((n, n), 0)
    col = _iota((n, n), 1)
    strict = (col < row, col > row)
    same = (row // SUB) == (col // SUB)

    def pair(name, dr, p):
        return prep[dr][name][:, p * LANES:(p + 1) * LANES]

    def stack(x):
        return jnp.concatenate([jnp.where(first, x, 0.0), jnp.where(first, 0.0, x)], axis=0)

    def ms(name, dr, p):
        return stack(pair(name, dr, p))

    def each(fn, *lists):
        return [fn(*xs) for xs in zip(*lists)]

    lhs = [jnp.concatenate([pair("at", dr, p), pair("rt", dr, p)], axis=0).astype(BF16)
           for dr, p in chains]
    rhs = [jnp.concatenate([ms("bt", dr, p), ms("kt", dr, p)], axis=0).astype(BF16) for dr, p in chains]
    at_ms = [ms("at", dr, p).astype(BF16) for dr, p in chains]
    v_ms = [ms("v", dr, p).astype(BF16) for dr, p in chains]
    gmat = each(_dot_nt, lhs, rhs)
    nmat = [jnp.where(strict[dr], stack(g[:c, :n]), 0.0) for (dr, _), g in zip(chains, gmat)]
    a_ak = [jnp.where(strict[dr], stack(g[:c, n:]), 0.0).astype(BF16) for (dr, _), g in zip(chains, gmat)]
    t_pl = _iota((c, 2 * n), 0)
    s_pl = _iota((c, 2 * n), 1) % c
    incl_pl = (s_pl <= t_pl, s_pl >= t_pl)
    a_r = [jnp.where(incl_pl[dr], g[c:], 0.0).astype(BF16)
           for (dr, _), g in zip(chains, gmat)]

    nd = [jnp.where(same, x, 0.0) for x in nmat]
    no = [jnp.where(same, 0.0, x) for x in nmat]
    pw = nd
    tp = nd
    span = 2
    while span < SUB:
        pw = each(_bdot, pw, pw)
        tp = each(lambda t, q: t + q + _bdot(t, q), tp, pw)
        span *= 2
    m = each(lambda t, o: o + _bdot(t, o), tp, no)
    qp = m
    pw = m
    span = 2
    while span < c // SUB:
        pw = each(_bdot, pw, pw)
        qp = each(lambda t, q: t + q + _bdot(t, q), qp, pw)
        span *= 2
    tp = each(lambda t, q: t + q + _bdot(q, t), tp, qp)

    bk_t = [jnp.concatenate([ms("bp", dr, p), ms("kp", dr, p)], axis=0).T.astype(BF16)
            for dr, p in chains]
    diag = row == col
    pc_col = [jnp.sum(jnp.where(diag, prep[dr]["pc"][:, p * LANES:(p + 1) * LANES], 0.0),
                      axis=1, keepdims=True) for dr, p in chains]

    s_t = [states[dr][p].astype(BF16) for dr, p in chains]
    rhs_z = each(lambda l, a, st, v: _dot(jnp.concatenate([l, a], axis=1),
                                          jnp.concatenate([st, v], axis=0)), at_ms, a_ak, s_t, v_ms)
    z = each(lambda t, x: x + _bdot(t, x), tp, rhs_z)
    zv = each(lambda zz, v: jnp.concatenate([zz.astype(BF16), v], axis=0), z, v_ms)
    y_pl = each(lambda l, st, ar, w: _dot(l[c:], st) + _dot(ar, w), lhs, s_t, a_r, zv)
    y = [jnp.concatenate([yy for (d2, _), yy in zip(chains, y_pl) if d2 == dr], axis=1)
         for dr in range(2)]
    new_states = [[None] * pp for _ in range(2)]
    for (dr, p), bt_, w, pc in zip(chains, bk_t, zv, pc_col):
        new_states[dr][p] = states[dr][p] * pc + _dot(bt_, w)
    return y, new_states


def _scan_kernel(pp, nc, has_s0, want_y, want_state, *refs):
    refs = list(refs)
    ka_ref = refs[12]
    dirs = (refs[:6], refs[6:12])
    refs = refs[13:]
    if has_s0:
        s0_ref = refs.pop(0)
    if want_y:
        y_refs = (refs.pop(0), refs.pop(0))
    if want_state:
        st_ref = refs.pop(0)
    s_scr = refs.pop(0)
    c = pl.program_id(2)

    @pl.when(c == 0)
    def _():
        if has_s0:
            s_scr[...] = s0_ref[0]
        else:
            s_scr[...] = jnp.zeros_like(s_scr)

    prep = []
    for dr, (r_ref, k_ref, v_ref, kk_ref, ic_ref, lw_ref) in enumerate(dirs):
        ic = ic_ref[0].astype(F32)
        kd = k_ref[0].astype(F32) * (1.0 + (ic - 1.0) * ka_ref[...])
        prep.append(_scan_prep(r_ref[0].astype(F32), kd, v_ref[0].astype(F32), lw_ref[0],
                               kk_ref[0].astype(F32), ic, reverse=(dr == 1)))
    states = [[s_scr[dr, p] for p in range(pp)] for dr in range(2)]
    y, new_states = _scan_step(prep, states)
    for dr in range(2):
        for p in range(pp):
            s_scr[dr, p] = new_states[dr][p]
        if want_y:
            y_refs[dr][0] = y[dr]

    if want_state:
        @pl.when(c == nc - 1)
        def _():
            st_ref[0] = s_scr[...]


def _scan(tok, ka, s0, want_y, want_state):
    r, k, v, kk, ic0, ic1, lw0, lw1 = tok
    b, t, d = r.shape
    nc = t // CHUNK
    npair = d // LANES
    pp = min(SCAN_PAIRS, npair)
    w = LANES * pp
    f_spec = pl.BlockSpec((1, CHUNK, w), lambda bi, hi, ci: (bi, ci, hi))
    b_spec = pl.BlockSpec((1, CHUNK, w), lambda bi, hi, ci: (bi, nc - 1 - ci, hi))
    st_spec = pl.BlockSpec((1, 2, pp, LANES, LANES), lambda bi, hi, ci: (bi, 0, hi, 0, 0))
    in_specs = [f_spec] * 6 + [b_spec] * 6 + [pl.BlockSpec((1, w), lambda bi, hi, ci: (0, hi))]
    args = [r, k, v, kk, ic0, lw0, r, k, v, kk, ic1, lw1, ka]
    if s0 is not None:
        in_specs.append(st_spec)
        args.append(s0)
    out_specs, out_shape = [], []
    if want_y:
        out_specs += [f_spec, b_spec]
        out_shape += [jax.ShapeDtypeStruct((b, t, d), F32)] * 2
    if want_state:
        out_specs.append(st_spec)
        out_shape.append(jax.ShapeDtypeStruct((b, 2, npair, LANES, LANES), F32))
    return pl.pallas_call(
        functools.partial(_scan_kernel, pp, nc, s0 is not None, want_y, want_state),
        grid=(b, npair // pp, nc),
        in_specs=in_specs,
        out_specs=out_specs,
        out_shape=out_shape,
        scratch_shapes=[pltpu.VMEM((2, pp, LANES, LANES), F32)],
        compiler_params=_cparams(("arbitrary", "arbitrary", "arbitrary")),
        name="rwkv_scan_ctx" if s0 is None else "rwkv_scan_latent",
    )(*args)


def _readout_kernel(yf_ref, yb_ref, bon_ref, gate_ref, x_ref, mod_ref, lnw_ref, lnb_ref,
                    wo_ref, g1_ref, g2_ref, x_o, h_o):
    d = x_ref.shape[-1]
    y = yf_ref[0] + yb_ref[0]
    mean = _head_sum(y) * (1.0 / HEAD)
    yc = y - mean
    var = _head_sum(yc * yc) * (1.0 / HEAD)
    o = yc * lax.rsqrt(var + GN_EPS) * lnw_ref[...] + lnb_ref[...] + bon_ref[0]
    att = _dot((o * gate_ref[0]).astype(BF16), wo_ref[...])
    gt_a = mod_ref[0, :, 2 * d:3 * d]
    sh_f = mod_ref[0, :, 3 * d:4 * d]
    sc_f = mod_ref[0, :, 4 * d:5 * d]
    x1 = x_ref[0] + gt_a * _rms(att, g1_ref[...])
    x_o[0] = x1
    h_o[0] = (_rms(x1, g2_ref[...]) * (1.0 + sc_f) + sh_f).astype(BF16)


def _readout(yf, yb, bon, gate, x, mod, lnw, lnb, wo, g1, g2, tT):
    b, t, d = x.shape
    row_spec = pl.BlockSpec((1, tT, d), lambda bi, i: (bi, i, 0))
    vec = pl.BlockSpec((1, d), lambda bi, i: (0, 0))
    return pl.pallas_call(
        _readout_kernel,
        grid=(b, t // tT),
        in_specs=[row_spec] * 5 + [pl.BlockSpec((1, 1, 6 * d), lambda bi, i: (bi, 0, 0)),
                                   vec, vec, pl.BlockSpec((d, d), lambda bi, i: (0, 0)), vec, vec],
        out_specs=[row_spec, row_spec],
        out_shape=[jax.ShapeDtypeStruct((b, t, d), F32), jax.ShapeDtypeStruct((b, t, d), BF16)],
        compiler_params=_cparams(("arbitrary", "arbitrary")),
        name="rwkv_readout",
    )(yf, yb, bon, gate, x, mod, lnw, lnb, wo, g1, g2)


def _ffn_kernel(nf, h_ref, wg_ref, wu_ref, wd_ref, x_ref, mod_ref, g3_ref, gn_ref, mod2_ref,
                x_o, h_o, acc):
    f = pl.program_id(1)
    d = x_ref.shape[-1]

    @pl.when(f == 0)
    def _():
        acc[...] = jnp.zeros_like(acc)

    h = h_ref[...]
    act = _silu(_dot(h, wg_ref[...])) * _dot(h, wu_ref[...])
    acc[...] += _dot(act.astype(BF16), wd_ref[...])

    @pl.when(f == nf - 1)
    def _():
        gt_f = mod_ref[0, :, 5 * d:6 * d]
        x2 = x_ref[...] + gt_f * _rms(acc[...], g3_ref[...])
        x_o[...] = x2
        sh = mod2_ref[0, :, 0:d]
        sc = mod2_ref[0, :, d:2 * d]
        h_o[...] = (_rms(x2, gn_ref[...]) * (1.0 + sc) + sh).astype(BF16)


def _ffn(h, w_gu, w_down, x, mod, g3, gn, mod2, rows_per_batch, tm, tf):
    n, d = x.shape
    ff = w_down.shape[0]
    nf = ff // tf
    per = rows_per_batch // tm
    row = pl.BlockSpec((tm, d), lambda i, f: (i, 0))
    vec = pl.BlockSpec((1, d), lambda i, f: (0, 0))
    modspec = pl.BlockSpec((1, 1, 6 * d), lambda i, f: (i // per, 0, 0))
    return pl.pallas_call(
        functools.partial(_ffn_kernel, nf),
        grid=(n // tm, nf),
        in_specs=[row,
                  pl.BlockSpec((d, tf), lambda i, f: (0, f)),
                  pl.BlockSpec((d, tf), lambda i, f: (0, nf + f)),
                  pl.BlockSpec((tf, d), lambda i, f: (f, 0)),
                  row, modspec, vec, vec, modspec],
        out_specs=[row, row],
        out_shape=[jax.ShapeDtypeStruct((n, d), F32), jax.ShapeDtypeStruct((n, d), BF16)],
        scratch_shapes=[pltpu.VMEM((tm, d), F32)],
        compiler_params=_cparams(("arbitrary", "arbitrary")),
        name="dense_swiglu",
    )(h, w_gu, w_gu, w_down, x, mod, g3, gn, mod2)


HALO = 16


def _conv_kernel(tT, nT, h_ref, hp_ref, hn_ref, x_ref, mod_ref, win_ref, cw_ref, wo_ref,
                 g1_ref, g2_ref, wr_ref, x_o, h_o, lg_o):
    i = pl.program_id(1)
    d = x_ref.shape[-1]
    n = tT + 2 * HALO
    h_ext = jnp.concatenate([hp_ref[0], h_ref[0], hn_ref[0]], axis=0)
    proj = _dot(h_ext, win_ref[...])
    z = proj[:, d:2 * d] * proj[:, 2 * d:]
    row = _iota((n, d), 0)
    dead = ((row < HALO) & (i == 0)) | ((row >= HALO + tT) & (i == nT - 1))
    z = jnp.where(dead, 0.0, z)
    conv = (pltpu.roll(z, 1, 0) * cw_ref[0:1, :] + z * cw_ref[1:2, :]
            + pltpu.roll(z, n - 1, 0) * cw_ref[2:3, :])
    gated = (proj[HALO:HALO + tT, 0:d] * conv[HALO:HALO + tT]).astype(BF16)
    y = _dot(gated, wo_ref[...])
    gt_a = mod_ref[0, :, 2 * d:3 * d]
    sh_f = mod_ref[0, :, 3 * d:4 * d]
    sc_f = mod_ref[0, :, 4 * d:5 * d]
    x3 = x_ref[0] + gt_a * _rms(y, g1_ref[...])
    x_o[0] = x3
    h4 = _rms(x3, g2_ref[...]) * (1.0 + sc_f) + sh_f
    s_rows = d // LANES
    for c in range(s_rows):
        h_o[0, pl.ds(c, tT, stride=s_rows), :] = h4[:, c * LANES:(c + 1) * LANES]
    h_hi, h_lo = _split_bf16(h4)
    lg_o[0] = _dot(h_hi, wr_ref[0]) + _dot(h_lo, wr_ref[0]) + _dot(h_hi, wr_ref[1])


def _conv_layer(h, x, mod, w_in, conv_w, w_out, g1, g2, w_router, tT):
    b, t, d = x.shape
    nT = t // tT
    hb = tT // HALO
    row_spec = pl.BlockSpec((1, tT, d), lambda bi, i: (bi, i, 0))
    vec = pl.BlockSpec((1, d), lambda bi, i: (0, 0))
    full = lambda a: pl.BlockSpec(a.shape, lambda bi, i, _n=a.ndim: (0,) * _n)
    return pl.pallas_call(
        functools.partial(_conv_kernel, tT, nT),
        grid=(b, nT),
        in_specs=[row_spec,
                  pl.BlockSpec((1, HALO, d), lambda bi, i: (bi, jnp.maximum(i * hb - 1, 0), 0)),
                  pl.BlockSpec((1, HALO, d), lambda bi, i: (bi, jnp.minimum((i + 1) * hb, t // HALO - 1), 0)),
                  row_spec, pl.BlockSpec((1, 1, 6 * d), lambda bi, i: (bi, 0, 0)),
                  full(w_in), full(conv_w), full(w_out), vec, vec, full(w_router)],
        out_specs=[row_spec, pl.BlockSpec((1, tT * (d // LANES), LANES), lambda bi, i: (bi, i, 0)),
                   pl.BlockSpec((1, tT, LANES), lambda bi, i: (bi, i, 0))],
        out_shape=[jax.ShapeDtypeStruct((b, t, d), F32),
                   jax.ShapeDtypeStruct((b, t * (d // LANES), LANES), F32),
                   jax.ShapeDtypeStruct((b, t, LANES), F32)],
        compiler_params=_cparams(("arbitrary", "arbitrary")),
        name="short_conv",
    )(h, h, h, x, mod, w_in, conv_w, w_out, g1, g2, w_router)


GATHER_ROWS = 512


def _gather_kernel(tg, nsteps, live_ref, tok_cur, tok_nxt, h_hbm, o_ref, buf, sem):
    j = pl.program_id(0)
    slot = j % 2
    s_rows = buf.shape[1] // tg

    def issue(tok_ref, s):
        for r in range(tg):
            src = pl.multiple_of(tok_ref[0, 0, r] * s_rows, s_rows)
            pltpu.make_async_copy(h_hbm.at[pl.ds(src, s_rows)],
                                  buf.at[s, pl.ds(r * s_rows, s_rows)], sem.at[s]).start(priority=r % 2)

    @pl.when((j == 0) & (live_ref[0] > 0))
    def _():
        issue(tok_cur, 0)

    @pl.when(j + 1 < nsteps)
    def _():
        @pl.when(live_ref[j + 1] > 0)
        def _():
            issue(tok_nxt, 1 - slot)

    @pl.when(live_ref[j] > 0)
    def _():
        pltpu.make_async_copy(h_hbm.at[pl.ds(0, tg * s_rows)], buf.at[slot], sem.at[slot]).wait()
        for c in range(s_rows):
            o_ref[:, c * LANES:(c + 1) * LANES] = buf[slot, pl.ds(c, tg, stride=s_rows), :].astype(BF16)

    @pl.when(live_ref[j] == 0)
    def _():
        o_ref[...] = jnp.zeros(o_ref.shape, BF16)


def _gather_rows(h, d, slot_tok, live, tg):
    s_rows = d // LANES
    nsteps = slot_tok.shape[0] // tg
    tok3 = slot_tok.reshape(nsteps, 1, tg)
    grid_spec = pltpu.PrefetchScalarGridSpec(
        num_scalar_prefetch=1,
        grid=(nsteps,),
        in_specs=[pl.BlockSpec((1, 1, tg), lambda j, lv: (j, 0, 0), memory_space=pltpu.SMEM),
                  pl.BlockSpec((1, 1, tg), lambda j, lv: (jnp.minimum(j + 1, nsteps - 1), 0, 0),
                               memory_space=pltpu.SMEM),
                  pl.BlockSpec(memory_space=pl.ANY)],
        out_specs=pl.BlockSpec((tg, d), lambda j, lv: (j, 0)),
        scratch_shapes=[pltpu.VMEM((2, tg * s_rows, LANES), F32), pltpu.SemaphoreType.DMA((2,))],
    )
    return pl.pallas_call(
        functools.partial(_gather_kernel, tg, nsteps),
        grid_spec=grid_spec,
        out_shape=jax.ShapeDtypeStruct((nsteps * tg, d), BF16),
        compiler_params=_cparams(("arbitrary",)),
        name="moe_dispatch",
    )(live, tok3, tok3, h)


MOE_SUB = 512


def _moe_kernel(nf, tm, e_ref, nv_ref, x_ref, wg_ref, wu_ref, wd_ref, y_o):
    i = pl.program_id(0)
    f = pl.program_id(1)
    nv = nv_ref[i]

    nsub = tm // MOE_SUB

    @pl.when(f == 0)
    def _():
        y_o[...] = jnp.zeros(y_o.shape, F32)

    def weights():
        return wg_ref[0].astype(BF16), wu_ref[0].astype(BF16), wd_ref[0].astype(BF16)

    def sub_block(s, wg, wu, wd):
        rows = slice(s * MOE_SUB, (s + 1) * MOE_SUB)
        h = x_ref[rows, :]
        act = _silu(_dot(h, wg)) * _dot(h, wu)
        y_o[rows, :] += _dot(act.astype(BF16), wd)

    @pl.when(nv > (nsub - 1) * MOE_SUB)
    def _():
        w = weights()
        for s in range(nsub):
            sub_block(s, *w)

    @pl.when((nv > 0) & (nv <= (nsub - 1) * MOE_SUB))
    def _():
        w = weights()
        for s in range(nsub - 1):
            @pl.when(nv > s * MOE_SUB)
            def _():
                sub_block(s, *w)


def _moe_experts(xs, blk_e, blk_nv, w_gu, w_down, tm, tf):
    n, d = xs.shape
    ff = w_down.shape[1]
    nf = ff // tf
    nblk = n // tm

    def f_eff(i, f, nv_ref):
        return jnp.where(nv_ref[i] > 0, f, nf - 1)

    grid_spec = pltpu.PrefetchScalarGridSpec(
        num_scalar_prefetch=2,
        grid=(nblk, nf),
        in_specs=[pl.BlockSpec((tm, d), lambda i, f, e, nv: (i, 0)),
                  pl.BlockSpec((1, d, tf), lambda i, f, e, nv: (e[i], 0, f_eff(i, f, nv))),
                  pl.BlockSpec((1, d, tf), lambda i, f, e, nv: (e[i], 0, nf + f_eff(i, f, nv))),
                  pl.BlockSpec((1, tf, d), lambda i, f, e, nv: (e[i], f_eff(i, f, nv), 0))],
        out_specs=pl.BlockSpec((tm, d), lambda i, f, e, nv: (i, 0)),
    )
    return pl.pallas_call(
        functools.partial(_moe_kernel, nf, tm),
        grid_spec=grid_spec,
        out_shape=jax.ShapeDtypeStruct((n, d), F32),
        compiler_params=_cparams(("arbitrary", "arbitrary")),
        name="moe_experts",
    )(blk_e, blk_nv, xs, w_gu, w_gu, w_down)


def _combine_kernel(tT, nsteps, pos_cur, pos_nxt, ys_hbm, gt_ref, x_ref, mod_ref, g_ref, o_ref,
                    buf, sem):
    j = pl.program_id(0)
    slot = j % 2
    d = x_ref.shape[-1]

    def issue(pos_ref, s):
        for r in range(tT):
            for k in range(TOP_K):
                pltpu.make_async_copy(ys_hbm.at[pl.ds(pos_ref[0, k, r], 1)],
                                      buf.at[s, k, pl.ds(r, 1)], sem.at[s]).start(priority=k)

    @pl.when(j == 0)
    def _():
        issue(pos_cur, 0)

    @pl.when(j + 1 < nsteps)
    def _():
        issue(pos_nxt, 1 - slot)

    for k in range(TOP_K):
        pltpu.make_async_copy(ys_hbm.at[pl.ds(0, tT)], buf.at[slot, k], sem.at[slot]).wait()
    gates = gt_ref[...]
    y = buf[slot, 0] * gates[:, 0:1] + buf[slot, 1] * gates[:, 1:2]
    gt_f = mod_ref[0, :, 5 * d:6 * d]
    o_ref[...] = x_ref[...] + gt_f * _rms(y, g_ref[...])


def _combine(ys, pos, gates, x, mod, g, rows_per_batch, tT):
    n, d = x.shape
    nsteps = n // tT
    per = rows_per_batch // tT
    pos3 = pos.reshape(nsteps, tT, TOP_K).transpose(0, 2, 1)
    row_spec = pl.BlockSpec((tT, d), lambda j: (j, 0))
    smem = functools.partial(pl.BlockSpec, (1, TOP_K, tT), memory_space=pltpu.SMEM)
    return pl.pallas_call(
        functools.partial(_combine_kernel, tT, nsteps),
        grid=(nsteps,),
        in_specs=[smem(lambda j: (j, 0, 0)),
                  smem(lambda j: (jnp.minimum(j + 1, nsteps - 1), 0, 0)),
                  pl.BlockSpec(memory_space=pl.ANY),
                  pl.BlockSpec((tT, LANES), lambda j: (j, 0)),
                  row_spec, pl.BlockSpec((1, 1, 6 * d), lambda j: (j // per, 0, 0)),
                  pl.BlockSpec((1, d), lambda j: (0, 0))],
        out_specs=row_spec,
        out_shape=jax.ShapeDtypeStruct((n, d), F32),
        scratch_shapes=[pltpu.VMEM((2, TOP_K, tT, d), F32), pltpu.SemaphoreType.DMA((2,))],
        compiler_params=_cparams(("arbitrary",)),
        name="moe_combine",
    )(pos3, pos3, ys, gates, x, mod, g)


def _route(logits, n_experts, tm):
    n = logits.shape[0]
    nk = n * TOP_K
    top_v, top_e = lax.top_k(logits, TOP_K)
    gates = jax.nn.softmax(top_v, axis=-1)
    flat_e = top_e.reshape(-1).astype(jnp.int32)
    onehot = (flat_e[None, :] == jnp.arange(n_experts, dtype=jnp.int32)[:, None]).astype(jnp.int32)
    csum = jnp.cumsum(onehot, axis=1)
    counts = csum[:, -1]
    rank = jnp.sum((csum - onehot) * onehot, axis=0)
    nblk_e = (counts + tm - 1) // tm
    start = jnp.cumsum(counts) - counts
    bend = jnp.cumsum(nblk_e)
    bstart = bend - nblk_e
    pos = (jnp.sum(onehot * (bstart * tm)[:, None], axis=0) + rank).reshape(n, TOP_K)
    nblk = -(-nk // tm) + n_experts
    blk = jnp.arange(nblk, dtype=jnp.int32)
    blk_e = jnp.minimum(jnp.searchsorted(bend, blk, side="right"), n_experts - 1).astype(jnp.int32)
    blk_nv = jnp.clip(counts[blk_e] - (blk - bstart[blk_e]) * tm, 0, tm)
    blk_nv = jnp.where(blk < bend[-1], blk_nv, 0).astype(jnp.int32)
    order = jnp.argsort(flat_e, stable=True).astype(jnp.int32)
    in_blk = jnp.arange(tm, dtype=jnp.int32)[None, :]
    src = (start[blk_e] + (blk - bstart[blk_e]) * tm)[:, None] + in_blk
    src = jnp.where(in_blk < blk_nv[:, None], src, 0).reshape(-1)
    slot_tok = order.at[src].get(mode="promise_in_bounds") // TOP_K
    return gates, slot_tok, pos, blk_e, blk_nv


def kernel(x, c, ctx, c_ctx, mod_w, mod_b, norm_g, rwkv_mu, rwkv_w_rkv, rwkv_w0, rwkv_w1, rwkv_w2,
           rwkv_a0, rwkv_a1, rwkv_a2, rwkv_g1, rwkv_g2, rwkv_k_k, rwkv_k_a, rwkv_r_k, rwkv_ln_w,
           rwkv_ln_b, rwkv_w_out, conv_w_in, conv_w, conv_w_out, ffn_w_gu, ffn_w_down,
           moe_router, moe_w_gu, moe_w_down):
    b, t, d = x.shape
    n_experts = moe_router.shape[-1]
    rows = 16
    cs = jnp.zeros((rows, d), F32).at[:b].set(c).at[b].set(c_ctx)
    mods = _modulation(cs, mod_w, mod_b)
    mod0 = mods[0, :b].reshape(b, 1, 6 * d)
    mod0c = mods[0, b].reshape(1, 1, 6 * d)
    mod1 = mods[1, :b].reshape(b, 1, 6 * d)

    def pad_lora(w):
        zr = jnp.zeros_like(w[0])
        return jnp.stack([jnp.concatenate([w[0], zr], 0), jnp.concatenate([zr, w[1]], 0)]).astype(BF16)

    lg = rwkv_g1.shape[-1]
    lgp = -(-lg // LANES) * LANES
    p = {
        "ng": norm_g[0, 0].reshape(1, d),
        "mu": rwkv_mu[0],
        "wr": rwkv_w_rkv[0, 0].astype(BF16), "wk": rwkv_w_rkv[0, 1].astype(BF16),
        "wv": rwkv_w_rkv[0, 2].astype(BF16),
        "w1": jnp.concatenate([rwkv_w1[0, 0], rwkv_w1[0, 1]], axis=1).astype(BF16),
        "w2": pad_lora(rwkv_w2[0]), "w0": rwkv_w0[0],
        "a1": jnp.concatenate([rwkv_a1[0, 0], rwkv_a1[0, 1]], axis=1).astype(BF16),
        "a2": pad_lora(rwkv_a2[0]), "a0": rwkv_a0[0],
        "k_k": rwkv_k_k[0].reshape(1, d), "k_a": rwkv_k_a[0].reshape(1, d),
        "g1": jnp.pad(rwkv_g1[0], ((0, 0), (0, lgp - lg))).astype(BF16),
        "g2": jnp.pad(rwkv_g2[0], ((0, lgp - lg), (0, 0))).astype(BF16),
        "r_k": rwkv_r_k[0].reshape(1, d),
    }

    tok_c = _tokenwise(ctx, mod0c, "seq", False, p, ctx.shape[1])
    s_ctx = _scan(tok_c, p["k_a"], None, want_y=False, want_state=True)[0]
    tt = min(ROW_TILE, t)
    tm = min(FFN_ROWS, t)
    tf = min(FFN_COLS, ffn_w_down.shape[1])
    tok_l = _tokenwise(x, mod0, "grid", True, p, tt)
    yf, yb = _scan(tok_l[:8], p["k_a"], s_ctx, want_y=True, want_state=False)
    x1, h2 = _readout(yf, yb, tok_l[9], tok_l[8], x, mod0,
                      rwkv_ln_w[0].reshape(1, d), rwkv_ln_b[0].reshape(1, d),
                      rwkv_w_out[0].astype(BF16), norm_g[0, 1].reshape(1, d),
                      norm_g[0, 2].reshape(1, d), tt)

    x2, h3 = _ffn(h2.reshape(b * t, d), ffn_w_gu[0].astype(BF16), ffn_w_down[0].astype(BF16),
                  x1.reshape(b * t, d), mod0, norm_g[0, 3].reshape(1, d),
                  norm_g[1, 0].reshape(1, d), mod1, t, tm, tf)

    w_router = jnp.stack(_split_bf16(jnp.pad(moe_router[0], ((0, 0), (0, LANES - n_experts)))))
    x3, h4, logits = _conv_layer(h3.reshape(b, t, d), x2.reshape(b, t, d), mod1,
                                 conv_w_in[0].astype(BF16), conv_w[0], conv_w_out[0].astype(BF16),
                                 norm_g[1, 1].reshape(1, d), norm_g[1, 2].reshape(1, d), w_router, tt)

    tme = min(MOE_ROWS, b * t * TOP_K)
    gates, slot_tok, pos, blk_e, blk_nv = _route(logits.reshape(b * t, LANES)[:, :n_experts],
                                                 n_experts, tme)
    tg = min(GATHER_ROWS, tme)
    live = (blk_nv[:, None] > jnp.arange(0, tme, tg, dtype=jnp.int32)[None, :]).astype(jnp.int32)
    xs = _gather_rows(h4.reshape(-1, LANES), d, slot_tok, live.reshape(-1), tg)
    ys = _moe_experts(xs, blk_e, blk_nv, moe_w_gu[0], moe_w_down[0], tme,
                      min(MOE_COLS, moe_w_down.shape[2]))
    gates_p = jnp.pad(gates, ((0, 0), (0, LANES - TOP_K)))
    out = _combine(ys, pos, gates_p, x3.reshape(b * t, d), mod1, norm_g[1, 3].reshape(1, d), t, tt)
    return out.reshape(b, t, d)
```

```python
import functools
import math

import jax
import jax.numpy as jnp
from jax import lax
from jax.experimental import pallas as pl
from jax.experimental.pallas import tpu as pltpu

F32 = jnp.float32
BF16 = jnp.bfloat16

HEAD = 64
LANES = 128
GRID_W = 64
CHUNK = 64
SUB = 16
SCAN_PAIRS = 8
NORM_EPS = 1e-6
GN_EPS = 64e-5
DECAY_SCALE = math.exp(-0.5)
TOP_K = 2
VMEM_LIMIT = 56 * 1024 * 1024


ROW_TILE = 256
FFN_ROWS = 1024
FFN_COLS = 512
MOE_ROWS = 2560
MOE_COLS = 512


def _cparams(sem):
    return pltpu.CompilerParams(dimension_semantics=sem, vmem_limit_bytes=VMEM_LIMIT)


def _dot(a, b):
    return jnp.dot(a, b, preferred_element_type=F32)


def _dot_nt(a, b):
    return lax.dot_general(a, b, (((1,), (1,)), ((), ())), preferred_element_type=F32)


def _bdot(a, b):
    return _dot(a.astype(BF16), b.astype(BF16))


def _iota(shape, axis):
    return lax.broadcasted_iota(jnp.int32, shape, axis)


def _rms(x, g):
    return x * lax.rsqrt(jnp.mean(x * x, axis=-1, keepdims=True) + NORM_EPS) * g


def _silu(x):
    return x * jax.nn.sigmoid(x)


def _split_bf16(x):
    hi = x.astype(BF16)
    lo = (x - hi.astype(F32)).astype(BF16)
    return hi, lo


def _head_sum(x):
    ones = (_iota((LANES, LANES), 0) // HEAD == _iota((LANES, LANES), 1) // HEAD).astype(BF16)
    hi, lo = _split_bf16(x)
    outs = []
    for p in range(x.shape[-1] // LANES):
        sl = slice(p * LANES, (p + 1) * LANES)
        outs.append(_dot(hi[:, sl], ones) + _dot(lo[:, sl], ones))
    return jnp.concatenate(outs, axis=1)


def _mod_kernel(c_ref, w_ref, b_ref, o_ref):
    s = _silu(c_ref[...])
    o_ref[0] = jnp.dot(s, w_ref[0], preferred_element_type=F32,
                       precision=lax.Precision.HIGHEST) + b_ref[0]


def _modulation(cs, mod_w, mod_b):
    depth, d, n = mod_w.shape
    rows = cs.shape[0]
    tn = 1536
    return pl.pallas_call(
        _mod_kernel,
        grid=(depth, n // tn),
        in_specs=[pl.BlockSpec((rows, d), lambda i, j: (0, 0)),
                  pl.BlockSpec((1, d, tn), lambda i, j: (i, 0, j)),
                  pl.BlockSpec((1, 1, tn), lambda i, j: (i, 0, j))],
        out_specs=pl.BlockSpec((1, rows, tn), lambda i, j: (i, 0, j)),
        out_shape=jax.ShapeDtypeStruct((depth, rows, n), F32),
        compiler_params=_cparams(("arbitrary", "arbitrary")),
        name="modulation",
    )(cs, mod_w, mod_b.reshape(depth, 1, n))


def _tok_kernel(mode, readout, tT, nT, *refs):
    refs = list(refs)
    x_ref = refs.pop(0)
    if mode == "grid":
        xp_ref = refs.pop(0)
        xn_ref = refs.pop(0)
    (mod_ref, ng_ref, mu_ref, wr_ref, wk_ref, wv_ref, w1_ref, w2_ref, w0_ref,
     a1_ref, a2_ref, a0_ref, kkp_ref, ka_ref) = refs[:14]
    refs = refs[14:]
    if readout:
        g1_ref, g2_ref, rk_ref = refs[:3]
        refs = refs[3:]
    r_o, k_o, v_o, kk_o, ic0_o, ic1_o, lw0_o, lw1_o = refs[:8]
    refs = refs[8:]

    d = x_ref.shape[-1]
    sh = mod_ref[0, :, 0:d]
    sc = mod_ref[0, :, d:2 * d]
    g = ng_ref[...]

    def norm_mod(xx):
        return _rms(xx, g) * (1.0 + sc) + sh

    h = norm_mod(x_ref[0])
    if mode == "grid":
        i = pl.program_id(1)
        q = d // 4
        hp = jnp.where(i > 0, norm_mod(xp_ref[0]), 0.0)
        hn = jnp.where(i < nT - 1, norm_mod(xn_ref[0]), 0.0)
        col = _iota((tT, q), 0) % GRID_W
        left = jnp.where(col == 0, 0.0, pltpu.roll(h[:, 0:q], 1, 0))
        right = jnp.where(col == GRID_W - 1, 0.0, pltpu.roll(h[:, q:2 * q], tT - 1, 0))
        up = jnp.concatenate([hp[:, 2 * q:3 * q], h[:tT - GRID_W, 2 * q:3 * q]], axis=0)
        down = jnp.concatenate([h[GRID_W:, 3 * q:], hn[:, 3 * q:]], axis=0)
        hs = jnp.concatenate([left, right, up, down], axis=1)
    else:
        half = d // 2
        row = _iota((tT, half), 0)
        prev = jnp.where(row == 0, 0.0, pltpu.roll(h[:, :half], 1, 0))
        nxt = jnp.where(row == tT - 1, 0.0, pltpu.roll(h[:, half:], tT - 1, 0))
        hs = jnp.concatenate([prev, nxt], axis=1)

    dx = hs - h

    def mix(n):
        return (h + dx * mu_ref[n:n + 1, :]).astype(BF16)

    r = _dot(mix(0), wr_ref[...])
    k = _dot(mix(2), wk_ref[...])
    v = _dot(mix(3), wv_ref[...])
    w1o = jnp.tanh(_dot(mix(1), w1_ref[...])).astype(BF16)
    a1o = _dot(mix(4), a1_ref[...]).astype(BF16)
    ics = []
    for z, (lw_o, ic_o) in enumerate(((lw0_o, ic0_o), (lw1_o, ic1_o))):
        w_pre = _dot(w1o, w2_ref[z]) + w0_ref[z:z + 1, :]
        lw_o[0] = -DECAY_SCALE * jax.nn.sigmoid(w_pre)
        ic = jax.nn.sigmoid(_dot(a1o, a2_ref[z]) + a0_ref[z:z + 1, :])
        ic_o[0] = ic.astype(BF16)
        ics.append(ic)
    kk = k * kkp_ref[...]
    kk = kk * lax.rsqrt(jnp.maximum(_head_sum(kk * kk), 1e-24))
    r_o[0] = r.astype(BF16)
    k_o[0] = k.astype(BF16)
    v_o[0] = v.astype(BF16)
    kk_o[0] = kk.astype(BF16)
    if readout:
        gate_o, bon_o = refs
        gate = _dot(jax.nn.sigmoid(_dot(mix(5), g1_ref[...])).astype(BF16), g2_ref[...])
        gate_o[0] = gate.astype(BF16)
        k_avg = k * (1.0 + (0.5 * (ics[0] + ics[1]) - 1.0) * ka_ref[...])
        bon_o[0] = (_head_sum(r * k_avg * rk_ref[...]) * v).astype(BF16)


def _tokenwise(x, mod, mode, readout, p, tT):
    b, t, d = x.shape
    nT = t // tT
    row_spec = pl.BlockSpec((1, tT, d), lambda bi, i: (bi, i, 0))
    in_specs = [row_spec]
    args = [x]
    if mode == "grid":
        hb = tT // GRID_W
        in_specs += [
            pl.BlockSpec((1, GRID_W, d), lambda bi, i: (bi, jnp.maximum(i * hb - 1, 0), 0)),
            pl.BlockSpec((1, GRID_W, d), lambda bi, i: (bi, jnp.minimum((i + 1) * hb, t // GRID_W - 1), 0)),
        ]
        args += [x, x]
    if mod.shape[0] == 1:
        in_specs.append(pl.BlockSpec((1, 1, mod.shape[-1]), lambda bi, i: (0, 0, 0)))
    else:
        in_specs.append(pl.BlockSpec((1, 1, mod.shape[-1]), lambda bi, i: (bi, 0, 0)))
    args.append(mod)
    consts = [p["ng"], p["mu"], p["wr"], p["wk"], p["wv"], p["w1"], p["w2"], p["w0"],
              p["a1"], p["a2"], p["a0"], p["k_k"], p["k_a"]]
    if readout:
        consts += [p["g1"], p["g2"], p["r_k"]]
    for a in consts:
        in_specs.append(pl.BlockSpec(a.shape, lambda bi, i, _n=a.ndim: (0,) * _n))
        args.append(a)
    dtypes = [BF16] * 6 + [F32] * 2 + ([BF16] * 2 if readout else [])
    return pl.pallas_call(
        functools.partial(_tok_kernel, mode, readout, tT, nT),
        grid=(b, nT),
        in_specs=in_specs,
        out_specs=[row_spec] * len(dtypes),
        out_shape=[jax.ShapeDtypeStruct((b, t, d), dt) for dt in dtypes],
        compiler_params=_cparams(("arbitrary", "arbitrary")),
        name="rwkv_tokenwise_" + mode,
    )(*args)


def _scan_prep(r, kd, v, lw, kk, ic, reverse):
    c = r.shape[0]
    ti = _iota((c, c), 0)
    si = _iota((c, c), 1)
    tri = ((si >= ti) if reverse else (si <= ti)).astype(BF16)
    lw_hi, lw_lo = _split_bf16(lw)
    cum = _dot(tri, lw_hi) + _dot(tri, lw_lo)
    pc_log = cum[0:1] if reverse else cum[c - 1:c]
    p_inv = jnp.exp(-cum)
    p_rest = jnp.exp(pc_log - cum)
    b = kk * ic
    return {"rt": r * jnp.exp(cum), "at": -(kk * jnp.exp(cum - lw)),
            "bt": b * p_inv, "kt": kd * p_inv, "bp": b * p_rest, "kp": kd * p_rest,
            "v": v, "pc": jnp.exp(pc_log)}


def _scan_step(prep, states):
    c = prep[0]["v"].shape[0]
    n = 2 * c
    pp = len(states[0])
    chains = [(dr, p) for dr in range(2) for p in range(pp)]
    first = _iota((c, LANES), 1) < HEAD
    row = _iota((n, n), 0)
    col = _iota((n, n), 1)
    strict = (col < row, col > row)
    same = (row // SUB) == (col // SUB)

    def pair(name, dr, p):
        return prep[dr][name][:, p * LANES:(p + 1) * LANES]

    def stack(x):
        return jnp.concatenate([jnp.where(first, x, 0.0), jnp.where(first, 0.0, x)], axis=0)

    def ms(name, dr, p):
        return stack(pair(name, dr, p))

    def each(fn, *lists):
        return [fn(*xs) for xs in zip(*lists)]

    lhs = [jnp.concatenate([pair("at", dr, p), pair("rt", dr, p)], axis=0).astype(BF16)
           for dr, p in chains]
    rhs = [jnp.concatenate([ms("bt", dr, p), ms("kt", dr, p)], axis=0).astype(BF16) for dr, p in chains]
    at_ms = [ms("at", dr, p).astype(BF16) for dr, p in chains]
    v_ms = [ms("v", dr, p).astype(BF16) for dr, p in chains]
    gmat = each(_dot_nt, lhs, rhs)
    nmat = [jnp.where(strict[dr], stack(g[:c, :n]), 0.0) for (dr, _), g in zip(chains, gmat)]
    a_ak = [jnp.where(strict[dr], stack(g[:c, n:]), 0.0).astype(BF16) for (dr, _), g in zip(chains, gmat)]
    t_pl = _iota((c, 2 * n), 0)
    s_pl = _iota((c, 2 * n), 1) % c
    incl_pl = (s_pl <= t_pl, s_pl >= t_pl)
    a_r = [jnp.where(incl_pl[dr], g[c:], 0.0).astype(BF16)
           for (dr, _), g in zip(chains, gmat)]

    nd = [jnp.where(same, x, 0.0) for x in nmat]
    no = [jnp.where(same, 0.0, x) for x in nmat]
    pw = nd
    tp = nd
    span = 2
    while span < SUB:
        pw = each(_bdot, pw, pw)
        tp = each(lambda t, q: t + q + _bdot(t, q), tp, pw)
        span *= 2
    m = each(lambda t, o: o + _bdot(t, o), tp, no)
    qp = m
    pw = m
    span = 2
    while span < c // SUB:
        pw = each(_bdot, pw, pw)
        qp = each(lambda t, q: t + q + _bdot(t, q), qp, pw)
        span *= 2
    tp = each(lambda t, q: t + q + _bdot(q, t), tp, qp)

    bk_t = [jnp.concatenate([ms("bp", dr, p), ms("kp", dr, p)], axis=0).T.astype(BF16)
            for dr, p in chains]
    diag = row == col
    pc_col = [jnp.sum(jnp.where(diag, prep[dr]["pc"][:, p * LANES:(p + 1) * LANES], 0.0),
                      axis=1, keepdims=True) for dr, p in chains]

    s_t = [states[dr][p].astype(BF16) for dr, p in chains]
    rhs_z = each(lambda l, a, st, v: _dot(jnp.concatenate([l, a], axis=1),
                                          jnp.concatenate([st, v], axis=0)), at_ms, a_ak, s_t, v_ms)
    z = each(lambda t, x: x + _bdot(t, x), tp, rhs_z)
    zv = each(lambda zz, v: jnp.concatenate([zz.astype(BF16), v], axis=0), z, v_ms)
    y_pl = each(lambda l, st, ar, w: _dot(l[c:], st) + _dot(ar, w), lhs, s_t, a_r, zv)
    y = [jnp.concatenate([yy for (d2, _), yy in zip(chains, y_pl) if d2 == dr], axis=1)
         for dr in range(2)]
    new_states = [[None] * pp for _ in range(2)]
    for (dr, p), bt_, w, pc in zip(chains, bk_t, zv, pc_col):
        new_states[dr][p] = states[dr][p] * pc + _dot(bt_, w)
    return y, new_states


def _scan_kernel(pp, nc, has_s0, want_y, want_state, *refs):
    refs = list(refs)
    ka_ref = refs[12]
    dirs = (refs[:6], refs[6:12])
    refs = refs[13:]
    if has_s0:
        s0_ref = refs.pop(0)
    if want_y:
        y_refs = (refs.pop(0), refs.pop(0))
    if want_state:
        st_ref = refs.pop(0)
    s_scr = refs.pop(0)
    c = pl.program_id(2)

    @pl.when(c == 0)
    def _():
        if has_s0:
            s_scr[...] = s0_ref[0]
        else:
            s_scr[...] = jnp.zeros_like(s_scr)

    prep = []
    for dr, (r_ref, k_ref, v_ref, kk_ref, ic_ref, lw_ref) in enumerate(dirs):
        ic = ic_ref[0].astype(F32)
        kd = k_ref[0].astype(F32) * (1.0 + (ic - 1.0) * ka_ref[...])
        prep.append(_scan_prep(r_ref[0].astype(F32), kd, v_ref[0].astype(F32), lw_ref[0],
                               kk_ref[0].astype(F32), ic, reverse=(dr == 1)))
    states = [[s_scr[dr, p] for p in range(pp)] for dr in range(2)]
    y, new_states = _scan_step(prep, states)
    for dr in range(2):
        for p in range(pp):
            s_scr[dr, p] = new_states[dr][p]
        if want_y:
            y_refs[dr][0] = y[dr].astype(BF16)

    if want_state:
        @pl.when(c == nc - 1)
        def _():
            st_ref[0] = s_scr[...]


def _scan(tok, ka, s0, want_y, want_state):
    r, k, v, kk, ic0, ic1, lw0, lw1 = tok
    b, t, d = r.shape
    nc = t // CHUNK
    npair = d // LANES
    pp = min(SCAN_PAIRS, npair)
    w = LANES * pp
    f_spec = pl.BlockSpec((1, CHUNK, w), lambda bi, hi, ci: (bi, ci, hi))
    b_spec = pl.BlockSpec((1, CHUNK, w), lambda bi, hi, ci: (bi, nc - 1 - ci, hi))
    st_spec = pl.BlockSpec((1, 2, pp, LANES, LANES), lambda bi, hi, ci: (bi, 0, hi, 0, 0))
    in_specs = [f_spec] * 6 + [b_spec] * 6 + [pl.BlockSpec((1, w), lambda bi, hi, ci: (0, hi))]
    args = [r, k, v, kk, ic0, lw0, r, k, v, kk, ic1, lw1, ka]
    if s0 is not None:
        in_specs.append(st_spec)
        args.append(s0)
    out_specs, out_shape = [], []
    if want_y:
        out_specs += [f_spec, b_spec]
        out_shape += [jax.ShapeDtypeStruct((b, t, d), BF16)] * 2
    if want_state:
        out_specs.append(st_spec)
        out_shape.append(jax.ShapeDtypeStruct((b, 2, npair, LANES, LANES), F32))
    return pl.pallas_call(
        functools.partial(_scan_kernel, pp, nc, s0 is not None, want_y, want_state),
        grid=(b, npair // pp, nc),
        in_specs=in_specs,
        out_specs=out_specs,
        out_shape=out_shape,
        scratch_shapes=[pltpu.VMEM((2, pp, LANES, LANES), F32)],
        compiler_params=_cparams(("arbitrary", "arbitrary", "arbitrary")),
        name="rwkv_scan_ctx" if s0 is None else "rwkv_scan_latent",
    )(*args)


def _readout_kernel(yf_ref, yb_ref, bon_ref, gate_ref, x_ref, mod_ref, lnw_ref, lnb_ref,
                    wo_ref, g1_ref, g2_ref, x_o, h_o):
    d = x_ref.shape[-1]
    y = yf_ref[0].astype(F32) + yb_ref[0].astype(F32)
    mean = _head_sum(y) * (1.0 / HEAD)
    yc = y - mean
    var = _head_sum(yc * yc) * (1.0 / HEAD)
    o = yc * lax.rsqrt(var + GN_EPS) * lnw_ref[...] + lnb_ref[...] + bon_ref[0]
    att = _dot((o * gate_ref[0]).astype(BF16), wo_ref[...])
    gt_a = mod_ref[0, :, 2 * d:3 * d]
    sh_f = mod_ref[0, :, 3 * d:4 * d]
    sc_f = mod_ref[0, :, 4 * d:5 * d]
    x1 = x_ref[0] + gt_a * _rms(att, g1_ref[...])
    x_o[0] = x1
    h_o[0] = (_rms(x1, g2_ref[...]) * (1.0 + sc_f) + sh_f).astype(BF16)


def _readout(yf, yb, bon, gate, x, mod, lnw, lnb, wo, g1, g2, tT):
    b, t, d = x.shape
    row_spec = pl.BlockSpec((1, tT, d), lambda bi, i: (bi, i, 0))
    vec = pl.BlockSpec((1, d), lambda bi, i: (0, 0))
    return pl.pallas_call(
        _readout_kernel,
        grid=(b, t // tT),
        in_specs=[row_spec] * 5 + [pl.BlockSpec((1, 1, 6 * d), lambda bi, i: (bi, 0, 0)),
                                   vec, vec, pl.BlockSpec((d, d), lambda bi, i: (0, 0)), vec, vec],
        out_specs=[row_spec, row_spec],
        out_shape=[jax.ShapeDtypeStruct((b, t, d), F32), jax.ShapeDtypeStruct((b, t, d), BF16)],
        compiler_params=_cparams(("arbitrary", "arbitrary")),
        name="rwkv_readout",
    )(yf, yb, bon, gate, x, mod, lnw, lnb, wo, g1, g2)


def _ffn_kernel(nf, h_ref, wg_ref, wu_ref, wd_ref, x_ref, mod_ref, g3_ref, gn_ref, mod2_ref,
                x_o, h_o, acc):
    f = pl.program_id(1)
    d = x_ref.shape[-1]

    @pl.when(f == 0)
    def _():
        acc[...] = jnp.zeros_like(acc)

    h = h_ref[...]
    act = _silu(_dot(h, wg_ref[...])) * _dot(h, wu_ref[...])
    acc[...] += _dot(act.astype(BF16), wd_ref[...])

    @pl.when(f == nf - 1)
    def _():
        gt_f = mod_ref[0, :, 5 * d:6 * d]
        x2 = x_ref[...] + gt_f * _rms(acc[...], g3_ref[...])
        x_o[...] = x2
        sh = mod2_ref[0, :, 0:d]
        sc = mod2_ref[0, :, d:2 * d]
        h_o[...] = (_rms(x2, gn_ref[...]) * (1.0 + sc) + sh).astype(BF16)


def _ffn(h, w_gu, w_down, x, mod, g3, gn, mod2, rows_per_batch, tm, tf):
    n, d = x.shape
    ff = w_down.shape[0]
    nf = ff // tf
    per = rows_per_batch // tm
    row = pl.BlockSpec((tm, d), lambda i, f: (i, 0))
    vec = pl.BlockSpec((1, d), lambda i, f: (0, 0))
    modspec = pl.BlockSpec((1, 1, 6 * d), lambda i, f: (i // per, 0, 0))
    return pl.pallas_call(
        functools.partial(_ffn_kernel, nf),
        grid=(n // tm, nf),
        in_specs=[row,
                  pl.BlockSpec((d, tf), lambda i, f: (0, f)),
                  pl.BlockSpec((d, tf), lambda i, f: (0, nf + f)),
                  pl.BlockSpec((tf, d), lambda i, f: (f, 0)),
                  row, modspec, vec, vec, modspec],
        out_specs=[row, row],
        out_shape=[jax.ShapeDtypeStruct((n, d), F32), jax.ShapeDtypeStruct((n, d), BF16)],
        scratch_shapes=[pltpu.VMEM((tm, d), F32)],
        compiler_params=_cparams(("arbitrary", "arbitrary")),
        name="dense_swiglu",
    )(h, w_gu, w_gu, w_down, x, mod, g3, gn, mod2)


HALO = 16


def _conv_kernel(tT, nT, h_ref, hp_ref, hn_ref, x_ref, mod_ref, win_ref, cw_ref, wo_ref,
                 g1_ref, g2_ref, wr_ref, x_o, h_o, lg_o):
    i = pl.program_id(1)
    d = x_ref.shape[-1]
    n = tT + 2 * HALO
    h_ext = jnp.concatenate([hp_ref[0], h_ref[0], hn_ref[0]], axis=0)
    proj = _dot(h_ext, win_ref[...])
    z = proj[:, d:2 * d] * proj[:, 2 * d:]
    row = _iota((n, d), 0)
    dead = ((row < HALO) & (i == 0)) | ((row >= HALO + tT) & (i == nT - 1))
    z = jnp.where(dead, 0.0, z)
    conv = (pltpu.roll(z, 1, 0) * cw_ref[0:1, :] + z * cw_ref[1:2, :]
            + pltpu.roll(z, n - 1, 0) * cw_ref[2:3, :])
    gated = (proj[HALO:HALO + tT, 0:d] * conv[HALO:HALO + tT]).astype(BF16)
    y = _dot(gated, wo_ref[...])
    gt_a = mod_ref[0, :, 2 * d:3 * d]
    sh_f = mod_ref[0, :, 3 * d:4 * d]
    sc_f = mod_ref[0, :, 4 * d:5 * d]
    x3 = x_ref[0] + gt_a * _rms(y, g1_ref[...])
    x_o[0] = x3
    h4 = _rms(x3, g2_ref[...]) * (1.0 + sc_f) + sh_f
    s_rows = d // LANES
    for c in range(s_rows):
        h_o[0, pl.ds(c, tT, stride=s_rows), :] = h4[:, c * LANES:(c + 1) * LANES]
    h_hi, h_lo = _split_bf16(h4)
    lg_o[0] = _dot(h_hi, wr_ref[0]) + _dot(h_lo, wr_ref[0]) + _dot(h_hi, wr_ref[1])


def _conv_layer(h, x, mod, w_in, conv_w, w_out, g1, g2, w_router, tT):
    b, t, d = x.shape
    nT = t // tT
    hb = tT // HALO
    row_spec = pl.BlockSpec((1, tT, d), lambda bi, i: (bi, i, 0))
    vec = pl.BlockSpec((1, d), lambda bi, i: (0, 0))
    full = lambda a: pl.BlockSpec(a.shape, lambda bi, i, _n=a.ndim: (0,) * _n)
    return pl.pallas_call(
        functools.partial(_conv_kernel, tT, nT),
        grid=(b, nT),
        in_specs=[row_spec,
                  pl.BlockSpec((1, HALO, d), lambda bi, i: (bi, jnp.maximum(i * hb - 1, 0), 0)),
                  pl.BlockSpec((1, HALO, d), lambda bi, i: (bi, jnp.minimum((i + 1) * hb, t // HALO - 1), 0)),
                  row_spec, pl.BlockSpec((1, 1, 6 * d), lambda bi, i: (bi, 0, 0)),
                  full(w_in), full(conv_w), full(w_out), vec, vec, full(w_router)],
        out_specs=[row_spec, pl.BlockSpec((1, tT * (d // LANES), LANES), lambda bi, i: (bi, i, 0)),
                   pl.BlockSpec((1, tT, LANES), lambda bi, i: (bi, i, 0))],
        out_shape=[jax.ShapeDtypeStruct((b, t, d), F32),
                   jax.ShapeDtypeStruct((b, t * (d // LANES), LANES), F32),
                   jax.ShapeDtypeStruct((b, t, LANES), F32)],
        compiler_params=_cparams(("arbitrary", "arbitrary")),
        name="short_conv",
    )(h, h, h, x, mod, w_in, conv_w, w_out, g1, g2, w_router)


GATHER_ROWS = 512


def _gather_kernel(tg, nsteps, live_ref, tok_cur, tok_nxt, h_hbm, o_ref, buf, sem):
    j = pl.program_id(0)
    slot = j % 2
    s_rows = buf.shape[1] // tg

    def issue(tok_ref, s):
        for r in range(tg):
            src = pl.multiple_of(tok_ref[0, 0, r] * s_rows, s_rows)
            pltpu.make_async_copy(h_hbm.at[pl.ds(src, s_rows)],
                                  buf.at[s, pl.ds(r * s_rows, s_rows)], sem.at[s]).start(priority=r % 2)

    @pl.when((j == 0) & (live_ref[0] > 0))
    def _():
        issue(tok_cur, 0)

    @pl.when(j + 1 < nsteps)
    def _():
        @pl.when(live_ref[j + 1] > 0)
        def _():
            issue(tok_nxt, 1 - slot)

    @pl.when(live_ref[j] > 0)
    def _():
        pltpu.make_async_copy(h_hbm.at[pl.ds(0, tg * s_rows)], buf.at[slot], sem.at[slot]).wait()
        for c in range(s_rows):
            o_ref[:, c * LANES:(c + 1) * LANES] = buf[slot, pl.ds(c, tg, stride=s_rows), :].astype(BF16)

    @pl.when(live_ref[j] == 0)
    def _():
        o_ref[...] = jnp.zeros(o_ref.shape, BF16)


def _gather_rows(h, d, slot_tok, live, tg):
    s_rows = d // LANES
    nsteps = slot_tok.shape[0] // tg
    tok3 = slot_tok.reshape(nsteps, 1, tg)
    grid_spec = pltpu.PrefetchScalarGridSpec(
        num_scalar_prefetch=1,
        grid=(nsteps,),
        in_specs=[pl.BlockSpec((1, 1, tg), lambda j, lv: (j, 0, 0), memory_space=pltpu.SMEM),
                  pl.BlockSpec((1, 1, tg), lambda j, lv: (jnp.minimum(j + 1, nsteps - 1), 0, 0),
                               memory_space=pltpu.SMEM),
                  pl.BlockSpec(memory_space=pl.ANY)],
        out_specs=pl.BlockSpec((tg, d), lambda j, lv: (j, 0)),
        scratch_shapes=[pltpu.VMEM((2, tg * s_rows, LANES), F32), pltpu.SemaphoreType.DMA((2,))],
    )
    return pl.pallas_call(
        functools.partial(_gather_kernel, tg, nsteps),
        grid_spec=grid_spec,
        out_shape=jax.ShapeDtypeStruct((nsteps * tg, d), BF16),
        compiler_params=_cparams(("arbitrary",)),
        name="moe_dispatch",
    )(live, tok3, tok3, h)


MOE_SUB = 512


def _moe_kernel(nf, tm, e_ref, nv_ref, x_ref, wg_ref, wu_ref, wd_ref, y_o):
    i = pl.program_id(0)
    f = pl.program_id(1)
    nv = nv_ref[i]

    nsub = tm // MOE_SUB

    @pl.when(f == 0)
    def _():
        y_o[...] = jnp.zeros(y_o.shape, F32)

    def weights():
        return wg_ref[0].astype(BF16), wu_ref[0].astype(BF16), wd_ref[0].astype(BF16)

    def sub_block(s, wg, wu, wd):
        rows = slice(s * MOE_SUB, (s + 1) * MOE_SUB)
        h = x_ref[rows, :]
        act = _silu(_dot(h, wg)) * _dot(h, wu)
        y_o[rows, :] += _dot(act.astype(BF16), wd)

    @pl.when(nv > (nsub - 1) * MOE_SUB)
    def _():
        w = weights()
        for s in range(nsub):
            sub_block(s, *w)

    @pl.when((nv > 0) & (nv <= (nsub - 1) * MOE_SUB))
    def _():
        w = weights()
        for s in range(nsub - 1):
            @pl.when(nv > s * MOE_SUB)
            def _():
                sub_block(s, *w)


def _moe_experts(xs, blk_e, blk_nv, w_gu, w_down, tm, tf):
    n, d = xs.shape
    ff = w_down.shape[1]
    nf = ff // tf
    nblk = n // tm

    def f_eff(i, f, nv_ref):
        return jnp.where(nv_ref[i] > 0, f, nf - 1)

    grid_spec = pltpu.PrefetchScalarGridSpec(
        num_scalar_prefetch=2,
        grid=(nblk, nf),
        in_specs=[pl.BlockSpec((tm, d), lambda i, f, e, nv: (i, 0)),
                  pl.BlockSpec((1, d, tf), lambda i, f, e, nv: (e[i], 0, f_eff(i, f, nv))),
                  pl.BlockSpec((1, d, tf), lambda i, f, e, nv: (e[i], 0, nf + f_eff(i, f, nv))),
                  pl.BlockSpec((1, tf, d), lambda i, f, e, nv: (e[i], f_eff(i, f, nv), 0))],
        out_specs=pl.BlockSpec((tm, d), lambda i, f, e, nv: (i, 0)),
    )
    return pl.pallas_call(
        functools.partial(_moe_kernel, nf, tm),
        grid_spec=grid_spec,
        out_shape=jax.ShapeDtypeStruct((n, d), F32),
        compiler_params=_cparams(("arbitrary", "arbitrary")),
        name="moe_experts",
    )(blk_e, blk_nv, xs, w_gu, w_gu, w_down)


def _combine_kernel(tT, nsteps, pos_cur, pos_nxt, ys_hbm, gt_ref, x_ref, mod_ref, g_ref, o_ref,
                    buf, sem):
    j = pl.program_id(0)
    slot = j % 2
    d = x_ref.shape[-1]

    def issue(pos_ref, s):
        for r in range(tT):
            for k in range(TOP_K):
                pltpu.make_async_copy(ys_hbm.at[pl.ds(pos_ref[0, k, r], 1)],
                                      buf.at[s, k, pl.ds(r, 1)], sem.at[s]).start(priority=k)

    @pl.when(j == 0)
    def _():
        issue(pos_cur, 0)

    @pl.when(j + 1 < nsteps)
    def _():
        issue(pos_nxt, 1 - slot)

    for k in range(TOP_K):
        pltpu.make_async_copy(ys_hbm.at[pl.ds(0, tT)], buf.at[slot, k], sem.at[slot]).wait()
    gates = gt_ref[...]
    y = buf[slot, 0] * gates[:, 0:1] + buf[slot, 1] * gates[:, 1:2]
    gt_f = mod_ref[0, :, 5 * d:6 * d]
    o_ref[...] = x_ref[...] + gt_f * _rms(y, g_ref[...])


def _combine(ys, pos, gates, x, mod, g, rows_per_batch, tT):
    n, d = x.shape
    nsteps = n // tT
    per = rows_per_batch // tT
    pos3 = pos.reshape(nsteps, tT, TOP_K).transpose(0, 2, 1)
    row_spec = pl.BlockSpec((tT, d), lambda j: (j, 0))
    smem = functools.partial(pl.BlockSpec, (1, TOP_K, tT), memory_space=pltpu.SMEM)
    return pl.pallas_call(
        functools.partial(_combine_kernel, tT, nsteps),
        grid=(nsteps,),
        in_specs=[smem(lambda j: (j, 0, 0)),
                  smem(lambda j: (jnp.minimum(j + 1, nsteps - 1), 0, 0)),
                  pl.BlockSpec(memory_space=pl.ANY),
                  pl.BlockSpec((tT, LANES), lambda j: (j, 0)),
                  row_spec, pl.BlockSpec((1, 1, 6 * d), lambda j: (j // per, 0, 0)),
                  pl.BlockSpec((1, d), lambda j: (0, 0))],
        out_specs=row_spec,
        out_shape=jax.ShapeDtypeStruct((n, d), F32),
        scratch_shapes=[pltpu.VMEM((2, TOP_K, tT, d), F32), pltpu.SemaphoreType.DMA((2,))],
        compiler_params=_cparams(("arbitrary",)),
        name="moe_combine",
    )(pos3, pos3, ys, gates, x, mod, g)


def _route(logits, n_experts, tm):
    n = logits.shape[0]
    nk = n * TOP_K
    top_v, top_e = lax.top_k(logits, TOP_K)
    gates = jax.nn.softmax(top_v, axis=-1)
    flat_e = top_e.reshape(-1).astype(jnp.int32)
    onehot = (flat_e[None, :] == jnp.arange(n_experts, dtype=jnp.int32)[:, None]).astype(jnp.int32)
    csum = jnp.cumsum(onehot, axis=1)
    counts = csum[:, -1]
    rank = jnp.sum((csum - onehot) * onehot, axis=0)
    nblk_e = (counts + tm - 1) // tm
    start = jnp.cumsum(counts) - counts
    bend = jnp.cumsum(nblk_e)
    bstart = bend - nblk_e
    pos = (jnp.sum(onehot * (bstart * tm)[:, None], axis=0) + rank).reshape(n, TOP_K)
    nblk = -(-nk // tm) + n_experts
    blk = jnp.arange(nblk, dtype=jnp.int32)
    blk_e = jnp.minimum(jnp.searchsorted(bend, blk, side="right"), n_experts - 1).astype(jnp.int32)
    blk_nv = jnp.clip(counts[blk_e] - (blk - bstart[blk_e]) * tm, 0, tm)
    blk_nv = jnp.where(blk < bend[-1], blk_nv, 0).astype(jnp.int32)
    order = jnp.argsort(flat_e, stable=True).astype(jnp.int32)
    order = jnp.concatenate([order, jnp.zeros((tm,), jnp.int32)])
    src0 = jnp.where(blk_nv > 0, start[blk_e] + (blk - bstart[blk_e]) * tm, 0)
    slot_tok = jnp.concatenate([lax.dynamic_slice_in_dim(order, src0[k], tm) for k in range(nblk)])
    return gates, slot_tok // TOP_K, pos, blk_e, blk_nv


def kernel(x, c, ctx, c_ctx, mod_w, mod_b, norm_g, rwkv_mu, rwkv_w_rkv, rwkv_w0, rwkv_w1, rwkv_w2,
           rwkv_a0, rwkv_a1, rwkv_a2, rwkv_g1, rwkv_g2, rwkv_k_k, rwkv_k_a, rwkv_r_k, rwkv_ln_w,
           rwkv_ln_b, rwkv_w_out, conv_w_in, conv_w, conv_w_out, ffn_w_gu, ffn_w_down,
           moe_router, moe_w_gu, moe_w_down):
    b, t, d = x.shape
    n_experts = moe_router.shape[-1]
    rows = 16
    cs = jnp.zeros((rows, d), F32).at[:b].set(c).at[b].set(c_ctx)
    mods = _modulation(cs, mod_w, mod_b)
    mod0 = mods[0, :b].reshape(b, 1, 6 * d)
    mod0c = mods[0, b].reshape(1, 1, 6 * d)
    mod1 = mods[1, :b].reshape(b, 1, 6 * d)

    def pad_lora(w):
        zr = jnp.zeros_like(w[0])
        return jnp.stack([jnp.concatenate([w[0], zr], 0), jnp.concatenate([zr, w[1]], 0)]).astype(BF16)

    lg = rwkv_g1.shape[-1]
    lgp = -(-lg // LANES) * LANES
    p = {
        "ng": norm_g[0, 0].reshape(1, d),
        "mu": rwkv_mu[0],
        "wr": rwkv_w_rkv[0, 0].astype(BF16), "wk": rwkv_w_rkv[0, 1].astype(BF16),
        "wv": rwkv_w_rkv[0, 2].astype(BF16),
        "w1": jnp.concatenate([rwkv_w1[0, 0], rwkv_w1[0, 1]], axis=1).astype(BF16),
        "w2": pad_lora(rwkv_w2[0]), "w0": rwkv_w0[0],
        "a1": jnp.concatenate([rwkv_a1[0, 0], rwkv_a1[0, 1]], axis=1).astype(BF16),
        "a2": pad_lora(rwkv_a2[0]), "a0": rwkv_a0[0],
        "k_k": rwkv_k_k[0].reshape(1, d), "k_a": rwkv_k_a[0].reshape(1, d),
        "g1": jnp.pad(rwkv_g1[0], ((0, 0), (0, lgp - lg))).astype(BF16),
        "g2": jnp.pad(rwkv_g2[0], ((0, lgp - lg), (0, 0))).astype(BF16),
        "r_k": rwkv_r_k[0].reshape(1, d),
    }

    tok_c = _tokenwise(ctx, mod0c, "seq", False, p, ctx.shape[1])
    s_ctx = _scan(tok_c, p["k_a"], None, want_y=False, want_state=True)[0]
    tt = min(ROW_TILE, t)
    tm = min(FFN_ROWS, t)
    tf = min(FFN_COLS, ffn_w_down.shape[1])
    tok_l = _tokenwise(x, mod0, "grid", True, p, tt)
    yf, yb = _scan(tok_l[:8], p["k_a"], s_ctx, want_y=True, want_state=False)
    x1, h2 = _readout(yf, yb, tok_l[9], tok_l[8], x, mod0,
                      rwkv_ln_w[0].reshape(1, d), rwkv_ln_b[0].reshape(1, d),
                      rwkv_w_out[0].astype(BF16), norm_g[0, 1].reshape(1, d),
                      norm_g[0, 2].reshape(1, d), tt)

    x2, h3 = _ffn(h2.reshape(b * t, d), ffn_w_gu[0].astype(BF16), ffn_w_down[0].astype(BF16),
                  x1.reshape(b * t, d), mod0, norm_g[0, 3].reshape(1, d),
                  norm_g[1, 0].reshape(1, d), mod1, t, tm, tf)

    w_router = jnp.stack(_split_bf16(jnp.pad(moe_router[0], ((0, 0), (0, LANES - n_experts)))))
    x3, h4, logits = _conv_layer(h3.reshape(b, t, d), x2.reshape(b, t, d), mod1,
                                 conv_w_in[0].astype(BF16), conv_w[0], conv_w_out[0].astype(BF16),
                                 norm_g[1, 1].reshape(1, d), norm_g[1, 2].reshape(1, d), w_router, tt)

    tme = min(MOE_ROWS, b * t * TOP_K)
    gates, slot_tok, pos, blk_e, blk_nv = _route(logits.reshape(b * t, LANES)[:, :n_experts],
                                                 n_experts, tme)
    tg = min(GATHER_ROWS, tme)
    live = (blk_nv[:, None] > jnp.arange(0, tme, tg, dtype=jnp.int32)[None, :]).astype(jnp.int32)
    xs = _gather_rows(h4.reshape(-1, LANES), d, slot_tok, live.reshape(-1), tg)
    ys = _moe_experts(xs, blk_e, blk_nv, moe_w_gu[0], moe_w_down[0], tme,
                      min(MOE_COLS, moe_w_down.shape[2]))
    gates_p = jnp.pad(gates, ((0, 0), (0, LANES - TOP_K)))
    out = _combine(ys, pos, gates_p, x3.reshape(b * t, d), mod1, norm_g[1, 3].reshape(1, d), t, tt)
    return out.reshape(b, t, d)
```

```python
import functools
import math

import jax
import jax.numpy as jnp
from jax import lax
from jax.experimental import pallas as pl
from jax.experimental.pallas import tpu as pltpu

F32 = jnp.float32
BF16 = jnp.bfloat16

HEAD = 64
LANES = 128
GRID_W = 64
CHUNK = 64
SUB = 16
SCAN_PAIRS = 8
NORM_EPS = 1e-6
GN_EPS = 64e-5
DECAY_SCALE = math.exp(-0.5)
TOP_K = 2
VMEM_LIMIT = 56 * 1024 * 1024


ROW_TILE = 256
FFN_ROWS = 1024
FFN_COLS = 512
MOE_ROWS = 2560
MOE_COLS = 512


def _cparams(sem):
    return pltpu.CompilerParams(dimension_semantics=sem, vmem_limit_bytes=VMEM_LIMIT)


def _dot(a, b):
    return jnp.dot(a, b, preferred_element_type=F32)


def _dot_nt(a, b):
    return lax.dot_general(a, b, (((1,), (1,)), ((), ())), preferred_element_type=F32)


def _bdot(a, b):
    return _dot(a.astype(BF16), b.astype(BF16))


def _iota(shape, axis):
    return lax.broadcasted_iota(jnp.int32, shape, axis)


def _rms(x, g):
    return x * lax.rsqrt(jnp.mean(x * x, axis=-1, keepdims=True) + NORM_EPS) * g


def _silu(x):
    return x * jax.nn.sigmoid(x)


def _split_bf16(x):
    hi = x.astype(BF16)
    lo = (x - hi.astype(F32)).astype(BF16)
    return hi, lo


def _head_sum(x):
    ones = (_iota((LANES, LANES), 0) // HEAD == _iota((LANES, LANES), 1) // HEAD).astype(BF16)
    xb = x.astype(BF16)
    outs = []
    for p in range(x.shape[-1] // LANES):
        outs.append(_dot(xb[:, p * LANES:(p + 1) * LANES], ones))
    return jnp.concatenate(outs, axis=1)


def _sigmoid(x):
    return 0.5 * jnp.tanh(0.5 * x) + 0.5


def _mod_kernel(c_ref, w_ref, b_ref, o_ref):
    s = _silu(c_ref[...])
    o_ref[0] = jnp.dot(s, w_ref[0], preferred_element_type=F32,
                       precision=lax.Precision.HIGHEST) + b_ref[0]


def _modulation(cs, mod_w, mod_b):
    depth, d, n = mod_w.shape
    rows = cs.shape[0]
    tn = 1536
    return pl.pallas_call(
        _mod_kernel,
        grid=(depth, n // tn),
        in_specs=[pl.BlockSpec((rows, d), lambda i, j: (0, 0)),
                  pl.BlockSpec((1, d, tn), lambda i, j: (i, 0, j)),
                  pl.BlockSpec((1, 1, tn), lambda i, j: (i, 0, j))],
        out_specs=pl.BlockSpec((1, rows, tn), lambda i, j: (i, 0, j)),
        out_shape=jax.ShapeDtypeStruct((depth, rows, n), F32),
        compiler_params=_cparams(("arbitrary", "arbitrary")),
        name="modulation",
    )(cs, mod_w, mod_b.reshape(depth, 1, n))


def _tok_kernel(mode, readout, tT, nT, *refs):
    refs = list(refs)
    x_ref = refs.pop(0)
    if mode == "grid":
        xp_ref = refs.pop(0)
        xn_ref = refs.pop(0)
    (mod_ref, ng_ref, mu_ref, wr_ref, wk_ref, wv_ref, w1_ref, w2_ref, w0_ref,
     a1_ref, a2_ref, a0_ref, kkp_ref, ka_ref) = refs[:14]
    refs = refs[14:]
    if readout:
        g1_ref, g2_ref, rk_ref = refs[:3]
        refs = refs[3:]
    r_o, k_o, v_o, kk_o, ic0_o, ic1_o, lw0_o, lw1_o = refs[:8]
    refs = refs[8:]

    d = x_ref.shape[-1]
    sh = mod_ref[0, :, 0:d]
    sc = mod_ref[0, :, d:2 * d]
    g = ng_ref[...]

    def norm_mod(xx):
        return _rms(xx, g) * (1.0 + sc) + sh

    h = norm_mod(x_ref[0])
    if mode == "grid":
        i = pl.program_id(1)
        q = d // 4
        hp = jnp.where(i > 0, norm_mod(xp_ref[0]), 0.0)
        hn = jnp.where(i < nT - 1, norm_mod(xn_ref[0]), 0.0)
        col = _iota((tT, q), 0) % GRID_W
        left = jnp.where(col == 0, 0.0, pltpu.roll(h[:, 0:q], 1, 0))
        right = jnp.where(col == GRID_W - 1, 0.0, pltpu.roll(h[:, q:2 * q], tT - 1, 0))
        up = jnp.concatenate([hp[:, 2 * q:3 * q], h[:tT - GRID_W, 2 * q:3 * q]], axis=0)
        down = jnp.concatenate([h[GRID_W:, 3 * q:], hn[:, 3 * q:]], axis=0)
        hs = jnp.concatenate([left, right, up, down], axis=1)
    else:
        half = d // 2
        row = _iota((tT, half), 0)
        prev = jnp.where(row == 0, 0.0, pltpu.roll(h[:, :half], 1, 0))
        nxt = jnp.where(row == tT - 1, 0.0, pltpu.roll(h[:, half:], tT - 1, 0))
        hs = jnp.concatenate([prev, nxt], axis=1)

    hb = h.astype(BF16)
    dxb = (hs - h).astype(BF16)

    def mix(n):
        return hb + dxb * mu_ref[n:n + 1, :].astype(BF16)

    r = _dot(mix(0), wr_ref[...])
    k = _dot(mix(2), wk_ref[...])
    v = _dot(mix(3), wv_ref[...])
    w1o = jnp.tanh(_dot(mix(1), w1_ref[...])).astype(BF16)
    a1o = _dot(mix(4), a1_ref[...]).astype(BF16)
    ics = []
    for z, (lw_o, ic_o) in enumerate(((lw0_o, ic0_o), (lw1_o, ic1_o))):
        w_pre = _dot(w1o, w2_ref[z]) + w0_ref[z:z + 1, :]
        lw_o[0] = (-0.5 * DECAY_SCALE) * jnp.tanh(0.5 * w_pre) - 0.5 * DECAY_SCALE
        ic = _sigmoid(_dot(a1o, a2_ref[z]) + a0_ref[z:z + 1, :])
        ic_o[0] = ic.astype(BF16)
        ics.append(ic)
    kk = k * kkp_ref[...]
    kk = kk * lax.rsqrt(jnp.maximum(_head_sum(kk * kk), 1e-24))
    r_o[0] = r.astype(BF16)
    k_o[0] = k.astype(BF16)
    v_o[0] = v.astype(BF16)
    kk_o[0] = kk.astype(BF16)
    if readout:
        gate_o, bon_o = refs
        gate = _dot(_sigmoid(_dot(mix(5), g1_ref[...])).astype(BF16), g2_ref[...])
        gate_o[0] = gate.astype(BF16)
        k_avg = k * (1.0 + (0.5 * (ics[0] + ics[1]) - 1.0) * ka_ref[...])
        bon_o[0] = (_head_sum(r * k_avg * rk_ref[...]) * v).astype(BF16)


def _tokenwise(x, mod, mode, readout, p, tT):
    b, t, d = x.shape
    nT = t // tT
    row_spec = pl.BlockSpec((1, tT, d), lambda bi, i: (bi, i, 0))
    in_specs = [row_spec]
    args = [x]
    if mode == "grid":
        hb = tT // GRID_W
        in_specs += [
            pl.BlockSpec((1, GRID_W, d), lambda bi, i: (bi, jnp.maximum(i * hb - 1, 0), 0)),
            pl.BlockSpec((1, GRID_W, d), lambda bi, i: (bi, jnp.minimum((i + 1) * hb, t // GRID_W - 1), 0)),
        ]
        args += [x, x]
    if mod.shape[0] == 1:
        in_specs.append(pl.BlockSpec((1, 1, mod.shape[-1]), lambda bi, i: (0, 0, 0)))
    else:
        in_specs.append(pl.BlockSpec((1, 1, mod.shape[-1]), lambda bi, i: (bi, 0, 0)))
    args.append(mod)
    consts = [p["ng"], p["mu"], p["wr"], p["wk"], p["wv"], p["w1"], p["w2"], p["w0"],
              p["a1"], p["a2"], p["a0"], p["k_k"], p["k_a"]]
    if readout:
        consts += [p["g1"], p["g2"], p["r_k"]]
    for a in consts:
        in_specs.append(pl.BlockSpec(a.shape, lambda bi, i, _n=a.ndim: (0,) * _n))
        args.append(a)
    dtypes = [BF16] * 6 + [F32] * 2 + ([BF16] * 2 if readout else [])
    return pl.pallas_call(
        functools.partial(_tok_kernel, mode, readout, tT, nT),
        grid=(b, nT),
        in_specs=in_specs,
        out_specs=[row_spec] * len(dtypes),
        out_shape=[jax.ShapeDtypeStruct((b, t, d), dt) for dt in dtypes],
        compiler_params=_cparams(("arbitrary", "arbitrary")),
        name="rwkv_tokenwise_" + mode,
    )(*args)


def _scan_prep(r, kd, v, lw, kk, ic, reverse):
    c = r.shape[0]
    ti = _iota((c, c), 0)
    si = _iota((c, c), 1)
    tri = ((si >= ti) if reverse else (si <= ti)).astype(BF16)
    lw_hi, lw_lo = _split_bf16(lw)
    cum = _dot(tri, lw_hi) + _dot(tri, lw_lo)
    pc_log = cum[0:1] if reverse else cum[c - 1:c]
    p_inv = jnp.exp(-cum)
    p_rest = jnp.exp(pc_log - cum)
    b = kk * ic
    return {"rt": r * jnp.exp(cum), "at": -(kk * jnp.exp(cum - lw)),
            "bt": b * p_inv, "kt": kd * p_inv, "bp": b * p_rest, "kp": kd * p_rest,
            "v": v, "pc": jnp.exp(pc_log)}


def _scan_step(prep, states):
    c = prep[0]["v"].shape[0]
    n = 2 * c
    pp = len(states[0])
    chains = [(dr, p) for dr in range(2) for p in range(pp)]
    first = _iota((c, LANES), 1) < HEAD
    row = _iota((n, n), 0)
    col = _iota((n, n), 1)
    strict = (col < row, col > row)
    same = (row // SUB) == (col // SUB)

    def pair(name, dr, p):
        return prep[dr][name][:, p * LANES:(p + 1) * LANES]

    def stack(x):
        return jnp.concatenate([jnp.where(first, x, 0.0), jnp.where(first, 0.0, x)], axis=0)

    def ms(name, dr, p):
        return stack(pair(name, dr, p))

    def each(fn, *lists):
        return [fn(*xs) for xs in zip(*lists)]

    lhs = [jnp.concatenate([pair("at", dr, p), pair("rt", dr, p)], axis=0).astype(BF16)
           for dr, p in chains]
    rhs = [jnp.concatenate([ms("bt", dr, p), ms("kt", dr, p)], axis=0).astype(BF16) for dr, p in chains]
    at_ms = [ms("at", dr, p).astype(BF16) for dr, p in chains]
    v_ms = [ms("v", dr, p).astype(BF16) for dr, p in chains]
    gmat = each(_dot_nt, lhs, rhs)
    nmat = [jnp.where(strict[dr], stack(g[:c, :n]), 0.0) for (dr, _), g in zip(chains, gmat)]
    a_ak = [jnp.where(strict[dr], stack(g[:c, n:]), 0.0).astype(BF16) for (dr, _), g in zip(chains, gmat)]
    t_pl = _iota((c, 2 * n), 0)
    s_pl = _iota((c, 2 * n), 1) % c
    incl_pl = (s_pl <= t_pl, s_pl >= t_pl)
    a_r = [jnp.where(incl_pl[dr], g[c:], 0.0).astype(BF16)
           for (dr, _), g in zip(chains, gmat)]

    nd = [jnp.where(same, x, 0.0) for x in nmat]
    no = [jnp.where(same, 0.0, x) for x in nmat]
    pw = nd
    tp = nd
    span = 2
    while span < SUB:
        pw = each(_bdot, pw, pw)
        tp = each(lambda t, q: t + q + _bdot(t, q), tp, pw)
        span *= 2
    m = each(lambda t, o: o + _bdot(t, o), tp, no)
    qp = m
    pw = m
    span = 2
    while span < c // SUB:
        pw = each(_bdot, pw, pw)
        qp = each(lambda t, q: t + q + _bdot(t, q), qp, pw)
        span *= 2
    tp = each(lambda t, q: t + q + _bdot(q, t), tp, qp)

    bk_t = [jnp.concatenate([ms("bp", dr, p), ms("kp", dr, p)], axis=0).T.astype(BF16)
            for dr, p in chains]
    diag = row == col
    pc_col = [jnp.sum(jnp.where(diag, prep[dr]["pc"][:, p * LANES:(p + 1) * LANES], 0.0),
                      axis=1, keepdims=True) for dr, p in chains]

    s_t = [states[dr][p].astype(BF16) for dr, p in chains]
    rhs_z = each(lambda l, a, st, v: _dot(jnp.concatenate([l, a], axis=1),
                                          jnp.concatenate([st, v], axis=0)), at_ms, a_ak, s_t, v_ms)
    z = each(lambda t, x: x + _bdot(t, x), tp, rhs_z)
    zv = each(lambda zz, v: jnp.concatenate([zz.astype(BF16), v], axis=0), z, v_ms)
    y_pl = each(lambda l, st, ar, w: _dot(l[c:], st) + _dot(ar, w), lhs, s_t, a_r, zv)
    y = [jnp.concatenate([yy for (d2, _), yy in zip(chains, y_pl) if d2 == dr], axis=1)
         for dr in range(2)]
    new_states = [[None] * pp for _ in range(2)]
    for (dr, p), bt_, w, pc in zip(chains, bk_t, zv, pc_col):
        new_states[dr][p] = states[dr][p] * pc + _dot(bt_, w)
    return y, new_states


def _scan_kernel(pp, nc, has_s0, want_y, want_state, *refs):
    refs = list(refs)
    ka_ref = refs[12]
    dirs = (refs[:6], refs[6:12])
    refs = refs[13:]
    if has_s0:
        s0_ref = refs.pop(0)
    if want_y:
        y_refs = (refs.pop(0), refs.pop(0))
    if want_state:
        st_ref = refs.pop(0)
    s_scr = refs.pop(0)
    c = pl.program_id(2)

    @pl.when(c == 0)
    def _():
        if has_s0:
            s_scr[...] = s0_ref[0]
        else:
            s_scr[...] = jnp.zeros_like(s_scr)

    prep = []
    for dr, (r_ref, k_ref, v_ref, kk_ref, ic_ref, lw_ref) in enumerate(dirs):
        ic = ic_ref[0].astype(F32)
        kd = k_ref[0].astype(F32) * (1.0 + (ic - 1.0) * ka_ref[...])
        prep.append(_scan_prep(r_ref[0].astype(F32), kd, v_ref[0].astype(F32), lw_ref[0],
                               kk_ref[0].astype(F32), ic, reverse=(dr == 1)))
    states = [[s_scr[dr, p] for p in range(pp)] for dr in range(2)]
    y, new_states = _scan_step(prep, states)
    for dr in range(2):
        for p in range(pp):
            s_scr[dr, p] = new_states[dr][p]
        if want_y:
            y_refs[dr][0] = y[dr].astype(BF16)

    if want_state:
        @pl.when(c == nc - 1)
        def _():
            st_ref[0] = s_scr[...]


def _scan(tok, ka, s0, want_y, want_state):
    r, k, v, kk, ic0, ic1, lw0, lw1 = tok
    b, t, d = r.shape
    nc = t // CHUNK
    npair = d // LANES
    pp = min(SCAN_PAIRS, npair)
    w = LANES * pp
    f_spec = pl.BlockSpec((1, CHUNK, w), lambda bi, hi, ci: (bi, ci, hi))
    b_spec = pl.BlockSpec((1, CHUNK, w), lambda bi, hi, ci: (bi, nc - 1 - ci, hi))
    st_spec = pl.BlockSpec((1, 2, pp, LANES, LANES), lambda bi, hi, ci: (bi, 0, hi, 0, 0))
    in_specs = [f_spec] * 6 + [b_spec] * 6 + [pl.BlockSpec((1, w), lambda bi, hi, ci: (0, hi))]
    args = [r, k, v, kk, ic0, lw0, r, k, v, kk, ic1, lw1, ka]
    if s0 is not None:
        in_specs.append(st_spec)
        args.append(s0)
    out_specs, out_shape = [], []
    if want_y:
        out_specs += [f_spec, b_spec]
        out_shape += [jax.ShapeDtypeStruct((b, t, d), BF16)] * 2
    if want_state:
        out_specs.append(st_spec)
        out_shape.append(jax.ShapeDtypeStruct((b, 2, npair, LANES, LANES), F32))
    return pl.pallas_call(
        functools.partial(_scan_kernel, pp, nc, s0 is not None, want_y, want_state),
        grid=(b, npair // pp, nc),
        in_specs=in_specs,
        out_specs=out_specs,
        out_shape=out_shape,
        scratch_shapes=[pltpu.VMEM((2, pp, LANES, LANES), F32)],
        compiler_params=_cparams(("arbitrary", "arbitrary", "arbitrary")),
        name="rwkv_scan_ctx" if s0 is None else "rwkv_scan_latent",
    )(*args)


def _readout_kernel(yf_ref, yb_ref, bon_ref, gate_ref, x_ref, mod_ref, lnw_ref, lnb_ref,
                    wo_ref, g1_ref, g2_ref, x_o, h_o):
    d = x_ref.shape[-1]
    y = yf_ref[0].astype(F32) + yb_ref[0].astype(F32)
    mean = _head_sum(y) * (1.0 / HEAD)
    yc = y - mean
    var = _head_sum(yc * yc) * (1.0 / HEAD)
    o = yc * lax.rsqrt(var + GN_EPS) * lnw_ref[...] + lnb_ref[...] + bon_ref[0]
    att = _dot((o * gate_ref[0]).astype(BF16), wo_ref[...])
    gt_a = mod_ref[0, :, 2 * d:3 * d]
    sh_f = mod_ref[0, :, 3 * d:4 * d]
    sc_f = mod_ref[0, :, 4 * d:5 * d]
    x1 = x_ref[0] + gt_a * _rms(att, g1_ref[...])
    x_o[0] = x1
    h_o[0] = (_rms(x1, g2_ref[...]) * (1.0 + sc_f) + sh_f).astype(BF16)


def _readout(yf, yb, bon, gate, x, mod, lnw, lnb, wo, g1, g2, tT):
    b, t, d = x.shape
    row_spec = pl.BlockSpec((1, tT, d), lambda bi, i: (bi, i, 0))
    vec = pl.BlockSpec((1, d), lambda bi, i: (0, 0))
    return pl.pallas_call(
        _readout_kernel,
        grid=(b, t // tT),
        in_specs=[row_spec] * 5 + [pl.BlockSpec((1, 1, 6 * d), lambda bi, i: (bi, 0, 0)),
                                   vec, vec, pl.BlockSpec((d, d), lambda bi, i: (0, 0)), vec, vec],
        out_specs=[row_spec, row_spec],
        out_shape=[jax.ShapeDtypeStruct((b, t, d), F32), jax.ShapeDtypeStruct((b, t, d), BF16)],
        compiler_params=_cparams(("arbitrary", "arbitrary")),
        name="rwkv_readout",
    )(yf, yb, bon, gate, x, mod, lnw, lnb, wo, g1, g2)


def _ffn_kernel(nf, h_ref, wg_ref, wu_ref, wd_ref, x_ref, mod_ref, g3_ref, gn_ref, mod2_ref,
                x_o, h_o, acc):
    f = pl.program_id(1)
    d = x_ref.shape[-1]

    @pl.when(f == 0)
    def _():
        acc[...] = jnp.zeros_like(acc)

    h = h_ref[...]
    act = _silu(_dot(h, wg_ref[...])) * _dot(h, wu_ref[...])
    acc[...] += _dot(act.astype(BF16), wd_ref[...])

    @pl.when(f == nf - 1)
    def _():
        gt_f = mod_ref[0, :, 5 * d:6 * d]
        x2 = x_ref[...] + gt_f * _rms(acc[...], g3_ref[...])
        x_o[...] = x2
        sh = mod2_ref[0, :, 0:d]
        sc = mod2_ref[0, :, d:2 * d]
        h_o[...] = (_rms(x2, gn_ref[...]) * (1.0 + sc) + sh).astype(BF16)


def _ffn(h, w_gu, w_down, x, mod, g3, gn, mod2, rows_per_batch, tm, tf):
    n, d = x.shape
    ff = w_down.shape[0]
    nf = ff // tf
    per = rows_per_batch // tm
    row = pl.BlockSpec((tm, d), lambda i, f: (i, 0))
    vec = pl.BlockSpec((1, d), lambda i, f: (0, 0))
    modspec = pl.BlockSpec((1, 1, 6 * d), lambda i, f: (i // per, 0, 0))
    return pl.pallas_call(
        functools.partial(_ffn_kernel, nf),
        grid=(n // tm, nf),
        in_specs=[row,
                  pl.BlockSpec((d, tf), lambda i, f: (0, f)),
                  pl.BlockSpec((d, tf), lambda i, f: (0, nf + f)),
                  pl.BlockSpec((tf, d), lambda i, f: (f, 0)),
                  row, modspec, vec, vec, modspec],
        out_specs=[row, row],
        out_shape=[jax.ShapeDtypeStruct((n, d), F32), jax.ShapeDtypeStruct((n, d), BF16)],
        scratch_shapes=[pltpu.VMEM((tm, d), F32)],
        compiler_params=_cparams(("arbitrary", "arbitrary")),
        name="dense_swiglu",
    )(h, w_gu, w_gu, w_down, x, mod, g3, gn, mod2)


HALO = 16


def _conv_kernel(tT, nT, h_ref, hp_ref, hn_ref, x_ref, mod_ref, win_ref, cw_ref, wo_ref,
                 g1_ref, g2_ref, wr_ref, x_o, h_o, lg_o):
    i = pl.program_id(1)
    d = x_ref.shape[-1]
    n = tT + 2 * HALO
    h_ext = jnp.concatenate([hp_ref[0], h_ref[0], hn_ref[0]], axis=0)
    proj = _dot(h_ext, win_ref[...])
    z = proj[:, d:2 * d] * proj[:, 2 * d:]
    row = _iota((n, d), 0)
    dead = ((row < HALO) & (i == 0)) | ((row >= HALO + tT) & (i == nT - 1))
    z = jnp.where(dead, 0.0, z)
    conv = (pltpu.roll(z, 1, 0) * cw_ref[0:1, :] + z * cw_ref[1:2, :]
            + pltpu.roll(z, n - 1, 0) * cw_ref[2:3, :])
    gated = (proj[HALO:HALO + tT, 0:d] * conv[HALO:HALO + tT]).astype(BF16)
    y = _dot(gated, wo_ref[...])
    gt_a = mod_ref[0, :, 2 * d:3 * d]
    sh_f = mod_ref[0, :, 3 * d:4 * d]
    sc_f = mod_ref[0, :, 4 * d:5 * d]
    x3 = x_ref[0] + gt_a * _rms(y, g1_ref[...])
    x_o[0] = x3
    h4 = _rms(x3, g2_ref[...]) * (1.0 + sc_f) + sh_f
    s_rows = d // LANES
    for c in range(s_rows):
        h_o[0, pl.ds(c, tT, stride=s_rows), :] = h4[:, c * LANES:(c + 1) * LANES]
    h_hi, h_lo = _split_bf16(h4)
    lg_o[0] = _dot(h_hi, wr_ref[0]) + _dot(h_lo, wr_ref[0]) + _dot(h_hi, wr_ref[1])


def _conv_layer(h, x, mod, w_in, conv_w, w_out, g1, g2, w_router, tT):
    b, t, d = x.shape
    nT = t // tT
    hb = tT // HALO
    row_spec = pl.BlockSpec((1, tT, d), lambda bi, i: (bi, i, 0))
    vec = pl.BlockSpec((1, d), lambda bi, i: (0, 0))
    full = lambda a: pl.BlockSpec(a.shape, lambda bi, i, _n=a.ndim: (0,) * _n)
    return pl.pallas_call(
        functools.partial(_conv_kernel, tT, nT),
        grid=(b, nT),
        in_specs=[row_spec,
                  pl.BlockSpec((1, HALO, d), lambda bi, i: (bi, jnp.maximum(i * hb - 1, 0), 0)),
                  pl.BlockSpec((1, HALO, d), lambda bi, i: (bi, jnp.minimum((i + 1) * hb, t // HALO - 1), 0)),
                  row_spec, pl.BlockSpec((1, 1, 6 * d), lambda bi, i: (bi, 0, 0)),
                  full(w_in), full(conv_w), full(w_out), vec, vec, full(w_router)],
        out_specs=[row_spec, pl.BlockSpec((1, tT * (d // LANES), LANES), lambda bi, i: (bi, i, 0)),
                   pl.BlockSpec((1, tT, LANES), lambda bi, i: (bi, i, 0))],
        out_shape=[jax.ShapeDtypeStruct((b, t, d), F32),
                   jax.ShapeDtypeStruct((b, t * (d // LANES), LANES), F32),
                   jax.ShapeDtypeStruct((b, t, LANES), F32)],
        compiler_params=_cparams(("arbitrary", "arbitrary")),
        name="short_conv",
    )(h, h, h, x, mod, w_in, conv_w, w_out, g1, g2, w_router)


GATHER_ROWS = 512


def _gather_kernel(tg, nsteps, live_ref, tok_cur, tok_nxt, h_hbm, o_ref, buf, sem):
    j = pl.program_id(0)
    slot = j % 2
    s_rows = buf.shape[1] // tg

    def issue(tok_ref, s):
        for r in range(tg):
            src = pl.multiple_of(tok_ref[0, 0, r] * s_rows, s_rows)
            pltpu.make_async_copy(h_hbm.at[pl.ds(src, s_rows)],
                                  buf.at[s, pl.ds(r * s_rows, s_rows)], sem.at[s]).start(priority=r % 2)

    @pl.when((j == 0) & (live_ref[0] > 0))
    def _():
        issue(tok_cur, 0)

    @pl.when(j + 1 < nsteps)
    def _():
        @pl.when(live_ref[j + 1] > 0)
        def _():
            issue(tok_nxt, 1 - slot)

    @pl.when(live_ref[j] > 0)
    def _():
        pltpu.make_async_copy(h_hbm.at[pl.ds(0, tg * s_rows)], buf.at[slot], sem.at[slot]).wait()
        for c in range(s_rows):
            o_ref[:, c * LANES:(c + 1) * LANES] = buf[slot, pl.ds(c, tg, stride=s_rows), :].astype(BF16)

    @pl.when(live_ref[j] == 0)
    def _():
        o_ref[...] = jnp.zeros(o_ref.shape, BF16)


def _gather_rows(h, d, slot_tok, live, tg):
    s_rows = d // LANES
    nsteps = slot_tok.shape[0] // tg
    tok3 = slot_tok.reshape(nsteps, 1, tg)
    grid_spec = pltpu.PrefetchScalarGridSpec(
        num_scalar_prefetch=1,
        grid=(nsteps,),
        in_specs=[pl.BlockSpec((1, 1, tg), lambda j, lv: (j, 0, 0), memory_space=pltpu.SMEM),
                  pl.BlockSpec((1, 1, tg), lambda j, lv: (jnp.minimum(j + 1, nsteps - 1), 0, 0),
                               memory_space=pltpu.SMEM),
                  pl.BlockSpec(memory_space=pl.ANY)],
        out_specs=pl.BlockSpec((tg, d), lambda j, lv: (j, 0)),
        scratch_shapes=[pltpu.VMEM((2, tg * s_rows, LANES), F32), pltpu.SemaphoreType.DMA((2,))],
    )
    return pl.pallas_call(
        functools.partial(_gather_kernel, tg, nsteps),
        grid_spec=grid_spec,
        out_shape=jax.ShapeDtypeStruct((nsteps * tg, d), BF16),
        compiler_params=_cparams(("arbitrary",)),
        name="moe_dispatch",
    )(live, tok3, tok3, h)


MOE_SUB = 512


def _moe_kernel(nf, tm, e_ref, nv_ref, x_ref, wg_ref, wu_ref, wd_ref, y_o):
    i = pl.program_id(0)
    f = pl.program_id(1)
    nv = nv_ref[i]

    nsub = tm // MOE_SUB

    @pl.when(f == 0)
    def _():
        y_o[...] = jnp.zeros(y_o.shape, F32)

    def weights():
        return wg_ref[0].astype(BF16), wu_ref[0].astype(BF16), wd_ref[0].astype(BF16)

    def sub_block(s, wg, wu, wd):
        rows = slice(s * MOE_SUB, (s + 1) * MOE_SUB)
        h = x_ref[rows, :]
        act = _silu(_dot(h, wg)) * _dot(h, wu)
        y_o[rows, :] += _dot(act.astype(BF16), wd)

    @pl.when(nv > (nsub - 1) * MOE_SUB)
    def _():
        w = weights()
        for s in range(nsub):
            sub_block(s, *w)

    @pl.when((nv > 0) & (nv <= (nsub - 1) * MOE_SUB))
    def _():
        w = weights()
        for s in range(nsub - 1):
            @pl.when(nv > s * MOE_SUB)
            def _():
                sub_block(s, *w)


def _moe_experts(xs, blk_e, blk_nv, w_gu, w_down, tm, tf):
    n, d = xs.shape
    ff = w_down.shape[1]
    nf = ff // tf
    nblk = n // tm

    def f_eff(i, f, nv_ref):
        return jnp.where(nv_ref[i] > 0, f, nf - 1)

    grid_spec = pltpu.PrefetchScalarGridSpec(
        num_scalar_prefetch=2,
        grid=(nblk, nf),
        in_specs=[pl.BlockSpec((tm, d), lambda i, f, e, nv: (i, 0)),
                  pl.BlockSpec((1, d, tf), lambda i, f, e, nv: (e[i], 0, f_eff(i, f, nv))),
                  pl.BlockSpec((1, d, tf), lambda i, f, e, nv: (e[i], 0, nf + f_eff(i, f, nv))),
                  pl.BlockSpec((1, tf, d), lambda i, f, e, nv: (e[i], f_eff(i, f, nv), 0))],
        out_specs=pl.BlockSpec((tm, d), lambda i, f, e, nv: (i, 0)),
    )
    return pl.pallas_call(
        functools.partial(_moe_kernel, nf, tm),
        grid_spec=grid_spec,
        out_shape=jax.ShapeDtypeStruct((n, d), F32),
        compiler_params=_cparams(("arbitrary", "arbitrary")),
        name="moe_experts",
    )(blk_e, blk_nv, xs, w_gu, w_gu, w_down)


def _combine_kernel(tT, nsteps, pos_cur, pos_nxt, ys_hbm, gt_ref, x_ref, mod_ref, g_ref, o_ref,
                    buf, sem):
    j = pl.program_id(0)
    slot = j % 2
    d = x_ref.shape[-1]

    def issue(pos_ref, s):
        for r in range(tT):
            for k in range(TOP_K):
                pltpu.make_async_copy(ys_hbm.at[pl.ds(pos_ref[0, k, r], 1)],
                                      buf.at[s, k, pl.ds(r, 1)], sem.at[s]).start(priority=k)

    @pl.when(j == 0)
    def _():
        issue(pos_cur, 0)

    @pl.when(j + 1 < nsteps)
    def _():
        issue(pos_nxt, 1 - slot)

    for k in range(TOP_K):
        pltpu.make_async_copy(ys_hbm.at[pl.ds(0, tT)], buf.at[slot, k], sem.at[slot]).wait()
    gates = gt_ref[...]
    y = buf[slot, 0] * gates[:, 0:1] + buf[slot, 1] * gates[:, 1:2]
    gt_f = mod_ref[0, :, 5 * d:6 * d]
    o_ref[...] = x_ref[...] + gt_f * _rms(y, g_ref[...])


def _combine(ys, pos, gates, x, mod, g, rows_per_batch, tT):
    n, d = x.shape
    nsteps = n // tT
    per = rows_per_batch // tT
    pos3 = pos.reshape(nsteps, tT, TOP_K).transpose(0, 2, 1)
    row_spec = pl.BlockSpec((tT, d), lambda j: (j, 0))
    smem = functools.partial(pl.BlockSpec, (1, TOP_K, tT), memory_space=pltpu.SMEM)
    return pl.pallas_call(
        functools.partial(_combine_kernel, tT, nsteps),
        grid=(nsteps,),
        in_specs=[smem(lambda j: (j, 0, 0)),
                  smem(lambda j: (jnp.minimum(j + 1, nsteps - 1), 0, 0)),
                  pl.BlockSpec(memory_space=pl.ANY),
                  pl.BlockSpec((tT, LANES), lambda j: (j, 0)),
                  row_spec, pl.BlockSpec((1, 1, 6 * d), lambda j: (j // per, 0, 0)),
                  pl.BlockSpec((1, d), lambda j: (0, 0))],
        out_specs=row_spec,
        out_shape=jax.ShapeDtypeStruct((n, d), F32),
        scratch_shapes=[pltpu.VMEM((2, TOP_K, tT, d), F32), pltpu.SemaphoreType.DMA((2,))],
        compiler_params=_cparams(("arbitrary",)),
        name="moe_combine",
    )(pos3, pos3, ys, gates, x, mod, g)


def _route(logits, n_experts, tm):
    n = logits.shape[0]
    nk = n * TOP_K
    top_v, top_e = lax.top_k(logits, TOP_K)
    gates = jax.nn.softmax(top_v, axis=-1)
    flat_e = top_e.reshape(-1).astype(jnp.int32)
    onehot = (flat_e[None, :] == jnp.arange(n_experts, dtype=jnp.int32)[:, None]).astype(jnp.int32)
    csum = jnp.cumsum(onehot, axis=1)
    counts = csum[:, -1]
    rank = jnp.sum((csum - onehot) * onehot, axis=0)
    nblk_e = (counts + tm - 1) // tm
    start = jnp.cumsum(counts) - counts
    bend = jnp.cumsum(nblk_e)
    bstart = bend - nblk_e
    pos = (jnp.sum(onehot * (bstart * tm)[:, None], axis=0) + rank).reshape(n, TOP_K)
    nblk = -(-nk // tm) + n_experts
    blk = jnp.arange(nblk, dtype=jnp.int32)
    blk_e = jnp.minimum(jnp.searchsorted(bend, blk, side="right"), n_experts - 1).astype(jnp.int32)
    blk_nv = jnp.clip(counts[blk_e] - (blk - bstart[blk_e]) * tm, 0, tm)
    blk_nv = jnp.where(blk < bend[-1], blk_nv, 0).astype(jnp.int32)
    order = jnp.argsort(flat_e, stable=True).astype(jnp.int32)
    order = jnp.concatenate([order, jnp.zeros((tm,), jnp.int32)])
    src0 = jnp.where(blk_nv > 0, start[blk_e] + (blk - bstart[blk_e]) * tm, 0)
    slot_tok = jnp.concatenate([lax.dynamic_slice_in_dim(order, src0[k], tm) for k in range(nblk)])
    return gates, slot_tok // TOP_K, pos, blk_e, blk_nv


def kernel(x, c, ctx, c_ctx, mod_w, mod_b, norm_g, rwkv_mu, rwkv_w_rkv, rwkv_w0, rwkv_w1, rwkv_w2,
           rwkv_a0, rwkv_a1, rwkv_a2, rwkv_g1, rwkv_g2, rwkv_k_k, rwkv_k_a, rwkv_r_k, rwkv_ln_w,
           rwkv_ln_b, rwkv_w_out, conv_w_in, conv_w, conv_w_out, ffn_w_gu, ffn_w_down,
           moe_router, moe_w_gu, moe_w_down):
    b, t, d = x.shape
    n_experts = moe_router.shape[-1]
    rows = 16
    cs = jnp.zeros((rows, d), F32).at[:b].set(c).at[b].set(c_ctx)
    mods = _modulation(cs, mod_w, mod_b)
    mod0 = mods[0, :b].reshape(b, 1, 6 * d)
    mod0c = mods[0, b].reshape(1, 1, 6 * d)
    mod1 = mods[1, :b].reshape(b, 1, 6 * d)

    def pad_lora(w):
        zr = jnp.zeros_like(w[0])
        return jnp.stack([jnp.concatenate([w[0], zr], 0), jnp.concatenate([zr, w[1]], 0)]).astype(BF16)

    lg = rwkv_g1.shape[-1]
    lgp = -(-lg // LANES) * LANES
    p = {
        "ng": norm_g[0, 0].reshape(1, d),
        "mu": rwkv_mu[0],
        "wr": rwkv_w_rkv[0, 0].astype(BF16), "wk": rwkv_w_rkv[0, 1].astype(BF16),
        "wv": rwkv_w_rkv[0, 2].astype(BF16),
        "w1": jnp.concatenate([rwkv_w1[0, 0], rwkv_w1[0, 1]], axis=1).astype(BF16),
        "w2": pad_lora(rwkv_w2[0]), "w0": rwkv_w0[0],
        "a1": jnp.concatenate([rwkv_a1[0, 0], rwkv_a1[0, 1]], axis=1).astype(BF16),
        "a2": pad_lora(rwkv_a2[0]), "a0": rwkv_a0[0],
        "k_k": rwkv_k_k[0].reshape(1, d), "k_a": rwkv_k_a[0].reshape(1, d),
        "g1": jnp.pad(rwkv_g1[0], ((0, 0), (0, lgp - lg))).astype(BF16),
        "g2": jnp.pad(rwkv_g2[0], ((0, lgp - lg), (0, 0))).astype(BF16),
        "r_k": rwkv_r_k[0].reshape(1, d),
    }

    tok_c = _tokenwise(ctx, mod0c, "seq", False, p, ctx.shape[1])
    s_ctx = _scan(tok_c, p["k_a"], None, want_y=False, want_state=True)[0]
    tt = min(ROW_TILE, t)
    tm = min(FFN_ROWS, t)
    tf = min(FFN_COLS, ffn_w_down.shape[1])
    tok_l = _tokenwise(x, mod0, "grid", True, p, tt)
    yf, yb = _scan(tok_l[:8], p["k_a"], s_ctx, want_y=True, want_state=False)
    x1, h2 = _readout(yf, yb, tok_l[9], tok_l[8], x, mod0,
                      rwkv_ln_w[0].reshape(1, d), rwkv_ln_b[0].reshape(1, d),
                      rwkv_w_out[0].astype(BF16), norm_g[0, 1].reshape(1, d),
                      norm_g[0, 2].reshape(1, d), tt)

    x2, h3 = _ffn(h2.reshape(b * t, d), ffn_w_gu[0].astype(BF16), ffn_w_down[0].astype(BF16),
                  x1.reshape(b * t, d), mod0, norm_g[0, 3].reshape(1, d),
                  norm_g[1, 0].reshape(1, d), mod1, t, tm, tf)

    w_router = jnp.stack(_split_bf16(jnp.pad(moe_router[0], ((0, 0), (0, LANES - n_experts)))))
    x3, h4, logits = _conv_layer(h3.reshape(b, t, d), x2.reshape(b, t, d), mod1,
                                 conv_w_in[0].astype(BF16), conv_w[0], conv_w_out[0].astype(BF16),
                                 norm_g[1, 1].reshape(1, d), norm_g[1, 2].reshape(1, d), w_router, tt)

    tme = min(MOE_ROWS, b * t * TOP_K)
    gates, slot_tok, pos, blk_e, blk_nv = _route(logits.reshape(b * t, LANES)[:, :n_experts],
                                                 n_experts, tme)
    tg = min(GATHER_ROWS, tme)
    live = (blk_nv[:, None] > jnp.arange(0, tme, tg, dtype=jnp.int32)[None, :]).astype(jnp.int32)
    xs = _gather_rows(h4.reshape(-1, LANES), d, slot_tok, live.reshape(-1), tg)
    ys = _moe_experts(xs, blk_e, blk_nv, moe_w_gu[0], moe_w_down[0], tme,
                      min(MOE_COLS, moe_w_down.shape[2]))
    gates_p = jnp.pad(gates, ((0, 0), (0, LANES - TOP_K)))
    out = _combine(ys, pos, gates_p, x3.reshape(b * t, d), mod1, norm_g[1, 3].reshape(1, d), t, tt)
    return out.reshape(b, t, d)
```

```python
import functools
import math

import jax
import jax.numpy as jnp
from jax import lax
from jax.experimental import pallas as pl
from jax.experimental.pallas import tpu as pltpu

F32 = jnp.float32
BF16 = jnp.bfloat16

HEAD = 64
LANES = 128
GRID_W = 64
CHUNK = 64
SUB = 16
SCAN_PAIRS = 8
NORM_EPS = 1e-6
GN_EPS = 64e-5
DECAY_SCALE = math.exp(-0.5)
TOP_K = 2
VMEM_LIMIT = 56 * 1024 * 1024


ROW_TILE = 256
FFN_ROWS = 1024
FFN_COLS = 512
MOE_ROWS = 2560
MOE_COLS = 512


def _cparams(sem):
    return pltpu.CompilerParams(dimension_semantics=sem, vmem_limit_bytes=VMEM_LIMIT)


def _dot(a, b):
    return jnp.dot(a, b, preferred_element_type=F32)


def _dot_nt(a, b):
    return lax.dot_general(a, b, (((1,), (1,)), ((), ())), preferred_element_type=F32)


def _bdot(a, b):
    return _dot(a.astype(BF16), b.astype(BF16))


def _iota(shape, axis):
    return lax.broadcasted_iota(jnp.int32, shape, axis)


def _rms(x, g):
    return x * lax.rsqrt(jnp.mean(x * x, axis=-1, keepdims=True) + NORM_EPS) * g


def _silu(x):
    h = 0.5 * x
    return h * jnp.tanh(h) + h


def _split_bf16(x):
    hi = x.astype(BF16)
    lo = (x - hi.astype(F32)).astype(BF16)
    return hi, lo


def _head_sum(x):
    ones = (_iota((LANES, LANES), 0) // HEAD == _iota((LANES, LANES), 1) // HEAD).astype(BF16)
    xb = x.astype(BF16)
    outs = []
    for p in range(x.shape[-1] // LANES):
        outs.append(_dot(xb[:, p * LANES:(p + 1) * LANES], ones))
    return jnp.concatenate(outs, axis=1)


def _sigmoid(x):
    return 0.5 * jnp.tanh(0.5 * x) + 0.5


def _mod_kernel(c_ref, w_ref, b_ref, o_ref):
    s = _silu(c_ref[...])
    o_ref[0] = jnp.dot(s, w_ref[0], preferred_element_type=F32,
                       precision=lax.Precision.HIGHEST) + b_ref[0]


def _modulation(cs, mod_w, mod_b):
    depth, d, n = mod_w.shape
    rows = cs.shape[0]
    tn = 1536
    return pl.pallas_call(
        _mod_kernel,
        grid=(depth, n // tn),
        in_specs=[pl.BlockSpec((rows, d), lambda i, j: (0, 0)),
                  pl.BlockSpec((1, d, tn), lambda i, j: (i, 0, j)),
                  pl.BlockSpec((1, 1, tn), lambda i, j: (i, 0, j))],
        out_specs=pl.BlockSpec((1, rows, tn), lambda i, j: (i, 0, j)),
        out_shape=jax.ShapeDtypeStruct((depth, rows, n), F32),
        compiler_params=_cparams(("arbitrary", "arbitrary")),
        name="modulation",
    )(cs, mod_w, mod_b.reshape(depth, 1, n))


def _tok_kernel(mode, readout, tT, nT, *refs):
    refs = list(refs)
    x_ref = refs.pop(0)
    if mode == "grid":
        xp_ref = refs.pop(0)
        xn_ref = refs.pop(0)
    (mod_ref, ng_ref, mu_ref, wr_ref, wk_ref, wv_ref, w1_ref, w2_ref, w0_ref,
     a1_ref, a2_ref, a0_ref, kkp_ref, ka_ref) = refs[:14]
    refs = refs[14:]
    if readout:
        g1_ref, g2_ref, rk_ref = refs[:3]
        refs = refs[3:]
    r_o, k_o, v_o, kk_o, ic0_o, ic1_o, lw0_o, lw1_o = refs[:8]
    refs = refs[8:]

    d = x_ref.shape[-1]
    sh = mod_ref[0, :, 0:d]
    sc = mod_ref[0, :, d:2 * d]
    g = ng_ref[...]

    def norm_mod(xx):
        return _rms(xx, g) * (1.0 + sc) + sh

    h = norm_mod(x_ref[0])
    if mode == "grid":
        i = pl.program_id(1)
        q = d // 4
        hp = jnp.where(i > 0, norm_mod(xp_ref[0]), 0.0)
        hn = jnp.where(i < nT - 1, norm_mod(xn_ref[0]), 0.0)
        col = _iota((tT, q), 0) % GRID_W
        left = jnp.where(col == 0, 0.0, pltpu.roll(h[:, 0:q], 1, 0))
        right = jnp.where(col == GRID_W - 1, 0.0, pltpu.roll(h[:, q:2 * q], tT - 1, 0))
        up = jnp.concatenate([hp[:, 2 * q:3 * q], h[:tT - GRID_W, 2 * q:3 * q]], axis=0)
        down = jnp.concatenate([h[GRID_W:, 3 * q:], hn[:, 3 * q:]], axis=0)
        hs = jnp.concatenate([left, right, up, down], axis=1)
    else:
        half = d // 2
        row = _iota((tT, half), 0)
        prev = jnp.where(row == 0, 0.0, pltpu.roll(h[:, :half], 1, 0))
        nxt = jnp.where(row == tT - 1, 0.0, pltpu.roll(h[:, half:], tT - 1, 0))
        hs = jnp.concatenate([prev, nxt], axis=1)

    hb = h.astype(BF16)
    dxb = (hs - h).astype(BF16)

    def mix(n):
        return hb + dxb * mu_ref[n:n + 1, :].astype(BF16)

    r = _dot(mix(0), wr_ref[...])
    k = _dot(mix(2), wk_ref[...])
    v = _dot(mix(3), wv_ref[...])
    w1o = jnp.tanh(_dot(mix(1), w1_ref[...])).astype(BF16)
    a1o = _dot(mix(4), a1_ref[...]).astype(BF16)
    ics = []
    for z, (lw_o, ic_o) in enumerate(((lw0_o, ic0_o), (lw1_o, ic1_o))):
        w_pre = _dot(w1o, w2_ref[z]) + w0_ref[z:z + 1, :]
        lw_o[0] = (-0.5 * DECAY_SCALE) * jnp.tanh(0.5 * w_pre) - 0.5 * DECAY_SCALE
        ic = _sigmoid(_dot(a1o, a2_ref[z]) + a0_ref[z:z + 1, :])
        ic_o[0] = ic.astype(BF16)
        ics.append(ic)
    kk = k * kkp_ref[...]
    kk = kk * lax.rsqrt(jnp.maximum(_head_sum(kk * kk), 1e-24))
    r_o[0] = r.astype(BF16)
    k_o[0] = k.astype(BF16)
    v_o[0] = v.astype(BF16)
    kk_o[0] = kk.astype(BF16)
    if readout:
        gate_o, bon_o = refs
        gate = _dot(_sigmoid(_dot(mix(5), g1_ref[...])).astype(BF16), g2_ref[...])
        gate_o[0] = gate.astype(BF16)
        k_avg = k * (1.0 + (0.5 * (ics[0] + ics[1]) - 1.0) * ka_ref[...])
        bon_o[0] = (_head_sum(r * k_avg * rk_ref[...]) * v).astype(BF16)


def _tokenwise(x, mod, mode, readout, p, tT):
    b, t, d = x.shape
    nT = t // tT
    row_spec = pl.BlockSpec((1, tT, d), lambda bi, i: (bi, i, 0))
    in_specs = [row_spec]
    args = [x]
    if mode == "grid":
        hb = tT // GRID_W
        in_specs += [
            pl.BlockSpec((1, GRID_W, d), lambda bi, i: (bi, jnp.maximum(i * hb - 1, 0), 0)),
            pl.BlockSpec((1, GRID_W, d), lambda bi, i: (bi, jnp.minimum((i + 1) * hb, t // GRID_W - 1), 0)),
        ]
        args += [x, x]
    if mod.shape[0] == 1:
        in_specs.append(pl.BlockSpec((1, 1, mod.shape[-1]), lambda bi, i: (0, 0, 0)))
    else:
        in_specs.append(pl.BlockSpec((1, 1, mod.shape[-1]), lambda bi, i: (bi, 0, 0)))
    args.append(mod)
    consts = [p["ng"], p["mu"], p["wr"], p["wk"], p["wv"], p["w1"], p["w2"], p["w0"],
              p["a1"], p["a2"], p["a0"], p["k_k"], p["k_a"]]
    if readout:
        consts += [p["g1"], p["g2"], p["r_k"]]
    for a in consts:
        in_specs.append(pl.BlockSpec(a.shape, lambda bi, i, _n=a.ndim: (0,) * _n))
        args.append(a)
    dtypes = [BF16] * 6 + [F32] * 2 + ([BF16] * 2 if readout else [])
    return pl.pallas_call(
        functools.partial(_tok_kernel, mode, readout, tT, nT),
        grid=(b, nT),
        in_specs=in_specs,
        out_specs=[row_spec] * len(dtypes),
        out_shape=[jax.ShapeDtypeStruct((b, t, d), dt) for dt in dtypes],
        compiler_params=_cparams(("arbitrary", "arbitrary")),
        name="rwkv_tokenwise_" + mode,
    )(*args)


def _scan_prep(r, kd, v, lw, kk, ic, reverse):
    c = r.shape[0]
    ti = _iota((c, c), 0)
    si = _iota((c, c), 1)
    tri = ((si >= ti) if reverse else (si <= ti)).astype(BF16)
    lw_hi, lw_lo = _split_bf16(lw)
    cum = _dot(tri, lw_hi) + _dot(tri, lw_lo)
    pc_log = cum[0:1] if reverse else cum[c - 1:c]
    p_inv = jnp.exp(-cum)
    p_rest = jnp.exp(pc_log - cum)
    b = kk * ic
    return {"rt": r * jnp.exp(cum), "at": -(kk * jnp.exp(cum - lw)),
            "bt": b * p_inv, "kt": kd * p_inv, "bp": b * p_rest, "kp": kd * p_rest,
            "v": v, "pc": jnp.exp(pc_log)}


def _scan_step(prep, states):
    c = prep[0]["v"].shape[0]
    n = 2 * c
    pp = len(states[0])
    chains = [(dr, p) for dr in range(2) for p in range(pp)]
    first = _iota((c, LANES), 1) < HEAD
    row = _iota((n, n), 0)
    col = _iota((n, n), 1)
    strict = (col < row, col > row)
    same = (row // SUB) == (col // SUB)

    def pair(name, dr, p):
        return prep[dr][name][:, p * LANES:(p + 1) * LANES]

    def stack(x):
        return jnp.concatenate([jnp.where(first, x, 0.0), jnp.where(first, 0.0, x)], axis=0)

    def ms(name, dr, p):
        return stack(pair(name, dr, p))

    def each(fn, *lists):
        return [fn(*xs) for xs in zip(*lists)]

    lhs = [jnp.concatenate([pair("at", dr, p), pair("rt", dr, p)], axis=0).astype(BF16)
           for dr, p in chains]
    rhs = [jnp.concatenate([ms("bt", dr, p), ms("kt", dr, p)], axis=0).astype(BF16) for dr, p in chains]
    at_ms = [ms("at", dr, p).astype(BF16) for dr, p in chains]
    v_ms = [ms("v", dr, p).astype(BF16) for dr, p in chains]
    gmat = each(_dot_nt, lhs, rhs)
    nmat = [jnp.where(strict[dr], stack(g[:c, :n]), 0.0) for (dr, _), g in zip(chains, gmat)]
    a_ak = [jnp.where(strict[dr], stack(g[:c, n:]), 0.0).astype(BF16) for (dr, _), g in zip(chains, gmat)]
    t_pl = _iota((c, 2 * n), 0)
    s_pl = _iota((c, 2 * n), 1) % c
    incl_pl = (s_pl <= t_pl, s_pl >= t_pl)
    a_r = [jnp.where(incl_pl[dr], g[c:], 0.0).astype(BF16)
           for (dr, _), g in zip(chains, gmat)]

    nd = [jnp.where(same, x, 0.0) for x in nmat]
    no = [jnp.where(same, 0.0, x) for x in nmat]
    pw = nd
    tp = nd
    span = 2
    while span < SUB:
        pw = each(_bdot, pw, pw)
        tp = each(lambda t, q: t + q + _bdot(t, q), tp, pw)
        span *= 2
    m = each(lambda t, o: o + _bdot(t, o), tp, no)
    qp = m
    pw = m
    span = 2
    while span < c // SUB:
        pw = each(_bdot, pw, pw)
        qp = each(lambda t, q: t + q + _bdot(t, q), qp, pw)
        span *= 2
    tp = each(lambda t, q: t + q + _bdot(q, t), tp, qp)

    bk_t = [jnp.concatenate([ms("bp", dr, p), ms("kp", dr, p)], axis=0).T.astype(BF16)
            for dr, p in chains]
    diag = row == col
    pc_col = [jnp.sum(jnp.where(diag, prep[dr]["pc"][:, p * LANES:(p + 1) * LANES], 0.0),
                      axis=1, keepdims=True) for dr, p in chains]

    s_t = [states[dr][p].astype(BF16) for dr, p in chains]
    rhs_z = each(lambda l, a, st, v: _dot(jnp.concatenate([l, a], axis=1),
                                          jnp.concatenate([st, v], axis=0)), at_ms, a_ak, s_t, v_ms)
    z = each(lambda t, x: x + _bdot(t, x), tp, rhs_z)
    zv = each(lambda zz, v: jnp.concatenate([zz.astype(BF16), v], axis=0), z, v_ms)
    y_pl = each(lambda l, st, ar, w: _dot(l[c:], st) + _dot(ar, w), lhs, s_t, a_r, zv)
    y = [jnp.concatenate([yy for (d2, _), yy in zip(chains, y_pl) if d2 == dr], axis=1)
         for dr in range(2)]
    new_states = [[None] * pp for _ in range(2)]
    for (dr, p), bt_, w, pc in zip(chains, bk_t, zv, pc_col):
        new_states[dr][p] = states[dr][p] * pc + _dot(bt_, w)
    return y, new_states


def _scan_kernel(pp, nc, has_s0, want_y, want_state, *refs):
    refs = list(refs)
    ka_ref = refs[12]
    dirs = (refs[:6], refs[6:12])
    refs = refs[13:]
    if has_s0:
        s0_ref = refs.pop(0)
    if want_y:
        y_refs = (refs.pop(0), refs.pop(0))
    if want_state:
        st_ref = refs.pop(0)
    s_scr = refs.pop(0)
    c = pl.program_id(2)

    @pl.when(c == 0)
    def _():
        if has_s0:
            s_scr[...] = s0_ref[0]
        else:
            s_scr[...] = jnp.zeros_like(s_scr)

    prep = []
    for dr, (r_ref, k_ref, v_ref, kk_ref, ic_ref, lw_ref) in enumerate(dirs):
        ic = ic_ref[0].astype(F32)
        kd = k_ref[0].astype(F32) * (1.0 + (ic - 1.0) * ka_ref[...])
        prep.append(_scan_prep(r_ref[0].astype(F32), kd, v_ref[0].astype(F32), lw_ref[0],
                               kk_ref[0].astype(F32), ic, reverse=(dr == 1)))
    states = [[s_scr[dr, p] for p in range(pp)] for dr in range(2)]
    y, new_states = _scan_step(prep, states)
    for dr in range(2):
        for p in range(pp):
            s_scr[dr, p] = new_states[dr][p]
        if want_y:
            y_refs[dr][0] = y[dr].astype(BF16)

    if want_state:
        @pl.when(c == nc - 1)
        def _():
            st_ref[0] = s_scr[...]


def _scan(tok, ka, s0, want_y, want_state):
    r, k, v, kk, ic0, ic1, lw0, lw1 = tok
    b, t, d = r.shape
    nc = t // CHUNK
    npair = d // LANES
    pp = min(SCAN_PAIRS, npair)
    w = LANES * pp
    f_spec = pl.BlockSpec((1, CHUNK, w), lambda bi, hi, ci: (bi, ci, hi))
    b_spec = pl.BlockSpec((1, CHUNK, w), lambda bi, hi, ci: (bi, nc - 1 - ci, hi))
    st_spec = pl.BlockSpec((1, 2, pp, LANES, LANES), lambda bi, hi, ci: (bi, 0, hi, 0, 0))
    in_specs = [f_spec] * 6 + [b_spec] * 6 + [pl.BlockSpec((1, w), lambda bi, hi, ci: (0, hi))]
    args = [r, k, v, kk, ic0, lw0, r, k, v, kk, ic1, lw1, ka]
    if s0 is not None:
        in_specs.append(st_spec)
        args.append(s0)
    out_specs, out_shape = [], []
    if want_y:
        out_specs += [f_spec, b_spec]
        out_shape += [jax.ShapeDtypeStruct((b, t, d), BF16)] * 2
    if want_state:
        out_specs.append(st_spec)
        out_shape.append(jax.ShapeDtypeStruct((b, 2, npair, LANES, LANES), F32))
    return pl.pallas_call(
        functools.partial(_scan_kernel, pp, nc, s0 is not None, want_y, want_state),
        grid=(b, npair // pp, nc),
        in_specs=in_specs,
        out_specs=out_specs,
        out_shape=out_shape,
        scratch_shapes=[pltpu.VMEM((2, pp, LANES, LANES), F32)],
        compiler_params=_cparams(("arbitrary", "arbitrary", "arbitrary")),
        name="rwkv_scan_ctx" if s0 is None else "rwkv_scan_latent",
    )(*args)


def _readout_kernel(yf_ref, yb_ref, bon_ref, gate_ref, x_ref, mod_ref, lnw_ref, lnb_ref,
                    wo_ref, g1_ref, g2_ref, x_o, h_o):
    d = x_ref.shape[-1]
    y = yf_ref[0].astype(F32) + yb_ref[0].astype(F32)
    mean = _head_sum(y) * (1.0 / HEAD)
    yc = y - mean
    var = _head_sum(yc * yc) * (1.0 / HEAD)
    o = yc * lax.rsqrt(var + GN_EPS) * lnw_ref[...] + lnb_ref[...] + bon_ref[0]
    att = _dot((o * gate_ref[0]).astype(BF16), wo_ref[...])
    gt_a = mod_ref[0, :, 2 * d:3 * d]
    sh_f = mod_ref[0, :, 3 * d:4 * d]
    sc_f = mod_ref[0, :, 4 * d:5 * d]
    x1 = x_ref[0] + gt_a * _rms(att, g1_ref[...])
    x_o[0] = x1
    h_o[0] = (_rms(x1, g2_ref[...]) * (1.0 + sc_f) + sh_f).astype(BF16)


def _readout(yf, yb, bon, gate, x, mod, lnw, lnb, wo, g1, g2, tT):
    b, t, d = x.shape
    row_spec = pl.BlockSpec((1, tT, d), lambda bi, i: (bi, i, 0))
    vec = pl.BlockSpec((1, d), lambda bi, i: (0, 0))
    return pl.pallas_call(
        _readout_kernel,
        grid=(b, t // tT),
        in_specs=[row_spec] * 5 + [pl.BlockSpec((1, 1, 6 * d), lambda bi, i: (bi, 0, 0)),
                                   vec, vec, pl.BlockSpec((d, d), lambda bi, i: (0, 0)), vec, vec],
        out_specs=[row_spec, row_spec],
        out_shape=[jax.ShapeDtypeStruct((b, t, d), F32), jax.ShapeDtypeStruct((b, t, d), BF16)],
        compiler_params=_cparams(("arbitrary", "arbitrary")),
        name="rwkv_readout",
    )(yf, yb, bon, gate, x, mod, lnw, lnb, wo, g1, g2)


def _ffn_kernel(nf, h_ref, wg_ref, wu_ref, wd_ref, x_ref, mod_ref, g3_ref, gn_ref, mod2_ref,
                x_o, h_o, acc):
    f = pl.program_id(1)
    d = x_ref.shape[-1]

    @pl.when(f == 0)
    def _():
        acc[...] = jnp.zeros_like(acc)

    h = h_ref[...]
    act = _silu(_dot(h, wg_ref[...])) * _dot(h, wu_ref[...])
    acc[...] += _dot(act.astype(BF16), wd_ref[...])

    @pl.when(f == nf - 1)
    def _():
        gt_f = mod_ref[0, :, 5 * d:6 * d]
        x2 = x_ref[...] + gt_f * _rms(acc[...], g3_ref[...])
        x_o[...] = x2
        sh = mod2_ref[0, :, 0:d]
        sc = mod2_ref[0, :, d:2 * d]
        h_o[...] = (_rms(x2, gn_ref[...]) * (1.0 + sc) + sh).astype(BF16)


def _ffn(h, w_gu, w_down, x, mod, g3, gn, mod2, rows_per_batch, tm, tf):
    n, d = x.shape
    ff = w_down.shape[0]
    nf = ff // tf
    per = rows_per_batch // tm
    row = pl.BlockSpec((tm, d), lambda i, f: (i, 0))
    vec = pl.BlockSpec((1, d), lambda i, f: (0, 0))
    modspec = pl.BlockSpec((1, 1, 6 * d), lambda i, f: (i // per, 0, 0))
    return pl.pallas_call(
        functools.partial(_ffn_kernel, nf),
        grid=(n // tm, nf),
        in_specs=[row,
                  pl.BlockSpec((d, tf), lambda i, f: (0, f)),
                  pl.BlockSpec((d, tf), lambda i, f: (0, nf + f)),
                  pl.BlockSpec((tf, d), lambda i, f: (f, 0)),
                  row, modspec, vec, vec, modspec],
        out_specs=[row, row],
        out_shape=[jax.ShapeDtypeStruct((n, d), F32), jax.ShapeDtypeStruct((n, d), BF16)],
        scratch_shapes=[pltpu.VMEM((tm, d), F32)],
        compiler_params=_cparams(("arbitrary", "arbitrary")),
        name="dense_swiglu",
    )(h, w_gu, w_gu, w_down, x, mod, g3, gn, mod2)


HALO = 16


def _conv_kernel(tT, nT, h_ref, hp_ref, hn_ref, x_ref, mod_ref, win_ref, cw_ref, wo_ref,
                 g1_ref, g2_ref, wr_ref, x_o, h_o, lg_o):
    i = pl.program_id(1)
    d = x_ref.shape[-1]
    n = tT + 2 * HALO
    h_ext = jnp.concatenate([hp_ref[0], h_ref[0], hn_ref[0]], axis=0)
    proj = _dot(h_ext, win_ref[...])
    z = proj[:, d:2 * d] * proj[:, 2 * d:]
    row = _iota((n, d), 0)
    dead = ((row < HALO) & (i == 0)) | ((row >= HALO + tT) & (i == nT - 1))
    z = jnp.where(dead, 0.0, z)
    conv = (pltpu.roll(z, 1, 0) * cw_ref[0:1, :] + z * cw_ref[1:2, :]
            + pltpu.roll(z, n - 1, 0) * cw_ref[2:3, :])
    gated = (proj[HALO:HALO + tT, 0:d] * conv[HALO:HALO + tT]).astype(BF16)
    y = _dot(gated, wo_ref[...])
    gt_a = mod_ref[0, :, 2 * d:3 * d]
    sh_f = mod_ref[0, :, 3 * d:4 * d]
    sc_f = mod_ref[0, :, 4 * d:5 * d]
    x3 = x_ref[0] + gt_a * _rms(y, g1_ref[...])
    x_o[0] = x3
    h4 = _rms(x3, g2_ref[...]) * (1.0 + sc_f) + sh_f
    s_rows = d // LANES
    for c in range(s_rows):
        h_o[0, pl.ds(c, tT, stride=s_rows), :] = h4[:, c * LANES:(c + 1) * LANES]
    h_hi, h_lo = _split_bf16(h4)
    lg_o[0] = _dot(h_hi, wr_ref[0]) + _dot(h_lo, wr_ref[0]) + _dot(h_hi, wr_ref[1])


def _conv_layer(h, x, mod, w_in, conv_w, w_out, g1, g2, w_router, tT):
    b, t, d = x.shape
    nT = t // tT
    hb = tT // HALO
    row_spec = pl.BlockSpec((1, tT, d), lambda bi, i: (bi, i, 0))
    vec = pl.BlockSpec((1, d), lambda bi, i: (0, 0))
    full = lambda a: pl.BlockSpec(a.shape, lambda bi, i, _n=a.ndim: (0,) * _n)
    return pl.pallas_call(
        functools.partial(_conv_kernel, tT, nT),
        grid=(b, nT),
        in_specs=[row_spec,
                  pl.BlockSpec((1, HALO, d), lambda bi, i: (bi, jnp.maximum(i * hb - 1, 0), 0)),
                  pl.BlockSpec((1, HALO, d), lambda bi, i: (bi, jnp.minimum((i + 1) * hb, t // HALO - 1), 0)),
                  row_spec, pl.BlockSpec((1, 1, 6 * d), lambda bi, i: (bi, 0, 0)),
                  full(w_in), full(conv_w), full(w_out), vec, vec, full(w_router)],
        out_specs=[row_spec, pl.BlockSpec((1, tT * (d // LANES), LANES), lambda bi, i: (bi, i, 0)),
                   pl.BlockSpec((1, tT, LANES), lambda bi, i: (bi, i, 0))],
        out_shape=[jax.ShapeDtypeStruct((b, t, d), F32),
                   jax.ShapeDtypeStruct((b, t * (d // LANES), LANES), F32),
                   jax.ShapeDtypeStruct((b, t, LANES), F32)],
        compiler_params=_cparams(("arbitrary", "arbitrary")),
        name="short_conv",
    )(h, h, h, x, mod, w_in, conv_w, w_out, g1, g2, w_router)


GATHER_ROWS = 512


def _gather_kernel(tg, nsteps, live_ref, tok_cur, tok_nxt, h_hbm, o_ref, buf, sem):
    j = pl.program_id(0)
    slot = j % 2
    s_rows = buf.shape[1] // tg

    def issue(tok_ref, s):
        for r in range(tg):
            src = pl.multiple_of(tok_ref[0, 0, r] * s_rows, s_rows)
            pltpu.make_async_copy(h_hbm.at[pl.ds(src, s_rows)],
                                  buf.at[s, pl.ds(r * s_rows, s_rows)], sem.at[s]).start(priority=r % 2)

    @pl.when((j == 0) & (live_ref[0] > 0))
    def _():
        issue(tok_cur, 0)

    @pl.when(j + 1 < nsteps)
    def _():
        @pl.when(live_ref[j + 1] > 0)
        def _():
            issue(tok_nxt, 1 - slot)

    @pl.when(live_ref[j] > 0)
    def _():
        pltpu.make_async_copy(h_hbm.at[pl.ds(0, tg * s_rows)], buf.at[slot], sem.at[slot]).wait()
        for c in range(s_rows):
            o_ref[:, c * LANES:(c + 1) * LANES] = buf[slot, pl.ds(c, tg, stride=s_rows), :].astype(BF16)

    @pl.when(live_ref[j] == 0)
    def _():
        o_ref[...] = jnp.zeros(o_ref.shape, BF16)


def _gather_rows(h, d, slot_tok, live, tg):
    s_rows = d // LANES
    nsteps = slot_tok.shape[0] // tg
    tok3 = slot_tok.reshape(nsteps, 1, tg)
    grid_spec = pltpu.PrefetchScalarGridSpec(
        num_scalar_prefetch=1,
        grid=(nsteps,),
        in_specs=[pl.BlockSpec((1, 1, tg), lambda j, lv: (j, 0, 0), memory_space=pltpu.SMEM),
                  pl.BlockSpec((1, 1, tg), lambda j, lv: (jnp.minimum(j + 1, nsteps - 1), 0, 0),
                               memory_space=pltpu.SMEM),
                  pl.BlockSpec(memory_space=pl.ANY)],
        out_specs=pl.BlockSpec((tg, d), lambda j, lv: (j, 0)),
        scratch_shapes=[pltpu.VMEM((2, tg * s_rows, LANES), F32), pltpu.SemaphoreType.DMA((2,))],
    )
    return pl.pallas_call(
        functools.partial(_gather_kernel, tg, nsteps),
        grid_spec=grid_spec,
        out_shape=jax.ShapeDtypeStruct((nsteps * tg, d), BF16),
        compiler_params=_cparams(("arbitrary",)),
        name="moe_dispatch",
    )(live, tok3, tok3, h)


MOE_SUB = 512


def _moe_kernel(nf, tm, e_ref, nv_ref, x_ref, wg_ref, wu_ref, wd_ref, y_o):
    i = pl.program_id(0)
    f = pl.program_id(1)
    nv = nv_ref[i]

    nsub = tm // MOE_SUB

    @pl.when(f == 0)
    def _():
        y_o[...] = jnp.zeros(y_o.shape, F32)

    def weights():
        return wg_ref[0].astype(BF16), wu_ref[0].astype(BF16), wd_ref[0].astype(BF16)

    def sub_block(s, wg, wu, wd):
        rows = slice(s * MOE_SUB, (s + 1) * MOE_SUB)
        h = x_ref[rows, :]
        act = _silu(_dot(h, wg)) * _dot(h, wu)
        y_o[rows, :] += _dot(act.astype(BF16), wd)

    @pl.when(nv > (nsub - 1) * MOE_SUB)
    def _():
        w = weights()
        for s in range(nsub):
            sub_block(s, *w)

    @pl.when((nv > 0) & (nv <= (nsub - 1) * MOE_SUB))
    def _():
        w = weights()
        for s in range(nsub - 1):
            @pl.when(nv > s * MOE_SUB)
            def _():
                sub_block(s, *w)


def _moe_experts(xs, blk_e, blk_nv, w_gu, w_down, tm, tf):
    n, d = xs.shape
    ff = w_down.shape[1]
    nf = ff // tf
    nblk = n // tm

    def f_eff(i, f, nv_ref):
        return jnp.where(nv_ref[i] > 0, f, nf - 1)

    grid_spec = pltpu.PrefetchScalarGridSpec(
        num_scalar_prefetch=2,
        grid=(nblk, nf),
        in_specs=[pl.BlockSpec((tm, d), lambda i, f, e, nv: (i, 0)),
                  pl.BlockSpec((1, d, tf), lambda i, f, e, nv: (e[i], 0, f_eff(i, f, nv))),
                  pl.BlockSpec((1, d, tf), lambda i, f, e, nv: (e[i], 0, nf + f_eff(i, f, nv))),
                  pl.BlockSpec((1, tf, d), lambda i, f, e, nv: (e[i], f_eff(i, f, nv), 0))],
        out_specs=pl.BlockSpec((tm, d), lambda i, f, e, nv: (i, 0)),
    )
    return pl.pallas_call(
        functools.partial(_moe_kernel, nf, tm),
        grid_spec=grid_spec,
        out_shape=jax.ShapeDtypeStruct((n, d), F32),
        compiler_params=_cparams(("arbitrary", "arbitrary")),
        name="moe_experts",
    )(blk_e, blk_nv, xs, w_gu, w_gu, w_down)


def _combine_kernel(tT, nsteps, pos_cur, pos_nxt, ys_hbm, gt_ref, x_ref, mod_ref, g_ref, o_ref,
                    buf, sem):
    j = pl.program_id(0)
    slot = j % 2
    d = x_ref.shape[-1]

    def issue(pos_ref, s):
        for r in range(tT):
            for k in range(TOP_K):
                pltpu.make_async_copy(ys_hbm.at[pl.ds(pos_ref[0, k, r], 1)],
                                      buf.at[s, k, pl.ds(r, 1)], sem.at[s]).start(priority=k)

    @pl.when(j == 0)
    def _():
        issue(pos_cur, 0)

    @pl.when(j + 1 < nsteps)
    def _():
        issue(pos_nxt, 1 - slot)

    for k in range(TOP_K):
        pltpu.make_async_copy(ys_hbm.at[pl.ds(0, tT)], buf.at[slot, k], sem.at[slot]).wait()
    gates = gt_ref[...]
    y = buf[slot, 0] * gates[:, 0:1] + buf[slot, 1] * gates[:, 1:2]
    gt_f = mod_ref[0, :, 5 * d:6 * d]
    o_ref[...] = x_ref[...] + gt_f * _rms(y, g_ref[...])


def _combine(ys, pos, gates, x, mod, g, rows_per_batch, tT):
    n, d = x.shape
    nsteps = n // tT
    per = rows_per_batch // tT
    pos3 = pos.reshape(nsteps, tT, TOP_K).transpose(0, 2, 1)
    row_spec = pl.BlockSpec((tT, d), lambda j: (j, 0))
    smem = functools.partial(pl.BlockSpec, (1, TOP_K, tT), memory_space=pltpu.SMEM)
    return pl.pallas_call(
        functools.partial(_combine_kernel, tT, nsteps),
        grid=(nsteps,),
        in_specs=[smem(lambda j: (j, 0, 0)),
                  smem(lambda j: (jnp.minimum(j + 1, nsteps - 1), 0, 0)),
                  pl.BlockSpec(memory_space=pl.ANY),
                  pl.BlockSpec((tT, LANES), lambda j: (j, 0)),
                  row_spec, pl.BlockSpec((1, 1, 6 * d), lambda j: (j // per, 0, 0)),
                  pl.BlockSpec((1, d), lambda j: (0, 0))],
        out_specs=row_spec,
        out_shape=jax.ShapeDtypeStruct((n, d), F32),
        scratch_shapes=[pltpu.VMEM((2, TOP_K, tT, d), F32), pltpu.SemaphoreType.DMA((2,))],
        compiler_params=_cparams(("arbitrary",)),
        name="moe_combine",
    )(pos3, pos3, ys, gates, x, mod, g)


def _route(logits, n_experts, tm):
    n = logits.shape[0]
    nk = n * TOP_K
    top_v, top_e = lax.top_k(logits, TOP_K)
    gates = jax.nn.softmax(top_v, axis=-1)
    flat_e = top_e.reshape(-1).astype(jnp.int32)
    onehot = (flat_e[None, :] == jnp.arange(n_experts, dtype=jnp.int32)[:, None]).astype(jnp.int32)
    csum = jnp.cumsum(onehot, axis=1)
    counts = csum[:, -1]
    rank = jnp.sum((csum - onehot) * onehot, axis=0)
    nblk_e = (counts + tm - 1) // tm
    start = jnp.cumsum(counts) - counts
    bend = jnp.cumsum(nblk_e)
    bstart = bend - nblk_e
    pos = (jnp.sum(onehot * (bstart * tm)[:, None], axis=0) + rank).reshape(n, TOP_K)
    nblk = -(-nk // tm) + n_experts
    blk = jnp.arange(nblk, dtype=jnp.int32)
    blk_e = jnp.minimum(jnp.searchsorted(bend, blk, side="right"), n_experts - 1).astype(jnp.int32)
    blk_nv = jnp.clip(counts[blk_e] - (blk - bstart[blk_e]) * tm, 0, tm)
    blk_nv = jnp.where(blk < bend[-1], blk_nv, 0).astype(jnp.int32)
    order = jnp.argsort(flat_e, stable=True).astype(jnp.int32)
    order = jnp.concatenate([order, jnp.zeros((tm,), jnp.int32)])
    src0 = jnp.where(blk_nv > 0, start[blk_e] + (blk - bstart[blk_e]) * tm, 0)
    slot_tok = jnp.concatenate([lax.dynamic_slice_in_dim(order, src0[k], tm) for k in range(nblk)])
    return gates, slot_tok // TOP_K, pos, blk_e, blk_nv


def kernel(x, c, ctx, c_ctx, mod_w, mod_b, norm_g, rwkv_mu, rwkv_w_rkv, rwkv_w0, rwkv_w1, rwkv_w2,
           rwkv_a0, rwkv_a1, rwkv_a2, rwkv_g1, rwkv_g2, rwkv_k_k, rwkv_k_a, rwkv_r_k, rwkv_ln_w,
           rwkv_ln_b, rwkv_w_out, conv_w_in, conv_w, conv_w_out, ffn_w_gu, ffn_w_down,
           moe_router, moe_w_gu, moe_w_down):
    b, t, d = x.shape
    n_experts = moe_router.shape[-1]
    rows = 16
    cs = jnp.zeros((rows, d), F32).at[:b].set(c).at[b].set(c_ctx)
    mods = _modulation(cs, mod_w, mod_b)
    mod0 = mods[0, :b].reshape(b, 1, 6 * d)
    mod0c = mods[0, b].reshape(1, 1, 6 * d)
    mod1 = mods[1, :b].reshape(b, 1, 6 * d)

    def pad_lora(w):
        zr = jnp.zeros_like(w[0])
        return jnp.stack([jnp.concatenate([w[0], zr], 0), jnp.concatenate([zr, w[1]], 0)]).astype(BF16)

    lg = rwkv_g1.shape[-1]
    lgp = -(-lg // LANES) * LANES
    p = {
        "ng": norm_g[0, 0].reshape(1, d),
        "mu": rwkv_mu[0],
        "wr": rwkv_w_rkv[0, 0].astype(BF16), "wk": rwkv_w_rkv[0, 1].astype(BF16),
        "wv": rwkv_w_rkv[0, 2].astype(BF16),
        "w1": jnp.concatenate([rwkv_w1[0, 0], rwkv_w1[0, 1]], axis=1).astype(BF16),
        "w2": pad_lora(rwkv_w2[0]), "w0": rwkv_w0[0],
        "a1": jnp.concatenate([rwkv_a1[0, 0], rwkv_a1[0, 1]], axis=1).astype(BF16),
        "a2": pad_lora(rwkv_a2[0]), "a0": rwkv_a0[0],
        "k_k": rwkv_k_k[0].reshape(1, d), "k_a": rwkv_k_a[0].reshape(1, d),
        "g1": jnp.pad(rwkv_g1[0], ((0, 0), (0, lgp - lg))).astype(BF16),
        "g2": jnp.pad(rwkv_g2[0], ((0, lgp - lg), (0, 0))).astype(BF16),
        "r_k": rwkv_r_k[0].reshape(1, d),
    }

    tok_c = _tokenwise(ctx, mod0c, "seq", False, p, ctx.shape[1])
    s_ctx = _scan(tok_c, p["k_a"], None, want_y=False, want_state=True)[0]
    tt = min(ROW_TILE, t)
    tm = min(FFN_ROWS, t)
    tf = min(FFN_COLS, ffn_w_down.shape[1])
    tok_l = _tokenwise(x, mod0, "grid", True, p, tt)
    yf, yb = _scan(tok_l[:8], p["k_a"], s_ctx, want_y=True, want_state=False)
    x1, h2 = _readout(yf, yb, tok_l[9], tok_l[8], x, mod0,
                      rwkv_ln_w[0].reshape(1, d), rwkv_ln_b[0].reshape(1, d),
                      rwkv_w_out[0].astype(BF16), norm_g[0, 1].reshape(1, d),
                      norm_g[0, 2].reshape(1, d), min(2 * ROW_TILE, t))

    x2, h3 = _ffn(h2.reshape(b * t, d), ffn_w_gu[0].astype(BF16), ffn_w_down[0].astype(BF16),
                  x1.reshape(b * t, d), mod0, norm_g[0, 3].reshape(1, d),
                  norm_g[1, 0].reshape(1, d), mod1, t, tm, tf)

    w_router = jnp.stack(_split_bf16(jnp.pad(moe_router[0], ((0, 0), (0, LANES - n_experts)))))
    x3, h4, logits = _conv_layer(h3.reshape(b, t, d), x2.reshape(b, t, d), mod1,
                                 conv_w_in[0].astype(BF16), conv_w[0], conv_w_out[0].astype(BF16),
                                 norm_g[1, 1].reshape(1, d), norm_g[1, 2].reshape(1, d), w_router, tt)

    tme = min(MOE_ROWS, b * t * TOP_K)
    gates, slot_tok, pos, blk_e, blk_nv = _route(logits.reshape(b * t, LANES)[:, :n_experts],
                                                 n_experts, tme)
    tg = min(GATHER_ROWS, tme)
    live = (blk_nv[:, None] > jnp.arange(0, tme, tg, dtype=jnp.int32)[None, :]).astype(jnp.int32)
    xs = _gather_rows(h4.reshape(-1, LANES), d, slot_tok, live.reshape(-1), tg)
    ys = _moe_experts(xs, blk_e, blk_nv, moe_w_gu[0], moe_w_down[0], tme,
                      min(MOE_COLS, moe_w_down.shape[2]))
    gates_p = jnp.pad(gates, ((0, 0), (0, LANES - TOP_K)))
    out = _combine(ys, pos, gates_p, x3.reshape(b * t, d), mod1, norm_g[1, 3].reshape(1, d), t, tt)
    return out.reshape(b, t, d)
```

```python
import functools
import math

import jax
import jax.numpy as jnp
from jax import lax
from jax.experimental import pallas as pl
from jax.experimental.pallas import tpu as pltpu

F32 = jnp.float32
BF16 = jnp.bfloat16

HEAD = 64
LANES = 128
GRID_W = 64
CHUNK = 64
SUB = 16
SCAN_PAIRS = 8
NORM_EPS = 1e-6
GN_EPS = 64e-5
DECAY_SCALE = math.exp(-0.5)
TOP_K = 2
VMEM_LIMIT = 56 * 1024 * 1024


ROW_TILE = 256
FFN_ROWS = 1024
FFN_COLS = 512
MOE_ROWS = 2560
MOE_COLS = 512


def _cparams(sem):
    return pltpu.CompilerParams(dimension_semantics=sem, vmem_limit_bytes=VMEM_LIMIT)


def _dot(a, b):
    return jnp.dot(a, b, preferred_element_type=F32)


def _dot_nt(a, b):
    return lax.dot_general(a, b, (((1,), (1,)), ((), ())), preferred_element_type=F32)


def _bdot(a, b):
    return _dot(a.astype(BF16), b.astype(BF16))


def _iota(shape, axis):
    return lax.broadcasted_iota(jnp.int32, shape, axis)


def _rms(x, g):
    return x * lax.rsqrt(jnp.mean(x * x, axis=-1, keepdims=True) + NORM_EPS) * g


def _silu(x):
    h = 0.5 * x
    return h * jnp.tanh(h) + h


def _split_bf16(x):
    hi = x.astype(BF16)
    lo = (x - hi.astype(F32)).astype(BF16)
    return hi, lo


def _head_sum(x):
    ones = (_iota((LANES, LANES), 0) // HEAD == _iota((LANES, LANES), 1) // HEAD).astype(BF16)
    xb = x.astype(BF16)
    outs = []
    for p in range(x.shape[-1] // LANES):
        outs.append(_dot(xb[:, p * LANES:(p + 1) * LANES], ones))
    return jnp.concatenate(outs, axis=1)


def _sigmoid(x):
    return 0.5 * jnp.tanh(0.5 * x) + 0.5


def _mod_kernel(c_ref, w_ref, b_ref, o_ref):
    s = _silu(c_ref[...])
    o_ref[0] = jnp.dot(s, w_ref[0], preferred_element_type=F32,
                       precision=lax.Precision.HIGHEST) + b_ref[0]


def _modulation(cs, mod_w, mod_b):
    depth, d, n = mod_w.shape
    rows = cs.shape[0]
    tn = 1536
    return pl.pallas_call(
        _mod_kernel,
        grid=(depth, n // tn),
        in_specs=[pl.BlockSpec((rows, d), lambda i, j: (0, 0)),
                  pl.BlockSpec((1, d, tn), lambda i, j: (i, 0, j)),
                  pl.BlockSpec((1, 1, tn), lambda i, j: (i, 0, j))],
        out_specs=pl.BlockSpec((1, rows, tn), lambda i, j: (i, 0, j)),
        out_shape=jax.ShapeDtypeStruct((depth, rows, n), F32),
        compiler_params=_cparams(("arbitrary", "arbitrary")),
        name="modulation",
    )(cs, mod_w, mod_b.reshape(depth, 1, n))


def _tok_kernel(mode, readout, tT, nT, *refs):
    refs = list(refs)
    x_ref = refs.pop(0)
    if mode == "grid":
        xp_ref = refs.pop(0)
        xn_ref = refs.pop(0)
    (mod_ref, ng_ref, mu_ref, wr_ref, wk_ref, wv_ref, w1_ref, w2_ref, w0_ref,
     a1_ref, a2_ref, a0_ref, kkp_ref, ka_ref) = refs[:14]
    refs = refs[14:]
    if readout:
        g1_ref, g2_ref, rk_ref = refs[:3]
        refs = refs[3:]
    r_o, k_o, v_o, kk_o, ic0_o, ic1_o, lw0_o, lw1_o = refs[:8]
    refs = refs[8:]

    d = x_ref.shape[-1]
    sh = mod_ref[0, :, 0:d]
    sc = mod_ref[0, :, d:2 * d]
    g = ng_ref[...]

    def norm_mod(xx):
        return _rms(xx, g) * (1.0 + sc) + sh

    h = norm_mod(x_ref[0])
    if mode == "grid":
        i = pl.program_id(1)
        q = d // 4
        hp = jnp.where(i > 0, norm_mod(xp_ref[0]), 0.0)
        hn = jnp.where(i < nT - 1, norm_mod(xn_ref[0]), 0.0)
        col = _iota((tT, q), 0) % GRID_W
        left = jnp.where(col == 0, 0.0, pltpu.roll(h[:, 0:q], 1, 0))
        right = jnp.where(col == GRID_W - 1, 0.0, pltpu.roll(h[:, q:2 * q], tT - 1, 0))
        up = jnp.concatenate([hp[:, 2 * q:3 * q], h[:tT - GRID_W, 2 * q:3 * q]], axis=0)
        down = jnp.concatenate([h[GRID_W:, 3 * q:], hn[:, 3 * q:]], axis=0)
        hs = jnp.concatenate([left, right, up, down], axis=1)
    else:
        half = d // 2
        row = _iota((tT, half), 0)
        prev = jnp.where(row == 0, 0.0, pltpu.roll(h[:, :half], 1, 0))
        nxt = jnp.where(row == tT - 1, 0.0, pltpu.roll(h[:, half:], tT - 1, 0))
        hs = jnp.concatenate([prev, nxt], axis=1)

    hb = h.astype(BF16)
    dxb = (hs - h).astype(BF16)

    def mix(n):
        return hb + dxb * mu_ref[n:n + 1, :].astype(BF16)

    r = _dot(mix(0), wr_ref[...])
    k = _dot(mix(2), wk_ref[...])
    v = _dot(mix(3), wv_ref[...])
    w1o = jnp.tanh(_dot(mix(1), w1_ref[...])).astype(BF16)
    a1o = _dot(mix(4), a1_ref[...]).astype(BF16)
    ics = []
    for z, (lw_o, ic_o) in enumerate(((lw0_o, ic0_o), (lw1_o, ic1_o))):
        w_pre = _dot(w1o, w2_ref[z]) + w0_ref[z:z + 1, :]
        lw_o[0] = (-0.5 * DECAY_SCALE) * jnp.tanh(0.5 * w_pre) - 0.5 * DECAY_SCALE
        ic = _sigmoid(_dot(a1o, a2_ref[z]) + a0_ref[z:z + 1, :])
        ic_o[0] = ic.astype(BF16)
        ics.append(ic)
    kk = k * kkp_ref[...]
    kk = kk * lax.rsqrt(jnp.maximum(_head_sum(kk * kk), 1e-24))
    r_o[0] = r.astype(BF16)
    k_o[0] = k.astype(BF16)
    v_o[0] = v.astype(BF16)
    kk_o[0] = kk.astype(BF16)
    if readout:
        gate_o, bon_o = refs
        gate = _dot(_sigmoid(_dot(mix(5), g1_ref[...])).astype(BF16), g2_ref[...])
        gate_o[0] = gate.astype(BF16)
        k_avg = k * (1.0 + (0.5 * (ics[0] + ics[1]) - 1.0) * ka_ref[...])
        bon_o[0] = (_head_sum(r * k_avg * rk_ref[...]) * v).astype(BF16)


def _tokenwise(x, mod, mode, readout, p, tT):
    b, t, d = x.shape
    nT = t // tT
    row_spec = pl.BlockSpec((1, tT, d), lambda bi, i: (bi, i, 0))
    in_specs = [row_spec]
    args = [x]
    if mode == "grid":
        hb = tT // GRID_W
        in_specs += [
            pl.BlockSpec((1, GRID_W, d), lambda bi, i: (bi, jnp.maximum(i * hb - 1, 0), 0)),
            pl.BlockSpec((1, GRID_W, d), lambda bi, i: (bi, jnp.minimum((i + 1) * hb, t // GRID_W - 1), 0)),
        ]
        args += [x, x]
    if mod.shape[0] == 1:
        in_specs.append(pl.BlockSpec((1, 1, mod.shape[-1]), lambda bi, i: (0, 0, 0)))
    else:
        in_specs.append(pl.BlockSpec((1, 1, mod.shape[-1]), lambda bi, i: (bi, 0, 0)))
    args.append(mod)
    consts = [p["ng"], p["mu"], p["wr"], p["wk"], p["wv"], p["w1"], p["w2"], p["w0"],
              p["a1"], p["a2"], p["a0"], p["k_k"], p["k_a"]]
    if readout:
        consts += [p["g1"], p["g2"], p["r_k"]]
    for a in consts:
        in_specs.append(pl.BlockSpec(a.shape, lambda bi, i, _n=a.ndim: (0,) * _n))
        args.append(a)
    dtypes = [BF16] * 6 + [F32] * 2 + ([BF16] * 2 if readout else [])
    return pl.pallas_call(
        functools.partial(_tok_kernel, mode, readout, tT, nT),
        grid=(b, nT),
        in_specs=in_specs,
        out_specs=[row_spec] * len(dtypes),
        out_shape=[jax.ShapeDtypeStruct((b, t, d), dt) for dt in dtypes],
        compiler_params=_cparams(("arbitrary", "arbitrary")),
        name="rwkv_tokenwise_" + mode,
    )(*args)


def _scan_prep(r, kd, v, lw, kk, ic, reverse):
    c = r.shape[0]
    ti = _iota((c, c), 0)
    si = _iota((c, c), 1)
    tri = ((si >= ti) if reverse else (si <= ti)).astype(BF16)
    lw_hi, lw_lo = _split_bf16(lw)
    cum = _dot(tri, lw_hi) + _dot(tri, lw_lo)
    pc_log = cum[0:1] if reverse else cum[c - 1:c]
    p_inv = jnp.exp(-cum)
    p_rest = jnp.exp(pc_log - cum)
    b = kk * ic
    return {"rt": r * jnp.exp(cum), "at": -(kk * jnp.exp(cum - lw)),
            "bt": b * p_inv, "kt": kd * p_inv, "bp": b * p_rest, "kp": kd * p_rest,
            "v": v, "pc": jnp.exp(pc_log)}


def _scan_step(prep, states):
    c = prep[0]["v"].shape[0]
    n = 2 * c
    pp = len(states[0])
    chains = [(dr, p) for dr in range(2) for p in range(pp)]
    first = _iota((c, LANES), 1) < HEAD
    row = _iota((n, n), 0)
    col = _iota((n, n), 1)
    strict = (col < row, col > row)
    same = (row // SUB) == (col // SUB)

    def pair(name, dr, p):
        return prep[dr][name][:, p * LANES:(p + 1) * LANES]

    def stack(x):
        return jnp.concatenate([jnp.where(first, x, 0.0), jnp.where(first, 0.0, x)], axis=0)

    def ms(name, dr, p):
        return stack(pair(name, dr, p))

    def each(fn, *lists):
        return [fn(*xs) for xs in zip(*lists)]

    lhs = [jnp.concatenate([pair("at", dr, p), pair("rt", dr, p)], axis=0).astype(BF16)
           for dr, p in chains]
    rhs = [jnp.concatenate([ms("bt", dr, p), ms("kt", dr, p)], axis=0).astype(BF16) for dr, p in chains]
    at_ms = [ms("at", dr, p).astype(BF16) for dr, p in chains]
    v_ms = [ms("v", dr, p).astype(BF16) for dr, p in chains]
    gmat = each(_dot_nt, lhs, rhs)
    nmat = [jnp.where(strict[dr], stack(g[:c, :n]), 0.0) for (dr, _), g in zip(chains, gmat)]
    a_ak = [jnp.where(strict[dr], stack(g[:c, n:]), 0.0).astype(BF16) for (dr, _), g in zip(chains, gmat)]
    t_pl = _iota((c, 2 * n), 0)
    s_pl = _iota((c, 2 * n), 1) % c
    incl_pl = (s_pl <= t_pl, s_pl >= t_pl)
    a_r = [jnp.where(incl_pl[dr], g[c:], 0.0).astype(BF16)
           for (dr, _), g in zip(chains, gmat)]

    nd = [jnp.where(same, x, 0.0) for x in nmat]
    no = [jnp.where(same, 0.0, x) for x in nmat]
    pw = nd
    tp = nd
    span = 2
    while span < SUB:
        pw = each(_bdot, pw, pw)
        tp = each(lambda t, q: t + q + _bdot(t, q), tp, pw)
        span *= 2
    m = each(lambda t, o: o + _bdot(t, o), tp, no)
    qp = m
    pw = m
    span = 2
    while span < c // SUB:
        pw = each(_bdot, pw, pw)
        qp = each(lambda t, q: t + q + _bdot(t, q), qp, pw)
        span *= 2
    tp = each(lambda t, q: t + q + _bdot(q, t), tp, qp)

    bk_t = [jnp.concatenate([ms("bp", dr, p), ms("kp", dr, p)], axis=0).T.astype(BF16)
            for dr, p in chains]
    diag = row == col
    pc_col = [jnp.sum(jnp.where(diag, prep[dr]["pc"][:, p * LANES:(p + 1) * LANES], 0.0),
                      axis=1, keepdims=True) for dr, p in chains]

    s_t = [states[dr][p].astype(BF16) for dr, p in chains]
    rhs_z = each(lambda l, a, st, v: _dot(jnp.concatenate([l, a], axis=1),
                                          jnp.concatenate([st, v], axis=0)), at_ms, a_ak, s_t, v_ms)
    z = each(lambda t, x: x + _bdot(t, x), tp, rhs_z)
    zv = each(lambda zz, v: jnp.concatenate([zz.astype(BF16), v], axis=0), z, v_ms)
    y_pl = each(lambda l, st, ar, w: _dot(l[c:], st) + _dot(ar, w), lhs, s_t, a_r, zv)
    y = [jnp.concatenate([yy for (d2, _), yy in zip(chains, y_pl) if d2 == dr], axis=1)
         for dr in range(2)]
    new_states = [[None] * pp for _ in range(2)]
    for (dr, p), bt_, w, pc in zip(chains, bk_t, zv, pc_col):
        new_states[dr][p] = states[dr][p] * pc + _dot(bt_, w)
    return y, new_states


def _scan_kernel(pp, nc, has_s0, want_y, want_state, *refs):
    refs = list(refs)
    ka_ref = refs[12]
    dirs = (refs[:6], refs[6:12])
    refs = refs[13:]
    if has_s0:
        s0_ref = refs.pop(0)
    if want_y:
        y_refs = (refs.pop(0), refs.pop(0))
    if want_state:
        st_ref = refs.pop(0)
    s_scr = refs.pop(0)
    c = pl.program_id(2)

    @pl.when(c == 0)
    def _():
        if has_s0:
            s_scr[...] = s0_ref[0]
        else:
            s_scr[...] = jnp.zeros_like(s_scr)

    prep = []
    for dr, (r_ref, k_ref, v_ref, kk_ref, ic_ref, lw_ref) in enumerate(dirs):
        ic = ic_ref[0].astype(F32)
        kd = k_ref[0].astype(F32) * (1.0 + (ic - 1.0) * ka_ref[...])
        prep.append(_scan_prep(r_ref[0].astype(F32), kd, v_ref[0].astype(F32), lw_ref[0],
                               kk_ref[0].astype(F32), ic, reverse=(dr == 1)))
    states = [[s_scr[dr, p] for p in range(pp)] for dr in range(2)]
    y, new_states = _scan_step(prep, states)
    for dr in range(2):
        for p in range(pp):
            s_scr[dr, p] = new_states[dr][p]
        if want_y:
            y_refs[dr][0] = y[dr].astype(BF16)

    if want_state:
        @pl.when(c == nc - 1)
        def _():
            st_ref[0] = s_scr[...]


def _scan(tok, ka, s0, want_y, want_state):
    r, k, v, kk, ic0, ic1, lw0, lw1 = tok
    b, t, d = r.shape
    nc = t // CHUNK
    npair = d // LANES
    pp = min(SCAN_PAIRS, npair)
    w = LANES * pp
    f_spec = pl.BlockSpec((1, CHUNK, w), lambda bi, hi, ci: (bi, ci, hi))
    b_spec = pl.BlockSpec((1, CHUNK, w), lambda bi, hi, ci: (bi, nc - 1 - ci, hi))
    st_spec = pl.BlockSpec((1, 2, pp, LANES, LANES), lambda bi, hi, ci: (bi, 0, hi, 0, 0))
    in_specs = [f_spec] * 6 + [b_spec] * 6 + [pl.BlockSpec((1, w), lambda bi, hi, ci: (0, hi))]
    args = [r, k, v, kk, ic0, lw0, r, k, v, kk, ic1, lw1, ka]
    if s0 is not None:
        in_specs.append(st_spec)
        args.append(s0)
    out_specs, out_shape = [], []
    if want_y:
        out_specs += [f_spec, b_spec]
        out_shape += [jax.ShapeDtypeStruct((b, t, d), BF16)] * 2
    if want_state:
        out_specs.append(st_spec)
        out_shape.append(jax.ShapeDtypeStruct((b, 2, npair, LANES, LANES), F32))
    return pl.pallas_call(
        functools.partial(_scan_kernel, pp, nc, s0 is not None, want_y, want_state),
        grid=(b, npair // pp, nc),
        in_specs=in_specs,
        out_specs=out_specs,
        out_shape=out_shape,
        scratch_shapes=[pltpu.VMEM((2, pp, LANES, LANES), F32)],
        compiler_params=_cparams(("arbitrary", "arbitrary", "arbitrary")),
        name="rwkv_scan_ctx" if s0 is None else "rwkv_scan_latent",
    )(*args)


def _readout_kernel(yf_ref, yb_ref, bon_ref, gate_ref, x_ref, mod_ref, lnw_ref, lnb_ref,
                    wo_ref, g1_ref, g2_ref, x_o, h_o):
    d = x_ref.shape[-1]
    y = yf_ref[0].astype(F32) + yb_ref[0].astype(F32)
    mean = _head_sum(y) * (1.0 / HEAD)
    yc = y - mean
    var = _head_sum(yc * yc) * (1.0 / HEAD)
    o = yc * lax.rsqrt(var + GN_EPS) * lnw_ref[...] + lnb_ref[...] + bon_ref[0]
    att = _dot((o * gate_ref[0]).astype(BF16), wo_ref[...])
    gt_a = mod_ref[0, :, 2 * d:3 * d]
    sh_f = mod_ref[0, :, 3 * d:4 * d]
    sc_f = mod_ref[0, :, 4 * d:5 * d]
    x1 = x_ref[0] + gt_a * _rms(att, g1_ref[...])
    x_o[0] = x1
    h_o[0] = (_rms(x1, g2_ref[...]) * (1.0 + sc_f) + sh_f).astype(BF16)


def _readout(yf, yb, bon, gate, x, mod, lnw, lnb, wo, g1, g2, tT):
    b, t, d = x.shape
    row_spec = pl.BlockSpec((1, tT, d), lambda bi, i: (bi, i, 0))
    vec = pl.BlockSpec((1, d), lambda bi, i: (0, 0))
    return pl.pallas_call(
        _readout_kernel,
        grid=(b, t // tT),
        in_specs=[row_spec] * 5 + [pl.BlockSpec((1, 1, 6 * d), lambda bi, i: (bi, 0, 0)),
                                   vec, vec, pl.BlockSpec((d, d), lambda bi, i: (0, 0)), vec, vec],
        out_specs=[row_spec, row_spec],
        out_shape=[jax.ShapeDtypeStruct((b, t, d), F32), jax.ShapeDtypeStruct((b, t, d), BF16)],
        compiler_params=_cparams(("arbitrary", "arbitrary")),
        name="rwkv_readout",
    )(yf, yb, bon, gate, x, mod, lnw, lnb, wo, g1, g2)


def _ffn_kernel(nf, h_ref, wg_ref, wu_ref, wd_ref, x_ref, mod_ref, g3_ref, gn_ref, mod2_ref,
                x_o, h_o, acc):
    f = pl.program_id(1)
    d = x_ref.shape[-1]

    @pl.when(f == 0)
    def _():
        acc[...] = jnp.zeros_like(acc)

    h = h_ref[...]
    act = _silu(_dot(h, wg_ref[...])) * _dot(h, wu_ref[...])
    acc[...] += _dot(act.astype(BF16), wd_ref[...])

    @pl.when(f == nf - 1)
    def _():
        gt_f = mod_ref[0, :, 5 * d:6 * d]
        x2 = x_ref[...] + gt_f * _rms(acc[...], g3_ref[...])
        x_o[...] = x2
        sh = mod2_ref[0, :, 0:d]
        sc = mod2_ref[0, :, d:2 * d]
        h_o[...] = (_rms(x2, gn_ref[...]) * (1.0 + sc) + sh).astype(BF16)


def _ffn(h, w_gu, w_down, x, mod, g3, gn, mod2, rows_per_batch, tm, tf):
    n, d = x.shape
    ff = w_down.shape[0]
    nf = ff // tf
    per = rows_per_batch // tm
    row = pl.BlockSpec((tm, d), lambda i, f: (i, 0))
    vec = pl.BlockSpec((1, d), lambda i, f: (0, 0))
    modspec = pl.BlockSpec((1, 1, 6 * d), lambda i, f: (i // per, 0, 0))
    return pl.pallas_call(
        functools.partial(_ffn_kernel, nf),
        grid=(n // tm, nf),
        in_specs=[row,
                  pl.BlockSpec((d, tf), lambda i, f: (0, f)),
                  pl.BlockSpec((d, tf), lambda i, f: (0, nf + f)),
                  pl.BlockSpec((tf, d), lambda i, f: (f, 0)),
                  row, modspec, vec, vec, modspec],
        out_specs=[row, row],
        out_shape=[jax.ShapeDtypeStruct((n, d), F32), jax.ShapeDtypeStruct((n, d), BF16)],
        scratch_shapes=[pltpu.VMEM((tm, d), F32)],
        compiler_params=_cparams(("arbitrary", "arbitrary")),
        name="dense_swiglu",
    )(h, w_gu, w_gu, w_down, x, mod, g3, gn, mod2)


HALO = 16


def _conv_kernel(tT, nT, h_ref, hp_ref, hn_ref, x_ref, mod_ref, win_ref, cw_ref, wo_ref,
                 g1_ref, g2_ref, wr_ref, x_o, h_o, lg_o):
    i = pl.program_id(1)
    d = x_ref.shape[-1]
    n = tT + 2 * HALO
    h_ext = jnp.concatenate([hp_ref[0], h_ref[0], hn_ref[0]], axis=0)
    proj = _dot(h_ext, win_ref[...])
    z = proj[:, d:2 * d] * proj[:, 2 * d:]
    row = _iota((n, d), 0)
    dead = ((row < HALO) & (i == 0)) | ((row >= HALO + tT) & (i == nT - 1))
    z = jnp.where(dead, 0.0, z)
    conv = (pltpu.roll(z, 1, 0) * cw_ref[0:1, :] + z * cw_ref[1:2, :]
            + pltpu.roll(z, n - 1, 0) * cw_ref[2:3, :])
    gated = (proj[HALO:HALO + tT, 0:d] * conv[HALO:HALO + tT]).astype(BF16)
    y = _dot(gated, wo_ref[...])
    gt_a = mod_ref[0, :, 2 * d:3 * d]
    sh_f = mod_ref[0, :, 3 * d:4 * d]
    sc_f = mod_ref[0, :, 4 * d:5 * d]
    x3 = x_ref[0] + gt_a * _rms(y, g1_ref[...])
    x_o[0] = x3
    h4 = _rms(x3, g2_ref[...]) * (1.0 + sc_f) + sh_f
    s_rows = d // LANES
    for c in range(s_rows):
        h_o[0, pl.ds(c, tT, stride=s_rows), :] = h4[:, c * LANES:(c + 1) * LANES]
    h_hi, h_lo = _split_bf16(h4)
    lg = _dot(h_hi, wr_ref[0]) + _dot(h_lo, wr_ref[0]) + _dot(h_hi, wr_ref[1])
    lg_o[0] = lg.T[:lg_o.shape[1]]


def _conv_layer(h, x, mod, w_in, conv_w, w_out, g1, g2, w_router, n_experts, tT):
    b, t, d = x.shape
    nT = t // tT
    hb = tT // HALO
    row_spec = pl.BlockSpec((1, tT, d), lambda bi, i: (bi, i, 0))
    vec = pl.BlockSpec((1, d), lambda bi, i: (0, 0))
    full = lambda a: pl.BlockSpec(a.shape, lambda bi, i, _n=a.ndim: (0,) * _n)
    return pl.pallas_call(
        functools.partial(_conv_kernel, tT, nT),
        grid=(b, nT),
        in_specs=[row_spec,
                  pl.BlockSpec((1, HALO, d), lambda bi, i: (bi, jnp.maximum(i * hb - 1, 0), 0)),
                  pl.BlockSpec((1, HALO, d), lambda bi, i: (bi, jnp.minimum((i + 1) * hb, t // HALO - 1), 0)),
                  row_spec, pl.BlockSpec((1, 1, 6 * d), lambda bi, i: (bi, 0, 0)),
                  full(w_in), full(conv_w), full(w_out), vec, vec, full(w_router)],
        out_specs=[row_spec, pl.BlockSpec((1, tT * (d // LANES), LANES), lambda bi, i: (bi, i, 0)),
                   pl.BlockSpec((1, n_experts, tT), lambda bi, i: (bi, 0, i))],
        out_shape=[jax.ShapeDtypeStruct((b, t, d), F32),
                   jax.ShapeDtypeStruct((b, t * (d // LANES), LANES), F32),
                   jax.ShapeDtypeStruct((b, n_experts, t), F32)],
        compiler_params=_cparams(("arbitrary", "arbitrary")),
        name="short_conv",
    )(h, h, h, x, mod, w_in, conv_w, w_out, g1, g2, w_router)


GATHER_ROWS = 512


def _gather_kernel(tg, nsteps, live_ref, tok_cur, tok_nxt, h_hbm, o_ref, buf, sem):
    j = pl.program_id(0)
    slot = j % 2
    s_rows = buf.shape[1] // tg

    def issue(tok_ref, s):
        for r in range(tg):
            src = pl.multiple_of(tok_ref[0, 0, r] * s_rows, s_rows)
            pltpu.make_async_copy(h_hbm.at[pl.ds(src, s_rows)],
                                  buf.at[s, pl.ds(r * s_rows, s_rows)], sem.at[s]).start(priority=r % 2)

    @pl.when((j == 0) & (live_ref[0] > 0))
    def _():
        issue(tok_cur, 0)

    @pl.when(j + 1 < nsteps)
    def _():
        @pl.when(live_ref[j + 1] > 0)
        def _():
            issue(tok_nxt, 1 - slot)

    @pl.when(live_ref[j] > 0)
    def _():
        pltpu.make_async_copy(h_hbm.at[pl.ds(0, tg * s_rows)], buf.at[slot], sem.at[slot]).wait()
        for c in range(s_rows):
            o_ref[:, c * LANES:(c + 1) * LANES] = buf[slot, pl.ds(c, tg, stride=s_rows), :].astype(BF16)

    @pl.when(live_ref[j] == 0)
    def _():
        o_ref[...] = jnp.zeros(o_ref.shape, BF16)


def _gather_rows(h, d, slot_tok, live, tg):
    s_rows = d // LANES
    nsteps = slot_tok.shape[0] // tg
    tok3 = slot_tok.reshape(nsteps, 1, tg)
    grid_spec = pltpu.PrefetchScalarGridSpec(
        num_scalar_prefetch=1,
        grid=(nsteps,),
        in_specs=[pl.BlockSpec((1, 1, tg), lambda j, lv: (j, 0, 0), memory_space=pltpu.SMEM),
                  pl.BlockSpec((1, 1, tg), lambda j, lv: (jnp.minimum(j + 1, nsteps - 1), 0, 0),
                               memory_space=pltpu.SMEM),
                  pl.BlockSpec(memory_space=pl.ANY)],
        out_specs=pl.BlockSpec((tg, d), lambda j, lv: (j, 0)),
        scratch_shapes=[pltpu.VMEM((2, tg * s_rows, LANES), F32), pltpu.SemaphoreType.DMA((2,))],
    )
    return pl.pallas_call(
        functools.partial(_gather_kernel, tg, nsteps),
        grid_spec=grid_spec,
        out_shape=jax.ShapeDtypeStruct((nsteps * tg, d), BF16),
        compiler_params=_cparams(("arbitrary",)),
        name="moe_dispatch",
    )(live, tok3, tok3, h)


MOE_SUB = 512


def _moe_kernel(nf, tm, e_ref, nv_ref, x_ref, wg_ref, wu_ref, wd_ref, y_o):
    i = pl.program_id(0)
    f = pl.program_id(1)
    nv = nv_ref[i]

    nsub = tm // MOE_SUB

    @pl.when(f == 0)
    def _():
        y_o[...] = jnp.zeros(y_o.shape, F32)

    def weights():
        return wg_ref[0].astype(BF16), wu_ref[0].astype(BF16), wd_ref[0].astype(BF16)

    def sub_block(s, wg, wu, wd):
        rows = slice(s * MOE_SUB, (s + 1) * MOE_SUB)
        h = x_ref[rows, :]
        act = _silu(_dot(h, wg)) * _dot(h, wu)
        y_o[rows, :] += _dot(act.astype(BF16), wd)

    @pl.when(nv > (nsub - 1) * MOE_SUB)
    def _():
        w = weights()
        for s in range(nsub):
            sub_block(s, *w)

    @pl.when((nv > 0) & (nv <= (nsub - 1) * MOE_SUB))
    def _():
        w = weights()
        for s in range(nsub - 1):
            @pl.when(nv > s * MOE_SUB)
            def _():
                sub_block(s, *w)


def _moe_experts(xs, blk_e, blk_nv, w_gu, w_down, tm, tf):
    n, d = xs.shape
    ff = w_down.shape[1]
    nf = ff // tf
    nblk = n // tm

    def f_eff(i, f, nv_ref):
        return jnp.where(nv_ref[i] > 0, f, nf - 1)

    grid_spec = pltpu.PrefetchScalarGridSpec(
        num_scalar_prefetch=2,
        grid=(nblk, nf),
        in_specs=[pl.BlockSpec((tm, d), lambda i, f, e, nv: (i, 0)),
                  pl.BlockSpec((1, d, tf), lambda i, f, e, nv: (e[i], 0, f_eff(i, f, nv))),
                  pl.BlockSpec((1, d, tf), lambda i, f, e, nv: (e[i], 0, nf + f_eff(i, f, nv))),
                  pl.BlockSpec((1, tf, d), lambda i, f, e, nv: (e[i], f_eff(i, f, nv), 0))],
        out_specs=pl.BlockSpec((tm, d), lambda i, f, e, nv: (i, 0)),
    )
    return pl.pallas_call(
        functools.partial(_moe_kernel, nf, tm),
        grid_spec=grid_spec,
        out_shape=jax.ShapeDtypeStruct((n, d), F32),
        compiler_params=_cparams(("arbitrary", "arbitrary")),
        name="moe_experts",
    )(blk_e, blk_nv, xs, w_gu, w_gu, w_down)


def _combine_kernel(tT, nsteps, pos_cur, pos_nxt, ys_hbm, gt_ref, x_ref, mod_ref, g_ref, o_ref,
                    buf, sem):
    j = pl.program_id(0)
    slot = j % 2
    d = x_ref.shape[-1]

    def issue(pos_ref, s):
        for r in range(tT):
            for k in range(TOP_K):
                pltpu.make_async_copy(ys_hbm.at[pl.ds(pos_ref[0, k, r], 1)],
                                      buf.at[s, k, pl.ds(r, 1)], sem.at[s]).start(priority=k)

    @pl.when(j == 0)
    def _():
        issue(pos_cur, 0)

    @pl.when(j + 1 < nsteps)
    def _():
        issue(pos_nxt, 1 - slot)

    for k in range(TOP_K):
        pltpu.make_async_copy(ys_hbm.at[pl.ds(0, tT)], buf.at[slot, k], sem.at[slot]).wait()
    gates = gt_ref[...]
    y = buf[slot, 0] * gates[:, 0:1] + buf[slot, 1] * gates[:, 1:2]
    gt_f = mod_ref[0, :, 5 * d:6 * d]
    o_ref[...] = x_ref[...] + gt_f * _rms(y, g_ref[...])


def _combine(ys, pos, gates, x, mod, g, rows_per_batch, tT):
    n, d = x.shape
    nsteps = n // tT
    per = rows_per_batch // tT
    pos3 = pos.reshape(TOP_K, nsteps, tT).transpose(1, 0, 2)
    row_spec = pl.BlockSpec((tT, d), lambda j: (j, 0))
    smem = functools.partial(pl.BlockSpec, (1, TOP_K, tT), memory_space=pltpu.SMEM)
    return pl.pallas_call(
        functools.partial(_combine_kernel, tT, nsteps),
        grid=(nsteps,),
        in_specs=[smem(lambda j: (j, 0, 0)),
                  smem(lambda j: (jnp.minimum(j + 1, nsteps - 1), 0, 0)),
                  pl.BlockSpec(memory_space=pl.ANY),
                  pl.BlockSpec((tT, LANES), lambda j: (j, 0)),
                  row_spec, pl.BlockSpec((1, 1, 6 * d), lambda j: (j // per, 0, 0)),
                  pl.BlockSpec((1, d), lambda j: (0, 0))],
        out_specs=row_spec,
        out_shape=jax.ShapeDtypeStruct((n, d), F32),
        scratch_shapes=[pltpu.VMEM((2, TOP_K, tT, d), F32), pltpu.SemaphoreType.DMA((2,))],
        compiler_params=_cparams(("arbitrary",)),
        name="moe_combine",
    )(pos3, pos3, ys, gates, x, mod, g)


def _route(lg, tm):
    n_experts, n = lg.shape
    nk = n * TOP_K
    idx = jnp.arange(n_experts, dtype=jnp.int32)[:, None]
    m1 = jnp.max(lg, axis=0)
    i1 = jnp.min(jnp.where(lg == m1[None], idx, n_experts), axis=0)
    rest = jnp.where(idx == i1[None], -jnp.inf, lg)
    m2 = jnp.max(rest, axis=0)
    i2 = jnp.min(jnp.where(rest == m2[None], idx, n_experts), axis=0)
    tw = jnp.exp(m2 - m1)
    gates = jnp.stack([1.0 / (1.0 + tw), tw / (1.0 + tw)], axis=1)
    oh = [(idx == i[None]).astype(jnp.int32) for i in (i1, i2)]
    cs = [jnp.cumsum(o, axis=1) for o in oh]
    first = cs[0][:, -1]
    counts = first + cs[1][:, -1]
    nblk_e = (counts + tm - 1) // tm
    start = jnp.cumsum(counts) - counts
    bend = jnp.cumsum(nblk_e)
    bstart = bend - nblk_e
    base = (bstart * tm)[:, None]
    pos = jnp.stack([jnp.sum(oh[0] * (base + cs[0] - oh[0]), axis=0),
                     jnp.sum(oh[1] * (base + first[:, None] + cs[1] - oh[1]), axis=0)])
    flat_e = jnp.concatenate([i1, i2])
    nblk = -(-nk // tm) + n_experts
    blk = jnp.arange(nblk, dtype=jnp.int32)
    blk_e = jnp.minimum(jnp.searchsorted(bend, blk, side="right"), n_experts - 1).astype(jnp.int32)
    blk_nv = jnp.clip(counts[blk_e] - (blk - bstart[blk_e]) * tm, 0, tm)
    blk_nv = jnp.where(blk < bend[-1], blk_nv, 0).astype(jnp.int32)
    order = jnp.argsort(flat_e, stable=True).astype(jnp.int32)
    order = jnp.concatenate([order, jnp.zeros((tm,), jnp.int32)])
    src0 = jnp.where(blk_nv > 0, start[blk_e] + (blk - bstart[blk_e]) * tm, 0)
    slot_pair = jnp.concatenate([lax.dynamic_slice_in_dim(order, src0[k], tm) for k in range(nblk)])
    slot_tok = jnp.where(slot_pair >= n, slot_pair - n, slot_pair)
    return gates, slot_tok, pos, blk_e, blk_nv


def kernel(x, c, ctx, c_ctx, mod_w, mod_b, norm_g, rwkv_mu, rwkv_w_rkv, rwkv_w0, rwkv_w1, rwkv_w2,
           rwkv_a0, rwkv_a1, rwkv_a2, rwkv_g1, rwkv_g2, rwkv_k_k, rwkv_k_a, rwkv_r_k, rwkv_ln_w,
           rwkv_ln_b, rwkv_w_out, conv_w_in, conv_w, conv_w_out, ffn_w_gu, ffn_w_down,
           moe_router, moe_w_gu, moe_w_down):
    b, t, d = x.shape
    n_experts = moe_router.shape[-1]
    rows = 16
    cs = jnp.zeros((rows, d), F32).at[:b].set(c).at[b].set(c_ctx)
    mods = _modulation(cs, mod_w, mod_b)
    mod0 = mods[0, :b].reshape(b, 1, 6 * d)
    mod0c = mods[0, b].reshape(1, 1, 6 * d)
    mod1 = mods[1, :b].reshape(b, 1, 6 * d)

    def pad_lora(w):
        zr = jnp.zeros_like(w[0])
        return jnp.stack([jnp.concatenate([w[0], zr], 0), jnp.concatenate([zr, w[1]], 0)]).astype(BF16)

    lg = rwkv_g1.shape[-1]
    lgp = -(-lg // LANES) * LANES
    p = {
        "ng": norm_g[0, 0].reshape(1, d),
        "mu": rwkv_mu[0],
        "wr": rwkv_w_rkv[0, 0].astype(BF16), "wk": rwkv_w_rkv[0, 1].astype(BF16),
        "wv": rwkv_w_rkv[0, 2].astype(BF16),
        "w1": jnp.concatenate([rwkv_w1[0, 0], rwkv_w1[0, 1]], axis=1).astype(BF16),
        "w2": pad_lora(rwkv_w2[0]), "w0": rwkv_w0[0],
        "a1": jnp.concatenate([rwkv_a1[0, 0], rwkv_a1[0, 1]], axis=1).astype(BF16),
        "a2": pad_lora(rwkv_a2[0]), "a0": rwkv_a0[0],
        "k_k": rwkv_k_k[0].reshape(1, d), "k_a": rwkv_k_a[0].reshape(1, d),
        "g1": jnp.pad(rwkv_g1[0], ((0, 0), (0, lgp - lg))).astype(BF16),
        "g2": jnp.pad(rwkv_g2[0], ((0, lgp - lg), (0, 0))).astype(BF16),
        "r_k": rwkv_r_k[0].reshape(1, d),
    }

    tok_c = _tokenwise(ctx, mod0c, "seq", False, p, ctx.shape[1])
    s_ctx = _scan(tok_c, p["k_a"], None, want_y=False, want_state=True)[0]
    tt = min(ROW_TILE, t)
    tm = min(FFN_ROWS, t)
    tf = min(FFN_COLS, ffn_w_down.shape[1])
    tok_l = _tokenwise(x, mod0, "grid", True, p, tt)
    yf, yb = _scan(tok_l[:8], p["k_a"], s_ctx, want_y=True, want_state=False)
    x1, h2 = _readout(yf, yb, tok_l[9], tok_l[8], x, mod0,
                      rwkv_ln_w[0].reshape(1, d), rwkv_ln_b[0].reshape(1, d),
                      rwkv_w_out[0].astype(BF16), norm_g[0, 1].reshape(1, d),
                      norm_g[0, 2].reshape(1, d), min(2 * ROW_TILE, t))

    x2, h3 = _ffn(h2.reshape(b * t, d), ffn_w_gu[0].astype(BF16), ffn_w_down[0].astype(BF16),
                  x1.reshape(b * t, d), mod0, norm_g[0, 3].reshape(1, d),
                  norm_g[1, 0].reshape(1, d), mod1, t, tm, tf)

    w_router = jnp.stack(_split_bf16(jnp.pad(moe_router[0], ((0, 0), (0, LANES - n_experts)))))
    x3, h4, logits = _conv_layer(h3.reshape(b, t, d), x2.reshape(b, t, d), mod1,
                                 conv_w_in[0].astype(BF16), conv_w[0], conv_w_out[0].astype(BF16),
                                 norm_g[1, 1].reshape(1, d), norm_g[1, 2].reshape(1, d), w_router,
                                 n_experts, tt)

    tme = min(MOE_ROWS, b * t * TOP_K)
    gates, slot_tok, pos, blk_e, blk_nv = _route(
        logits.transpose(1, 0, 2).reshape(n_experts, b * t), tme)
    tg = min(GATHER_ROWS, tme)
    live = (blk_nv[:, None] > jnp.arange(0, tme, tg, dtype=jnp.int32)[None, :]).astype(jnp.int32)
    xs = _gather_rows(h4.reshape(-1, LANES), d, slot_tok, live.reshape(-1), tg)
    ys = _moe_experts(xs, blk_e, blk_nv, moe_w_gu[0], moe_w_down[0], tme,
                      min(MOE_COLS, moe_w_down.shape[2]))
    gates_p = jnp.pad(gates, ((0, 0), (0, LANES - TOP_K)))
    out = _combine(ys, pos, gates_p, x3.reshape(b * t, d), mod1, norm_g[1, 3].reshape(1, d), t, tt)
    return out.reshape(b, t, d)
```

```python
import functools
import math

import jax
import jax.numpy as jnp
from jax import lax
from jax.experimental import pallas as pl
from jax.experimental.pallas import tpu as pltpu

F32 = jnp.float32
BF16 = jnp.bfloat16

HEAD = 64
LANES = 128
GRID_W = 64
CHUNK = 64
SUB = 16
SCAN_PAIRS = 8
NORM_EPS = 1e-6
GN_EPS = 64e-5
DECAY_SCALE = math.exp(-0.5)
TOP_K = 2
VMEM_LIMIT = 56 * 1024 * 1024


ROW_TILE = 256
FFN_ROWS = 1024
FFN_COLS = 512
MOE_ROWS = 2560
MOE_COLS = 512


def _cparams(sem):
    return pltpu.CompilerParams(dimension_semantics=sem, vmem_limit_bytes=VMEM_LIMIT)


def _dot(a, b):
    return jnp.dot(a, b, preferred_element_type=F32)


def _dot_nt(a, b):
    return lax.dot_general(a, b, (((1,), (1,)), ((), ())), preferred_element_type=F32)


def _bdot(a, b):
    return _dot(a.astype(BF16), b.astype(BF16))


def _iota(shape, axis):
    return lax.broadcasted_iota(jnp.int32, shape, axis)


def _rms(x, g):
    return x * lax.rsqrt(jnp.mean(x * x, axis=-1, keepdims=True) + NORM_EPS) * g


def _silu(x):
    h = 0.5 * x
    return h * jnp.tanh(h) + h


def _split_bf16(x):
    hi = x.astype(BF16)
    lo = (x - hi.astype(F32)).astype(BF16)
    return hi, lo


def _head_sum(x):
    ones = (_iota((LANES, LANES), 0) // HEAD == _iota((LANES, LANES), 1) // HEAD).astype(BF16)
    xb = x.astype(BF16)
    outs = []
    for p in range(x.shape[-1] // LANES):
        outs.append(_dot(xb[:, p * LANES:(p + 1) * LANES], ones))
    return jnp.concatenate(outs, axis=1)


def _sigmoid(x):
    return 0.5 * jnp.tanh(0.5 * x) + 0.5


def _mod_kernel(c_ref, w_ref, b_ref, o_ref):
    s = _silu(c_ref[...])
    o_ref[0] = jnp.dot(s, w_ref[0], preferred_element_type=F32,
                       precision=lax.Precision.HIGHEST) + b_ref[0]


def _modulation(cs, mod_w, mod_b):
    depth, d, n = mod_w.shape
    rows = cs.shape[0]
    tn = 1536
    return pl.pallas_call(
        _mod_kernel,
        grid=(depth, n // tn),
        in_specs=[pl.BlockSpec((rows, d), lambda i, j: (0, 0)),
                  pl.BlockSpec((1, d, tn), lambda i, j: (i, 0, j)),
                  pl.BlockSpec((1, 1, tn), lambda i, j: (i, 0, j))],
        out_specs=pl.BlockSpec((1, rows, tn), lambda i, j: (i, 0, j)),
        out_shape=jax.ShapeDtypeStruct((depth, rows, n), F32),
        compiler_params=_cparams(("arbitrary", "arbitrary")),
        name="modulation",
    )(cs, mod_w, mod_b.reshape(depth, 1, n))


def _tok_kernel(mode, readout, tT, nT, *refs):
    refs = list(refs)
    x_ref = refs.pop(0)
    if mode == "grid":
        xp_ref = refs.pop(0)
        xn_ref = refs.pop(0)
    (mod_ref, ng_ref, mu_ref, wr_ref, wk_ref, wv_ref, w1_ref, w2_ref, w0_ref,
     a1_ref, a2_ref, a0_ref, kkp_ref, ka_ref) = refs[:14]
    refs = refs[14:]
    if readout:
        g1_ref, g2_ref, rk_ref = refs[:3]
        refs = refs[3:]
    r_o, k_o, v_o, kk_o, ic0_o, ic1_o, lw0_o, lw1_o = refs[:8]
    refs = refs[8:]

    d = x_ref.shape[-1]
    sh = mod_ref[0, :, 0:d]
    sc = mod_ref[0, :, d:2 * d]
    g = ng_ref[...]

    def norm_mod(xx):
        return _rms(xx, g) * (1.0 + sc) + sh

    h = norm_mod(x_ref[0])
    if mode == "grid":
        i = pl.program_id(1)
        q = d // 4
        hp = jnp.where(i > 0, norm_mod(xp_ref[0]), 0.0)
        hn = jnp.where(i < nT - 1, norm_mod(xn_ref[0]), 0.0)
        col = _iota((tT, q), 0) % GRID_W
        left = jnp.where(col == 0, 0.0, pltpu.roll(h[:, 0:q], 1, 0))
        right = jnp.where(col == GRID_W - 1, 0.0, pltpu.roll(h[:, q:2 * q], tT - 1, 0))
        up = jnp.concatenate([hp[:, 2 * q:3 * q], h[:tT - GRID_W, 2 * q:3 * q]], axis=0)
        down = jnp.concatenate([h[GRID_W:, 3 * q:], hn[:, 3 * q:]], axis=0)
        hs = jnp.concatenate([left, right, up, down], axis=1)
    else:
        half = d // 2
        row = _iota((tT, half), 0)
        prev = jnp.where(row == 0, 0.0, pltpu.roll(h[:, :half], 1, 0))
        nxt = jnp.where(row == tT - 1, 0.0, pltpu.roll(h[:, half:], tT - 1, 0))
        hs = jnp.concatenate([prev, nxt], axis=1)

    hb = h.astype(BF16)
    dxb = (hs - h).astype(BF16)

    def mix(n):
        return hb + dxb * mu_ref[n:n + 1, :].astype(BF16)

    r = _dot(mix(0), wr_ref[...])
    k = _dot(mix(2), wk_ref[...])
    v = _dot(mix(3), wv_ref[...])
    w1o = jnp.tanh(_dot(mix(1), w1_ref[...])).astype(BF16)
    a1o = _dot(mix(4), a1_ref[...]).astype(BF16)
    ics = []
    for z, (lw_o, ic_o) in enumerate(((lw0_o, ic0_o), (lw1_o, ic1_o))):
        w_pre = _dot(w1o, w2_ref[z]) + w0_ref[z:z + 1, :]
        lw_o[0] = (-0.5 * DECAY_SCALE) * jnp.tanh(0.5 * w_pre) - 0.5 * DECAY_SCALE
        ic = _sigmoid(_dot(a1o, a2_ref[z]) + a0_ref[z:z + 1, :])
        ic_o[0] = ic.astype(BF16)
        ics.append(ic)
    kk = k * kkp_ref[...]
    kk = kk * lax.rsqrt(jnp.maximum(_head_sum(kk * kk), 1e-24))
    r_o[0] = r.astype(BF16)
    k_o[0] = k.astype(BF16)
    v_o[0] = v.astype(BF16)
    kk_o[0] = kk.astype(BF16)
    if readout:
        gate_o, bon_o = refs
        gate = _dot(_sigmoid(_dot(mix(5), g1_ref[...])).astype(BF16), g2_ref[...])
        gate_o[0] = gate.astype(BF16)
        k_avg = k * (1.0 + (0.5 * (ics[0] + ics[1]) - 1.0) * ka_ref[...])
        bon_o[0] = (_head_sum(r * k_avg * rk_ref[...]) * v).astype(BF16)


def _tokenwise(x, mod, mode, readout, p, tT):
    b, t, d = x.shape
    nT = t // tT
    row_spec = pl.BlockSpec((1, tT, d), lambda bi, i: (bi, i, 0))
    in_specs = [row_spec]
    args = [x]
    if mode == "grid":
        hb = tT // GRID_W
        in_specs += [
            pl.BlockSpec((1, GRID_W, d), lambda bi, i: (bi, jnp.maximum(i * hb - 1, 0), 0)),
            pl.BlockSpec((1, GRID_W, d), lambda bi, i: (bi, jnp.minimum((i + 1) * hb, t // GRID_W - 1), 0)),
        ]
        args += [x, x]
    if mod.shape[0] == 1:
        in_specs.append(pl.BlockSpec((1, 1, mod.shape[-1]), lambda bi, i: (0, 0, 0)))
    else:
        in_specs.append(pl.BlockSpec((1, 1, mod.shape[-1]), lambda bi, i: (bi, 0, 0)))
    args.append(mod)
    consts = [p["ng"], p["mu"], p["wr"], p["wk"], p["wv"], p["w1"], p["w2"], p["w0"],
              p["a1"], p["a2"], p["a0"], p["k_k"], p["k_a"]]
    if readout:
        consts += [p["g1"], p["g2"], p["r_k"]]
    for a in consts:
        in_specs.append(pl.BlockSpec(a.shape, lambda bi, i, _n=a.ndim: (0,) * _n))
        args.append(a)
    dtypes = [BF16] * 6 + [F32] * 2 + ([BF16] * 2 if readout else [])
    return pl.pallas_call(
        functools.partial(_tok_kernel, mode, readout, tT, nT),
        grid=(b, nT),
        in_specs=in_specs,
        out_specs=[row_spec] * len(dtypes),
        out_shape=[jax.ShapeDtypeStruct((b, t, d), dt) for dt in dtypes],
        compiler_params=_cparams(("arbitrary", "arbitrary")),
        name="rwkv_tokenwise_" + mode,
    )(*args)


def _scan_prep(r, kd, v, lw, kk, ic, reverse):
    c = r.shape[0]
    ti = _iota((c, c), 0)
    si = _iota((c, c), 1)
    tri = ((si >= ti) if reverse else (si <= ti)).astype(BF16)
    lw_hi, lw_lo = _split_bf16(lw)
    cum = _dot(tri, lw_hi) + _dot(tri, lw_lo)
    pc_log = cum[0:1] if reverse else cum[c - 1:c]
    p_inv = jnp.exp(-cum)
    p_rest = jnp.exp(pc_log - cum)
    b = kk * ic
    return {"rt": r * jnp.exp(cum), "at": -(kk * jnp.exp(cum - lw)),
            "bt": b * p_inv, "kt": kd * p_inv, "bp": b * p_rest, "kp": kd * p_rest,
            "v": v, "pc": jnp.exp(pc_log)}


def _scan_step(prep, states):
    c = prep[0]["v"].shape[0]
    n = 2 * c
    pp = len(states[0])
    chains = [(dr, p) for dr in range(2) for p in range(pp)]
    first = _iota((c, LANES), 1) < HEAD
    row = _iota((n, n), 0)
    col = _iota((n, n), 1)
    strict = (col < row, col > row)
    same = (row // SUB) == (col // SUB)

    def pair(name, dr, p):
        return prep[dr][name][:, p * LANES:(p + 1) * LANES]

    def stack(x):
        return jnp.concatenate([jnp.where(first, x, 0.0), jnp.where(first, 0.0, x)], axis=0)

    def ms(name, dr, p):
        return stack(pair(name, dr, p))

    def each(fn, *lists):
        return [fn(*xs) for xs in zip(*lists)]

    lhs = [jnp.concatenate([pair("at", dr, p), pair("rt", dr, p)], axis=0).astype(BF16)
           for dr, p in chains]
    rhs = [jnp.concatenate([ms("bt", dr, p), ms("kt", dr, p)], axis=0).astype(BF16) for dr, p in chains]
    at_ms = [ms("at", dr, p).astype(BF16) for dr, p in chains]
    v_ms = [ms("v", dr, p).astype(BF16) for dr, p in chains]
    gmat = each(_dot_nt, lhs, rhs)
    own = (row // c) == (col // c)
    tri = [own & m for m in strict]
    tri_d = [m & same for m in tri]
    tri_o = [m & jnp.logical_not(same) for m in tri]

    def twice(x):
        return jnp.concatenate([x, x], axis=0)

    nd = [jnp.where(tri_d[dr], twice(g[:c, :n]), 0.0) for (dr, _), g in zip(chains, gmat)]
    no = [jnp.where(tri_o[dr], twice(g[:c, :n]), 0.0) for (dr, _), g in zip(chains, gmat)]
    a_ak = [jnp.where(tri[dr], twice(g[:c, n:]), 0.0).astype(BF16) for (dr, _), g in zip(chains, gmat)]
    t_pl = _iota((c, 2 * n), 0)
    s_pl = _iota((c, 2 * n), 1) % c
    incl_pl = (s_pl <= t_pl, s_pl >= t_pl)
    a_r = [jnp.where(incl_pl[dr], g[c:], 0.0).astype(BF16)
           for (dr, _), g in zip(chains, gmat)]

    pw = nd
    tp = nd
    span = 2
    while span < SUB:
        pw = each(_bdot, pw, pw)
        tp = each(lambda t, q: t + q + _bdot(t, q), tp, pw)
        span *= 2
    m = each(lambda t, o: o + _bdot(t, o), tp, no)
    qp = m
    pw = m
    span = 2
    while span < c // SUB:
        pw = each(_bdot, pw, pw)
        qp = each(lambda t, q: t + q + _bdot(t, q), qp, pw)
        span *= 2
    tp = each(lambda t, q: t + q + _bdot(q, t), tp, qp)

    bk_t = [jnp.concatenate([ms("bp", dr, p), ms("kp", dr, p)], axis=0).T.astype(BF16)
            for dr, p in chains]
    diag = row == col
    pc_col = [jnp.sum(jnp.where(diag, prep[dr]["pc"][:, p * LANES:(p + 1) * LANES], 0.0),
                      axis=1, keepdims=True) for dr, p in chains]

    s_t = [states[dr][p].astype(BF16) for dr, p in chains]
    rhs_z = each(lambda l, a, st, v: _dot(jnp.concatenate([l, a], axis=1),
                                          jnp.concatenate([st, v], axis=0)), at_ms, a_ak, s_t, v_ms)
    z = each(lambda t, x: x + _bdot(t, x), tp, rhs_z)
    zv = each(lambda zz, v: jnp.concatenate([zz.astype(BF16), v], axis=0), z, v_ms)
    y_pl = each(lambda l, st, ar, w: _dot(l[c:], st) + _dot(ar, w), lhs, s_t, a_r, zv)
    y = [jnp.concatenate([yy for (d2, _), yy in zip(chains, y_pl) if d2 == dr], axis=1)
         for dr in range(2)]
    new_states = [[None] * pp for _ in range(2)]
    for (dr, p), bt_, w, pc in zip(chains, bk_t, zv, pc_col):
        new_states[dr][p] = states[dr][p] * pc + _dot(bt_, w)
    return y, new_states


def _scan_kernel(pp, nc, has_s0, want_y, want_state, *refs):
    refs = list(refs)
    ka_ref = refs[12]
    dirs = (refs[:6], refs[6:12])
    refs = refs[13:]
    if has_s0:
        s0_ref = refs.pop(0)
    if want_y:
        y_refs = (refs.pop(0), refs.pop(0))
    if want_state:
        st_ref = refs.pop(0)
    s_scr = refs.pop(0)
    c = pl.program_id(2)

    @pl.when(c == 0)
    def _():
        if has_s0:
            s_scr[...] = s0_ref[0]
        else:
            s_scr[...] = jnp.zeros_like(s_scr)

    prep = []
    for dr, (r_ref, k_ref, v_ref, kk_ref, ic_ref, lw_ref) in enumerate(dirs):
        ic = ic_ref[0].astype(F32)
        kd = k_ref[0].astype(F32) * (1.0 + (ic - 1.0) * ka_ref[...])
        prep.append(_scan_prep(r_ref[0].astype(F32), kd, v_ref[0].astype(F32), lw_ref[0],
                               kk_ref[0].astype(F32), ic, reverse=(dr == 1)))
    states = [[s_scr[dr, p] for p in range(pp)] for dr in range(2)]
    y, new_states = _scan_step(prep, states)
    for dr in range(2):
        for p in range(pp):
            s_scr[dr, p] = new_states[dr][p]
        if want_y:
            y_refs[dr][0] = y[dr].astype(BF16)

    if want_state:
        @pl.when(c == nc - 1)
        def _():
            st_ref[0] = s_scr[...]


def _scan(tok, ka, s0, want_y, want_state):
    r, k, v, kk, ic0, ic1, lw0, lw1 = tok
    b, t, d = r.shape
    nc = t // CHUNK
    npair = d // LANES
    pp = min(SCAN_PAIRS, npair)
    w = LANES * pp
    f_spec = pl.BlockSpec((1, CHUNK, w), lambda bi, hi, ci: (bi, ci, hi))
    b_spec = pl.BlockSpec((1, CHUNK, w), lambda bi, hi, ci: (bi, nc - 1 - ci, hi))
    st_spec = pl.BlockSpec((1, 2, pp, LANES, LANES), lambda bi, hi, ci: (bi, 0, hi, 0, 0))
    in_specs = [f_spec] * 6 + [b_spec] * 6 + [pl.BlockSpec((1, w), lambda bi, hi, ci: (0, hi))]
    args = [r, k, v, kk, ic0, lw0, r, k, v, kk, ic1, lw1, ka]
    if s0 is not None:
        in_specs.append(st_spec)
        args.append(s0)
    out_specs, out_shape = [], []
    if want_y:
        out_specs += [f_spec, b_spec]
        out_shape += [jax.ShapeDtypeStruct((b, t, d), BF16)] * 2
    if want_state:
        out_specs.append(st_spec)
        out_shape.append(jax.ShapeDtypeStruct((b, 2, npair, LANES, LANES), F32))
    return pl.pallas_call(
        functools.partial(_scan_kernel, pp, nc, s0 is not None, want_y, want_state),
        grid=(b, npair // pp, nc),
        in_specs=in_specs,
        out_specs=out_specs,
        out_shape=out_shape,
        scratch_shapes=[pltpu.VMEM((2, pp, LANES, LANES), F32)],
        compiler_params=_cparams(("arbitrary", "arbitrary", "arbitrary")),
        name="rwkv_scan_ctx" if s0 is None else "rwkv_scan_latent",
    )(*args)


def _readout_kernel(yf_ref, yb_ref, bon_ref, gate_ref, x_ref, mod_ref, lnw_ref, lnb_ref,
                    wo_ref, g1_ref, g2_ref, x_o, h_o):
    d = x_ref.shape[-1]
    y = yf_ref[0].astype(F32) + yb_ref[0].astype(F32)
    mean = _head_sum(y) * (1.0 / HEAD)
    yc = y - mean
    var = _head_sum(yc * yc) * (1.0 / HEAD)
    o = yc * lax.rsqrt(var + GN_EPS) * lnw_ref[...] + lnb_ref[...] + bon_ref[0]
    att = _dot((o * gate_ref[0]).astype(BF16), wo_ref[...])
    gt_a = mod_ref[0, :, 2 * d:3 * d]
    sh_f = mod_ref[0, :, 3 * d:4 * d]
    sc_f = mod_ref[0, :, 4 * d:5 * d]
    x1 = x_ref[0] + gt_a * _rms(att, g1_ref[...])
    x_o[0] = x1
    h_o[0] = (_rms(x1, g2_ref[...]) * (1.0 + sc_f) + sh_f).astype(BF16)


def _readout(yf, yb, bon, gate, x, mod, lnw, lnb, wo, g1, g2, tT):
    b, t, d = x.shape
    row_spec = pl.BlockSpec((1, tT, d), lambda bi, i: (bi, i, 0))
    vec = pl.BlockSpec((1, d), lambda bi, i: (0, 0))
    return pl.pallas_call(
        _readout_kernel,
        grid=(b, t // tT),
        in_specs=[row_spec] * 5 + [pl.BlockSpec((1, 1, 6 * d), lambda bi, i: (bi, 0, 0)),
                                   vec, vec, pl.BlockSpec((d, d), lambda bi, i: (0, 0)), vec, vec],
        out_specs=[row_spec, row_spec],
        out_shape=[jax.ShapeDtypeStruct((b, t, d), F32), jax.ShapeDtypeStruct((b, t, d), BF16)],
        compiler_params=_cparams(("arbitrary", "arbitrary")),
        name="rwkv_readout",
    )(yf, yb, bon, gate, x, mod, lnw, lnb, wo, g1, g2)


def _ffn_kernel(nf, h_ref, wg_ref, wu_ref, wd_ref, x_ref, mod_ref, g3_ref, gn_ref, mod2_ref,
                x_o, h_o, acc):
    f = pl.program_id(1)
    d = x_ref.shape[-1]

    @pl.when(f == 0)
    def _():
        acc[...] = jnp.zeros_like(acc)

    h = h_ref[...]
    act = _silu(_dot(h, wg_ref[...])) * _dot(h, wu_ref[...])
    acc[...] += _dot(act.astype(BF16), wd_ref[...])

    @pl.when(f == nf - 1)
    def _():
        gt_f = mod_ref[0, :, 5 * d:6 * d]
        x2 = x_ref[...] + gt_f * _rms(acc[...], g3_ref[...])
        x_o[...] = x2
        sh = mod2_ref[0, :, 0:d]
        sc = mod2_ref[0, :, d:2 * d]
        h_o[...] = (_rms(x2, gn_ref[...]) * (1.0 + sc) + sh).astype(BF16)


def _ffn(h, w_gu, w_down, x, mod, g3, gn, mod2, rows_per_batch, tm, tf):
    n, d = x.shape
    ff = w_down.shape[0]
    nf = ff // tf
    per = rows_per_batch // tm
    row = pl.BlockSpec((tm, d), lambda i, f: (i, 0))
    vec = pl.BlockSpec((1, d), lambda i, f: (0, 0))
    modspec = pl.BlockSpec((1, 1, 6 * d), lambda i, f: (i // per, 0, 0))
    return pl.pallas_call(
        functools.partial(_ffn_kernel, nf),
        grid=(n // tm, nf),
        in_specs=[row,
                  pl.BlockSpec((d, tf), lambda i, f: (0, f)),
                  pl.BlockSpec((d, tf), lambda i, f: (0, nf + f)),
                  pl.BlockSpec((tf, d), lambda i, f: (f, 0)),
                  row, modspec, vec, vec, modspec],
        out_specs=[row, row],
        out_shape=[jax.ShapeDtypeStruct((n, d), F32), jax.ShapeDtypeStruct((n, d), BF16)],
        scratch_shapes=[pltpu.VMEM((tm, d), F32)],
        compiler_params=_cparams(("arbitrary", "arbitrary")),
        name="dense_swiglu",
    )(h, w_gu, w_gu, w_down, x, mod, g3, gn, mod2)


HALO = 16


def _conv_kernel(tT, nT, h_ref, hp_ref, hn_ref, x_ref, mod_ref, win_ref, cw_ref, wo_ref,
                 g1_ref, g2_ref, wr_ref, x_o, h_o, lg_o):
    i = pl.program_id(1)
    d = x_ref.shape[-1]
    n = tT + 2 * HALO
    h_ext = jnp.concatenate([hp_ref[0], h_ref[0], hn_ref[0]], axis=0)
    proj = _dot(h_ext, win_ref[...])
    z = proj[:, d:2 * d] * proj[:, 2 * d:]
    row = _iota((n, d), 0)
    dead = ((row < HALO) & (i == 0)) | ((row >= HALO + tT) & (i == nT - 1))
    z = jnp.where(dead, 0.0, z)
    conv = (pltpu.roll(z, 1, 0) * cw_ref[0:1, :] + z * cw_ref[1:2, :]
            + pltpu.roll(z, n - 1, 0) * cw_ref[2:3, :])
    gated = (proj[HALO:HALO + tT, 0:d] * conv[HALO:HALO + tT]).astype(BF16)
    y = _dot(gated, wo_ref[...])
    gt_a = mod_ref[0, :, 2 * d:3 * d]
    sh_f = mod_ref[0, :, 3 * d:4 * d]
    sc_f = mod_ref[0, :, 4 * d:5 * d]
    x3 = x_ref[0] + gt_a * _rms(y, g1_ref[...])
    x_o[0] = x3
    h4 = _rms(x3, g2_ref[...]) * (1.0 + sc_f) + sh_f
    s_rows = d // LANES
    for c in range(s_rows):
        h_o[0, pl.ds(c, tT, stride=s_rows), :] = h4[:, c * LANES:(c + 1) * LANES]
    h_hi, h_lo = _split_bf16(h4)
    lg = _dot(h_hi, wr_ref[0]) + _dot(h_lo, wr_ref[0]) + _dot(h_hi, wr_ref[1])
    lg_o[0] = lg.T[:lg_o.shape[1]]


def _conv_layer(h, x, mod, w_in, conv_w, w_out, g1, g2, w_router, n_experts, tT):
    b, t, d = x.shape
    nT = t // tT
    hb = tT // HALO
    row_spec = pl.BlockSpec((1, tT, d), lambda bi, i: (bi, i, 0))
    vec = pl.BlockSpec((1, d), lambda bi, i: (0, 0))
    full = lambda a: pl.BlockSpec(a.shape, lambda bi, i, _n=a.ndim: (0,) * _n)
    return pl.pallas_call(
        functools.partial(_conv_kernel, tT, nT),
        grid=(b, nT),
        in_specs=[row_spec,
                  pl.BlockSpec((1, HALO, d), lambda bi, i: (bi, jnp.maximum(i * hb - 1, 0), 0)),
                  pl.BlockSpec((1, HALO, d), lambda bi, i: (bi, jnp.minimum((i + 1) * hb, t // HALO - 1), 0)),
                  row_spec, pl.BlockSpec((1, 1, 6 * d), lambda bi, i: (bi, 0, 0)),
                  full(w_in), full(conv_w), full(w_out), vec, vec, full(w_router)],
        out_specs=[row_spec, pl.BlockSpec((1, tT * (d // LANES), LANES), lambda bi, i: (bi, i, 0)),
                   pl.BlockSpec((1, n_experts, tT), lambda bi, i: (bi, 0, i))],
        out_shape=[jax.ShapeDtypeStruct((b, t, d), F32),
                   jax.ShapeDtypeStruct((b, t * (d // LANES), LANES), F32),
                   jax.ShapeDtypeStruct((b, n_experts, t), F32)],
        compiler_params=_cparams(("arbitrary", "arbitrary")),
        name="short_conv",
    )(h, h, h, x, mod, w_in, conv_w, w_out, g1, g2, w_router)


GATHER_ROWS = 512


def _gather_kernel(tg, nsteps, live_ref, tok_cur, tok_nxt, h_hbm, o_ref, buf, sem):
    j = pl.program_id(0)
    slot = j % 2
    s_rows = buf.shape[1] // tg

    def issue(tok_ref, s):
        for r in range(tg):
            src = pl.multiple_of(tok_ref[0, 0, r] * s_rows, s_rows)
            pltpu.make_async_copy(h_hbm.at[pl.ds(src, s_rows)],
                                  buf.at[s, pl.ds(r * s_rows, s_rows)], sem.at[s]).start(priority=r % 2)

    @pl.when((j == 0) & (live_ref[0] > 0))
    def _():
        issue(tok_cur, 0)

    @pl.when(j + 1 < nsteps)
    def _():
        @pl.when(live_ref[j + 1] > 0)
        def _():
            issue(tok_nxt, 1 - slot)

    @pl.when(live_ref[j] > 0)
    def _():
        pltpu.make_async_copy(h_hbm.at[pl.ds(0, tg * s_rows)], buf.at[slot], sem.at[slot]).wait()
        for c in range(s_rows):
            o_ref[:, c * LANES:(c + 1) * LANES] = buf[slot, pl.ds(c, tg, stride=s_rows), :].astype(BF16)

    @pl.when(live_ref[j] == 0)
    def _():
        o_ref[...] = jnp.zeros(o_ref.shape, BF16)


def _gather_rows(h, d, slot_tok, live, tg):
    s_rows = d // LANES
    nsteps = slot_tok.shape[0] // tg
    tok3 = slot_tok.reshape(nsteps, 1, tg)
    grid_spec = pltpu.PrefetchScalarGridSpec(
        num_scalar_prefetch=1,
        grid=(nsteps,),
        in_specs=[pl.BlockSpec((1, 1, tg), lambda j, lv: (j, 0, 0), memory_space=pltpu.SMEM),
                  pl.BlockSpec((1, 1, tg), lambda j, lv: (jnp.minimum(j + 1, nsteps - 1), 0, 0),
                               memory_space=pltpu.SMEM),
                  pl.BlockSpec(memory_space=pl.ANY)],
        out_specs=pl.BlockSpec((tg, d), lambda j, lv: (j, 0)),
        scratch_shapes=[pltpu.VMEM((2, tg * s_rows, LANES), F32), pltpu.SemaphoreType.DMA((2,))],
    )
    return pl.pallas_call(
        functools.partial(_gather_kernel, tg, nsteps),
        grid_spec=grid_spec,
        out_shape=jax.ShapeDtypeStruct((nsteps * tg, d), BF16),
        compiler_params=_cparams(("arbitrary",)),
        name="moe_dispatch",
    )(live, tok3, tok3, h)


MOE_SUB = 512


def _moe_kernel(nf, tm, e_ref, nv_ref, x_ref, wg_ref, wu_ref, wd_ref, y_o):
    i = pl.program_id(0)
    f = pl.program_id(1)
    nv = nv_ref[i]

    nsub = tm // MOE_SUB

    @pl.when(f == 0)
    def _():
        y_o[...] = jnp.zeros(y_o.shape, F32)

    def weights():
        return wg_ref[0].astype(BF16), wu_ref[0].astype(BF16), wd_ref[0].astype(BF16)

    def sub_block(s, wg, wu, wd):
        rows = slice(s * MOE_SUB, (s + 1) * MOE_SUB)
        h = x_ref[rows, :]
        act = _silu(_dot(h, wg)) * _dot(h, wu)
        y_o[rows, :] += _dot(act.astype(BF16), wd)

    @pl.when(nv > (nsub - 1) * MOE_SUB)
    def _():
        w = weights()
        for s in range(nsub):
            sub_block(s, *w)

    @pl.when((nv > 0) & (nv <= (nsub - 1) * MOE_SUB))
    def _():
        w = weights()
        for s in range(nsub - 1):
            @pl.when(nv > s * MOE_SUB)
            def _():
                sub_block(s, *w)


def _moe_experts(xs, blk_e, blk_nv, w_gu, w_down, tm, tf):
    n, d = xs.shape
    ff = w_down.shape[1]
    nf = ff // tf
    nblk = n // tm

    def f_eff(i, f, nv_ref):
        return jnp.where(nv_ref[i] > 0, f, nf - 1)

    grid_spec = pltpu.PrefetchScalarGridSpec(
        num_scalar_prefetch=2,
        grid=(nblk, nf),
        in_specs=[pl.BlockSpec((tm, d), lambda i, f, e, nv: (i, 0)),
                  pl.BlockSpec((1, d, tf), lambda i, f, e, nv: (e[i], 0, f_eff(i, f, nv))),
                  pl.BlockSpec((1, d, tf), lambda i, f, e, nv: (e[i], 0, nf + f_eff(i, f, nv))),
                  pl.BlockSpec((1, tf, d), lambda i, f, e, nv: (e[i], f_eff(i, f, nv), 0))],
        out_specs=pl.BlockSpec((tm, d), lambda i, f, e, nv: (i, 0)),
    )
    return pl.pallas_call(
        functools.partial(_moe_kernel, nf, tm),
        grid_spec=grid_spec,
        out_shape=jax.ShapeDtypeStruct((n, d), F32),
        compiler_params=_cparams(("arbitrary", "arbitrary")),
        name="moe_experts",
    )(blk_e, blk_nv, xs, w_gu, w_gu, w_down)


def _combine_kernel(tT, nsteps, pos_cur, pos_nxt, ys_hbm, gt_ref, x_ref, mod_ref, g_ref, o_ref,
                    buf, sem):
    j = pl.program_id(0)
    slot = j % 2
    d = x_ref.shape[-1]

    def issue(pos_ref, s):
        for r in range(tT):
            for k in range(TOP_K):
                pltpu.make_async_copy(ys_hbm.at[pl.ds(pos_ref[0, k, r], 1)],
                                      buf.at[s, k, pl.ds(r, 1)], sem.at[s]).start(priority=k)

    @pl.when(j == 0)
    def _():
        issue(pos_cur, 0)

    @pl.when(j + 1 < nsteps)
    def _():
        issue(pos_nxt, 1 - slot)

    for k in range(TOP_K):
        pltpu.make_async_copy(ys_hbm.at[pl.ds(0, tT)], buf.at[slot, k], sem.at[slot]).wait()
    gates = gt_ref[...]
    y = buf[slot, 0] * gates[:, 0:1] + buf[slot, 1] * gates[:, 1:2]
    gt_f = mod_ref[0, :, 5 * d:6 * d]
    o_ref[...] = x_ref[...] + gt_f * _rms(y, g_ref[...])


def _combine(ys, pos, gates, x, mod, g, rows_per_batch, tT):
    n, d = x.shape
    nsteps = n // tT
    per = rows_per_batch // tT
    pos3 = pos.reshape(TOP_K, nsteps, tT).transpose(1, 0, 2)
    row_spec = pl.BlockSpec((tT, d), lambda j: (j, 0))
    smem = functools.partial(pl.BlockSpec, (1, TOP_K, tT), memory_space=pltpu.SMEM)
    return pl.pallas_call(
        functools.partial(_combine_kernel, tT, nsteps),
        grid=(nsteps,),
        in_specs=[smem(lambda j: (j, 0, 0)),
                  smem(lambda j: (jnp.minimum(j + 1, nsteps - 1), 0, 0)),
                  pl.BlockSpec(memory_space=pl.ANY),
                  pl.BlockSpec((tT, LANES), lambda j: (j, 0)),
                  row_spec, pl.BlockSpec((1, 1, 6 * d), lambda j: (j // per, 0, 0)),
                  pl.BlockSpec((1, d), lambda j: (0, 0))],
        out_specs=row_spec,
        out_shape=jax.ShapeDtypeStruct((n, d), F32),
        scratch_shapes=[pltpu.VMEM((2, TOP_K, tT, d), F32), pltpu.SemaphoreType.DMA((2,))],
        compiler_params=_cparams(("arbitrary",)),
        name="moe_combine",
    )(pos3, pos3, ys, gates, x, mod, g)


def _route(lg, tm):
    n_experts, n = lg.shape
    nk = n * TOP_K
    idx = jnp.arange(n_experts, dtype=jnp.int32)[:, None]
    m1 = jnp.max(lg, axis=0)
    i1 = jnp.min(jnp.where(lg == m1[None], idx, n_experts), axis=0)
    rest = jnp.where(idx == i1[None], -jnp.inf, lg)
    m2 = jnp.max(rest, axis=0)
    i2 = jnp.min(jnp.where(rest == m2[None], idx, n_experts), axis=0)
    tw = jnp.exp(m2 - m1)
    gates = jnp.stack([1.0 / (1.0 + tw), tw / (1.0 + tw)], axis=1)
    oh = [(idx == i[None]).astype(jnp.int32) for i in (i1, i2)]
    cs = [jnp.cumsum(o, axis=1) for o in oh]
    first = cs[0][:, -1]
    counts = first + cs[1][:, -1]
    nblk_e = (counts + tm - 1) // tm
    start = jnp.cumsum(counts) - counts
    bend = jnp.cumsum(nblk_e)
    bstart = bend - nblk_e
    base = (bstart * tm)[:, None]
    pos = jnp.stack([jnp.sum(oh[0] * (base + cs[0] - oh[0]), axis=0),
                     jnp.sum(oh[1] * (base + first[:, None] + cs[1] - oh[1]), axis=0)])
    flat_e = jnp.concatenate([i1, i2])
    nblk = -(-nk // tm) + n_experts
    blk = jnp.arange(nblk, dtype=jnp.int32)
    blk_e = jnp.minimum(jnp.searchsorted(bend, blk, side="right"), n_experts - 1).astype(jnp.int32)
    blk_nv = jnp.clip(counts[blk_e] - (blk - bstart[blk_e]) * tm, 0, tm)
    blk_nv = jnp.where(blk < bend[-1], blk_nv, 0).astype(jnp.int32)
    order = jnp.argsort(flat_e, stable=True).astype(jnp.int32)
    order = jnp.concatenate([order, jnp.zeros((tm,), jnp.int32)])
    src0 = jnp.where(blk_nv > 0, start[blk_e] + (blk - bstart[blk_e]) * tm, 0)
    slot_pair = jnp.concatenate([lax.dynamic_slice_in_dim(order, src0[k], tm) for k in range(nblk)])
    slot_tok = jnp.where(slot_pair >= n, slot_pair - n, slot_pair)
    return gates, slot_tok, pos, blk_e, blk_nv


def kernel(x, c, ctx, c_ctx, mod_w, mod_b, norm_g, rwkv_mu, rwkv_w_rkv, rwkv_w0, rwkv_w1, rwkv_w2,
           rwkv_a0, rwkv_a1, rwkv_a2, rwkv_g1, rwkv_g2, rwkv_k_k, rwkv_k_a, rwkv_r_k, rwkv_ln_w,
           rwkv_ln_b, rwkv_w_out, conv_w_in, conv_w, conv_w_out, ffn_w_gu, ffn_w_down,
           moe_router, moe_w_gu, moe_w_down):
    b, t, d = x.shape
    n_experts = moe_router.shape[-1]
    rows = 16
    cs = jnp.zeros((rows, d), F32).at[:b].set(c).at[b].set(c_ctx)
    mods = _modulation(cs, mod_w, mod_b)
    mod0 = mods[0, :b].reshape(b, 1, 6 * d)
    mod0c = mods[0, b].reshape(1, 1, 6 * d)
    mod1 = mods[1, :b].reshape(b, 1, 6 * d)

    def pad_lora(w):
        zr = jnp.zeros_like(w[0])
        return jnp.stack([jnp.concatenate([w[0], zr], 0), jnp.concatenate([zr, w[1]], 0)]).astype(BF16)

    lg = rwkv_g1.shape[-1]
    lgp = -(-lg // LANES) * LANES
    p = {
        "ng": norm_g[0, 0].reshape(1, d),
        "mu": rwkv_mu[0],
        "wr": rwkv_w_rkv[0, 0].astype(BF16), "wk": rwkv_w_rkv[0, 1].astype(BF16),
        "wv": rwkv_w_rkv[0, 2].astype(BF16),
        "w1": jnp.concatenate([rwkv_w1[0, 0], rwkv_w1[0, 1]], axis=1).astype(BF16),
        "w2": pad_lora(rwkv_w2[0]), "w0": rwkv_w0[0],
        "a1": jnp.concatenate([rwkv_a1[0, 0], rwkv_a1[0, 1]], axis=1).astype(BF16),
        "a2": pad_lora(rwkv_a2[0]), "a0": rwkv_a0[0],
        "k_k": rwkv_k_k[0].reshape(1, d), "k_a": rwkv_k_a[0].reshape(1, d),
        "g1": jnp.pad(rwkv_g1[0], ((0, 0), (0, lgp - lg))).astype(BF16),
        "g2": jnp.pad(rwkv_g2[0], ((0, lgp - lg), (0, 0))).astype(BF16),
        "r_k": rwkv_r_k[0].reshape(1, d),
    }

    tok_c = _tokenwise(ctx, mod0c, "seq", False, p, ctx.shape[1])
    s_ctx = _scan(tok_c, p["k_a"], None, want_y=False, want_state=True)[0]
    tt = min(ROW_TILE, t)
    tm = min(FFN_ROWS, t)
    tf = min(FFN_COLS, ffn_w_down.shape[1])
    tok_l = _tokenwise(x, mod0, "grid", True, p, tt)
    yf, yb = _scan(tok_l[:8], p["k_a"], s_ctx, want_y=True, want_state=False)
    x1, h2 = _readout(yf, yb, tok_l[9], tok_l[8], x, mod0,
                      rwkv_ln_w[0].reshape(1, d), rwkv_ln_b[0].reshape(1, d),
                      rwkv_w_out[0].astype(BF16), norm_g[0, 1].reshape(1, d),
                      norm_g[0, 2].reshape(1, d), min(2 * ROW_TILE, t))

    x2, h3 = _ffn(h2.reshape(b * t, d), ffn_w_gu[0].astype(BF16), ffn_w_down[0].astype(BF16),
                  x1.reshape(b * t, d), mod0, norm_g[0, 3].reshape(1, d),
                  norm_g[1, 0].reshape(1, d), mod1, t, tm, tf)

    w_router = jnp.stack(_split_bf16(jnp.pad(moe_router[0], ((0, 0), (0, LANES - n_experts)))))
    x3, h4, logits = _conv_layer(h3.reshape(b, t, d), x2.reshape(b, t, d), mod1,
                                 conv_w_in[0].astype(BF16), conv_w[0], conv_w_out[0].astype(BF16),
                                 norm_g[1, 1].reshape(1, d), norm_g[1, 2].reshape(1, d), w_router,
                                 n_experts, tt)

    tme = min(MOE_ROWS, b * t * TOP_K)
    gates, slot_tok, pos, blk_e, blk_nv = _route(
        logits.transpose(1, 0, 2).reshape(n_experts, b * t), tme)
    tg = min(GATHER_ROWS, tme)
    live = (blk_nv[:, None] > jnp.arange(0, tme, tg, dtype=jnp.int32)[None, :]).astype(jnp.int32)
    xs = _gather_rows(h4.reshape(-1, LANES), d, slot_tok, live.reshape(-1), tg)
    ys = _moe_experts(xs, blk_e, blk_nv, moe_w_gu[0], moe_w_down[0], tme,
                      min(MOE_COLS, moe_w_down.shape[2]))
    gates_p = jnp.pad(gates, ((0, 0), (0, LANES - TOP_K)))
    out = _combine(ys, pos, gates_p, x3.reshape(b * t, d), mod1, norm_g[1, 3].reshape(1, d), t, tt)
    return out.reshape(b, t, d)
```

```python
import functools
import math

import jax
import jax.numpy as jnp
from jax import lax
from jax.experimental import pallas as pl
from jax.experimental.pallas import tpu as pltpu

F32 = jnp.float32
BF16 = jnp.bfloat16

HEAD = 64
LANES = 128
GRID_W = 64
CHUNK = 64
SUB = 16
SCAN_PAIRS = 8
NORM_EPS = 1e-6
GN_EPS = 64e-5
DECAY_SCALE = math.exp(-0.5)
TOP_K = 2
VMEM_LIMIT = 56 * 1024 * 1024


ROW_TILE = 256
FFN_ROWS = 1024
FFN_COLS = 512
MOE_ROWS = 2560
MOE_COLS = 512


def _cparams(sem):
    return pltpu.CompilerParams(dimension_semantics=sem, vmem_limit_bytes=VMEM_LIMIT)


def _dot(a, b):
    return jnp.dot(a, b, preferred_element_type=F32)


def _dot_nt(a, b):
    return lax.dot_general(a, b, (((1,), (1,)), ((), ())), preferred_element_type=F32)


def _bdot(a, b):
    return _dot(a.astype(BF16), b.astype(BF16))


def _iota(shape, axis):
    return lax.broadcasted_iota(jnp.int32, shape, axis)


def _rms(x, g):
    return x * lax.rsqrt(jnp.mean(x * x, axis=-1, keepdims=True) + NORM_EPS) * g


def _silu(x):
    h = 0.5 * x
    return h * jnp.tanh(h) + h


def _split_bf16(x):
    hi = x.astype(BF16)
    lo = (x - hi.astype(F32)).astype(BF16)
    return hi, lo


def _head_sum(x):
    ones = (_iota((LANES, LANES), 0) // HEAD == _iota((LANES, LANES), 1) // HEAD).astype(BF16)
    xb = x.astype(BF16)
    outs = []
    for p in range(x.shape[-1] // LANES):
        outs.append(_dot(xb[:, p * LANES:(p + 1) * LANES], ones))
    return jnp.concatenate(outs, axis=1)


def _sigmoid(x):
    return 0.5 * jnp.tanh(0.5 * x) + 0.5


def _mod_kernel(c_ref, w_ref, b_ref, o_ref):
    s = _silu(c_ref[...])
    o_ref[0] = jnp.dot(s, w_ref[0], preferred_element_type=F32,
                       precision=lax.Precision.HIGHEST) + b_ref[0]


def _modulation(cs, mod_w, mod_b):
    depth, d, n = mod_w.shape
    rows = cs.shape[0]
    tn = 1536
    return pl.pallas_call(
        _mod_kernel,
        grid=(depth, n // tn),
        in_specs=[pl.BlockSpec((rows, d), lambda i, j: (0, 0)),
                  pl.BlockSpec((1, d, tn), lambda i, j: (i, 0, j)),
                  pl.BlockSpec((1, 1, tn), lambda i, j: (i, 0, j))],
        out_specs=pl.BlockSpec((1, rows, tn), lambda i, j: (i, 0, j)),
        out_shape=jax.ShapeDtypeStruct((depth, rows, n), F32),
        compiler_params=_cparams(("arbitrary", "arbitrary")),
        name="modulation",
    )(cs, mod_w, mod_b.reshape(depth, 1, n))


def _tok_kernel(mode, readout, tT, nT, *refs):
    refs = list(refs)
    x_ref = refs.pop(0)
    if mode == "grid":
        xp_ref = refs.pop(0)
        xn_ref = refs.pop(0)
    (mod_ref, ng_ref, mu_ref, wr_ref, wk_ref, wv_ref, w1_ref, w2_ref, w0_ref,
     a1_ref, a2_ref, a0_ref, kkp_ref, ka_ref) = refs[:14]
    refs = refs[14:]
    if readout:
        g1_ref, g2_ref, rk_ref = refs[:3]
        refs = refs[3:]
    r_o, k_o, v_o, kk_o, ic0_o, ic1_o, lw0_o, lw1_o = refs[:8]
    refs = refs[8:]

    d = x_ref.shape[-1]
    sh = mod_ref[0, :, 0:d]
    sc = mod_ref[0, :, d:2 * d]
    g = ng_ref[...]

    def norm_mod(xx):
        return _rms(xx, g) * (1.0 + sc) + sh

    h = norm_mod(x_ref[0])
    if mode == "grid":
        i = pl.program_id(1)
        q = d // 4
        hp = jnp.where(i > 0, norm_mod(xp_ref[0]), 0.0)
        hn = jnp.where(i < nT - 1, norm_mod(xn_ref[0]), 0.0)
        col = _iota((tT, q), 0) % GRID_W
        left = jnp.where(col == 0, 0.0, pltpu.roll(h[:, 0:q], 1, 0))
        right = jnp.where(col == GRID_W - 1, 0.0, pltpu.roll(h[:, q:2 * q], tT - 1, 0))
        up = jnp.concatenate([hp[:, 2 * q:3 * q], h[:tT - GRID_W, 2 * q:3 * q]], axis=0)
        down = jnp.concatenate([h[GRID_W:, 3 * q:], hn[:, 3 * q:]], axis=0)
        hs = jnp.concatenate([left, right, up, down], axis=1)
    else:
        half = d // 2
        row = _iota((tT, half), 0)
        prev = jnp.where(row == 0, 0.0, pltpu.roll(h[:, :half], 1, 0))
        nxt = jnp.where(row == tT - 1, 0.0, pltpu.roll(h[:, half:], tT - 1, 0))
        hs = jnp.concatenate([prev, nxt], axis=1)

    hb = h.astype(BF16)
    dxb = (hs - h).astype(BF16)

    def mix(n):
        return hb + dxb * mu_ref[n:n + 1, :].astype(BF16)

    r = _dot(mix(0), wr_ref[...])
    k = _dot(mix(2), wk_ref[...])
    v = _dot(mix(3), wv_ref[...])
    w1o = jnp.tanh(_dot(mix(1), w1_ref[...])).astype(BF16)
    a1o = _dot(mix(4), a1_ref[...]).astype(BF16)
    ics = []
    for z, (lw_o, ic_o) in enumerate(((lw0_o, ic0_o), (lw1_o, ic1_o))):
        w_pre = _dot(w1o, w2_ref[z]) + w0_ref[z:z + 1, :]
        lw_o[0] = (-0.5 * DECAY_SCALE) * jnp.tanh(0.5 * w_pre) - 0.5 * DECAY_SCALE
        ic = _sigmoid(_dot(a1o, a2_ref[z]) + a0_ref[z:z + 1, :])
        ic_o[0] = ic.astype(BF16)
        ics.append(ic)
    kk = k * kkp_ref[...]
    kk = kk * lax.rsqrt(jnp.maximum(_head_sum(kk * kk), 1e-24))
    r_o[0] = r.astype(BF16)
    k_o[0] = k.astype(BF16)
    v_o[0] = v.astype(BF16)
    kk_o[0] = kk.astype(BF16)
    if readout:
        gate_o, bon_o = refs
        gate = _dot(_sigmoid(_dot(mix(5), g1_ref[...])).astype(BF16), g2_ref[...])
        gate_o[0] = gate.astype(BF16)
        k_avg = k * (1.0 + (0.5 * (ics[0] + ics[1]) - 1.0) * ka_ref[...])
        bon_o[0] = (_head_sum(r * k_avg * rk_ref[...]) * v).astype(BF16)


def _tokenwise(x, mod, mode, readout, p, tT):
    b, t, d = x.shape
    nT = t // tT
    row_spec = pl.BlockSpec((1, tT, d), lambda bi, i: (bi, i, 0))
    in_specs = [row_spec]
    args = [x]
    if mode == "grid":
        hb = tT // GRID_W
        in_specs += [
            pl.BlockSpec((1, GRID_W, d), lambda bi, i: (bi, jnp.maximum(i * hb - 1, 0), 0)),
            pl.BlockSpec((1, GRID_W, d), lambda bi, i: (bi, jnp.minimum((i + 1) * hb, t // GRID_W - 1), 0)),
        ]
        args += [x, x]
    if mod.shape[0] == 1:
        in_specs.append(pl.BlockSpec((1, 1, mod.shape[-1]), lambda bi, i: (0, 0, 0)))
    else:
        in_specs.append(pl.BlockSpec((1, 1, mod.shape[-1]), lambda bi, i: (bi, 0, 0)))
    args.append(mod)
    consts = [p["ng"], p["mu"], p["wr"], p["wk"], p["wv"], p["w1"], p["w2"], p["w0"],
              p["a1"], p["a2"], p["a0"], p["k_k"], p["k_a"]]
    if readout:
        consts += [p["g1"], p["g2"], p["r_k"]]
    for a in consts:
        in_specs.append(pl.BlockSpec(a.shape, lambda bi, i, _n=a.ndim: (0,) * _n))
        args.append(a)
    dtypes = [BF16] * 6 + [F32] * 2 + ([BF16] * 2 if readout else [])
    return pl.pallas_call(
        functools.partial(_tok_kernel, mode, readout, tT, nT),
        grid=(b, nT),
        in_specs=in_specs,
        out_specs=[row_spec] * len(dtypes),
        out_shape=[jax.ShapeDtypeStruct((b, t, d), dt) for dt in dtypes],
        compiler_params=_cparams(("arbitrary", "arbitrary")),
        name="rwkv_tokenwise_" + mode,
    )(*args)


def _scan_prep(r, kd, v, lw, kk, ic, reverse):
    c = r.shape[0]
    ti = _iota((c, c), 0)
    si = _iota((c, c), 1)
    tri = ((si >= ti) if reverse else (si <= ti)).astype(BF16)
    lw_hi, lw_lo = _split_bf16(lw)
    cum = _dot(tri, lw_hi) + _dot(tri, lw_lo)
    pc_log = cum[0:1] if reverse else cum[c - 1:c]
    p_inv = jnp.exp(-cum)
    p_rest = jnp.exp(pc_log - cum)
    b = kk * ic
    return {"rt": r * jnp.exp(cum), "at": -(kk * jnp.exp(cum - lw)),
            "bt": b * p_inv, "kt": kd * p_inv, "bp": b * p_rest, "kp": kd * p_rest,
            "v": v, "pc": jnp.exp(pc_log)}


def _scan_step(prep, states):
    c = prep[0]["v"].shape[0]
    n = 2 * c
    pp = len(states[0])
    chains = [(dr, p) for dr in range(2) for p in range(pp)]
    first = _iota((c, LANES), 1) < HEAD
    row = _iota((n, n), 0)
    col = _iota((n, n), 1)
    strict = (col < row, col > row)
    same = (row // SUB) == (col // SUB)

    def pair(name, dr, p):
        return prep[dr][name][:, p * LANES:(p + 1) * LANES]

    def stack(x):
        return jnp.concatenate([jnp.where(first, x, 0.0), jnp.where(first, 0.0, x)], axis=0)

    def ms(name, dr, p):
        return stack(pair(name, dr, p))

    def each(fn, *lists):
        return [fn(*xs) for xs in zip(*lists)]

    lhs = [jnp.concatenate([pair("at", dr, p), pair("rt", dr, p)], axis=0).astype(BF16)
           for dr, p in chains]
    rhs = [jnp.concatenate([ms("bt", dr, p), ms("kt", dr, p)], axis=0).astype(BF16) for dr, p in chains]
    at_ms = [ms("at", dr, p).astype(BF16) for dr, p in chains]
    v_ms = [ms("v", dr, p).astype(BF16) for dr, p in chains]
    gmat = each(_dot_nt, lhs, rhs)
    own = (row // c) == (col // c)
    tri = [own & m for m in strict]
    tri_d = [m & same for m in tri]
    tri_o = [m & jnp.logical_not(same) for m in tri]

    def twice(x):
        return jnp.concatenate([x, x], axis=0)

    nd = [jnp.where(tri_d[dr], twice(g[:c, :n]), 0.0) for (dr, _), g in zip(chains, gmat)]
    no = [jnp.where(tri_o[dr], twice(g[:c, :n]), 0.0) for (dr, _), g in zip(chains, gmat)]
    a_ak = [jnp.where(tri[dr], twice(g[:c, n:]), 0.0).astype(BF16) for (dr, _), g in zip(chains, gmat)]
    t_pl = _iota((c, 2 * n), 0)
    s_pl = _iota((c, 2 * n), 1) % c
    incl_pl = (s_pl <= t_pl, s_pl >= t_pl)
    a_r = [jnp.where(incl_pl[dr], g[c:], 0.0).astype(BF16)
           for (dr, _), g in zip(chains, gmat)]

    pw = nd
    tp = nd
    span = 2
    while span < SUB:
        pw = each(_bdot, pw, pw)
        tp = each(lambda t, q: t + q + _bdot(t, q), tp, pw)
        span *= 2
    m = each(lambda t, o: o + _bdot(t, o), tp, no)
    qp = m
    pw = m
    span = 2
    while span < c // SUB:
        pw = each(_bdot, pw, pw)
        qp = each(lambda t, q: t + q + _bdot(t, q), qp, pw)
        span *= 2
    tp = each(lambda t, q: t + q + _bdot(q, t), tp, qp)

    bk_t = [jnp.concatenate([ms("bp", dr, p), ms("kp", dr, p)], axis=0).T.astype(BF16)
            for dr, p in chains]
    diag = row == col
    pc_col = [jnp.sum(jnp.where(diag, prep[dr]["pc"][:, p * LANES:(p + 1) * LANES], 0.0),
                      axis=1, keepdims=True) for dr, p in chains]

    s_t = [states[dr][p].astype(BF16) for dr, p in chains]
    rhs_z = each(lambda l, a, st, v: _dot(jnp.concatenate([l, a], axis=1),
                                          jnp.concatenate([st, v], axis=0)), at_ms, a_ak, s_t, v_ms)
    z = each(lambda t, x: x + _bdot(t, x), tp, rhs_z)
    zv = each(lambda zz, v: jnp.concatenate([zz.astype(BF16), v], axis=0), z, v_ms)
    y_pl = each(lambda l, st, ar, w: _dot(l[c:], st) + _dot(ar, w), lhs, s_t, a_r, zv)
    y = [jnp.concatenate([yy for (d2, _), yy in zip(chains, y_pl) if d2 == dr], axis=1)
         for dr in range(2)]
    new_states = [[None] * pp for _ in range(2)]
    for (dr, p), bt_, w, pc in zip(chains, bk_t, zv, pc_col):
        new_states[dr][p] = states[dr][p] * pc + _dot(bt_, w)
    return y, new_states


def _scan_kernel(pp, nc, has_s0, want_y, want_state, *refs):
    refs = list(refs)
    ka_ref = refs[12]
    dirs = (refs[:6], refs[6:12])
    refs = refs[13:]
    if has_s0:
        s0_ref = refs.pop(0)
    if want_y:
        y_refs = (refs.pop(0), refs.pop(0))
    if want_state:
        st_ref = refs.pop(0)
    s_scr = refs.pop(0)
    c = pl.program_id(2)

    @pl.when(c == 0)
    def _():
        if has_s0:
            s_scr[...] = s0_ref[0]
        else:
            s_scr[...] = jnp.zeros_like(s_scr)

    prep = []
    for dr, (r_ref, k_ref, v_ref, kk_ref, ic_ref, lw_ref) in enumerate(dirs):
        ic = ic_ref[0].astype(F32)
        kd = k_ref[0].astype(F32) * (1.0 + (ic - 1.0) * ka_ref[...])
        prep.append(_scan_prep(r_ref[0].astype(F32), kd, v_ref[0].astype(F32), lw_ref[0],
                               kk_ref[0].astype(F32), ic, reverse=(dr == 1)))
    states = [[s_scr[dr, p] for p in range(pp)] for dr in range(2)]
    y, new_states = _scan_step(prep, states)
    for dr in range(2):
        for p in range(pp):
            s_scr[dr, p] = new_states[dr][p]
        if want_y:
            y_refs[dr][0] = y[dr].astype(BF16)

    if want_state:
        @pl.when(c == nc - 1)
        def _():
            st_ref[0] = s_scr[...]


def _scan(tok, ka, s0, want_y, want_state):
    r, k, v, kk, ic0, ic1, lw0, lw1 = tok
    b, t, d = r.shape
    nc = t // CHUNK
    npair = d // LANES
    pp = min(SCAN_PAIRS, npair)
    w = LANES * pp
    f_spec = pl.BlockSpec((1, CHUNK, w), lambda bi, hi, ci: (bi, ci, hi))
    b_spec = pl.BlockSpec((1, CHUNK, w), lambda bi, hi, ci: (bi, nc - 1 - ci, hi))
    st_spec = pl.BlockSpec((1, 2, pp, LANES, LANES), lambda bi, hi, ci: (bi, 0, hi, 0, 0))
    in_specs = [f_spec] * 6 + [b_spec] * 6 + [pl.BlockSpec((1, w), lambda bi, hi, ci: (0, hi))]
    args = [r, k, v, kk, ic0, lw0, r, k, v, kk, ic1, lw1, ka]
    if s0 is not None:
        in_specs.append(st_spec)
        args.append(s0)
    out_specs, out_shape = [], []
    if want_y:
        out_specs += [f_spec, b_spec]
        out_shape += [jax.ShapeDtypeStruct((b, t, d), BF16)] * 2
    if want_state:
        out_specs.append(st_spec)
        out_shape.append(jax.ShapeDtypeStruct((b, 2, npair, LANES, LANES), F32))
    return pl.pallas_call(
        functools.partial(_scan_kernel, pp, nc, s0 is not None, want_y, want_state),
        grid=(b, npair // pp, nc),
        in_specs=in_specs,
        out_specs=out_specs,
        out_shape=out_shape,
        scratch_shapes=[pltpu.VMEM((2, pp, LANES, LANES), F32)],
        compiler_params=_cparams(("arbitrary", "arbitrary", "arbitrary")),
        name="rwkv_scan_ctx" if s0 is None else "rwkv_scan_latent",
    )(*args)


def _readout_kernel(yf_ref, yb_ref, bon_ref, gate_ref, x_ref, mod_ref, lnw_ref, lnb_ref,
                    wo_ref, g1_ref, g2_ref, x_o, h_o):
    d = x_ref.shape[-1]
    y = yf_ref[0].astype(F32) + yb_ref[0].astype(F32)
    mean = _head_sum(y) * (1.0 / HEAD)
    yc = y - mean
    var = _head_sum(yc * yc) * (1.0 / HEAD)
    o = yc * lax.rsqrt(var + GN_EPS) * lnw_ref[...] + lnb_ref[...] + bon_ref[0]
    att = _dot((o * gate_ref[0]).astype(BF16), wo_ref[...])
    gt_a = mod_ref[0, :, 2 * d:3 * d]
    sh_f = mod_ref[0, :, 3 * d:4 * d]
    sc_f = mod_ref[0, :, 4 * d:5 * d]
    x1 = x_ref[0] + gt_a * _rms(att, g1_ref[...])
    x_o[0] = x1
    h_o[0] = (_rms(x1, g2_ref[...]) * (1.0 + sc_f) + sh_f).astype(BF16)


def _readout(yf, yb, bon, gate, x, mod, lnw, lnb, wo, g1, g2, tT):
    b, t, d = x.shape
    row_spec = pl.BlockSpec((1, tT, d), lambda bi, i: (bi, i, 0))
    vec = pl.BlockSpec((1, d), lambda bi, i: (0, 0))
    return pl.pallas_call(
        _readout_kernel,
        grid=(b, t // tT),
        in_specs=[row_spec] * 5 + [pl.BlockSpec((1, 1, 6 * d), lambda bi, i: (bi, 0, 0)),
                                   vec, vec, pl.BlockSpec((d, d), lambda bi, i: (0, 0)), vec, vec],
        out_specs=[row_spec, row_spec],
        out_shape=[jax.ShapeDtypeStruct((b, t, d), F32), jax.ShapeDtypeStruct((b, t, d), BF16)],
        compiler_params=_cparams(("arbitrary", "arbitrary")),
        name="rwkv_readout",
    )(yf, yb, bon, gate, x, mod, lnw, lnb, wo, g1, g2)


def _ffn_kernel(nf, h_ref, wg_ref, wu_ref, wd_ref, x_ref, mod_ref, g3_ref, gn_ref, mod2_ref,
                x_o, h_o, acc):
    f = pl.program_id(1)
    d = x_ref.shape[-1]

    @pl.when(f == 0)
    def _():
        acc[...] = jnp.zeros_like(acc)

    h = h_ref[...]
    act = _silu(_dot(h, wg_ref[...])) * _dot(h, wu_ref[...])
    acc[...] += _dot(act.astype(BF16), wd_ref[...])

    @pl.when(f == nf - 1)
    def _():
        gt_f = mod_ref[0, :, 5 * d:6 * d]
        x2 = x_ref[...] + gt_f * _rms(acc[...], g3_ref[...])
        x_o[...] = x2
        sh = mod2_ref[0, :, 0:d]
        sc = mod2_ref[0, :, d:2 * d]
        h_o[...] = (_rms(x2, gn_ref[...]) * (1.0 + sc) + sh).astype(BF16)


def _ffn(h, w_gu, w_down, x, mod, g3, gn, mod2, rows_per_batch, tm, tf):
    n, d = x.shape
    ff = w_down.shape[0]
    nf = ff // tf
    per = rows_per_batch // tm
    row = pl.BlockSpec((tm, d), lambda i, f: (i, 0))
    vec = pl.BlockSpec((1, d), lambda i, f: (0, 0))
    modspec = pl.BlockSpec((1, 1, 6 * d), lambda i, f: (i // per, 0, 0))
    return pl.pallas_call(
        functools.partial(_ffn_kernel, nf),
        grid=(n // tm, nf),
        in_specs=[row,
                  pl.BlockSpec((d, tf), lambda i, f: (0, f)),
                  pl.BlockSpec((d, tf), lambda i, f: (0, nf + f)),
                  pl.BlockSpec((tf, d), lambda i, f: (f, 0)),
                  row, modspec, vec, vec, modspec],
        out_specs=[row, row],
        out_shape=[jax.ShapeDtypeStruct((n, d), F32), jax.ShapeDtypeStruct((n, d), BF16)],
        scratch_shapes=[pltpu.VMEM((tm, d), F32)],
        compiler_params=_cparams(("arbitrary", "arbitrary")),
        name="dense_swiglu",
    )(h, w_gu, w_gu, w_down, x, mod, g3, gn, mod2)


HALO = 16


def _conv_kernel(tT, nT, n_experts, h_ref, hp_ref, hn_ref, x_ref, mod_ref, win_ref, cw_ref, wo_ref,
                 g1_ref, g2_ref, wr_ref, x_o, h_o, te_o, gt_o):
    i = pl.program_id(1)
    d = x_ref.shape[-1]
    n = tT + 2 * HALO
    h_ext = jnp.concatenate([hp_ref[0], h_ref[0], hn_ref[0]], axis=0)
    proj = _dot(h_ext, win_ref[...])
    z = proj[:, d:2 * d] * proj[:, 2 * d:]
    row = _iota((n, d), 0)
    dead = ((row < HALO) & (i == 0)) | ((row >= HALO + tT) & (i == nT - 1))
    z = jnp.where(dead, 0.0, z)
    conv = (pltpu.roll(z, 1, 0) * cw_ref[0:1, :] + z * cw_ref[1:2, :]
            + pltpu.roll(z, n - 1, 0) * cw_ref[2:3, :])
    gated = (proj[HALO:HALO + tT, 0:d] * conv[HALO:HALO + tT]).astype(BF16)
    y = _dot(gated, wo_ref[...])
    gt_a = mod_ref[0, :, 2 * d:3 * d]
    sh_f = mod_ref[0, :, 3 * d:4 * d]
    sc_f = mod_ref[0, :, 4 * d:5 * d]
    x3 = x_ref[0] + gt_a * _rms(y, g1_ref[...])
    x_o[0] = x3
    h4 = _rms(x3, g2_ref[...]) * (1.0 + sc_f) + sh_f
    s_rows = d // LANES
    for c in range(s_rows):
        h_o[0, pl.ds(c, tT, stride=s_rows), :] = h4[:, c * LANES:(c + 1) * LANES]
    h_hi, h_lo = _split_bf16(h4)
    lg = _dot(h_hi, wr_ref[0]) + _dot(h_lo, wr_ref[0]) + _dot(h_hi, wr_ref[1])
    lg_t = lg.T
    lg8 = lg_t[:n_experts]
    eidx = _iota(lg8.shape, 0)
    m1 = jnp.max(lg8, axis=0, keepdims=True)
    i1 = jnp.min(jnp.where(lg8 == m1, eidx, n_experts), axis=0, keepdims=True)
    rest = jnp.where(eidx == i1, -jnp.inf, lg8)
    m2 = jnp.max(rest, axis=0, keepdims=True)
    i2 = jnp.min(jnp.where(rest == m2, eidx, n_experts), axis=0, keepdims=True)
    tw = jnp.exp(m2 - m1)
    te_o[0] = jnp.concatenate([i1, i2], axis=0)
    gates = jnp.concatenate([1.0 / (1.0 + tw), tw / (1.0 + tw),
                             jnp.zeros((LANES - TOP_K, tT), F32)], axis=0)
    gt_o[0] = gates.T


def _conv_layer(h, x, mod, w_in, conv_w, w_out, g1, g2, w_router, n_experts, tT):
    b, t, d = x.shape
    nT = t // tT
    hb = tT // HALO
    row_spec = pl.BlockSpec((1, tT, d), lambda bi, i: (bi, i, 0))
    vec = pl.BlockSpec((1, d), lambda bi, i: (0, 0))
    full = lambda a: pl.BlockSpec(a.shape, lambda bi, i, _n=a.ndim: (0,) * _n)
    return pl.pallas_call(
        functools.partial(_conv_kernel, tT, nT, n_experts),
        grid=(b, nT),
        in_specs=[row_spec,
                  pl.BlockSpec((1, HALO, d), lambda bi, i: (bi, jnp.maximum(i * hb - 1, 0), 0)),
                  pl.BlockSpec((1, HALO, d), lambda bi, i: (bi, jnp.minimum((i + 1) * hb, t // HALO - 1), 0)),
                  row_spec, pl.BlockSpec((1, 1, 6 * d), lambda bi, i: (bi, 0, 0)),
                  full(w_in), full(conv_w), full(w_out), vec, vec, full(w_router)],
        out_specs=[row_spec, pl.BlockSpec((1, tT * (d // LANES), LANES), lambda bi, i: (bi, i, 0)),
                   pl.BlockSpec((1, TOP_K, tT), lambda bi, i: (bi, 0, i)),
                   pl.BlockSpec((1, tT, LANES), lambda bi, i: (bi, i, 0))],
        out_shape=[jax.ShapeDtypeStruct((b, t, d), F32),
                   jax.ShapeDtypeStruct((b, t * (d // LANES), LANES), F32),
                   jax.ShapeDtypeStruct((b, TOP_K, t), jnp.int32),
                   jax.ShapeDtypeStruct((b, t, LANES), F32)],
        compiler_params=_cparams(("arbitrary", "arbitrary")),
        name="short_conv",
    )(h, h, h, x, mod, w_in, conv_w, w_out, g1, g2, w_router)


GATHER_ROWS = 512


def _gather_kernel(tg, nsteps, live_ref, tok_cur, tok_nxt, h_hbm, o_ref, buf, sem):
    j = pl.program_id(0)
    slot = j % 2
    s_rows = buf.shape[1] // tg

    def issue(tok_ref, s):
        for r in range(tg):
            src = pl.multiple_of(tok_ref[0, 0, r] * s_rows, s_rows)
            pltpu.make_async_copy(h_hbm.at[pl.ds(src, s_rows)],
                                  buf.at[s, pl.ds(r * s_rows, s_rows)], sem.at[s]).start(priority=r % 2)

    @pl.when((j == 0) & (live_ref[0] > 0))
    def _():
        issue(tok_cur, 0)

    @pl.when(j + 1 < nsteps)
    def _():
        @pl.when(live_ref[j + 1] > 0)
        def _():
            issue(tok_nxt, 1 - slot)

    @pl.when(live_ref[j] > 0)
    def _():
        pltpu.make_async_copy(h_hbm.at[pl.ds(0, tg * s_rows)], buf.at[slot], sem.at[slot]).wait()
        for c in range(s_rows):
            o_ref[:, c * LANES:(c + 1) * LANES] = buf[slot, pl.ds(c, tg, stride=s_rows), :].astype(BF16)

    @pl.when(live_ref[j] == 0)
    def _():
        o_ref[...] = jnp.zeros(o_ref.shape, BF16)


def _gather_rows(h, d, slot_tok, live, tg):
    s_rows = d // LANES
    nsteps = slot_tok.shape[0] // tg
    tok3 = slot_tok.reshape(nsteps, 1, tg)
    grid_spec = pltpu.PrefetchScalarGridSpec(
        num_scalar_prefetch=1,
        grid=(nsteps,),
        in_specs=[pl.BlockSpec((1, 1, tg), lambda j, lv: (j, 0, 0), memory_space=pltpu.SMEM),
                  pl.BlockSpec((1, 1, tg), lambda j, lv: (jnp.minimum(j + 1, nsteps - 1), 0, 0),
                               memory_space=pltpu.SMEM),
                  pl.BlockSpec(memory_space=pl.ANY)],
        out_specs=pl.BlockSpec((tg, d), lambda j, lv: (j, 0)),
        scratch_shapes=[pltpu.VMEM((2, tg * s_rows, LANES), F32), pltpu.SemaphoreType.DMA((2,))],
    )
    return pl.pallas_call(
        functools.partial(_gather_kernel, tg, nsteps),
        grid_spec=grid_spec,
        out_shape=jax.ShapeDtypeStruct((nsteps * tg, d), BF16),
        compiler_params=_cparams(("arbitrary",)),
        name="moe_dispatch",
    )(live, tok3, tok3, h)


MOE_SUB = 512


def _moe_kernel(nf, tm, e_ref, nv_ref, x_ref, wg_ref, wu_ref, wd_ref, y_o):
    i = pl.program_id(0)
    f = pl.program_id(1)
    nv = nv_ref[i]

    nsub = tm // MOE_SUB

    @pl.when(f == 0)
    def _():
        y_o[...] = jnp.zeros(y_o.shape, F32)

    def weights():
        return wg_ref[0].astype(BF16), wu_ref[0].astype(BF16), wd_ref[0].astype(BF16)

    def sub_block(s, wg, wu, wd):
        rows = slice(s * MOE_SUB, (s + 1) * MOE_SUB)
        h = x_ref[rows, :]
        act = _silu(_dot(h, wg)) * _dot(h, wu)
        y_o[rows, :] += _dot(act.astype(BF16), wd)

    @pl.when(nv > (nsub - 1) * MOE_SUB)
    def _():
        w = weights()
        for s in range(nsub):
            sub_block(s, *w)

    @pl.when((nv > 0) & (nv <= (nsub - 1) * MOE_SUB))
    def _():
        w = weights()
        for s in range(nsub - 1):
            @pl.when(nv > s * MOE_SUB)
            def _():
                sub_block(s, *w)


def _moe_experts(xs, blk_e, blk_nv, w_gu, w_down, tm, tf):
    n, d = xs.shape
    ff = w_down.shape[1]
    nf = ff // tf
    nblk = n // tm

    def f_eff(i, f, nv_ref):
        return jnp.where(nv_ref[i] > 0, f, nf - 1)

    grid_spec = pltpu.PrefetchScalarGridSpec(
        num_scalar_prefetch=2,
        grid=(nblk, nf),
        in_specs=[pl.BlockSpec((tm, d), lambda i, f, e, nv: (i, 0)),
                  pl.BlockSpec((1, d, tf), lambda i, f, e, nv: (e[i], 0, f_eff(i, f, nv))),
                  pl.BlockSpec((1, d, tf), lambda i, f, e, nv: (e[i], 0, nf + f_eff(i, f, nv))),
                  pl.BlockSpec((1, tf, d), lambda i, f, e, nv: (e[i], f_eff(i, f, nv), 0))],
        out_specs=pl.BlockSpec((tm, d), lambda i, f, e, nv: (i, 0)),
    )
    return pl.pallas_call(
        functools.partial(_moe_kernel, nf, tm),
        grid_spec=grid_spec,
        out_shape=jax.ShapeDtypeStruct((n, d), F32),
        compiler_params=_cparams(("arbitrary", "arbitrary")),
        name="moe_experts",
    )(blk_e, blk_nv, xs, w_gu, w_gu, w_down)


def _combine_kernel(tT, nsteps, pos_cur, pos_nxt, ys_hbm, gt_ref, x_ref, mod_ref, g_ref, o_ref,
                    buf, sem):
    j = pl.program_id(0)
    slot = j % 2
    d = x_ref.shape[-1]

    def issue(pos_ref, s):
        for r in range(tT):
            for k in range(TOP_K):
                pltpu.make_async_copy(ys_hbm.at[pl.ds(pos_ref[0, k, r], 1)],
                                      buf.at[s, k, pl.ds(r, 1)], sem.at[s]).start(priority=k)

    @pl.when(j == 0)
    def _():
        issue(pos_cur, 0)

    @pl.when(j + 1 < nsteps)
    def _():
        issue(pos_nxt, 1 - slot)

    for k in range(TOP_K):
        pltpu.make_async_copy(ys_hbm.at[pl.ds(0, tT)], buf.at[slot, k], sem.at[slot]).wait()
    gates = gt_ref[...]
    y = buf[slot, 0] * gates[:, 0:1] + buf[slot, 1] * gates[:, 1:2]
    gt_f = mod_ref[0, :, 5 * d:6 * d]
    o_ref[...] = x_ref[...] + gt_f * _rms(y, g_ref[...])


def _combine(ys, pos, gates, x, mod, g, rows_per_batch, tT):
    n, d = x.shape
    nsteps = n // tT
    per = rows_per_batch // tT
    pos3 = pos.reshape(TOP_K, nsteps, tT).transpose(1, 0, 2)
    row_spec = pl.BlockSpec((tT, d), lambda j: (j, 0))
    smem = functools.partial(pl.BlockSpec, (1, TOP_K, tT), memory_space=pltpu.SMEM)
    return pl.pallas_call(
        functools.partial(_combine_kernel, tT, nsteps),
        grid=(nsteps,),
        in_specs=[smem(lambda j: (j, 0, 0)),
                  smem(lambda j: (jnp.minimum(j + 1, nsteps - 1), 0, 0)),
                  pl.BlockSpec(memory_space=pl.ANY),
                  pl.BlockSpec((tT, LANES), lambda j: (j, 0)),
                  row_spec, pl.BlockSpec((1, 1, 6 * d), lambda j: (j // per, 0, 0)),
                  pl.BlockSpec((1, d), lambda j: (0, 0))],
        out_specs=row_spec,
        out_shape=jax.ShapeDtypeStruct((n, d), F32),
        scratch_shapes=[pltpu.VMEM((2, TOP_K, tT, d), F32), pltpu.SemaphoreType.DMA((2,))],
        compiler_params=_cparams(("arbitrary",)),
        name="moe_combine",
    )(pos3, pos3, ys, gates, x, mod, g)


def _route(top_e, n_experts, tm):
    n = top_e.shape[1]
    nk = n * TOP_K
    i1, i2 = top_e[0], top_e[1]
    idx = jnp.arange(n_experts, dtype=jnp.int32)[:, None]
    oh = [(idx == i[None]).astype(jnp.int32) for i in (i1, i2)]
    cs = [jnp.cumsum(o, axis=1) for o in oh]
    first = cs[0][:, -1]
    counts = first + cs[1][:, -1]
    nblk_e = (counts + tm - 1) // tm
    start = jnp.cumsum(counts) - counts
    bend = jnp.cumsum(nblk_e)
    bstart = bend - nblk_e
    base = (bstart * tm)[:, None]
    pos = jnp.stack([jnp.sum(oh[0] * (base + cs[0] - oh[0]), axis=0),
                     jnp.sum(oh[1] * (base + first[:, None] + cs[1] - oh[1]), axis=0)])
    flat_e = jnp.concatenate([i1, i2])
    nblk = -(-nk // tm) + n_experts
    blk = jnp.arange(nblk, dtype=jnp.int32)
    blk_e = jnp.minimum(jnp.searchsorted(bend, blk, side="right"), n_experts - 1).astype(jnp.int32)
    blk_nv = jnp.clip(counts[blk_e] - (blk - bstart[blk_e]) * tm, 0, tm)
    blk_nv = jnp.where(blk < bend[-1], blk_nv, 0).astype(jnp.int32)
    order = jnp.argsort(flat_e, stable=True).astype(jnp.int32)
    order = jnp.concatenate([order, jnp.zeros((tm,), jnp.int32)])
    src0 = jnp.where(blk_nv > 0, start[blk_e] + (blk - bstart[blk_e]) * tm, 0)
    slot_pair = jnp.concatenate([lax.dynamic_slice_in_dim(order, src0[k], tm) for k in range(nblk)])
    slot_tok = jnp.where(slot_pair >= n, slot_pair - n, slot_pair)
    return slot_tok, pos, blk_e, blk_nv


def kernel(x, c, ctx, c_ctx, mod_w, mod_b, norm_g, rwkv_mu, rwkv_w_rkv, rwkv_w0, rwkv_w1, rwkv_w2,
           rwkv_a0, rwkv_a1, rwkv_a2, rwkv_g1, rwkv_g2, rwkv_k_k, rwkv_k_a, rwkv_r_k, rwkv_ln_w,
           rwkv_ln_b, rwkv_w_out, conv_w_in, conv_w, conv_w_out, ffn_w_gu, ffn_w_down,
           moe_router, moe_w_gu, moe_w_down):
    b, t, d = x.shape
    n_experts = moe_router.shape[-1]
    rows = 16
    cs = jnp.zeros((rows, d), F32).at[:b].set(c).at[b].set(c_ctx)
    mods = _modulation(cs, mod_w, mod_b)
    mod0 = mods[0, :b].reshape(b, 1, 6 * d)
    mod0c = mods[0, b].reshape(1, 1, 6 * d)
    mod1 = mods[1, :b].reshape(b, 1, 6 * d)

    def pad_lora(w):
        zr = jnp.zeros_like(w[0])
        return jnp.stack([jnp.concatenate([w[0], zr], 0), jnp.concatenate([zr, w[1]], 0)]).astype(BF16)

    lg = rwkv_g1.shape[-1]
    lgp = -(-lg // LANES) * LANES
    p = {
        "ng": norm_g[0, 0].reshape(1, d),
        "mu": rwkv_mu[0],
        "wr": rwkv_w_rkv[0, 0].astype(BF16), "wk": rwkv_w_rkv[0, 1].astype(BF16),
        "wv": rwkv_w_rkv[0, 2].astype(BF16),
        "w1": jnp.concatenate([rwkv_w1[0, 0], rwkv_w1[0, 1]], axis=1).astype(BF16),
        "w2": pad_lora(rwkv_w2[0]), "w0": rwkv_w0[0],
        "a1": jnp.concatenate([rwkv_a1[0, 0], rwkv_a1[0, 1]], axis=1).astype(BF16),
        "a2": pad_lora(rwkv_a2[0]), "a0": rwkv_a0[0],
        "k_k": rwkv_k_k[0].reshape(1, d), "k_a": rwkv_k_a[0].reshape(1, d),
        "g1": jnp.pad(rwkv_g1[0], ((0, 0), (0, lgp - lg))).astype(BF16),
        "g2": jnp.pad(rwkv_g2[0], ((0, lgp - lg), (0, 0))).astype(BF16),
        "r_k": rwkv_r_k[0].reshape(1, d),
    }

    tok_c = _tokenwise(ctx, mod0c, "seq", False, p, ctx.shape[1])
    s_ctx = _scan(tok_c, p["k_a"], None, want_y=False, want_state=True)[0]
    tt = min(ROW_TILE, t)
    tm = min(FFN_ROWS, t)
    tf = min(FFN_COLS, ffn_w_down.shape[1])
    tok_l = _tokenwise(x, mod0, "grid", True, p, tt)
    yf, yb = _scan(tok_l[:8], p["k_a"], s_ctx, want_y=True, want_state=False)
    x1, h2 = _readout(yf, yb, tok_l[9], tok_l[8], x, mod0,
                      rwkv_ln_w[0].reshape(1, d), rwkv_ln_b[0].reshape(1, d),
                      rwkv_w_out[0].astype(BF16), norm_g[0, 1].reshape(1, d),
                      norm_g[0, 2].reshape(1, d), min(2 * ROW_TILE, t))

    x2, h3 = _ffn(h2.reshape(b * t, d), ffn_w_gu[0].astype(BF16), ffn_w_down[0].astype(BF16),
                  x1.reshape(b * t, d), mod0, norm_g[0, 3].reshape(1, d),
                  norm_g[1, 0].reshape(1, d), mod1, t, tm, tf)

    w_router = jnp.stack(_split_bf16(jnp.pad(moe_router[0], ((0, 0), (0, LANES - n_experts)))))
    x3, h4, top_e, gates = _conv_layer(h3.reshape(b, t, d), x2.reshape(b, t, d), mod1,
                                       conv_w_in[0].astype(BF16), conv_w[0], conv_w_out[0].astype(BF16),
                                       norm_g[1, 1].reshape(1, d), norm_g[1, 2].reshape(1, d),
                                       w_router, n_experts, tt)

    tme = min(MOE_ROWS, b * t * TOP_K)
    slot_tok, pos, blk_e, blk_nv = _route(top_e.transpose(1, 0, 2).reshape(TOP_K, b * t),
                                          n_experts, tme)
    tg = min(GATHER_ROWS, tme)
    live = (blk_nv[:, None] > jnp.arange(0, tme, tg, dtype=jnp.int32)[None, :]).astype(jnp.int32)
    xs = _gather_rows(h4.reshape(-1, LANES), d, slot_tok, live.reshape(-1), tg)
    ys = _moe_experts(xs, blk_e, blk_nv, moe_w_gu[0], moe_w_down[0], tme,
                      min(MOE_COLS, moe_w_down.shape[2]))
    out = _combine(ys, pos, gates.reshape(b * t, LANES), x3.reshape(b * t, d), mod1,
                   norm_g[1, 3].reshape(1, d), t, tt)
    return out.reshape(b, t, d)
```

```python
import functools
import math

import jax
import jax.numpy as jnp
from jax import lax
from jax.experimental import pallas as pl
from jax.experimental.pallas import tpu as pltpu

F32 = jnp.float32
BF16 = jnp.bfloat16

HEAD = 64
LANES = 128
GRID_W = 64
CHUNK = 64
SUB = 16
SCAN_PAIRS = 8
SCAN_CHUNKS = 4
NORM_EPS = 1e-6
GN_EPS = 64e-5
DECAY_SCALE = math.exp(-0.5)
TOP_K = 2
VMEM_LIMIT = 56 * 1024 * 1024


ROW_TILE = 256
FFN_ROWS = 1024
FFN_COLS = 512
MOE_ROWS = 2560
MOE_COLS = 512


def _cparams(sem):
    return pltpu.CompilerParams(dimension_semantics=sem, vmem_limit_bytes=VMEM_LIMIT)


def _dot(a, b):
    return jnp.dot(a, b, preferred_element_type=F32)


def _dot_nt(a, b):
    return lax.dot_general(a, b, (((1,), (1,)), ((), ())), preferred_element_type=F32)


def _bdot(a, b):
    return _dot(a.astype(BF16), b.astype(BF16))


def _iota(shape, axis):
    return lax.broadcasted_iota(jnp.int32, shape, axis)


def _rms(x, g):
    return x * lax.rsqrt(jnp.mean(x * x, axis=-1, keepdims=True) + NORM_EPS) * g


def _silu(x):
    h = 0.5 * x
    return h * jnp.tanh(h) + h


def _split_bf16(x):
    hi = x.astype(BF16)
    lo = (x - hi.astype(F32)).astype(BF16)
    return hi, lo


def _head_sum(x):
    ones = (_iota((LANES, LANES), 0) // HEAD == _iota((LANES, LANES), 1) // HEAD).astype(BF16)
    xb = x.astype(BF16)
    outs = []
    for p in range(x.shape[-1] // LANES):
        outs.append(_dot(xb[:, p * LANES:(p + 1) * LANES], ones))
    return jnp.concatenate(outs, axis=1)


def _sigmoid(x):
    return 0.5 * jnp.tanh(0.5 * x) + 0.5


def _mod_kernel(c_ref, w_ref, b_ref, o_ref):
    s = _silu(c_ref[...])
    o_ref[0] = jnp.dot(s, w_ref[0], preferred_element_type=F32,
                       precision=lax.Precision.HIGHEST) + b_ref[0]


def _modulation(cs, mod_w, mod_b):
    depth, d, n = mod_w.shape
    rows = cs.shape[0]
    tn = 1536
    return pl.pallas_call(
        _mod_kernel,
        grid=(depth, n // tn),
        in_specs=[pl.BlockSpec((rows, d), lambda i, j: (0, 0)),
                  pl.BlockSpec((1, d, tn), lambda i, j: (i, 0, j)),
                  pl.BlockSpec((1, 1, tn), lambda i, j: (i, 0, j))],
        out_specs=pl.BlockSpec((1, rows, tn), lambda i, j: (i, 0, j)),
        out_shape=jax.ShapeDtypeStruct((depth, rows, n), F32),
        compiler_params=_cparams(("arbitrary", "arbitrary")),
        name="modulation",
    )(cs, mod_w, mod_b.reshape(depth, 1, n))


def _tok_kernel(mode, readout, tT, nT, *refs):
    refs = list(refs)
    x_ref = refs.pop(0)
    if mode == "grid":
        xp_ref = refs.pop(0)
        xn_ref = refs.pop(0)
    (mod_ref, ng_ref, mu_ref, wr_ref, wk_ref, wv_ref, w1_ref, w2_ref, w0_ref,
     a1_ref, a2_ref, a0_ref, kkp_ref, ka_ref) = refs[:14]
    refs = refs[14:]
    if readout:
        g1_ref, g2_ref, rk_ref = refs[:3]
        refs = refs[3:]
    r_o, k_o, v_o, kk_o, ic0_o, ic1_o, lw0_o, lw1_o = refs[:8]
    refs = refs[8:]

    d = x_ref.shape[-1]
    sh = mod_ref[0, :, 0:d]
    sc = mod_ref[0, :, d:2 * d]
    g = ng_ref[...]

    def norm_mod(xx):
        return _rms(xx, g) * (1.0 + sc) + sh

    h = norm_mod(x_ref[0])
    if mode == "grid":
        i = pl.program_id(1)
        q = d // 4
        hp = jnp.where(i > 0, norm_mod(xp_ref[0]), 0.0)
        hn = jnp.where(i < nT - 1, norm_mod(xn_ref[0]), 0.0)
        col = _iota((tT, q), 0) % GRID_W
        left = jnp.where(col == 0, 0.0, pltpu.roll(h[:, 0:q], 1, 0))
        right = jnp.where(col == GRID_W - 1, 0.0, pltpu.roll(h[:, q:2 * q], tT - 1, 0))
        up = jnp.concatenate([hp[:, 2 * q:3 * q], h[:tT - GRID_W, 2 * q:3 * q]], axis=0)
        down = jnp.concatenate([h[GRID_W:, 3 * q:], hn[:, 3 * q:]], axis=0)
        hs = jnp.concatenate([left, right, up, down], axis=1)
    else:
        half = d // 2
        row = _iota((tT, half), 0)
        prev = jnp.where(row == 0, 0.0, pltpu.roll(h[:, :half], 1, 0))
        nxt = jnp.where(row == tT - 1, 0.0, pltpu.roll(h[:, half:], tT - 1, 0))
        hs = jnp.concatenate([prev, nxt], axis=1)

    hb = h.astype(BF16)
    dxb = (hs - h).astype(BF16)

    def mix(n):
        return hb + dxb * mu_ref[n:n + 1, :].astype(BF16)

    r = _dot(mix(0), wr_ref[...])
    k = _dot(mix(2), wk_ref[...])
    v = _dot(mix(3), wv_ref[...])
    w1o = jnp.tanh(_dot(mix(1), w1_ref[...])).astype(BF16)
    a1o = _dot(mix(4), a1_ref[...]).astype(BF16)
    ics = []
    for z, (lw_o, ic_o) in enumerate(((lw0_o, ic0_o), (lw1_o, ic1_o))):
        w_pre = _dot(w1o, w2_ref[z]) + w0_ref[z:z + 1, :]
        lw_o[0] = (-0.5 * DECAY_SCALE) * jnp.tanh(0.5 * w_pre) - 0.5 * DECAY_SCALE
        ic = _sigmoid(_dot(a1o, a2_ref[z]) + a0_ref[z:z + 1, :])
        ic_o[0] = ic.astype(BF16)
        ics.append(ic)
    kk = k * kkp_ref[...]
    kk = kk * lax.rsqrt(jnp.maximum(_head_sum(kk * kk), 1e-24))
    r_o[0] = r.astype(BF16)
    k_o[0] = k.astype(BF16)
    v_o[0] = v.astype(BF16)
    kk_o[0] = kk.astype(BF16)
    if readout:
        gate_o, bon_o = refs
        gate = _dot(_sigmoid(_dot(mix(5), g1_ref[...])).astype(BF16), g2_ref[...])
        gate_o[0] = gate.astype(BF16)
        k_avg = k * (1.0 + (0.5 * (ics[0] + ics[1]) - 1.0) * ka_ref[...])
        bon_o[0] = (_head_sum(r * k_avg * rk_ref[...]) * v).astype(BF16)


def _tokenwise(x, mod, mode, readout, p, tT):
    b, t, d = x.shape
    nT = t // tT
    row_spec = pl.BlockSpec((1, tT, d), lambda bi, i: (bi, i, 0))
    in_specs = [row_spec]
    args = [x]
    if mode == "grid":
        hb = tT // GRID_W
        in_specs += [
            pl.BlockSpec((1, GRID_W, d), lambda bi, i: (bi, jnp.maximum(i * hb - 1, 0), 0)),
            pl.BlockSpec((1, GRID_W, d), lambda bi, i: (bi, jnp.minimum((i + 1) * hb, t // GRID_W - 1), 0)),
        ]
        args += [x, x]
    if mod.shape[0] == 1:
        in_specs.append(pl.BlockSpec((1, 1, mod.shape[-1]), lambda bi, i: (0, 0, 0)))
    else:
        in_specs.append(pl.BlockSpec((1, 1, mod.shape[-1]), lambda bi, i: (bi, 0, 0)))
    args.append(mod)
    consts = [p["ng"], p["mu"], p["wr"], p["wk"], p["wv"], p["w1"], p["w2"], p["w0"],
              p["a1"], p["a2"], p["a0"], p["k_k"], p["k_a"]]
    if readout:
        consts += [p["g1"], p["g2"], p["r_k"]]
    for a in consts:
        in_specs.append(pl.BlockSpec(a.shape, lambda bi, i, _n=a.ndim: (0,) * _n))
        args.append(a)
    dtypes = [BF16] * 6 + [F32] * 2 + ([BF16] * 2 if readout else [])
    return pl.pallas_call(
        functools.partial(_tok_kernel, mode, readout, tT, nT),
        grid=(b, nT),
        in_specs=in_specs,
        out_specs=[row_spec] * len(dtypes),
        out_shape=[jax.ShapeDtypeStruct((b, t, d), dt) for dt in dtypes],
        compiler_params=_cparams(("arbitrary", "arbitrary")),
        name="rwkv_tokenwise_" + mode,
    )(*args)


def _scan_prep(r, kd, v, lw, kk, ic, reverse):
    c = r.shape[0]
    ti = _iota((c, c), 0)
    si = _iota((c, c), 1)
    tri = ((si >= ti) if reverse else (si <= ti)).astype(BF16)
    lw_hi, lw_lo = _split_bf16(lw)
    cum = _dot(tri, lw_hi) + _dot(tri, lw_lo)
    pc_log = cum[0:1] if reverse else cum[c - 1:c]
    p_inv = jnp.exp(-cum)
    p_rest = jnp.exp(pc_log - cum)
    b = kk * ic
    return {"rt": r * jnp.exp(cum), "at": -(kk * jnp.exp(cum - lw)),
            "bt": b * p_inv, "kt": kd * p_inv, "bp": b * p_rest, "kp": kd * p_rest,
            "v": v, "pc": jnp.exp(pc_log)}


def _scan_step(prep, states):
    c = prep[0]["v"].shape[0]
    n = 2 * c
    pp = len(states[0])
    chains = [(dr, p) for dr in range(2) for p in range(pp)]
    first = _iota((c, LANES), 1) < HEAD
    row = _iota((n, n), 0)
    col = _iota((n, n), 1)
    strict = (col < row, col > row)
    same = (row // SUB) == (col // SUB)

    def pair(name, dr, p):
        return prep[dr][name][:, p * LANES:(p + 1) * LANES]

    def stack(x):
        return jnp.concatenate([jnp.where(first, x, 0.0), jnp.where(first, 0.0, x)], axis=0)

    def ms(name, dr, p):
        return stack(pair(name, dr, p))

    def each(fn, *lists):
        return [fn(*xs) for xs in zip(*lists)]

    lhs = [jnp.concatenate([pair("at", dr, p), pair("rt", dr, p)], axis=0).astype(BF16)
           for dr, p in chains]
    rhs = [jnp.concatenate([ms("bt", dr, p), ms("kt", dr, p)], axis=0).astype(BF16) for dr, p in chains]
    at_ms = [ms("at", dr, p).astype(BF16) for dr, p in chains]
    v_ms = [ms("v", dr, p).astype(BF16) for dr, p in chains]
    gmat = each(_dot_nt, lhs, rhs)
    own = (row // c) == (col // c)
    tri = [own & m for m in strict]
    tri_d = [m & same for m in tri]
    tri_o = [m & jnp.logical_not(same) for m in tri]

    def twice(x):
        return jnp.concatenate([x, x], axis=0)

    nd = [jnp.where(tri_d[dr], twice(g[:c, :n]), 0.0) for (dr, _), g in zip(chains, gmat)]
    no = [jnp.where(tri_o[dr], twice(g[:c, :n]), 0.0) for (dr, _), g in zip(chains, gmat)]
    a_ak = [jnp.where(tri[dr], twice(g[:c, n:]), 0.0).astype(BF16) for (dr, _), g in zip(chains, gmat)]
    t_pl = _iota((c, 2 * n), 0)
    s_pl = _iota((c, 2 * n), 1) % c
    incl_pl = (s_pl <= t_pl, s_pl >= t_pl)
    a_r = [jnp.where(incl_pl[dr], g[c:], 0.0).astype(BF16)
           for (dr, _), g in zip(chains, gmat)]

    pw = nd
    tp = nd
    span = 2
    while span < SUB:
        pw = each(_bdot, pw, pw)
        tp = each(lambda t, q: t + q + _bdot(t, q), tp, pw)
        span *= 2
    m = each(lambda t, o: o + _bdot(t, o), tp, no)
    qp = m
    pw = m
    span = 2
    while span < c // SUB:
        pw = each(_bdot, pw, pw)
        qp = each(lambda t, q: t + q + _bdot(t, q), qp, pw)
        span *= 2
    tp = each(lambda t, q: t + q + _bdot(q, t), tp, qp)

    bk_t = [jnp.concatenate([ms("bp", dr, p), ms("kp", dr, p)], axis=0).T.astype(BF16)
            for dr, p in chains]
    diag = row == col
    pc_col = [jnp.sum(jnp.where(diag, prep[dr]["pc"][:, p * LANES:(p + 1) * LANES], 0.0),
                      axis=1, keepdims=True) for dr, p in chains]

    s_t = [states[dr][p].astype(BF16) for dr, p in chains]
    rhs_z = each(lambda l, a, st, v: _dot(jnp.concatenate([l, a], axis=1),
                                          jnp.concatenate([st, v], axis=0)), at_ms, a_ak, s_t, v_ms)
    z = each(lambda t, x: x + _bdot(t, x), tp, rhs_z)
    zv = each(lambda zz, v: jnp.concatenate([zz.astype(BF16), v], axis=0), z, v_ms)
    y_pl = each(lambda l, st, ar, w: _dot(l[c:], st) + _dot(ar, w), lhs, s_t, a_r, zv)
    y = [jnp.concatenate([yy for (d2, _), yy in zip(chains, y_pl) if d2 == dr], axis=1)
         for dr in range(2)]
    new_states = [[None] * pp for _ in range(2)]
    for (dr, p), bt_, w, pc in zip(chains, bk_t, zv, pc_col):
        new_states[dr][p] = states[dr][p] * pc + _dot(bt_, w)
    return y, new_states


def _scan_kernel(pp, nc, has_s0, want_y, want_state, *refs):
    refs = list(refs)
    ka_ref = refs[12]
    dirs = (refs[:6], refs[6:12])
    refs = refs[13:]
    if has_s0:
        s0_ref = refs.pop(0)
    if want_y:
        y_refs = (refs.pop(0), refs.pop(0))
    if want_state:
        st_ref = refs.pop(0)
    s_scr = refs.pop(0)
    c = pl.program_id(2)

    @pl.when(c == 0)
    def _():
        if has_s0:
            s_scr[...] = s0_ref[0]
        else:
            s_scr[...] = jnp.zeros_like(s_scr)

    per_step = dirs[0][0].shape[1] // CHUNK
    states = [[s_scr[dr, p] for p in range(pp)] for dr in range(2)]
    for j in range(per_step):
        prep = []
        rows = []
        for dr, (r_ref, k_ref, v_ref, kk_ref, ic_ref, lw_ref) in enumerate(dirs):
            q = j if dr == 0 else per_step - 1 - j
            sl = slice(q * CHUNK, (q + 1) * CHUNK)
            rows.append(sl)
            ic = ic_ref[0, sl, :].astype(F32)
            kd = k_ref[0, sl, :].astype(F32) * (1.0 + (ic - 1.0) * ka_ref[...])
            prep.append(_scan_prep(r_ref[0, sl, :].astype(F32), kd, v_ref[0, sl, :].astype(F32),
                                   lw_ref[0, sl, :], kk_ref[0, sl, :].astype(F32), ic,
                                   reverse=(dr == 1)))
        y, states = _scan_step(prep, states)
        if want_y:
            for dr in range(2):
                y_refs[dr][0, rows[dr], :] = y[dr].astype(BF16)
    for dr in range(2):
        for p in range(pp):
            s_scr[dr, p] = states[dr][p]

    if want_state:
        @pl.when(c == nc - 1)
        def _():
            st_ref[0] = s_scr[...]


def _scan(tok, ka, s0, want_y, want_state):
    r, k, v, kk, ic0, ic1, lw0, lw1 = tok
    b, t, d = r.shape
    rows = min(CHUNK * SCAN_CHUNKS, t)
    nc = t // rows
    npair = d // LANES
    pp = min(SCAN_PAIRS, npair)
    w = LANES * pp
    f_spec = pl.BlockSpec((1, rows, w), lambda bi, hi, ci: (bi, ci, hi))
    b_spec = pl.BlockSpec((1, rows, w), lambda bi, hi, ci: (bi, nc - 1 - ci, hi))
    st_spec = pl.BlockSpec((1, 2, pp, LANES, LANES), lambda bi, hi, ci: (bi, 0, hi, 0, 0))
    in_specs = [f_spec] * 6 + [b_spec] * 6 + [pl.BlockSpec((1, w), lambda bi, hi, ci: (0, hi))]
    args = [r, k, v, kk, ic0, lw0, r, k, v, kk, ic1, lw1, ka]
    if s0 is not None:
        in_specs.append(st_spec)
        args.append(s0)
    out_specs, out_shape = [], []
    if want_y:
        out_specs += [f_spec, b_spec]
        out_shape += [jax.ShapeDtypeStruct((b, t, d), BF16)] * 2
    if want_state:
        out_specs.append(st_spec)
        out_shape.append(jax.ShapeDtypeStruct((b, 2, npair, LANES, LANES), F32))
    return pl.pallas_call(
        functools.partial(_scan_kernel, pp, nc, s0 is not None, want_y, want_state),
        grid=(b, npair // pp, nc),
        in_specs=in_specs,
        out_specs=out_specs,
        out_shape=out_shape,
        scratch_shapes=[pltpu.VMEM((2, pp, LANES, LANES), F32)],
        compiler_params=_cparams(("arbitrary", "arbitrary", "arbitrary")),
        name="rwkv_scan_ctx" if s0 is None else "rwkv_scan_latent",
    )(*args)


def _readout_kernel(yf_ref, yb_ref, bon_ref, gate_ref, x_ref, mod_ref, lnw_ref, lnb_ref,
                    wo_ref, g1_ref, g2_ref, x_o, h_o):
    d = x_ref.shape[-1]
    y = yf_ref[0].astype(F32) + yb_ref[0].astype(F32)
    mean = _head_sum(y) * (1.0 / HEAD)
    yc = y - mean
    var = _head_sum(yc * yc) * (1.0 / HEAD)
    o = yc * lax.rsqrt(var + GN_EPS) * lnw_ref[...] + lnb_ref[...] + bon_ref[0]
    att = _dot((o * gate_ref[0]).astype(BF16), wo_ref[...])
    gt_a = mod_ref[0, :, 2 * d:3 * d]
    sh_f = mod_ref[0, :, 3 * d:4 * d]
    sc_f = mod_ref[0, :, 4 * d:5 * d]
    x1 = x_ref[0] + gt_a * _rms(att, g1_ref[...])
    x_o[0] = x1
    h_o[0] = (_rms(x1, g2_ref[...]) * (1.0 + sc_f) + sh_f).astype(BF16)


def _readout(yf, yb, bon, gate, x, mod, lnw, lnb, wo, g1, g2, tT):
    b, t, d = x.shape
    row_spec = pl.BlockSpec((1, tT, d), lambda bi, i: (bi, i, 0))
    vec = pl.BlockSpec((1, d), lambda bi, i: (0, 0))
    return pl.pallas_call(
        _readout_kernel,
        grid=(b, t // tT),
        in_specs=[row_spec] * 5 + [pl.BlockSpec((1, 1, 6 * d), lambda bi, i: (bi, 0, 0)),
                                   vec, vec, pl.BlockSpec((d, d), lambda bi, i: (0, 0)), vec, vec],
        out_specs=[row_spec, row_spec],
        out_shape=[jax.ShapeDtypeStruct((b, t, d), F32), jax.ShapeDtypeStruct((b, t, d), BF16)],
        compiler_params=_cparams(("arbitrary", "arbitrary")),
        name="rwkv_readout",
    )(yf, yb, bon, gate, x, mod, lnw, lnb, wo, g1, g2)


def _ffn_kernel(nf, h_ref, wg_ref, wu_ref, wd_ref, x_ref, mod_ref, g3_ref, gn_ref, mod2_ref,
                x_o, h_o, acc):
    f = pl.program_id(1)
    d = x_ref.shape[-1]

    @pl.when(f == 0)
    def _():
        acc[...] = jnp.zeros_like(acc)

    h = h_ref[...]
    act = _silu(_dot(h, wg_ref[...])) * _dot(h, wu_ref[...])
    acc[...] += _dot(act.astype(BF16), wd_ref[...])

    @pl.when(f == nf - 1)
    def _():
        gt_f = mod_ref[0, :, 5 * d:6 * d]
        x2 = x_ref[...] + gt_f * _rms(acc[...], g3_ref[...])
        x_o[...] = x2
        sh = mod2_ref[0, :, 0:d]
        sc = mod2_ref[0, :, d:2 * d]
        h_o[...] = (_rms(x2, gn_ref[...]) * (1.0 + sc) + sh).astype(BF16)


def _ffn(h, w_gu, w_down, x, mod, g3, gn, mod2, rows_per_batch, tm, tf):
    n, d = x.shape
    ff = w_down.shape[0]
    nf = ff // tf
    per = rows_per_batch // tm
    row = pl.BlockSpec((tm, d), lambda i, f: (i, 0))
    vec = pl.BlockSpec((1, d), lambda i, f: (0, 0))
    modspec = pl.BlockSpec((1, 1, 6 * d), lambda i, f: (i // per, 0, 0))
    return pl.pallas_call(
        functools.partial(_ffn_kernel, nf),
        grid=(n // tm, nf),
        in_specs=[row,
                  pl.BlockSpec((d, tf), lambda i, f: (0, f)),
                  pl.BlockSpec((d, tf), lambda i, f: (0, nf + f)),
                  pl.BlockSpec((tf, d), lambda i, f: (f, 0)),
                  row, modspec, vec, vec, modspec],
        out_specs=[row, row],
        out_shape=[jax.ShapeDtypeStruct((n, d), F32), jax.ShapeDtypeStruct((n, d), BF16)],
        scratch_shapes=[pltpu.VMEM((tm, d), F32)],
        compiler_params=_cparams(("arbitrary", "arbitrary")),
        name="dense_swiglu",
    )(h, w_gu, w_gu, w_down, x, mod, g3, gn, mod2)


HALO = 16


def _conv_kernel(tT, nT, n_experts, h_ref, hp_ref, hn_ref, x_ref, mod_ref, win_ref, cw_ref, wo_ref,
                 g1_ref, g2_ref, wr_ref, x_o, h_o, te_o, gt_o):
    i = pl.program_id(1)
    d = x_ref.shape[-1]
    n = tT + 2 * HALO
    h_ext = jnp.concatenate([hp_ref[0], h_ref[0], hn_ref[0]], axis=0)
    proj = _dot(h_ext, win_ref[...])
    z = proj[:, d:2 * d] * proj[:, 2 * d:]
    row = _iota((n, d), 0)
    dead = ((row < HALO) & (i == 0)) | ((row >= HALO + tT) & (i == nT - 1))
    z = jnp.where(dead, 0.0, z)
    conv = (pltpu.roll(z, 1, 0) * cw_ref[0:1, :] + z * cw_ref[1:2, :]
            + pltpu.roll(z, n - 1, 0) * cw_ref[2:3, :])
    gated = (proj[HALO:HALO + tT, 0:d] * conv[HALO:HALO + tT]).astype(BF16)
    y = _dot(gated, wo_ref[...])
    gt_a = mod_ref[0, :, 2 * d:3 * d]
    sh_f = mod_ref[0, :, 3 * d:4 * d]
    sc_f = mod_ref[0, :, 4 * d:5 * d]
    x3 = x_ref[0] + gt_a * _rms(y, g1_ref[...])
    x_o[0] = x3
    h4 = _rms(x3, g2_ref[...]) * (1.0 + sc_f) + sh_f
    s_rows = d // LANES
    for c in range(s_rows):
        h_o[0, pl.ds(c, tT, stride=s_rows), :] = h4[:, c * LANES:(c + 1) * LANES]
    h_hi, h_lo = _split_bf16(h4)
    lg = _dot(h_hi, wr_ref[0]) + _dot(h_lo, wr_ref[0]) + _dot(h_hi, wr_ref[1])
    lg_t = lg.T
    lg8 = lg_t[:n_experts]
    eidx = _iota(lg8.shape, 0)
    m1 = jnp.max(lg8, axis=0, keepdims=True)
    i1 = jnp.min(jnp.where(lg8 == m1, eidx, n_experts), axis=0, keepdims=True)
    rest = jnp.where(eidx == i1, -jnp.inf, lg8)
    m2 = jnp.max(rest, axis=0, keepdims=True)
    i2 = jnp.min(jnp.where(rest == m2, eidx, n_experts), axis=0, keepdims=True)
    tw = jnp.exp(m2 - m1)
    te_o[0] = jnp.concatenate([i1, i2], axis=0)
    gates = jnp.concatenate([1.0 / (1.0 + tw), tw / (1.0 + tw),
                             jnp.zeros((LANES - TOP_K, tT), F32)], axis=0)
    gt_o[0] = gates.T


def _conv_layer(h, x, mod, w_in, conv_w, w_out, g1, g2, w_router, n_experts, tT):
    b, t, d = x.shape
    nT = t // tT
    hb = tT // HALO
    row_spec = pl.BlockSpec((1, tT, d), lambda bi, i: (bi, i, 0))
    vec = pl.BlockSpec((1, d), lambda bi, i: (0, 0))
    full = lambda a: pl.BlockSpec(a.shape, lambda bi, i, _n=a.ndim: (0,) * _n)
    return pl.pallas_call(
        functools.partial(_conv_kernel, tT, nT, n_experts),
        grid=(b, nT),
        in_specs=[row_spec,
                  pl.BlockSpec((1, HALO, d), lambda bi, i: (bi, jnp.maximum(i * hb - 1, 0), 0)),
                  pl.BlockSpec((1, HALO, d), lambda bi, i: (bi, jnp.minimum((i + 1) * hb, t // HALO - 1), 0)),
                  row_spec, pl.BlockSpec((1, 1, 6 * d), lambda bi, i: (bi, 0, 0)),
                  full(w_in), full(conv_w), full(w_out), vec, vec, full(w_router)],
        out_specs=[row_spec, pl.BlockSpec((1, tT * (d // LANES), LANES), lambda bi, i: (bi, i, 0)),
                   pl.BlockSpec((1, TOP_K, tT), lambda bi, i: (bi, 0, i)),
                   pl.BlockSpec((1, tT, LANES), lambda bi, i: (bi, i, 0))],
        out_shape=[jax.ShapeDtypeStruct((b, t, d), F32),
                   jax.ShapeDtypeStruct((b, t * (d // LANES), LANES), F32),
                   jax.ShapeDtypeStruct((b, TOP_K, t), jnp.int32),
                   jax.ShapeDtypeStruct((b, t, LANES), F32)],
        compiler_params=_cparams(("arbitrary", "arbitrary")),
        name="short_conv",
    )(h, h, h, x, mod, w_in, conv_w, w_out, g1, g2, w_router)


GATHER_ROWS = 512


def _gather_kernel(tg, nsteps, live_ref, tok_cur, tok_nxt, h_hbm, o_ref, buf, sem):
    j = pl.program_id(0)
    slot = j % 2
    s_rows = buf.shape[1] // tg

    def issue(tok_ref, s):
        for r in range(tg):
            src = pl.multiple_of(tok_ref[0, 0, r] * s_rows, s_rows)
            pltpu.make_async_copy(h_hbm.at[pl.ds(src, s_rows)],
                                  buf.at[s, pl.ds(r * s_rows, s_rows)], sem.at[s]).start(priority=r % 2)

    @pl.when((j == 0) & (live_ref[0] > 0))
    def _():
        issue(tok_cur, 0)

    @pl.when(j + 1 < nsteps)
    def _():
        @pl.when(live_ref[j + 1] > 0)
        def _():
            issue(tok_nxt, 1 - slot)

    @pl.when(live_ref[j] > 0)
    def _():
        pltpu.make_async_copy(h_hbm.at[pl.ds(0, tg * s_rows)], buf.at[slot], sem.at[slot]).wait()
        for c in range(s_rows):
            o_ref[:, c * LANES:(c + 1) * LANES] = buf[slot, pl.ds(c, tg, stride=s_rows), :].astype(BF16)

    @pl.when(live_ref[j] == 0)
    def _():
        o_ref[...] = jnp.zeros(o_ref.shape, BF16)


def _gather_rows(h, d, slot_tok, live, tg):
    s_rows = d // LANES
    nsteps = slot_tok.shape[0] // tg
    tok3 = slot_tok.reshape(nsteps, 1, tg)
    grid_spec = pltpu.PrefetchScalarGridSpec(
        num_scalar_prefetch=1,
        grid=(nsteps,),
        in_specs=[pl.BlockSpec((1, 1, tg), lambda j, lv: (j, 0, 0), memory_space=pltpu.SMEM),
                  pl.BlockSpec((1, 1, tg), lambda j, lv: (jnp.minimum(j + 1, nsteps - 1), 0, 0),
                               memory_space=pltpu.SMEM),
                  pl.BlockSpec(memory_space=pl.ANY)],
        out_specs=pl.BlockSpec((tg, d), lambda j, lv: (j, 0)),
        scratch_shapes=[pltpu.VMEM((2, tg * s_rows, LANES), F32), pltpu.SemaphoreType.DMA((2,))],
    )
    return pl.pallas_call(
        functools.partial(_gather_kernel, tg, nsteps),
        grid_spec=grid_spec,
        out_shape=jax.ShapeDtypeStruct((nsteps * tg, d), BF16),
        compiler_params=_cparams(("arbitrary",)),
        name="moe_dispatch",
    )(live, tok3, tok3, h)


MOE_SUB = 512


def _moe_kernel(nf, tm, e_ref, nv_ref, x_ref, wg_ref, wu_ref, wd_ref, y_o):
    i = pl.program_id(0)
    f = pl.program_id(1)
    nv = nv_ref[i]

    nsub = tm // MOE_SUB

    @pl.when(f == 0)
    def _():
        y_o[...] = jnp.zeros(y_o.shape, F32)

    def weights():
        return wg_ref[0].astype(BF16), wu_ref[0].astype(BF16), wd_ref[0].astype(BF16)

    def sub_block(s, wg, wu, wd):
        rows = slice(s * MOE_SUB, (s + 1) * MOE_SUB)
        h = x_ref[rows, :]
        act = _silu(_dot(h, wg)) * _dot(h, wu)
        y_o[rows, :] += _dot(act.astype(BF16), wd)

    @pl.when(nv > (nsub - 1) * MOE_SUB)
    def _():
        w = weights()
        for s in range(nsub):
            sub_block(s, *w)

    @pl.when((nv > 0) & (nv <= (nsub - 1) * MOE_SUB))
    def _():
        w = weights()
        for s in range(nsub - 1):
            @pl.when(nv > s * MOE_SUB)
            def _():
                sub_block(s, *w)


def _moe_experts(xs, blk_e, blk_nv, w_gu, w_down, tm, tf):
    n, d = xs.shape
    ff = w_down.shape[1]
    nf = ff // tf
    nblk = n // tm

    def f_eff(i, f, nv_ref):
        return jnp.where(nv_ref[i] > 0, f, nf - 1)

    grid_spec = pltpu.PrefetchScalarGridSpec(
        num_scalar_prefetch=2,
        grid=(nblk, nf),
        in_specs=[pl.BlockSpec((tm, d), lambda i, f, e, nv: (i, 0)),
                  pl.BlockSpec((1, d, tf), lambda i, f, e, nv: (e[i], 0, f_eff(i, f, nv))),
                  pl.BlockSpec((1, d, tf), lambda i, f, e, nv: (e[i], 0, nf + f_eff(i, f, nv))),
                  pl.BlockSpec((1, tf, d), lambda i, f, e, nv: (e[i], f_eff(i, f, nv), 0))],
        out_specs=pl.BlockSpec((tm, d), lambda i, f, e, nv: (i, 0)),
    )
    return pl.pallas_call(
        functools.partial(_moe_kernel, nf, tm),
        grid_spec=grid_spec,
        out_shape=jax.ShapeDtypeStruct((n, d), F32),
        compiler_params=_cparams(("arbitrary", "arbitrary")),
        name="moe_experts",
    )(blk_e, blk_nv, xs, w_gu, w_gu, w_down)


def _combine_kernel(tT, nsteps, pos_cur, pos_nxt, ys_hbm, gt_ref, x_ref, mod_ref, g_ref, o_ref,
                    buf, sem):
    j = pl.program_id(0)
    slot = j % 2
    d = x_ref.shape[-1]

    def issue(pos_ref, s):
        for r in range(tT):
            for k in range(TOP_K):
                pltpu.make_async_copy(ys_hbm.at[pl.ds(pos_ref[0, k, r], 1)],
                                      buf.at[s, k, pl.ds(r, 1)], sem.at[s]).start(priority=k)

    @pl.when(j == 0)
    def _():
        issue(pos_cur, 0)

    @pl.when(j + 1 < nsteps)
    def _():
        issue(pos_nxt, 1 - slot)

    for k in range(TOP_K):
        pltpu.make_async_copy(ys_hbm.at[pl.ds(0, tT)], buf.at[slot, k], sem.at[slot]).wait()
    gates = gt_ref[...]
    y = buf[slot, 0] * gates[:, 0:1] + buf[slot, 1] * gates[:, 1:2]
    gt_f = mod_ref[0, :, 5 * d:6 * d]
    o_ref[...] = x_ref[...] + gt_f * _rms(y, g_ref[...])


def _combine(ys, pos, gates, x, mod, g, rows_per_batch, tT):
    n, d = x.shape
    nsteps = n // tT
    per = rows_per_batch // tT
    pos3 = pos.reshape(TOP_K, nsteps, tT).transpose(1, 0, 2)
    row_spec = pl.BlockSpec((tT, d), lambda j: (j, 0))
    smem = functools.partial(pl.BlockSpec, (1, TOP_K, tT), memory_space=pltpu.SMEM)
    return pl.pallas_call(
        functools.partial(_combine_kernel, tT, nsteps),
        grid=(nsteps,),
        in_specs=[smem(lambda j: (j, 0, 0)),
                  smem(lambda j: (jnp.minimum(j + 1, nsteps - 1), 0, 0)),
                  pl.BlockSpec(memory_space=pl.ANY),
                  pl.BlockSpec((tT, LANES), lambda j: (j, 0)),
                  row_spec, pl.BlockSpec((1, 1, 6 * d), lambda j: (j // per, 0, 0)),
                  pl.BlockSpec((1, d), lambda j: (0, 0))],
        out_specs=row_spec,
        out_shape=jax.ShapeDtypeStruct((n, d), F32),
        scratch_shapes=[pltpu.VMEM((2, TOP_K, tT, d), F32), pltpu.SemaphoreType.DMA((2,))],
        compiler_params=_cparams(("arbitrary",)),
        name="moe_combine",
    )(pos3, pos3, ys, gates, x, mod, g)


def _route(top_e, n_experts, tm):
    n = top_e.shape[1]
    nk = n * TOP_K
    i1, i2 = top_e[0], top_e[1]
    idx = jnp.arange(n_experts, dtype=jnp.int32)[:, None]
    oh = [(idx == i[None]).astype(jnp.int32) for i in (i1, i2)]
    cs = [jnp.cumsum(o, axis=1) for o in oh]
    first = cs[0][:, -1]
    counts = first + cs[1][:, -1]
    nblk_e = (counts + tm - 1) // tm
    start = jnp.cumsum(counts) - counts
    bend = jnp.cumsum(nblk_e)
    bstart = bend - nblk_e
    base = (bstart * tm)[:, None]
    pos = jnp.stack([jnp.sum(oh[0] * (base + cs[0] - oh[0]), axis=0),
                     jnp.sum(oh[1] * (base + first[:, None] + cs[1] - oh[1]), axis=0)])
    flat_e = jnp.concatenate([i1, i2])
    nblk = -(-nk // tm) + n_experts
    blk = jnp.arange(nblk, dtype=jnp.int32)
    blk_e = jnp.minimum(jnp.searchsorted(bend, blk, side="right"), n_experts - 1).astype(jnp.int32)
    blk_nv = jnp.clip(counts[blk_e] - (blk - bstart[blk_e]) * tm, 0, tm)
    blk_nv = jnp.where(blk < bend[-1], blk_nv, 0).astype(jnp.int32)
    order = jnp.argsort(flat_e, stable=True).astype(jnp.int32)
    order = jnp.concatenate([order, jnp.zeros((tm,), jnp.int32)])
    src0 = jnp.where(blk_nv > 0, start[blk_e] + (blk - bstart[blk_e]) * tm, 0)
    slot_pair = jnp.concatenate([lax.dynamic_slice_in_dim(order, src0[k], tm) for k in range(nblk)])
    slot_tok = jnp.where(slot_pair >= n, slot_pair - n, slot_pair)
    return slot_tok, pos, blk_e, blk_nv


def kernel(x, c, ctx, c_ctx, mod_w, mod_b, norm_g, rwkv_mu, rwkv_w_rkv, rwkv_w0, rwkv_w1, rwkv_w2,
           rwkv_a0, rwkv_a1, rwkv_a2, rwkv_g1, rwkv_g2, rwkv_k_k, rwkv_k_a, rwkv_r_k, rwkv_ln_w,
           rwkv_ln_b, rwkv_w_out, conv_w_in, conv_w, conv_w_out, ffn_w_gu, ffn_w_down,
           moe_router, moe_w_gu, moe_w_down):
    b, t, d = x.shape
    n_experts = moe_router.shape[-1]
    rows = 16
    cs = jnp.zeros((rows, d), F32).at[:b].set(c).at[b].set(c_ctx)
    mods = _modulation(cs, mod_w, mod_b)
    mod0 = mods[0, :b].reshape(b, 1, 6 * d)
    mod0c = mods[0, b].reshape(1, 1, 6 * d)
    mod1 = mods[1, :b].reshape(b, 1, 6 * d)

    def pad_lora(w):
        zr = jnp.zeros_like(w[0])
        return jnp.stack([jnp.concatenate([w[0], zr], 0), jnp.concatenate([zr, w[1]], 0)]).astype(BF16)

    lg = rwkv_g1.shape[-1]
    lgp = -(-lg // LANES) * LANES
    p = {
        "ng": norm_g[0, 0].reshape(1, d),
        "mu": rwkv_mu[0],
        "wr": rwkv_w_rkv[0, 0].astype(BF16), "wk": rwkv_w_rkv[0, 1].astype(BF16),
        "wv": rwkv_w_rkv[0, 2].astype(BF16),
        "w1": jnp.concatenate([rwkv_w1[0, 0], rwkv_w1[0, 1]], axis=1).astype(BF16),
        "w2": pad_lora(rwkv_w2[0]), "w0": rwkv_w0[0],
        "a1": jnp.concatenate([rwkv_a1[0, 0], rwkv_a1[0, 1]], axis=1).astype(BF16),
        "a2": pad_lora(rwkv_a2[0]), "a0": rwkv_a0[0],
        "k_k": rwkv_k_k[0].reshape(1, d), "k_a": rwkv_k_a[0].reshape(1, d),
        "g1": jnp.pad(rwkv_g1[0], ((0, 0), (0, lgp - lg))).astype(BF16),
        "g2": jnp.pad(rwkv_g2[0], ((0, lgp - lg), (0, 0))).astype(BF16),
        "r_k": rwkv_r_k[0].reshape(1, d),
    }

    tok_c = _tokenwise(ctx, mod0c, "seq", False, p, ctx.shape[1])
    s_ctx = _scan(tok_c, p["k_a"], None, want_y=False, want_state=True)[0]
    tt = min(ROW_TILE, t)
    tm = min(FFN_ROWS, t)
    tf = min(FFN_COLS, ffn_w_down.shape[1])
    tok_l = _tokenwise(x, mod0, "grid", True, p, tt)
    yf, yb = _scan(tok_l[:8], p["k_a"], s_ctx, want_y=True, want_state=False)
    x1, h2 = _readout(yf, yb, tok_l[9], tok_l[8], x, mod0,
                      rwkv_ln_w[0].reshape(1, d), rwkv_ln_b[0].reshape(1, d),
                      rwkv_w_out[0].astype(BF16), norm_g[0, 1].reshape(1, d),
                      norm_g[0, 2].reshape(1, d), min(2 * ROW_TILE, t))

    x2, h3 = _ffn(h2.reshape(b * t, d), ffn_w_gu[0].astype(BF16), ffn_w_down[0].astype(BF16),
                  x1.reshape(b * t, d), mod0, norm_g[0, 3].reshape(1, d),
                  norm_g[1, 0].reshape(1, d), mod1, t, tm, tf)

    w_router = jnp.stack(_split_bf16(jnp.pad(moe_router[0], ((0, 0), (0, LANES - n_experts)))))
    x3, h4, top_e, gates = _conv_layer(h3.reshape(b, t, d), x2.reshape(b, t, d), mod1,
                                       conv_w_in[0].astype(BF16), conv_w[0], conv_w_out[0].astype(BF16),
                                       norm_g[1, 1].reshape(1, d), norm_g[1, 2].reshape(1, d),
                                       w_router, n_experts, tt)

    tme = min(MOE_ROWS, b * t * TOP_K)
    slot_tok, pos, blk_e, blk_nv = _route(top_e.transpose(1, 0, 2).reshape(TOP_K, b * t),
                                          n_experts, tme)
    tg = min(GATHER_ROWS, tme)
    live = (blk_nv[:, None] > jnp.arange(0, tme, tg, dtype=jnp.int32)[None, :]).astype(jnp.int32)
    xs = _gather_rows(h4.reshape(-1, LANES), d, slot_tok, live.reshape(-1), tg)
    ys = _moe_experts(xs, blk_e, blk_nv, moe_w_gu[0], moe_w_down[0], tme,
                      min(MOE_COLS, moe_w_down.shape[2]))
    out = _combine(ys, pos, gates.reshape(b * t, LANES), x3.reshape(b * t, d), mod1,
                   norm_g[1, 3].reshape(1, d), t, tt)
    return out.reshape(b, t, d)
```

```python
import functools
import math

import jax
import jax.numpy as jnp
from jax import lax
from jax.experimental import pallas as pl
from jax.experimental.pallas import tpu as pltpu

F32 = jnp.float32
BF16 = jnp.bfloat16

HEAD = 64
LANES = 128
GRID_W = 64
CHUNK = 64
SUB = 16
SCAN_PAIRS = 8
SCAN_CHUNKS = 4
NORM_EPS = 1e-6
GN_EPS = 64e-5
DECAY_SCALE = math.exp(-0.5)
TOP_K = 2
VMEM_LIMIT = 56 * 1024 * 1024


ROW_TILE = 256
FFN_ROWS = 1024
FFN_COLS = 512
MOE_ROWS = 2560
MOE_COLS = 512


def _cparams(sem):
    return pltpu.CompilerParams(dimension_semantics=sem, vmem_limit_bytes=VMEM_LIMIT)


def _dot(a, b):
    return jnp.dot(a, b, preferred_element_type=F32)


def _dot_nt(a, b):
    return lax.dot_general(a, b, (((1,), (1,)), ((), ())), preferred_element_type=F32)


def _bdot(a, b):
    return _dot(a.astype(BF16), b.astype(BF16))


def _iota(shape, axis):
    return lax.broadcasted_iota(jnp.int32, shape, axis)


def _rms(x, g):
    return x * lax.rsqrt(jnp.mean(x * x, axis=-1, keepdims=True) + NORM_EPS) * g


def _silu(x):
    h = 0.5 * x
    return h * jnp.tanh(h) + h


def _split_bf16(x):
    hi = x.astype(BF16)
    lo = (x - hi.astype(F32)).astype(BF16)
    return hi, lo


def _head_sum(x):
    ones = (_iota((LANES, LANES), 0) // HEAD == _iota((LANES, LANES), 1) // HEAD).astype(BF16)
    xb = x.astype(BF16)
    outs = []
    for p in range(x.shape[-1] // LANES):
        outs.append(_dot(xb[:, p * LANES:(p + 1) * LANES], ones))
    return jnp.concatenate(outs, axis=1)


def _sigmoid(x):
    return 0.5 * jnp.tanh(0.5 * x) + 0.5


def _mod_kernel(c_ref, w_ref, b_ref, o_ref):
    s = _silu(c_ref[...])
    o_ref[0] = jnp.dot(s, w_ref[0], preferred_element_type=F32,
                       precision=lax.Precision.HIGHEST) + b_ref[0]


def _modulation(cs, mod_w, mod_b):
    depth, d, n = mod_w.shape
    rows = cs.shape[0]
    tn = 1536
    return pl.pallas_call(
        _mod_kernel,
        grid=(depth, n // tn),
        in_specs=[pl.BlockSpec((rows, d), lambda i, j: (0, 0)),
                  pl.BlockSpec((1, d, tn), lambda i, j: (i, 0, j)),
                  pl.BlockSpec((1, 1, tn), lambda i, j: (i, 0, j))],
        out_specs=pl.BlockSpec((1, rows, tn), lambda i, j: (i, 0, j)),
        out_shape=jax.ShapeDtypeStruct((depth, rows, n), F32),
        compiler_params=_cparams(("arbitrary", "arbitrary")),
        name="modulation",
    )(cs, mod_w, mod_b.reshape(depth, 1, n))


def _tok_kernel(mode, readout, tT, nT, *refs):
    refs = list(refs)
    x_ref = refs.pop(0)
    if mode == "grid":
        xp_ref = refs.pop(0)
        xn_ref = refs.pop(0)
    (mod_ref, ng_ref, mu_ref, wr_ref, wk_ref, wv_ref, w1_ref, w2_ref, w0_ref,
     a1_ref, a2_ref, a0_ref, kkp_ref, ka_ref) = refs[:14]
    refs = refs[14:]
    if readout:
        g1_ref, g2_ref, rk_ref = refs[:3]
        refs = refs[3:]
    r_o, k_o, v_o, kk_o, ic0_o, ic1_o, lw0_o, lw1_o = refs[:8]
    refs = refs[8:]

    d = x_ref.shape[-1]
    sh = mod_ref[0, :, 0:d]
    sc = mod_ref[0, :, d:2 * d]
    g = ng_ref[...]

    def norm_mod(xx):
        return _rms(xx, g) * (1.0 + sc) + sh

    h = norm_mod(x_ref[0])
    if mode == "grid":
        i = pl.program_id(1)
        q = d // 4
        hp = jnp.where(i > 0, norm_mod(xp_ref[0]), 0.0)
        hn = jnp.where(i < nT - 1, norm_mod(xn_ref[0]), 0.0)
        col = _iota((tT, q), 0) % GRID_W
        left = jnp.where(col == 0, 0.0, pltpu.roll(h[:, 0:q], 1, 0))
        right = jnp.where(col == GRID_W - 1, 0.0, pltpu.roll(h[:, q:2 * q], tT - 1, 0))
        up = jnp.concatenate([hp[:, 2 * q:3 * q], h[:tT - GRID_W, 2 * q:3 * q]], axis=0)
        down = jnp.concatenate([h[GRID_W:, 3 * q:], hn[:, 3 * q:]], axis=0)
        hs = jnp.concatenate([left, right, up, down], axis=1)
    else:
        half = d // 2
        row = _iota((tT, half), 0)
        prev = jnp.where(row == 0, 0.0, pltpu.roll(h[:, :half], 1, 0))
        nxt = jnp.where(row == tT - 1, 0.0, pltpu.roll(h[:, half:], tT - 1, 0))
        hs = jnp.concatenate([prev, nxt], axis=1)

    hb = h.astype(BF16)
    dxb = (hs - h).astype(BF16)

    def mix(n):
        return hb + dxb * mu_ref[n:n + 1, :].astype(BF16)

    r = _dot(mix(0), wr_ref[...])
    k = _dot(mix(2), wk_ref[...])
    v = _dot(mix(3), wv_ref[...])
    w1o = jnp.tanh(_dot(mix(1), w1_ref[...])).astype(BF16)
    a1o = _dot(mix(4), a1_ref[...]).astype(BF16)
    ics = []
    for z, (lw_o, ic_o) in enumerate(((lw0_o, ic0_o), (lw1_o, ic1_o))):
        w_pre = _dot(w1o, w2_ref[z]) + w0_ref[z:z + 1, :]
        lw_o[0] = (-0.5 * DECAY_SCALE) * jnp.tanh(0.5 * w_pre) - 0.5 * DECAY_SCALE
        ic = _sigmoid(_dot(a1o, a2_ref[z]) + a0_ref[z:z + 1, :])
        ic_o[0] = ic.astype(BF16)
        ics.append(ic)
    kk = k * kkp_ref[...]
    kk = kk * lax.rsqrt(jnp.maximum(_head_sum(kk * kk), 1e-24))
    r_o[0] = r.astype(BF16)
    k_o[0] = k.astype(BF16)
    v_o[0] = v.astype(BF16)
    kk_o[0] = kk.astype(BF16)
    if readout:
        gate_o, bon_o = refs
        gate = _dot(_sigmoid(_dot(mix(5), g1_ref[...])).astype(BF16), g2_ref[...])
        gate_o[0] = gate.astype(BF16)
        k_avg = k * (1.0 + (0.5 * (ics[0] + ics[1]) - 1.0) * ka_ref[...])
        bon_o[0] = (_head_sum(r * k_avg * rk_ref[...]) * v).astype(BF16)


def _tokenwise(x, mod, mode, readout, p, tT):
    b, t, d = x.shape
    nT = t // tT
    row_spec = pl.BlockSpec((1, tT, d), lambda bi, i: (bi, i, 0))
    in_specs = [row_spec]
    args = [x]
    if mode == "grid":
        hb = tT // GRID_W
        in_specs += [
            pl.BlockSpec((1, GRID_W, d), lambda bi, i: (bi, jnp.maximum(i * hb - 1, 0), 0)),
            pl.BlockSpec((1, GRID_W, d), lambda bi, i: (bi, jnp.minimum((i + 1) * hb, t // GRID_W - 1), 0)),
        ]
        args += [x, x]
    if mod.shape[0] == 1:
        in_specs.append(pl.BlockSpec((1, 1, mod.shape[-1]), lambda bi, i: (0, 0, 0)))
    else:
        in_specs.append(pl.BlockSpec((1, 1, mod.shape[-1]), lambda bi, i: (bi, 0, 0)))
    args.append(mod)
    consts = [p["ng"], p["mu"], p["wr"], p["wk"], p["wv"], p["w1"], p["w2"], p["w0"],
              p["a1"], p["a2"], p["a0"], p["k_k"], p["k_a"]]
    if readout:
        consts += [p["g1"], p["g2"], p["r_k"]]
    for a in consts:
        in_specs.append(pl.BlockSpec(a.shape, lambda bi, i, _n=a.ndim: (0,) * _n))
        args.append(a)
    dtypes = [BF16] * 6 + [F32] * 2 + ([BF16] * 2 if readout else [])
    return pl.pallas_call(
        functools.partial(_tok_kernel, mode, readout, tT, nT),
        grid=(b, nT),
        in_specs=in_specs,
        out_specs=[row_spec] * len(dtypes),
        out_shape=[jax.ShapeDtypeStruct((b, t, d), dt) for dt in dtypes],
        compiler_params=_cparams(("arbitrary", "arbitrary")),
        name="rwkv_tokenwise_" + mode,
    )(*args)


def _scan_prep(r, kd, v, lw, kk, ic, reverse):
    c = r.shape[0]
    ti = _iota((c, c), 0)
    si = _iota((c, c), 1)
    tri = ((si >= ti) if reverse else (si <= ti)).astype(BF16)
    lw_hi, lw_lo = _split_bf16(lw)
    cum = _dot(tri, lw_hi) + _dot(tri, lw_lo)
    pc_log = cum[0:1] if reverse else cum[c - 1:c]
    p_inv = jnp.exp(-cum)
    p_rest = jnp.exp(pc_log - cum)
    b = kk * ic
    return {"rt": r * jnp.exp(cum), "at": -(kk * jnp.exp(cum - lw)),
            "bt": b * p_inv, "kt": kd * p_inv, "bp": b * p_rest, "kp": kd * p_rest,
            "v": v, "pc": jnp.exp(pc_log)}


def _scan_step(prep, states):
    c = prep[0]["v"].shape[0]
    n = 2 * c
    pp = len(states[0])
    chains = [(dr, p) for dr in range(2) for p in range(pp)]
    first = _iota((c, LANES), 1) < HEAD
    row = _iota((n, n), 0)
    col = _iota((n, n), 1)
    strict = (col < row, col > row)
    same = (row // SUB) == (col // SUB)

    def pair(name, dr, p):
        return prep[dr][name][:, p * LANES:(p + 1) * LANES]

    def stack(x):
        return jnp.concatenate([jnp.where(first, x, 0.0), jnp.where(first, 0.0, x)], axis=0)

    def ms(name, dr, p):
        return stack(pair(name, dr, p))

    def each(fn, *lists):
        return [fn(*xs) for xs in zip(*lists)]

    lhs = [jnp.concatenate([pair("at", dr, p), pair("rt", dr, p)], axis=0).astype(BF16)
           for dr, p in chains]
    rhs = [jnp.concatenate([ms("bt", dr, p), ms("kt", dr, p)], axis=0).astype(BF16) for dr, p in chains]
    at_ms = [ms("at", dr, p).astype(BF16) for dr, p in chains]
    v_ms = [ms("v", dr, p).astype(BF16) for dr, p in chains]
    gmat = each(_dot_nt, lhs, rhs)
    own = (row // c) == (col // c)
    tri = [own & m for m in strict]
    tri_d = [m & same for m in tri]
    tri_o = [m & jnp.logical_not(same) for m in tri]

    def twice(x):
        return jnp.concatenate([x, x], axis=0)

    nd = [jnp.where(tri_d[dr], twice(g[:c, :n]), 0.0) for (dr, _), g in zip(chains, gmat)]
    no = [jnp.where(tri_o[dr], twice(g[:c, :n]), 0.0) for (dr, _), g in zip(chains, gmat)]
    a_ak = [jnp.where(tri[dr], twice(g[:c, n:]), 0.0).astype(BF16) for (dr, _), g in zip(chains, gmat)]
    t_pl = _iota((c, 2 * n), 0)
    s_pl = _iota((c, 2 * n), 1) % c
    incl_pl = (s_pl <= t_pl, s_pl >= t_pl)
    a_r = [jnp.where(incl_pl[dr], g[c:], 0.0).astype(BF16)
           for (dr, _), g in zip(chains, gmat)]

    pw = nd
    tp = nd
    span = 2
    while span < SUB:
        pw = each(_bdot, pw, pw)
        tp = each(lambda t, q: t + q + _bdot(t, q), tp, pw)
        span *= 2
    m = each(lambda t, o: o + _bdot(t, o), tp, no)
    qp = m
    pw = m
    span = 2
    while span < c // SUB:
        pw = each(_bdot, pw, pw)
        qp = each(lambda t, q: t + q + _bdot(t, q), qp, pw)
        span *= 2
    tp = each(lambda t, q: t + q + _bdot(q, t), tp, qp)

    bk_t = [jnp.concatenate([ms("bp", dr, p), ms("kp", dr, p)], axis=0).T.astype(BF16)
            for dr, p in chains]
    diag = row == col
    pc_col = [jnp.sum(jnp.where(diag, prep[dr]["pc"][:, p * LANES:(p + 1) * LANES], 0.0),
                      axis=1, keepdims=True) for dr, p in chains]

    s_t = [states[dr][p].astype(BF16) for dr, p in chains]
    rhs_z = each(lambda l, a, st, v: _dot(jnp.concatenate([l, a], axis=1),
                                          jnp.concatenate([st, v], axis=0)), at_ms, a_ak, s_t, v_ms)
    z = each(lambda t, x: x + _bdot(t, x), tp, rhs_z)
    zv = each(lambda zz, v: jnp.concatenate([zz.astype(BF16), v], axis=0), z, v_ms)
    y_pl = each(lambda l, st, ar, w: _dot(l[c:], st) + _dot(ar, w), lhs, s_t, a_r, zv)
    y = [jnp.concatenate([yy for (d2, _), yy in zip(chains, y_pl) if d2 == dr], axis=1)
         for dr in range(2)]
    new_states = [[None] * pp for _ in range(2)]
    for (dr, p), bt_, w, pc in zip(chains, bk_t, zv, pc_col):
        new_states[dr][p] = states[dr][p] * pc + _dot(bt_, w)
    return y, new_states


def _scan_kernel(pp, nc, has_s0, want_y, want_state, *refs):
    refs = list(refs)
    ka_ref = refs[12]
    dirs = (refs[:6], refs[6:12])
    refs = refs[13:]
    if has_s0:
        s0_ref = refs.pop(0)
    if want_y:
        y_refs = (refs.pop(0), refs.pop(0))
    if want_state:
        st_ref = refs.pop(0)
    s_scr = refs.pop(0)
    c = pl.program_id(2)

    @pl.when(c == 0)
    def _():
        if has_s0:
            s_scr[...] = s0_ref[0]
        else:
            s_scr[...] = jnp.zeros_like(s_scr)

    per_step = dirs[0][0].shape[1] // CHUNK
    states = [[s_scr[dr, p] for p in range(pp)] for dr in range(2)]
    for j in range(per_step):
        prep = []
        rows = []
        for dr, (r_ref, k_ref, v_ref, kk_ref, ic_ref, lw_ref) in enumerate(dirs):
            q = j if dr == 0 else per_step - 1 - j
            sl = slice(q * CHUNK, (q + 1) * CHUNK)
            rows.append(sl)
            ic = ic_ref[0, sl, :].astype(F32)
            kd = k_ref[0, sl, :].astype(F32) * (1.0 + (ic - 1.0) * ka_ref[...])
            prep.append(_scan_prep(r_ref[0, sl, :].astype(F32), kd, v_ref[0, sl, :].astype(F32),
                                   lw_ref[0, sl, :], kk_ref[0, sl, :].astype(F32), ic,
                                   reverse=(dr == 1)))
        y, states = _scan_step(prep, states)
        if want_y:
            for dr in range(2):
                y_refs[dr][0, rows[dr], :] = y[dr].astype(BF16)
    for dr in range(2):
        for p in range(pp):
            s_scr[dr, p] = states[dr][p]

    if want_state:
        @pl.when(c == nc - 1)
        def _():
            st_ref[0] = s_scr[...]


def _scan(tok, ka, s0, want_y, want_state):
    r, k, v, kk, ic0, ic1, lw0, lw1 = tok
    b, t, d = r.shape
    rows = min(CHUNK * SCAN_CHUNKS, t)
    nc = t // rows
    npair = d // LANES
    pp = min(SCAN_PAIRS, npair)
    w = LANES * pp
    f_spec = pl.BlockSpec((1, rows, w), lambda bi, hi, ci: (bi, ci, hi))
    b_spec = pl.BlockSpec((1, rows, w), lambda bi, hi, ci: (bi, nc - 1 - ci, hi))
    st_spec = pl.BlockSpec((1, 2, pp, LANES, LANES), lambda bi, hi, ci: (bi, 0, hi, 0, 0))
    in_specs = [f_spec] * 6 + [b_spec] * 6 + [pl.BlockSpec((1, w), lambda bi, hi, ci: (0, hi))]
    args = [r, k, v, kk, ic0, lw0, r, k, v, kk, ic1, lw1, ka]
    if s0 is not None:
        in_specs.append(st_spec)
        args.append(s0)
    out_specs, out_shape = [], []
    if want_y:
        out_specs += [f_spec, b_spec]
        out_shape += [jax.ShapeDtypeStruct((b, t, d), BF16)] * 2
    if want_state:
        out_specs.append(st_spec)
        out_shape.append(jax.ShapeDtypeStruct((b, 2, npair, LANES, LANES), F32))
    return pl.pallas_call(
        functools.partial(_scan_kernel, pp, nc, s0 is not None, want_y, want_state),
        grid=(b, npair // pp, nc),
        in_specs=in_specs,
        out_specs=out_specs,
        out_shape=out_shape,
        scratch_shapes=[pltpu.VMEM((2, pp, LANES, LANES), F32)],
        compiler_params=_cparams(("arbitrary", "arbitrary", "arbitrary")),
        name="rwkv_scan_ctx" if s0 is None else "rwkv_scan_latent",
    )(*args)


def _readout_kernel(yf_ref, yb_ref, bon_ref, gate_ref, x_ref, mod_ref, lnw_ref, lnb_ref,
                    wo_ref, g1_ref, g2_ref, x_o, h_o):
    d = x_ref.shape[-1]
    y = yf_ref[0].astype(F32) + yb_ref[0].astype(F32)
    mean = _head_sum(y) * (1.0 / HEAD)
    yc = y - mean
    var = _head_sum(yc * yc) * (1.0 / HEAD)
    o = yc * lax.rsqrt(var + GN_EPS) * lnw_ref[...] + lnb_ref[...] + bon_ref[0]
    att = _dot((o * gate_ref[0]).astype(BF16), wo_ref[...])
    gt_a = mod_ref[0, :, 2 * d:3 * d]
    sh_f = mod_ref[0, :, 3 * d:4 * d]
    sc_f = mod_ref[0, :, 4 * d:5 * d]
    x1 = x_ref[0] + gt_a * _rms(att, g1_ref[...])
    x_o[0] = x1
    h_o[0] = (_rms(x1, g2_ref[...]) * (1.0 + sc_f) + sh_f).astype(BF16)


def _readout(yf, yb, bon, gate, x, mod, lnw, lnb, wo, g1, g2, tT):
    b, t, d = x.shape
    row_spec = pl.BlockSpec((1, tT, d), lambda bi, i: (bi, i, 0))
    vec = pl.BlockSpec((1, d), lambda bi, i: (0, 0))
    return pl.pallas_call(
        _readout_kernel,
        grid=(b, t // tT),
        in_specs=[row_spec] * 5 + [pl.BlockSpec((1, 1, 6 * d), lambda bi, i: (bi, 0, 0)),
                                   vec, vec, pl.BlockSpec((d, d), lambda bi, i: (0, 0)), vec, vec],
        out_specs=[row_spec, row_spec],
        out_shape=[jax.ShapeDtypeStruct((b, t, d), F32), jax.ShapeDtypeStruct((b, t, d), BF16)],
        compiler_params=_cparams(("arbitrary", "arbitrary")),
        name="rwkv_readout",
    )(yf, yb, bon, gate, x, mod, lnw, lnb, wo, g1, g2)


def _ffn_kernel(nf, h_ref, wg_ref, wu_ref, wd_ref, x_ref, mod_ref, g3_ref, gn_ref, mod2_ref,
                x_o, h_o, acc):
    f = pl.program_id(1)
    d = x_ref.shape[-1]

    @pl.when(f == 0)
    def _():
        acc[...] = jnp.zeros_like(acc)

    h = h_ref[...]
    act = _silu(_dot(h, wg_ref[...])) * _dot(h, wu_ref[...])
    acc[...] += _dot(act.astype(BF16), wd_ref[...])

    @pl.when(f == nf - 1)
    def _():
        gt_f = mod_ref[0, :, 5 * d:6 * d]
        x2 = x_ref[...] + gt_f * _rms(acc[...], g3_ref[...])
        x_o[...] = x2
        sh = mod2_ref[0, :, 0:d]
        sc = mod2_ref[0, :, d:2 * d]
        h_o[...] = (_rms(x2, gn_ref[...]) * (1.0 + sc) + sh).astype(BF16)


def _ffn(h, w_gu, w_down, x, mod, g3, gn, mod2, rows_per_batch, tm, tf):
    n, d = x.shape
    ff = w_down.shape[0]
    nf = ff // tf
    per = rows_per_batch // tm
    row = pl.BlockSpec((tm, d), lambda i, f: (i, 0))
    vec = pl.BlockSpec((1, d), lambda i, f: (0, 0))
    modspec = pl.BlockSpec((1, 1, 6 * d), lambda i, f: (i // per, 0, 0))
    return pl.pallas_call(
        functools.partial(_ffn_kernel, nf),
        grid=(n // tm, nf),
        in_specs=[row,
                  pl.BlockSpec((d, tf), lambda i, f: (0, f)),
                  pl.BlockSpec((d, tf), lambda i, f: (0, nf + f)),
                  pl.BlockSpec((tf, d), lambda i, f: (f, 0)),
                  row, modspec, vec, vec, modspec],
        out_specs=[row, row],
        out_shape=[jax.ShapeDtypeStruct((n, d), F32), jax.ShapeDtypeStruct((n, d), BF16)],
        scratch_shapes=[pltpu.VMEM((tm, d), F32)],
        compiler_params=_cparams(("arbitrary", "arbitrary")),
        name="dense_swiglu",
    )(h, w_gu, w_gu, w_down, x, mod, g3, gn, mod2)


HALO = 16


def _conv_kernel(tT, nT, n_experts, h_ref, hp_ref, hn_ref, x_ref, mod_ref, win_ref, cw_ref, wo_ref,
                 g1_ref, g2_ref, wr_ref, x_o, h_o, te_o, gt_o):
    i = pl.program_id(1)
    d = x_ref.shape[-1]
    n = tT + 2 * HALO
    h_ext = jnp.concatenate([hp_ref[0], h_ref[0], hn_ref[0]], axis=0)
    proj = _dot(h_ext, win_ref[...])
    z = proj[:, d:2 * d] * proj[:, 2 * d:]
    row = _iota((n, d), 0)
    dead = ((row < HALO) & (i == 0)) | ((row >= HALO + tT) & (i == nT - 1))
    z = jnp.where(dead, 0.0, z)
    conv = (pltpu.roll(z, 1, 0) * cw_ref[0:1, :] + z * cw_ref[1:2, :]
            + pltpu.roll(z, n - 1, 0) * cw_ref[2:3, :])
    gated = (proj[HALO:HALO + tT, 0:d] * conv[HALO:HALO + tT]).astype(BF16)
    y = _dot(gated, wo_ref[...])
    gt_a = mod_ref[0, :, 2 * d:3 * d]
    sh_f = mod_ref[0, :, 3 * d:4 * d]
    sc_f = mod_ref[0, :, 4 * d:5 * d]
    x3 = x_ref[0] + gt_a * _rms(y, g1_ref[...])
    x_o[0] = x3
    h4 = _rms(x3, g2_ref[...]) * (1.0 + sc_f) + sh_f
    s_rows = d // LANES
    for c in range(s_rows):
        h_o[0, pl.ds(c, tT, stride=s_rows), :] = h4[:, c * LANES:(c + 1) * LANES]
    h_hi, h_lo = _split_bf16(h4)
    lg = _dot(h_hi, wr_ref[0]) + _dot(h_lo, wr_ref[0]) + _dot(h_hi, wr_ref[1])
    lg_t = lg.T
    lg8 = lg_t[:n_experts]
    eidx = _iota(lg8.shape, 0)
    m1 = jnp.max(lg8, axis=0, keepdims=True)
    i1 = jnp.min(jnp.where(lg8 == m1, eidx, n_experts), axis=0, keepdims=True)
    rest = jnp.where(eidx == i1, -jnp.inf, lg8)
    m2 = jnp.max(rest, axis=0, keepdims=True)
    i2 = jnp.min(jnp.where(rest == m2, eidx, n_experts), axis=0, keepdims=True)
    tw = jnp.exp(m2 - m1)
    te_o[0] = jnp.concatenate([i1, i2], axis=0)
    gates = jnp.concatenate([1.0 / (1.0 + tw), tw / (1.0 + tw),
                             jnp.zeros((LANES - TOP_K, tT), F32)], axis=0)
    gt_o[0] = gates.T


def _conv_layer(h, x, mod, w_in, conv_w, w_out, g1, g2, w_router, n_experts, tT):
    b, t, d = x.shape
    nT = t // tT
    hb = tT // HALO
    row_spec = pl.BlockSpec((1, tT, d), lambda bi, i: (bi, i, 0))
    vec = pl.BlockSpec((1, d), lambda bi, i: (0, 0))
    full = lambda a: pl.BlockSpec(a.shape, lambda bi, i, _n=a.ndim: (0,) * _n)
    return pl.pallas_call(
        functools.partial(_conv_kernel, tT, nT, n_experts),
        grid=(b, nT),
        in_specs=[row_spec,
                  pl.BlockSpec((1, HALO, d), lambda bi, i: (bi, jnp.maximum(i * hb - 1, 0), 0)),
                  pl.BlockSpec((1, HALO, d), lambda bi, i: (bi, jnp.minimum((i + 1) * hb, t // HALO - 1), 0)),
                  row_spec, pl.BlockSpec((1, 1, 6 * d), lambda bi, i: (bi, 0, 0)),
                  full(w_in), full(conv_w), full(w_out), vec, vec, full(w_router)],
        out_specs=[row_spec, pl.BlockSpec((1, tT * (d // LANES), LANES), lambda bi, i: (bi, i, 0)),
                   pl.BlockSpec((1, TOP_K, tT), lambda bi, i: (bi, 0, i)),
                   pl.BlockSpec((1, tT, LANES), lambda bi, i: (bi, i, 0))],
        out_shape=[jax.ShapeDtypeStruct((b, t, d), F32),
                   jax.ShapeDtypeStruct((b, t * (d // LANES), LANES), F32),
                   jax.ShapeDtypeStruct((b, TOP_K, t), jnp.int32),
                   jax.ShapeDtypeStruct((b, t, LANES), F32)],
        compiler_params=_cparams(("arbitrary", "arbitrary")),
        name="short_conv",
    )(h, h, h, x, mod, w_in, conv_w, w_out, g1, g2, w_router)


GATHER_ROWS = 512


def _gather_kernel(tg, nsteps, live_ref, tok_cur, tok_nxt, h_hbm, o_ref, buf, sem):
    j = pl.program_id(0)
    slot = j % 2
    s_rows = buf.shape[1] // tg

    def issue(tok_ref, s):
        for r in range(tg):
            src = pl.multiple_of(tok_ref[0, 0, r] * s_rows, s_rows)
            pltpu.make_async_copy(h_hbm.at[pl.ds(src, s_rows)],
                                  buf.at[s, pl.ds(r * s_rows, s_rows)], sem.at[s]).start(priority=r % 2)

    @pl.when((j == 0) & (live_ref[0] > 0))
    def _():
        issue(tok_cur, 0)

    @pl.when(j + 1 < nsteps)
    def _():
        @pl.when(live_ref[j + 1] > 0)
        def _():
            issue(tok_nxt, 1 - slot)

    @pl.when(live_ref[j] > 0)
    def _():
        pltpu.make_async_copy(h_hbm.at[pl.ds(0, tg * s_rows)], buf.at[slot], sem.at[slot]).wait()
        for c in range(s_rows):
            o_ref[:, c * LANES:(c + 1) * LANES] = buf[slot, pl.ds(c, tg, stride=s_rows), :].astype(BF16)

    @pl.when(live_ref[j] == 0)
    def _():
        o_ref[...] = jnp.zeros(o_ref.shape, BF16)


def _gather_rows(h, d, slot_tok, live, tg):
    s_rows = d // LANES
    nsteps = slot_tok.shape[0] // tg
    tok3 = slot_tok.reshape(nsteps, 1, tg)
    grid_spec = pltpu.PrefetchScalarGridSpec(
        num_scalar_prefetch=1,
        grid=(nsteps,),
        in_specs=[pl.BlockSpec((1, 1, tg), lambda j, lv: (j, 0, 0), memory_space=pltpu.SMEM),
                  pl.BlockSpec((1, 1, tg), lambda j, lv: (jnp.minimum(j + 1, nsteps - 1), 0, 0),
                               memory_space=pltpu.SMEM),
                  pl.BlockSpec(memory_space=pl.ANY)],
        out_specs=pl.BlockSpec((tg, d), lambda j, lv: (j, 0)),
        scratch_shapes=[pltpu.VMEM((2, tg * s_rows, LANES), F32), pltpu.SemaphoreType.DMA((2,))],
    )
    return pl.pallas_call(
        functools.partial(_gather_kernel, tg, nsteps),
        grid_spec=grid_spec,
        out_shape=jax.ShapeDtypeStruct((nsteps * tg, d), BF16),
        compiler_params=_cparams(("arbitrary",)),
        name="moe_dispatch",
    )(live, tok3, tok3, h)


MOE_SUB = 512


def _moe_kernel(nf, tm, e_ref, nv_ref, x_ref, wg_ref, wu_ref, wd_ref, y_o):
    i = pl.program_id(0)
    f = pl.program_id(1)
    nv = nv_ref[i]

    nsub = tm // MOE_SUB

    @pl.when(f == 0)
    def _():
        y_o[...] = jnp.zeros(y_o.shape, F32)

    def weights():
        return wg_ref[0].astype(BF16), wu_ref[0].astype(BF16), wd_ref[0].astype(BF16)

    def sub_block(s, wg, wu, wd):
        rows = slice(s * MOE_SUB, (s + 1) * MOE_SUB)
        h = x_ref[rows, :]
        act = _silu(_dot(h, wg)) * _dot(h, wu)
        y_o[rows, :] += _dot(act.astype(BF16), wd)

    for live in range(1, nsub + 1):
        @pl.when((nv > (live - 1) * MOE_SUB) & (nv <= live * MOE_SUB))
        def _():
            w = weights()
            for s in range(live):
                sub_block(s, *w)


def _moe_experts(xs, blk_e, blk_nv, w_gu, w_down, tm, tf):
    n, d = xs.shape
    ff = w_down.shape[1]
    nf = ff // tf
    nblk = n // tm

    def f_eff(i, f, nv_ref):
        return jnp.where(nv_ref[i] > 0, f, nf - 1)

    grid_spec = pltpu.PrefetchScalarGridSpec(
        num_scalar_prefetch=2,
        grid=(nblk, nf),
        in_specs=[pl.BlockSpec((tm, d), lambda i, f, e, nv: (i, 0)),
                  pl.BlockSpec((1, d, tf), lambda i, f, e, nv: (e[i], 0, f_eff(i, f, nv))),
                  pl.BlockSpec((1, d, tf), lambda i, f, e, nv: (e[i], 0, nf + f_eff(i, f, nv))),
                  pl.BlockSpec((1, tf, d), lambda i, f, e, nv: (e[i], f_eff(i, f, nv), 0))],
        out_specs=pl.BlockSpec((tm, d), lambda i, f, e, nv: (i, 0)),
    )
    return pl.pallas_call(
        functools.partial(_moe_kernel, nf, tm),
        grid_spec=grid_spec,
        out_shape=jax.ShapeDtypeStruct((n, d), F32),
        compiler_params=_cparams(("arbitrary", "arbitrary")),
        name="moe_experts",
    )(blk_e, blk_nv, xs, w_gu, w_gu, w_down)


def _combine_kernel(tT, nsteps, pos_cur, pos_nxt, ys_hbm, gt_ref, x_ref, mod_ref, g_ref, o_ref,
                    buf, sem):
    j = pl.program_id(0)
    slot = j % 2
    d = x_ref.shape[-1]

    def issue(pos_ref, s):
        for r in range(tT):
            for k in range(TOP_K):
                pltpu.make_async_copy(ys_hbm.at[pl.ds(pos_ref[0, k, r], 1)],
                                      buf.at[s, k, pl.ds(r, 1)], sem.at[s]).start(priority=k)

    @pl.when(j == 0)
    def _():
        issue(pos_cur, 0)

    @pl.when(j + 1 < nsteps)
    def _():
        issue(pos_nxt, 1 - slot)

    for k in range(TOP_K):
        pltpu.make_async_copy(ys_hbm.at[pl.ds(0, tT)], buf.at[slot, k], sem.at[slot]).wait()
    gates = gt_ref[...]
    y = buf[slot, 0] * gates[:, 0:1] + buf[slot, 1] * gates[:, 1:2]
    gt_f = mod_ref[0, :, 5 * d:6 * d]
    o_ref[...] = x_ref[...] + gt_f * _rms(y, g_ref[...])


def _combine(ys, pos, gates, x, mod, g, rows_per_batch, tT):
    n, d = x.shape
    nsteps = n // tT
    per = rows_per_batch // tT
    pos3 = pos.reshape(TOP_K, nsteps, tT).transpose(1, 0, 2)
    row_spec = pl.BlockSpec((tT, d), lambda j: (j, 0))
    smem = functools.partial(pl.BlockSpec, (1, TOP_K, tT), memory_space=pltpu.SMEM)
    return pl.pallas_call(
        functools.partial(_combine_kernel, tT, nsteps),
        grid=(nsteps,),
        in_specs=[smem(lambda j: (j, 0, 0)),
                  smem(lambda j: (jnp.minimum(j + 1, nsteps - 1), 0, 0)),
                  pl.BlockSpec(memory_space=pl.ANY),
                  pl.BlockSpec((tT, LANES), lambda j: (j, 0)),
                  row_spec, pl.BlockSpec((1, 1, 6 * d), lambda j: (j // per, 0, 0)),
                  pl.BlockSpec((1, d), lambda j: (0, 0))],
        out_specs=row_spec,
        out_shape=jax.ShapeDtypeStruct((n, d), F32),
        scratch_shapes=[pltpu.VMEM((2, TOP_K, tT, d), F32), pltpu.SemaphoreType.DMA((2,))],
        compiler_params=_cparams(("arbitrary",)),
        name="moe_combine",
    )(pos3, pos3, ys, gates, x, mod, g)


def _route(top_e, n_experts, tm):
    n = top_e.shape[1]
    nk = n * TOP_K
    i1, i2 = top_e[0], top_e[1]
    idx = jnp.arange(n_experts, dtype=jnp.int32)[:, None]
    oh = [(idx == i[None]).astype(jnp.int32) for i in (i1, i2)]
    cs = [jnp.cumsum(o, axis=1) for o in oh]
    first = cs[0][:, -1]
    counts = first + cs[1][:, -1]
    nblk_e = (counts + tm - 1) // tm
    start = jnp.cumsum(counts) - counts
    bend = jnp.cumsum(nblk_e)
    bstart = bend - nblk_e
    base = (bstart * tm)[:, None]
    pos = jnp.stack([jnp.sum(oh[0] * (base + cs[0] - oh[0]), axis=0),
                     jnp.sum(oh[1] * (base + first[:, None] + cs[1] - oh[1]), axis=0)])
    flat_e = jnp.concatenate([i1, i2])
    nblk = -(-nk // tm) + n_experts
    blk = jnp.arange(nblk, dtype=jnp.int32)
    blk_e = jnp.minimum(jnp.searchsorted(bend, blk, side="right"), n_experts - 1).astype(jnp.int32)
    blk_nv = jnp.clip(counts[blk_e] - (blk - bstart[blk_e]) * tm, 0, tm)
    blk_nv = jnp.where(blk < bend[-1], blk_nv, 0).astype(jnp.int32)
    order = jnp.argsort(flat_e, stable=True).astype(jnp.int32)
    order = jnp.concatenate([order, jnp.zeros((tm,), jnp.int32)])
    src0 = jnp.where(blk_nv > 0, start[blk_e] + (blk - bstart[blk_e]) * tm, 0)
    slot_pair = jnp.concatenate([lax.dynamic_slice_in_dim(order, src0[k], tm) for k in range(nblk)])
    slot_tok = jnp.where(slot_pair >= n, slot_pair - n, slot_pair)
    return slot_tok, pos, blk_e, blk_nv


def kernel(x, c, ctx, c_ctx, mod_w, mod_b, norm_g, rwkv_mu, rwkv_w_rkv, rwkv_w0, rwkv_w1, rwkv_w2,
           rwkv_a0, rwkv_a1, rwkv_a2, rwkv_g1, rwkv_g2, rwkv_k_k, rwkv_k_a, rwkv_r_k, rwkv_ln_w,
           rwkv_ln_b, rwkv_w_out, conv_w_in, conv_w, conv_w_out, ffn_w_gu, ffn_w_down,
           moe_router, moe_w_gu, moe_w_down):
    b, t, d = x.shape
    n_experts = moe_router.shape[-1]
    rows = 16
    cs = jnp.zeros((rows, d), F32).at[:b].set(c).at[b].set(c_ctx)
    mods = _modulation(cs, mod_w, mod_b)
    mod0 = mods[0, :b].reshape(b, 1, 6 * d)
    mod0c = mods[0, b].reshape(1, 1, 6 * d)
    mod1 = mods[1, :b].reshape(b, 1, 6 * d)

    def pad_lora(w):
        zr = jnp.zeros_like(w[0])
        return jnp.stack([jnp.concatenate([w[0], zr], 0), jnp.concatenate([zr, w[1]], 0)]).astype(BF16)

    lg = rwkv_g1.shape[-1]
    lgp = -(-lg // LANES) * LANES
    p = {
        "ng": norm_g[0, 0].reshape(1, d),
        "mu": rwkv_mu[0],
        "wr": rwkv_w_rkv[0, 0].astype(BF16), "wk": rwkv_w_rkv[0, 1].astype(BF16),
        "wv": rwkv_w_rkv[0, 2].astype(BF16),
        "w1": jnp.concatenate([rwkv_w1[0, 0], rwkv_w1[0, 1]], axis=1).astype(BF16),
        "w2": pad_lora(rwkv_w2[0]), "w0": rwkv_w0[0],
        "a1": jnp.concatenate([rwkv_a1[0, 0], rwkv_a1[0, 1]], axis=1).astype(BF16),
        "a2": pad_lora(rwkv_a2[0]), "a0": rwkv_a0[0],
        "k_k": rwkv_k_k[0].reshape(1, d), "k_a": rwkv_k_a[0].reshape(1, d),
        "g1": jnp.pad(rwkv_g1[0], ((0, 0), (0, lgp - lg))).astype(BF16),
        "g2": jnp.pad(rwkv_g2[0], ((0, lgp - lg), (0, 0))).astype(BF16),
        "r_k": rwkv_r_k[0].reshape(1, d),
    }

    tok_c = _tokenwise(ctx, mod0c, "seq", False, p, ctx.shape[1])
    s_ctx = _scan(tok_c, p["k_a"], None, want_y=False, want_state=True)[0]
    tt = min(ROW_TILE, t)
    tm = min(FFN_ROWS, t)
    tf = min(FFN_COLS, ffn_w_down.shape[1])
    tok_l = _tokenwise(x, mod0, "grid", True, p, tt)
    yf, yb = _scan(tok_l[:8], p["k_a"], s_ctx, want_y=True, want_state=False)
    x1, h2 = _readout(yf, yb, tok_l[9], tok_l[8], x, mod0,
                      rwkv_ln_w[0].reshape(1, d), rwkv_ln_b[0].reshape(1, d),
                      rwkv_w_out[0].astype(BF16), norm_g[0, 1].reshape(1, d),
                      norm_g[0, 2].reshape(1, d), min(2 * ROW_TILE, t))

    x2, h3 = _ffn(h2.reshape(b * t, d), ffn_w_gu[0].astype(BF16), ffn_w_down[0].astype(BF16),
                  x1.reshape(b * t, d), mod0, norm_g[0, 3].reshape(1, d),
                  norm_g[1, 0].reshape(1, d), mod1, t, tm, tf)

    w_router = jnp.stack(_split_bf16(jnp.pad(moe_router[0], ((0, 0), (0, LANES - n_experts)))))
    x3, h4, top_e, gates = _conv_layer(h3.reshape(b, t, d), x2.reshape(b, t, d), mod1,
                                       conv_w_in[0].astype(BF16), conv_w[0], conv_w_out[0].astype(BF16),
                                       norm_g[1, 1].reshape(1, d), norm_g[1, 2].reshape(1, d),
                                       w_router, n_experts, tt)

    tme = min(MOE_ROWS, b * t * TOP_K)
    slot_tok, pos, blk_e, blk_nv = _route(top_e.transpose(1, 0, 2).reshape(TOP_K, b * t),
                                          n_experts, tme)
    tg = min(GATHER_ROWS, tme)
    live = (blk_nv[:, None] > jnp.arange(0, tme, tg, dtype=jnp.int32)[None, :]).astype(jnp.int32)
    xs = _gather_rows(h4.reshape(-1, LANES), d, slot_tok, live.reshape(-1), tg)
    ys = _moe_experts(xs, blk_e, blk_nv, moe_w_gu[0], moe_w_down[0], tme,
                      min(MOE_COLS, moe_w_down.shape[2]))
    out = _combine(ys, pos, gates.reshape(b * t, LANES), x3.reshape(b * t, d), mod1,
                   norm_g[1, 3].reshape(1, d), t, tt)
    return out.reshape(b, t, d)
```

```python
import functools
import math

import jax
import jax.numpy as jnp
from jax import lax
from jax.experimental import pallas as pl
from jax.experimental.pallas import tpu as pltpu

F32 = jnp.float32
BF16 = jnp.bfloat16

HEAD = 64
LANES = 128
GRID_W = 64
CHUNK = 64
SUB = 16
SCAN_PAIRS = 8
SCAN_CHUNKS = 4
NORM_EPS = 1e-6
GN_EPS = 64e-5
DECAY_SCALE = math.exp(-0.5)
TOP_K = 2
VMEM_LIMIT = 56 * 1024 * 1024


ROW_TILE = 256
FFN_ROWS = 1024
FFN_COLS = 512
MOE_ROWS = 2560
MOE_COLS = 512


def _cparams(sem):
    return pltpu.CompilerParams(dimension_semantics=sem, vmem_limit_bytes=VMEM_LIMIT)


def _dot(a, b):
    return jnp.dot(a, b, preferred_element_type=F32)


def _dot_nt(a, b):
    return lax.dot_general(a, b, (((1,), (1,)), ((), ())), preferred_element_type=F32)


def _bdot(a, b):
    return _dot(a.astype(BF16), b.astype(BF16))


def _iota(shape, axis):
    return lax.broadcasted_iota(jnp.int32, shape, axis)


def _rms(x, g):
    return x * lax.rsqrt(jnp.mean(x * x, axis=-1, keepdims=True) + NORM_EPS) * g


def _silu(x):
    h = 0.5 * x
    return h * jnp.tanh(h) + h


def _split_bf16(x):
    hi = x.astype(BF16)
    lo = (x - hi.astype(F32)).astype(BF16)
    return hi, lo


def _head_sum(x):
    ones = (_iota((LANES, LANES), 0) // HEAD == _iota((LANES, LANES), 1) // HEAD).astype(BF16)
    xb = x.astype(BF16)
    outs = []
    for p in range(x.shape[-1] // LANES):
        outs.append(_dot(xb[:, p * LANES:(p + 1) * LANES], ones))
    return jnp.concatenate(outs, axis=1)


def _sigmoid(x):
    return 0.5 * jnp.tanh(0.5 * x) + 0.5


def _mod_kernel(c_ref, w_ref, b_ref, o_ref):
    s = _silu(c_ref[...])
    o_ref[0] = jnp.dot(s, w_ref[0], preferred_element_type=F32,
                       precision=lax.Precision.HIGHEST) + b_ref[0]


def _modulation(cs, mod_w, mod_b):
    depth, d, n = mod_w.shape
    rows = cs.shape[0]
    tn = 1536
    return pl.pallas_call(
        _mod_kernel,
        grid=(depth, n // tn),
        in_specs=[pl.BlockSpec((rows, d), lambda i, j: (0, 0)),
                  pl.BlockSpec((1, d, tn), lambda i, j: (i, 0, j)),
                  pl.BlockSpec((1, 1, tn), lambda i, j: (i, 0, j))],
        out_specs=pl.BlockSpec((1, rows, tn), lambda i, j: (i, 0, j)),
        out_shape=jax.ShapeDtypeStruct((depth, rows, n), F32),
        compiler_params=_cparams(("arbitrary", "arbitrary")),
        name="modulation",
    )(cs, mod_w, mod_b.reshape(depth, 1, n))


def _tok_kernel(mode, readout, tT, nT, *refs):
    refs = list(refs)
    x_ref = refs.pop(0)
    if mode == "grid":
        xp_ref = refs.pop(0)
        xn_ref = refs.pop(0)
    (mod_ref, ng_ref, mu_ref, wr_ref, wk_ref, wv_ref, w1_ref, w2_ref, w0_ref,
     a1_ref, a2_ref, a0_ref, kkp_ref, ka_ref) = refs[:14]
    refs = refs[14:]
    if readout:
        g1_ref, g2_ref, rk_ref = refs[:3]
        refs = refs[3:]
    r_o, k_o, v_o, kk_o, ic0_o, ic1_o, lw0_o, lw1_o = refs[:8]
    refs = refs[8:]

    d = x_ref.shape[-1]
    sh = mod_ref[0, :, 0:d]
    sc = mod_ref[0, :, d:2 * d]
    g = ng_ref[...]

    def norm_mod(xx):
        return _rms(xx, g) * (1.0 + sc) + sh

    h = norm_mod(x_ref[0])
    if mode == "grid":
        i = pl.program_id(1)
        q = d // 4
        hp = jnp.where(i > 0, norm_mod(xp_ref[0]), 0.0)
        hn = jnp.where(i < nT - 1, norm_mod(xn_ref[0]), 0.0)
        col = _iota((tT, q), 0) % GRID_W
        left = jnp.where(col == 0, 0.0, pltpu.roll(h[:, 0:q], 1, 0))
        right = jnp.where(col == GRID_W - 1, 0.0, pltpu.roll(h[:, q:2 * q], tT - 1, 0))
        up = jnp.concatenate([hp[:, 2 * q:3 * q], h[:tT - GRID_W, 2 * q:3 * q]], axis=0)
        down = jnp.concatenate([h[GRID_W:, 3 * q:], hn[:, 3 * q:]], axis=0)
        hs = jnp.concatenate([left, right, up, down], axis=1)
    else:
        half = d // 2
        row = _iota((tT, half), 0)
        prev = jnp.where(row == 0, 0.0, pltpu.roll(h[:, :half], 1, 0))
        nxt = jnp.where(row == tT - 1, 0.0, pltpu.roll(h[:, half:], tT - 1, 0))
        hs = jnp.concatenate([prev, nxt], axis=1)

    hb = h.astype(BF16)
    dxb = (hs - h).astype(BF16)

    def mix(n):
        return hb + dxb * mu_ref[n:n + 1, :].astype(BF16)

    r = _dot(mix(0), wr_ref[...])
    k = _dot(mix(2), wk_ref[...])
    v = _dot(mix(3), wv_ref[...])
    w1o = jnp.tanh(_dot(mix(1), w1_ref[...])).astype(BF16)
    a1o = _dot(mix(4), a1_ref[...]).astype(BF16)
    ics = []
    for z, (lw_o, ic_o) in enumerate(((lw0_o, ic0_o), (lw1_o, ic1_o))):
        w_pre = _dot(w1o, w2_ref[z]) + w0_ref[z:z + 1, :]
        lw_o[0] = (-0.5 * DECAY_SCALE) * jnp.tanh(0.5 * w_pre) - 0.5 * DECAY_SCALE
        ic = _sigmoid(_dot(a1o, a2_ref[z]) + a0_ref[z:z + 1, :])
        ic_o[0] = ic.astype(BF16)
        ics.append(ic)
    kk = k * kkp_ref[...]
    kk = kk * lax.rsqrt(jnp.maximum(_head_sum(kk * kk), 1e-24))
    r_o[0] = r.astype(BF16)
    k_o[0] = k.astype(BF16)
    v_o[0] = v.astype(BF16)
    kk_o[0] = kk.astype(BF16)
    if readout:
        gate_o, bon_o = refs
        gate = _dot(_sigmoid(_dot(mix(5), g1_ref[...])).astype(BF16), g2_ref[...])
        gate_o[0] = gate.astype(BF16)
        k_avg = k * (1.0 + (0.5 * (ics[0] + ics[1]) - 1.0) * ka_ref[...])
        bon_o[0] = (_head_sum(r * k_avg * rk_ref[...]) * v).astype(BF16)


def _tokenwise(x, mod, mode, readout, p, tT):
    b, t, d = x.shape
    nT = t // tT
    row_spec = pl.BlockSpec((1, tT, d), lambda bi, i: (bi, i, 0))
    in_specs = [row_spec]
    args = [x]
    if mode == "grid":
        hb = tT // GRID_W
        in_specs += [
            pl.BlockSpec((1, GRID_W, d), lambda bi, i: (bi, jnp.maximum(i * hb - 1, 0), 0)),
            pl.BlockSpec((1, GRID_W, d), lambda bi, i: (bi, jnp.minimum((i + 1) * hb, t // GRID_W - 1), 0)),
        ]
        args += [x, x]
    if mod.shape[0] == 1:
        in_specs.append(pl.BlockSpec((1, 1, mod.shape[-1]), lambda bi, i: (0, 0, 0)))
    else:
        in_specs.append(pl.BlockSpec((1, 1, mod.shape[-1]), lambda bi, i: (bi, 0, 0)))
    args.append(mod)
    consts = [p["ng"], p["mu"], p["wr"], p["wk"], p["wv"], p["w1"], p["w2"], p["w0"],
              p["a1"], p["a2"], p["a0"], p["k_k"], p["k_a"]]
    if readout:
        consts += [p["g1"], p["g2"], p["r_k"]]
    for a in consts:
        in_specs.append(pl.BlockSpec(a.shape, lambda bi, i, _n=a.ndim: (0,) * _n))
        args.append(a)
    dtypes = [BF16] * 6 + [F32] * 2 + ([BF16] * 2 if readout else [])
    return pl.pallas_call(
        functools.partial(_tok_kernel, mode, readout, tT, nT),
        grid=(b, nT),
        in_specs=in_specs,
        out_specs=[row_spec] * len(dtypes),
        out_shape=[jax.ShapeDtypeStruct((b, t, d), dt) for dt in dtypes],
        compiler_params=_cparams(("arbitrary", "arbitrary")),
        name="rwkv_tokenwise_" + mode,
    )(*args)


def _scan_prep(r, kd, v, lw, kk, ic, reverse):
    c = r.shape[0]
    ti = _iota((c, c), 0)
    si = _iota((c, c), 1)
    tri = ((si >= ti) if reverse else (si <= ti)).astype(BF16)
    lw_hi, lw_lo = _split_bf16(lw)
    cum = _dot(tri, lw_hi) + _dot(tri, lw_lo)
    pc_log = cum[0:1] if reverse else cum[c - 1:c]
    p_inv = jnp.exp(-cum)
    p_rest = jnp.exp(pc_log - cum)
    b = kk * ic
    return {"rt": r * jnp.exp(cum), "at": -(kk * jnp.exp(cum - lw)),
            "bt": b * p_inv, "kt": kd * p_inv, "bp": b * p_rest, "kp": kd * p_rest,
            "v": v, "pc": jnp.exp(pc_log)}


def _scan_step(prep, states):
    c = prep[0]["v"].shape[0]
    n = 2 * c
    pp = len(states[0])
    chains = [(dr, p) for dr in range(2) for p in range(pp)]
    first = _iota((c, LANES), 1) < HEAD
    row = _iota((n, n), 0)
    col = _iota((n, n), 1)
    strict = (col < row, col > row)
    same = (row // SUB) == (col // SUB)

    def pair(name, dr, p):
        return prep[dr][name][:, p * LANES:(p + 1) * LANES]

    def stack(x):
        return jnp.concatenate([jnp.where(first, x, 0.0), jnp.where(first, 0.0, x)], axis=0)

    def ms(name, dr, p):
        return stack(pair(name, dr, p))

    def each(fn, *lists):
        return [fn(*xs) for xs in zip(*lists)]

    lhs = [jnp.concatenate([pair("at", dr, p), pair("rt", dr, p)], axis=0).astype(BF16)
           for dr, p in chains]
    rhs = [jnp.concatenate([ms("bt", dr, p), ms("kt", dr, p)], axis=0).astype(BF16) for dr, p in chains]
    at_ms = [ms("at", dr, p).astype(BF16) for dr, p in chains]
    v_ms = [ms("v", dr, p).astype(BF16) for dr, p in chains]
    gmat = each(_dot_nt, lhs, rhs)
    own = (row // c) == (col // c)
    tri = [own & m for m in strict]
    tri_d = [m & same for m in tri]
    tri_o = [m & jnp.logical_not(same) for m in tri]

    def twice(x):
        return jnp.concatenate([x, x], axis=0)

    nd = [jnp.where(tri_d[dr], twice(g[:c, :n]), 0.0) for (dr, _), g in zip(chains, gmat)]
    no = [jnp.where(tri_o[dr], twice(g[:c, :n]), 0.0) for (dr, _), g in zip(chains, gmat)]
    a_ak = [jnp.where(tri[dr], twice(g[:c, n:]), 0.0).astype(BF16) for (dr, _), g in zip(chains, gmat)]
    t_pl = _iota((c, 2 * n), 0)
    s_pl = _iota((c, 2 * n), 1) % c
    incl_pl = (s_pl <= t_pl, s_pl >= t_pl)
    a_r = [jnp.where(incl_pl[dr], g[c:], 0.0).astype(BF16)
           for (dr, _), g in zip(chains, gmat)]

    pw = nd
    tp = nd
    span = 2
    while span < SUB:
        pw = each(_bdot, pw, pw)
        tp = each(lambda t, q: t + q + _bdot(t, q), tp, pw)
        span *= 2
    m = each(lambda t, o: o + _bdot(t, o), tp, no)
    qp = m
    pw = m
    span = 2
    while span < c // SUB:
        pw = each(_bdot, pw, pw)
        qp = each(lambda t, q: t + q + _bdot(t, q), qp, pw)
        span *= 2
    tp = each(lambda t, q: t + q + _bdot(q, t), tp, qp)

    bk_t = [jnp.concatenate([ms("bp", dr, p), ms("kp", dr, p)], axis=0).T.astype(BF16)
            for dr, p in chains]
    diag = row == col
    pc_col = [jnp.sum(jnp.where(diag, prep[dr]["pc"][:, p * LANES:(p + 1) * LANES], 0.0),
                      axis=1, keepdims=True) for dr, p in chains]

    s_t = [states[dr][p].astype(BF16) for dr, p in chains]
    rhs_z = each(lambda l, a, st, v: _dot(jnp.concatenate([l, a], axis=1),
                                          jnp.concatenate([st, v], axis=0)), at_ms, a_ak, s_t, v_ms)
    z = each(lambda t, x: x + _bdot(t, x), tp, rhs_z)
    zv = each(lambda zz, v: jnp.concatenate([zz.astype(BF16), v], axis=0), z, v_ms)
    y_pl = each(lambda l, st, ar, w: _dot(l[c:], st) + _dot(ar, w), lhs, s_t, a_r, zv)
    y = [jnp.concatenate([yy for (d2, _), yy in zip(chains, y_pl) if d2 == dr], axis=1)
         for dr in range(2)]
    new_states = [[None] * pp for _ in range(2)]
    for (dr, p), bt_, w, pc in zip(chains, bk_t, zv, pc_col):
        new_states[dr][p] = states[dr][p] * pc + _dot(bt_, w)
    return y, new_states


def _scan_kernel(pp, nc, has_s0, want_y, want_state, *refs):
    refs = list(refs)
    ka_ref = refs[12]
    dirs = (refs[:6], refs[6:12])
    refs = refs[13:]
    if has_s0:
        s0_ref = refs.pop(0)
    if want_y:
        y_refs = (refs.pop(0), refs.pop(0))
    if want_state:
        st_ref = refs.pop(0)
    s_scr = refs.pop(0)
    c = pl.program_id(2)

    @pl.when(c == 0)
    def _():
        if has_s0:
            s_scr[...] = s0_ref[0]
        else:
            s_scr[...] = jnp.zeros_like(s_scr)

    per_step = dirs[0][0].shape[1] // CHUNK
    states = [[s_scr[dr, p] for p in range(pp)] for dr in range(2)]
    for j in range(per_step):
        prep = []
        rows = []
        for dr, (r_ref, k_ref, v_ref, kk_ref, ic_ref, lw_ref) in enumerate(dirs):
            q = j if dr == 0 else per_step - 1 - j
            sl = slice(q * CHUNK, (q + 1) * CHUNK)
            rows.append(sl)
            ic = ic_ref[0, sl, :].astype(F32)
            kd = k_ref[0, sl, :].astype(F32) * (1.0 + (ic - 1.0) * ka_ref[...])
            prep.append(_scan_prep(r_ref[0, sl, :].astype(F32), kd, v_ref[0, sl, :].astype(F32),
                                   lw_ref[0, sl, :], kk_ref[0, sl, :].astype(F32), ic,
                                   reverse=(dr == 1)))
        y, states = _scan_step(prep, states)
        if want_y:
            for dr in range(2):
                y_refs[dr][0, rows[dr], :] = y[dr].astype(BF16)
    for dr in range(2):
        for p in range(pp):
            s_scr[dr, p] = states[dr][p]

    if want_state:
        @pl.when(c == nc - 1)
        def _():
            st_ref[0] = s_scr[...]


def _scan(tok, ka, s0, want_y, want_state):
    r, k, v, kk, ic0, ic1, lw0, lw1 = tok
    b, t, d = r.shape
    rows = min(CHUNK * SCAN_CHUNKS, t)
    nc = t // rows
    npair = d // LANES
    pp = min(SCAN_PAIRS, npair)
    w = LANES * pp
    f_spec = pl.BlockSpec((1, rows, w), lambda bi, hi, ci: (bi, ci, hi))
    b_spec = pl.BlockSpec((1, rows, w), lambda bi, hi, ci: (bi, nc - 1 - ci, hi))
    st_spec = pl.BlockSpec((1, 2, pp, LANES, LANES), lambda bi, hi, ci: (bi, 0, hi, 0, 0))
    in_specs = [f_spec] * 6 + [b_spec] * 6 + [pl.BlockSpec((1, w), lambda bi, hi, ci: (0, hi))]
    args = [r, k, v, kk, ic0, lw0, r, k, v, kk, ic1, lw1, ka]
    if s0 is not None:
        in_specs.append(st_spec)
        args.append(s0)
    out_specs, out_shape = [], []
    if want_y:
        out_specs += [f_spec, b_spec]
        out_shape += [jax.ShapeDtypeStruct((b, t, d), BF16)] * 2
    if want_state:
        out_specs.append(st_spec)
        out_shape.append(jax.ShapeDtypeStruct((b, 2, npair, LANES, LANES), F32))
    return pl.pallas_call(
        functools.partial(_scan_kernel, pp, nc, s0 is not None, want_y, want_state),
        grid=(b, npair // pp, nc),
        in_specs=in_specs,
        out_specs=out_specs,
        out_shape=out_shape,
        scratch_shapes=[pltpu.VMEM((2, pp, LANES, LANES), F32)],
        compiler_params=_cparams(("arbitrary", "arbitrary", "arbitrary")),
        name="rwkv_scan_ctx" if s0 is None else "rwkv_scan_latent",
    )(*args)


def _readout_kernel(yf_ref, yb_ref, bon_ref, gate_ref, x_ref, mod_ref, lnw_ref, lnb_ref,
                    wo_ref, g1_ref, g2_ref, x_o, h_o):
    d = x_ref.shape[-1]
    y = yf_ref[0].astype(F32) + yb_ref[0].astype(F32)
    mean = _head_sum(y) * (1.0 / HEAD)
    yc = y - mean
    var = _head_sum(yc * yc) * (1.0 / HEAD)
    o = yc * lax.rsqrt(var + GN_EPS) * lnw_ref[...] + lnb_ref[...] + bon_ref[0]
    att = _dot((o * gate_ref[0]).astype(BF16), wo_ref[...])
    gt_a = mod_ref[0, :, 2 * d:3 * d]
    sh_f = mod_ref[0, :, 3 * d:4 * d]
    sc_f = mod_ref[0, :, 4 * d:5 * d]
    x1 = x_ref[0] + gt_a * _rms(att, g1_ref[...])
    x_o[0] = x1
    h_o[0] = (_rms(x1, g2_ref[...]) * (1.0 + sc_f) + sh_f).astype(BF16)


def _readout(yf, yb, bon, gate, x, mod, lnw, lnb, wo, g1, g2, tT):
    b, t, d = x.shape
    row_spec = pl.BlockSpec((1, tT, d), lambda bi, i: (bi, i, 0))
    vec = pl.BlockSpec((1, d), lambda bi, i: (0, 0))
    return pl.pallas_call(
        _readout_kernel,
        grid=(b, t // tT),
        in_specs=[row_spec] * 5 + [pl.BlockSpec((1, 1, 6 * d), lambda bi, i: (bi, 0, 0)),
                                   vec, vec, pl.BlockSpec((d, d), lambda bi, i: (0, 0)), vec, vec],
        out_specs=[row_spec, row_spec],
        out_shape=[jax.ShapeDtypeStruct((b, t, d), F32), jax.ShapeDtypeStruct((b, t, d), BF16)],
        compiler_params=_cparams(("arbitrary", "arbitrary")),
        name="rwkv_readout",
    )(yf, yb, bon, gate, x, mod, lnw, lnb, wo, g1, g2)


def _ffn_kernel(nf, h_ref, wg_ref, wu_ref, wd_ref, x_ref, mod_ref, g3_ref, gn_ref, mod2_ref,
                x_o, h_o, acc):
    f = pl.program_id(1)
    d = x_ref.shape[-1]

    @pl.when(f == 0)
    def _():
        acc[...] = jnp.zeros_like(acc)

    h = h_ref[...]
    act = _silu(_dot(h, wg_ref[...])) * _dot(h, wu_ref[...])
    acc[...] += _dot(act.astype(BF16), wd_ref[...])

    @pl.when(f == nf - 1)
    def _():
        gt_f = mod_ref[0, :, 5 * d:6 * d]
        x2 = x_ref[...] + gt_f * _rms(acc[...], g3_ref[...])
        x_o[...] = x2
        sh = mod2_ref[0, :, 0:d]
        sc = mod2_ref[0, :, d:2 * d]
        h_o[...] = (_rms(x2, gn_ref[...]) * (1.0 + sc) + sh).astype(BF16)


def _ffn(h, w_gu, w_down, x, mod, g3, gn, mod2, rows_per_batch, tm, tf):
    n, d = x.shape
    ff = w_down.shape[0]
    nf = ff // tf
    per = rows_per_batch // tm
    row = pl.BlockSpec((tm, d), lambda i, f: (i, 0))
    vec = pl.BlockSpec((1, d), lambda i, f: (0, 0))
    modspec = pl.BlockSpec((1, 1, 6 * d), lambda i, f: (i // per, 0, 0))
    return pl.pallas_call(
        functools.partial(_ffn_kernel, nf),
        grid=(n // tm, nf),
        in_specs=[row,
                  pl.BlockSpec((d, tf), lambda i, f: (0, f)),
                  pl.BlockSpec((d, tf), lambda i, f: (0, nf + f)),
                  pl.BlockSpec((tf, d), lambda i, f: (f, 0)),
                  row, modspec, vec, vec, modspec],
        out_specs=[row, row],
        out_shape=[jax.ShapeDtypeStruct((n, d), F32), jax.ShapeDtypeStruct((n, d), BF16)],
        scratch_shapes=[pltpu.VMEM((tm, d), F32)],
        compiler_params=_cparams(("arbitrary", "arbitrary")),
        name="dense_swiglu",
    )(h, w_gu, w_gu, w_down, x, mod, g3, gn, mod2)


HALO = 16


def _conv_kernel(tT, nT, n_experts, h_ref, hp_ref, hn_ref, x_ref, mod_ref, win_ref, cw_ref, wo_ref,
                 g1_ref, g2_ref, wr_ref, x_o, h_o, te_o, gt_o):
    i = pl.program_id(1)
    d = x_ref.shape[-1]
    n = tT + 2 * HALO
    h_ext = jnp.concatenate([hp_ref[0], h_ref[0], hn_ref[0]], axis=0)
    proj = _dot(h_ext, win_ref[...])
    z = proj[:, d:2 * d] * proj[:, 2 * d:]
    row = _iota((n, d), 0)
    dead = ((row < HALO) & (i == 0)) | ((row >= HALO + tT) & (i == nT - 1))
    z = jnp.where(dead, 0.0, z)
    conv = (pltpu.roll(z, 1, 0) * cw_ref[0:1, :] + z * cw_ref[1:2, :]
            + pltpu.roll(z, n - 1, 0) * cw_ref[2:3, :])
    gated = (proj[HALO:HALO + tT, 0:d] * conv[HALO:HALO + tT]).astype(BF16)
    y = _dot(gated, wo_ref[...])
    gt_a = mod_ref[0, :, 2 * d:3 * d]
    sh_f = mod_ref[0, :, 3 * d:4 * d]
    sc_f = mod_ref[0, :, 4 * d:5 * d]
    x3 = x_ref[0] + gt_a * _rms(y, g1_ref[...])
    x_o[0] = x3
    h4 = _rms(x3, g2_ref[...]) * (1.0 + sc_f) + sh_f
    s_rows = d // LANES
    for c in range(s_rows):
        h_o[0, pl.ds(c, tT, stride=s_rows), :] = h4[:, c * LANES:(c + 1) * LANES]
    h_hi, h_lo = _split_bf16(h4)
    lg = _dot(h_hi, wr_ref[0]) + _dot(h_lo, wr_ref[0]) + _dot(h_hi, wr_ref[1])
    lg_t = lg.T
    lg8 = lg_t[:n_experts]
    eidx = _iota(lg8.shape, 0)
    m1 = jnp.max(lg8, axis=0, keepdims=True)
    i1 = jnp.min(jnp.where(lg8 == m1, eidx, n_experts), axis=0, keepdims=True)
    rest = jnp.where(eidx == i1, -jnp.inf, lg8)
    m2 = jnp.max(rest, axis=0, keepdims=True)
    i2 = jnp.min(jnp.where(rest == m2, eidx, n_experts), axis=0, keepdims=True)
    tw = jnp.exp(m2 - m1)
    te_o[0] = jnp.concatenate([i1, i2], axis=0)
    gates = jnp.concatenate([1.0 / (1.0 + tw), tw / (1.0 + tw),
                             jnp.zeros((LANES - TOP_K, tT), F32)], axis=0)
    gt_o[0] = gates.T


def _conv_layer(h, x, mod, w_in, conv_w, w_out, g1, g2, w_router, n_experts, tT):
    b, t, d = x.shape
    nT = t // tT
    hb = tT // HALO
    row_spec = pl.BlockSpec((1, tT, d), lambda bi, i: (bi, i, 0))
    vec = pl.BlockSpec((1, d), lambda bi, i: (0, 0))
    full = lambda a: pl.BlockSpec(a.shape, lambda bi, i, _n=a.ndim: (0,) * _n)
    return pl.pallas_call(
        functools.partial(_conv_kernel, tT, nT, n_experts),
        grid=(b, nT),
        in_specs=[row_spec,
                  pl.BlockSpec((1, HALO, d), lambda bi, i: (bi, jnp.maximum(i * hb - 1, 0), 0)),
                  pl.BlockSpec((1, HALO, d), lambda bi, i: (bi, jnp.minimum((i + 1) * hb, t // HALO - 1), 0)),
                  row_spec, pl.BlockSpec((1, 1, 6 * d), lambda bi, i: (bi, 0, 0)),
                  full(w_in), full(conv_w), full(w_out), vec, vec, full(w_router)],
        out_specs=[row_spec, pl.BlockSpec((1, tT * (d // LANES), LANES), lambda bi, i: (bi, i, 0)),
                   pl.BlockSpec((1, TOP_K, tT), lambda bi, i: (bi, 0, i)),
                   pl.BlockSpec((1, tT, LANES), lambda bi, i: (bi, i, 0))],
        out_shape=[jax.ShapeDtypeStruct((b, t, d), F32),
                   jax.ShapeDtypeStruct((b, t * (d // LANES), LANES), F32),
                   jax.ShapeDtypeStruct((b, TOP_K, t), jnp.int32),
                   jax.ShapeDtypeStruct((b, t, LANES), F32)],
        compiler_params=_cparams(("arbitrary", "arbitrary")),
        name="short_conv",
    )(h, h, h, x, mod, w_in, conv_w, w_out, g1, g2, w_router)


GATHER_ROWS = 512


def _gather_kernel(tg, nsteps, live_ref, tok_cur, tok_nxt, h_hbm, o_ref, buf, sem):
    j = pl.program_id(0)
    slot = j % 2
    s_rows = buf.shape[1] // tg

    def issue(tok_ref, s):
        for r in range(tg):
            src = pl.multiple_of(tok_ref[0, 0, r] * s_rows, s_rows)
            pltpu.make_async_copy(h_hbm.at[pl.ds(src, s_rows)],
                                  buf.at[s, pl.ds(r * s_rows, s_rows)], sem.at[s]).start(priority=r % 2)

    @pl.when((j == 0) & (live_ref[0] > 0))
    def _():
        issue(tok_cur, 0)

    @pl.when(j + 1 < nsteps)
    def _():
        @pl.when(live_ref[j + 1] > 0)
        def _():
            issue(tok_nxt, 1 - slot)

    @pl.when(live_ref[j] > 0)
    def _():
        pltpu.make_async_copy(h_hbm.at[pl.ds(0, tg * s_rows)], buf.at[slot], sem.at[slot]).wait()
        for c in range(s_rows):
            o_ref[:, c * LANES:(c + 1) * LANES] = buf[slot, pl.ds(c, tg, stride=s_rows), :].astype(BF16)

    @pl.when(live_ref[j] == 0)
    def _():
        o_ref[...] = jnp.zeros(o_ref.shape, BF16)


def _gather_rows(h, d, slot_tok, live, tg):
    s_rows = d // LANES
    nsteps = slot_tok.shape[0] // tg
    tok3 = slot_tok.reshape(nsteps, 1, tg)
    grid_spec = pltpu.PrefetchScalarGridSpec(
        num_scalar_prefetch=1,
        grid=(nsteps,),
        in_specs=[pl.BlockSpec((1, 1, tg), lambda j, lv: (j, 0, 0), memory_space=pltpu.SMEM),
                  pl.BlockSpec((1, 1, tg), lambda j, lv: (jnp.minimum(j + 1, nsteps - 1), 0, 0),
                               memory_space=pltpu.SMEM),
                  pl.BlockSpec(memory_space=pl.ANY)],
        out_specs=pl.BlockSpec((tg, d), lambda j, lv: (j, 0)),
        scratch_shapes=[pltpu.VMEM((2, tg * s_rows, LANES), F32), pltpu.SemaphoreType.DMA((2,))],
    )
    return pl.pallas_call(
        functools.partial(_gather_kernel, tg, nsteps),
        grid_spec=grid_spec,
        out_shape=jax.ShapeDtypeStruct((nsteps * tg, d), BF16),
        compiler_params=_cparams(("arbitrary",)),
        name="moe_dispatch",
    )(live, tok3, tok3, h)


MOE_SUB = 512


def _moe_kernel(nf, tm, e_ref, nv_ref, x_ref, wg_ref, wu_ref, wd_ref, y_o):
    i = pl.program_id(0)
    f = pl.program_id(1)
    nv = nv_ref[i]

    nsub = tm // MOE_SUB

    @pl.when(f == 0)
    def _():
        y_o[...] = jnp.zeros(y_o.shape, F32)

    def weights():
        return wg_ref[0].astype(BF16), wu_ref[0].astype(BF16), wd_ref[0].astype(BF16)

    def sub_block(s, wg, wu, wd):
        rows = slice(s * MOE_SUB, (s + 1) * MOE_SUB)
        h = x_ref[rows, :]
        act = _silu(_dot(h, wg)) * _dot(h, wu)
        y_o[rows, :] += _dot(act.astype(BF16), wd)

    for live in range(1, nsub + 1):
        @pl.when((nv > (live - 1) * MOE_SUB) & (nv <= live * MOE_SUB))
        def _():
            w = weights()
            for s in range(live):
                sub_block(s, *w)


def _moe_experts(xs, blk_e, blk_nv, w_gu, w_down, tm, tf):
    n, d = xs.shape
    ff = w_down.shape[1]
    nf = ff // tf
    nblk = n // tm

    def f_eff(i, f, nv_ref):
        return jnp.where(nv_ref[i] > 0, f, nf - 1)

    grid_spec = pltpu.PrefetchScalarGridSpec(
        num_scalar_prefetch=2,
        grid=(nblk, nf),
        in_specs=[pl.BlockSpec((tm, d), lambda i, f, e, nv: (i, 0)),
                  pl.BlockSpec((1, d, tf), lambda i, f, e, nv: (e[i], 0, f_eff(i, f, nv))),
                  pl.BlockSpec((1, d, tf), lambda i, f, e, nv: (e[i], 0, nf + f_eff(i, f, nv))),
                  pl.BlockSpec((1, tf, d), lambda i, f, e, nv: (e[i], f_eff(i, f, nv), 0))],
        out_specs=pl.BlockSpec((tm, d), lambda i, f, e, nv: (i, 0)),
    )
    return pl.pallas_call(
        functools.partial(_moe_kernel, nf, tm),
        grid_spec=grid_spec,
        out_shape=jax.ShapeDtypeStruct((n, d), F32),
        compiler_params=_cparams(("arbitrary", "arbitrary")),
        name="moe_experts",
    )(blk_e, blk_nv, xs, w_gu, w_gu, w_down)


def _combine_kernel(tT, nsteps, pos_cur, pos_nxt, ys_hbm, gt_ref, x_ref, mod_ref, g_ref, o_ref,
                    buf, sem):
    j = pl.program_id(0)
    slot = j % 2
    d = x_ref.shape[-1]

    def issue(pos_ref, s):
        for r in range(tT):
            for k in range(TOP_K):
                pltpu.make_async_copy(ys_hbm.at[pl.ds(pos_ref[0, k, r], 1)],
                                      buf.at[s, k, pl.ds(r, 1)], sem.at[s]).start(priority=k)

    @pl.when(j == 0)
    def _():
        issue(pos_cur, 0)

    @pl.when(j + 1 < nsteps)
    def _():
        issue(pos_nxt, 1 - slot)

    for k in range(TOP_K):
        pltpu.make_async_copy(ys_hbm.at[pl.ds(0, tT)], buf.at[slot, k], sem.at[slot]).wait()
    gates = gt_ref[...]
    y = buf[slot, 0] * gates[:, 0:1] + buf[slot, 1] * gates[:, 1:2]
    gt_f = mod_ref[0, :, 5 * d:6 * d]
    o_ref[...] = x_ref[...] + gt_f * _rms(y, g_ref[...])


def _combine(ys, pos, gates, x, mod, g, rows_per_batch, tT):
    n, d = x.shape
    nsteps = n // tT
    per = rows_per_batch // tT
    pos3 = pos.reshape(TOP_K, nsteps, tT).transpose(1, 0, 2)
    row_spec = pl.BlockSpec((tT, d), lambda j: (j, 0))
    smem = functools.partial(pl.BlockSpec, (1, TOP_K, tT), memory_space=pltpu.SMEM)
    return pl.pallas_call(
        functools.partial(_combine_kernel, tT, nsteps),
        grid=(nsteps,),
        in_specs=[smem(lambda j: (j, 0, 0)),
                  smem(lambda j: (jnp.minimum(j + 1, nsteps - 1), 0, 0)),
                  pl.BlockSpec(memory_space=pl.ANY),
                  pl.BlockSpec((tT, LANES), lambda j: (j, 0)),
                  row_spec, pl.BlockSpec((1, 1, 6 * d), lambda j: (j // per, 0, 0)),
                  pl.BlockSpec((1, d), lambda j: (0, 0))],
        out_specs=row_spec,
        out_shape=jax.ShapeDtypeStruct((n, d), F32),
        scratch_shapes=[pltpu.VMEM((2, TOP_K, tT, d), F32), pltpu.SemaphoreType.DMA((2,))],
        compiler_params=_cparams(("arbitrary",)),
        name="moe_combine",
    )(pos3, pos3, ys, gates, x, mod, g)


def _route(top_e, n_experts, tm):
    n = top_e.shape[1]
    nk = n * TOP_K
    i1, i2 = top_e[0], top_e[1]
    idx = jnp.arange(n_experts, dtype=jnp.int32)[:, None]
    oh = [(idx == i[None]).astype(jnp.int32) for i in (i1, i2)]
    cs = [jnp.cumsum(o, axis=1) for o in oh]
    first = cs[0][:, -1]
    counts = first + cs[1][:, -1]
    nblk_e = (counts + tm - 1) // tm
    start = jnp.cumsum(counts) - counts
    bend = jnp.cumsum(nblk_e)
    bstart = bend - nblk_e
    base = (bstart * tm)[:, None]
    pos = jnp.stack([jnp.sum(oh[0] * (base + cs[0] - oh[0]), axis=0),
                     jnp.sum(oh[1] * (base + first[:, None] + cs[1] - oh[1]), axis=0)])
    flat_e = jnp.concatenate([i1, i2])
    nblk = -(-nk // tm) + n_experts
    blk = jnp.arange(nblk, dtype=jnp.int32)
    blk_e = jnp.minimum(jnp.searchsorted(bend, blk, side="right"), n_experts - 1).astype(jnp.int32)
    blk_nv = jnp.clip(counts[blk_e] - (blk - bstart[blk_e]) * tm, 0, tm)
    blk_nv = jnp.where(blk < bend[-1], blk_nv, 0).astype(jnp.int32)
    order = jnp.argsort(flat_e, stable=True).astype(jnp.int32)
    order = jnp.concatenate([order, jnp.zeros((tm,), jnp.int32)])
    src0 = jnp.where(blk_nv > 0, start[blk_e] + (blk - bstart[blk_e]) * tm, 0)
    slot_pair = jnp.concatenate([lax.dynamic_slice_in_dim(order, src0[k], tm) for k in range(nblk)])
    slot_tok = jnp.where(slot_pair >= n, slot_pair - n, slot_pair)
    return slot_tok, pos, blk_e, blk_nv


def kernel(x, c, ctx, c_ctx, mod_w, mod_b, norm_g, rwkv_mu, rwkv_w_rkv, rwkv_w0, rwkv_w1, rwkv_w2,
           rwkv_a0, rwkv_a1, rwkv_a2, rwkv_g1, rwkv_g2, rwkv_k_k, rwkv_k_a, rwkv_r_k, rwkv_ln_w,
           rwkv_ln_b, rwkv_w_out, conv_w_in, conv_w, conv_w_out, ffn_w_gu, ffn_w_down,
           moe_router, moe_w_gu, moe_w_down):
    b, t, d = x.shape
    n_experts = moe_router.shape[-1]
    rows = 16
    cs = jnp.zeros((rows, d), F32).at[:b].set(c).at[b].set(c_ctx)
    mods = _modulation(cs, mod_w, mod_b)
    mod0 = mods[0, :b].reshape(b, 1, 6 * d)
    mod0c = mods[0, b].reshape(1, 1, 6 * d)
    mod1 = mods[1, :b].reshape(b, 1, 6 * d)

    def pad_lora(w):
        zr = jnp.zeros_like(w[0])
        return jnp.stack([jnp.concatenate([w[0], zr], 0), jnp.concatenate([zr, w[1]], 0)]).astype(BF16)

    lg = rwkv_g1.shape[-1]
    lgp = -(-lg // LANES) * LANES
    p = {
        "ng": norm_g[0, 0].reshape(1, d),
        "mu": rwkv_mu[0],
        "wr": rwkv_w_rkv[0, 0].astype(BF16), "wk": rwkv_w_rkv[0, 1].astype(BF16),
        "wv": rwkv_w_rkv[0, 2].astype(BF16),
        "w1": jnp.concatenate([rwkv_w1[0, 0], rwkv_w1[0, 1]], axis=1).astype(BF16),
        "w2": pad_lora(rwkv_w2[0]), "w0": rwkv_w0[0],
        "a1": jnp.concatenate([rwkv_a1[0, 0], rwkv_a1[0, 1]], axis=1).astype(BF16),
        "a2": pad_lora(rwkv_a2[0]), "a0": rwkv_a0[0],
        "k_k": rwkv_k_k[0].reshape(1, d), "k_a": rwkv_k_a[0].reshape(1, d),
        "g1": jnp.pad(rwkv_g1[0], ((0, 0), (0, lgp - lg))).astype(BF16),
        "g2": jnp.pad(rwkv_g2[0], ((0, lgp - lg), (0, 0))).astype(BF16),
        "r_k": rwkv_r_k[0].reshape(1, d),
    }

    tok_c = _tokenwise(ctx, mod0c, "seq", False, p, ctx.shape[1])
    s_ctx = _scan(tok_c, p["k_a"], None, want_y=False, want_state=True)[0]
    tt = min(ROW_TILE, t)
    tm = min(FFN_ROWS, t)
    tf = min(FFN_COLS, ffn_w_down.shape[1])
    tok_l = _tokenwise(x, mod0, "grid", True, p, min(2 * ROW_TILE, t))
    yf, yb = _scan(tok_l[:8], p["k_a"], s_ctx, want_y=True, want_state=False)
    x1, h2 = _readout(yf, yb, tok_l[9], tok_l[8], x, mod0,
                      rwkv_ln_w[0].reshape(1, d), rwkv_ln_b[0].reshape(1, d),
                      rwkv_w_out[0].astype(BF16), norm_g[0, 1].reshape(1, d),
                      norm_g[0, 2].reshape(1, d), min(4 * ROW_TILE, t))

    x2, h3 = _ffn(h2.reshape(b * t, d), ffn_w_gu[0].astype(BF16), ffn_w_down[0].astype(BF16),
                  x1.reshape(b * t, d), mod0, norm_g[0, 3].reshape(1, d),
                  norm_g[1, 0].reshape(1, d), mod1, t, tm, tf)

    w_router = jnp.stack(_split_bf16(jnp.pad(moe_router[0], ((0, 0), (0, LANES - n_experts)))))
    x3, h4, top_e, gates = _conv_layer(h3.reshape(b, t, d), x2.reshape(b, t, d), mod1,
                                       conv_w_in[0].astype(BF16), conv_w[0], conv_w_out[0].astype(BF16),
                                       norm_g[1, 1].reshape(1, d), norm_g[1, 2].reshape(1, d),
                                       w_router, n_experts, tt)

    tme = min(MOE_ROWS, b * t * TOP_K)
    slot_tok, pos, blk_e, blk_nv = _route(top_e.transpose(1, 0, 2).reshape(TOP_K, b * t),
                                          n_experts, tme)
    tg = min(GATHER_ROWS, tme)
    live = (blk_nv[:, None] > jnp.arange(0, tme, tg, dtype=jnp.int32)[None, :]).astype(jnp.int32)
    xs = _gather_rows(h4.reshape(-1, LANES), d, slot_tok, live.reshape(-1), tg)
    ys = _moe_experts(xs, blk_e, blk_nv, moe_w_gu[0], moe_w_down[0], tme,
                      min(MOE_COLS, moe_w_down.shape[2]))
    out = _combine(ys, pos, gates.reshape(b * t, LANES), x3.reshape(b * t, d), mod1,
                   norm_g[1, 3].reshape(1, d), t, tt)
    return out.reshape(b, t, d)
```

```python
import functools
import math

import jax
import jax.numpy as jnp
from jax import lax
from jax.experimental import pallas as pl
from jax.experimental.pallas import tpu as pltpu

F32 = jnp.float32
BF16 = jnp.bfloat16

HEAD = 64
LANES = 128
GRID_W = 64
CHUNK = 64
SUB = 16
SCAN_PAIRS = 8
SCAN_CHUNKS = 4
NORM_EPS = 1e-6
GN_EPS = 64e-5
DECAY_SCALE = math.exp(-0.5)
TOP_K = 2
VMEM_LIMIT = 56 * 1024 * 1024


ROW_TILE = 256
FFN_ROWS = 1024
FFN_COLS = 512
MOE_ROWS = 2560
MOE_COLS = 512


def _cparams(sem):
    return pltpu.CompilerParams(dimension_semantics=sem, vmem_limit_bytes=VMEM_LIMIT)


def _dot(a, b):
    return jnp.dot(a, b, preferred_element_type=F32)


def _dot_nt(a, b):
    return lax.dot_general(a, b, (((1,), (1,)), ((), ())), preferred_element_type=F32)


def _bdot(a, b):
    return _dot(a.astype(BF16), b.astype(BF16))


def _iota(shape, axis):
    return lax.broadcasted_iota(jnp.int32, shape, axis)


def _rms(x, g):
    return x * lax.rsqrt(jnp.mean(x * x, axis=-1, keepdims=True) + NORM_EPS) * g


def _silu(x):
    h = 0.5 * x
    return h * jnp.tanh(h) + h


def _split_bf16(x):
    hi = x.astype(BF16)
    lo = (x - hi.astype(F32)).astype(BF16)
    return hi, lo


def _head_sum(x):
    ones = (_iota((LANES, LANES), 0) // HEAD == _iota((LANES, LANES), 1) // HEAD).astype(BF16)
    xb = x.astype(BF16)
    outs = []
    for p in range(x.shape[-1] // LANES):
        outs.append(_dot(xb[:, p * LANES:(p + 1) * LANES], ones))
    return jnp.concatenate(outs, axis=1)


def _sigmoid(x):
    return 0.5 * jnp.tanh(0.5 * x) + 0.5


def _mod_kernel(c_ref, w_ref, b_ref, o_ref):
    s = _silu(c_ref[...])
    o_ref[0] = jnp.dot(s, w_ref[0], preferred_element_type=F32,
                       precision=lax.Precision.HIGHEST) + b_ref[0]


def _modulation(cs, mod_w, mod_b):
    depth, d, n = mod_w.shape
    rows = cs.shape[0]
    tn = 1536
    return pl.pallas_call(
        _mod_kernel,
        grid=(depth, n // tn),
        in_specs=[pl.BlockSpec((rows, d), lambda i, j: (0, 0)),
                  pl.BlockSpec((1, d, tn), lambda i, j: (i, 0, j)),
                  pl.BlockSpec((1, 1, tn), lambda i, j: (i, 0, j))],
        out_specs=pl.BlockSpec((1, rows, tn), lambda i, j: (i, 0, j)),
        out_shape=jax.ShapeDtypeStruct((depth, rows, n), F32),
        compiler_params=_cparams(("arbitrary", "arbitrary")),
        name="modulation",
    )(cs, mod_w, mod_b.reshape(depth, 1, n))


def _tok_kernel(mode, readout, tT, nT, *refs):
    refs = list(refs)
    x_ref = refs.pop(0)
    if mode == "grid":
        xp_ref = refs.pop(0)
        xn_ref = refs.pop(0)
    (mod_ref, ng_ref, mu_ref, wr_ref, wk_ref, wv_ref, w1_ref, w2_ref, w0_ref,
     a1_ref, a2_ref, a0_ref, kkp_ref, ka_ref) = refs[:14]
    refs = refs[14:]
    if readout:
        g1_ref, g2_ref, rk_ref = refs[:3]
        refs = refs[3:]
    r_o, k_o, v_o, kk_o, ic0_o, ic1_o, lw0_o, lw1_o = refs[:8]
    refs = refs[8:]

    d = x_ref.shape[-1]
    sh = mod_ref[0, :, 0:d]
    sc = mod_ref[0, :, d:2 * d]
    g = ng_ref[...]

    def norm_mod(xx):
        return _rms(xx, g) * (1.0 + sc) + sh

    h = norm_mod(x_ref[0])
    if mode == "grid":
        i = pl.program_id(1)
        q = d // 4
        hp = jnp.where(i > 0, norm_mod(xp_ref[0]), 0.0)
        hn = jnp.where(i < nT - 1, norm_mod(xn_ref[0]), 0.0)
        col = _iota((tT, q), 0) % GRID_W
        left = jnp.where(col == 0, 0.0, pltpu.roll(h[:, 0:q], 1, 0))
        right = jnp.where(col == GRID_W - 1, 0.0, pltpu.roll(h[:, q:2 * q], tT - 1, 0))
        up = jnp.concatenate([hp[:, 2 * q:3 * q], h[:tT - GRID_W, 2 * q:3 * q]], axis=0)
        down = jnp.concatenate([h[GRID_W:, 3 * q:], hn[:, 3 * q:]], axis=0)
        hs = jnp.concatenate([left, right, up, down], axis=1)
    else:
        half = d // 2
        row = _iota((tT, half), 0)
        prev = jnp.where(row == 0, 0.0, pltpu.roll(h[:, :half], 1, 0))
        nxt = jnp.where(row == tT - 1, 0.0, pltpu.roll(h[:, half:], tT - 1, 0))
        hs = jnp.concatenate([prev, nxt], axis=1)

    hb = h.astype(BF16)
    dxb = (hs - h).astype(BF16)

    def mix(n):
        return hb + dxb * mu_ref[n:n + 1, :].astype(BF16)

    r = _dot(mix(0), wr_ref[...])
    k = _dot(mix(2), wk_ref[...])
    v = _dot(mix(3), wv_ref[...])
    w1o = jnp.tanh(_dot(mix(1), w1_ref[...])).astype(BF16)
    a1o = _dot(mix(4), a1_ref[...]).astype(BF16)
    ics = []
    for z, (lw_o, ic_o) in enumerate(((lw0_o, ic0_o), (lw1_o, ic1_o))):
        w_pre = _dot(w1o, w2_ref[z]) + w0_ref[z:z + 1, :]
        lw_o[0] = (-0.5 * DECAY_SCALE) * jnp.tanh(0.5 * w_pre) - 0.5 * DECAY_SCALE
        ic = _sigmoid(_dot(a1o, a2_ref[z]) + a0_ref[z:z + 1, :])
        ic_o[0] = ic.astype(BF16)
        ics.append(ic)
    kk = k * kkp_ref[...]
    kk = kk * lax.rsqrt(jnp.maximum(_head_sum(kk * kk), 1e-24))
    r_o[0] = r.astype(BF16)
    k_o[0] = k.astype(BF16)
    v_o[0] = v.astype(BF16)
    kk_o[0] = kk.astype(BF16)
    if readout:
        gate_o, bon_o = refs
        gate = _dot(_sigmoid(_dot(mix(5), g1_ref[...])).astype(BF16), g2_ref[...])
        gate_o[0] = gate.astype(BF16)
        k_avg = k * (1.0 + (0.5 * (ics[0] + ics[1]) - 1.0) * ka_ref[...])
        bon_o[0] = (_head_sum(r * k_avg * rk_ref[...]) * v).astype(BF16)


def _tokenwise(x, mod, mode, readout, p, tT):
    b, t, d = x.shape
    nT = t // tT
    row_spec = pl.BlockSpec((1, tT, d), lambda bi, i: (bi, i, 0))
    in_specs = [row_spec]
    args = [x]
    if mode == "grid":
        hb = tT // GRID_W
        in_specs += [
            pl.BlockSpec((1, GRID_W, d), lambda bi, i: (bi, jnp.maximum(i * hb - 1, 0), 0)),
            pl.BlockSpec((1, GRID_W, d), lambda bi, i: (bi, jnp.minimum((i + 1) * hb, t // GRID_W - 1), 0)),
        ]
        args += [x, x]
    if mod.shape[0] == 1:
        in_specs.append(pl.BlockSpec((1, 1, mod.shape[-1]), lambda bi, i: (0, 0, 0)))
    else:
        in_specs.append(pl.BlockSpec((1, 1, mod.shape[-1]), lambda bi, i: (bi, 0, 0)))
    args.append(mod)
    consts = [p["ng"], p["mu"], p["wr"], p["wk"], p["wv"], p["w1"], p["w2"], p["w0"],
              p["a1"], p["a2"], p["a0"], p["k_k"], p["k_a"]]
    if readout:
        consts += [p["g1"], p["g2"], p["r_k"]]
    for a in consts:
        in_specs.append(pl.BlockSpec(a.shape, lambda bi, i, _n=a.ndim: (0,) * _n))
        args.append(a)
    dtypes = [BF16] * 6 + [F32] * 2 + ([BF16] * 2 if readout else [])
    return pl.pallas_call(
        functools.partial(_tok_kernel, mode, readout, tT, nT),
        grid=(b, nT),
        in_specs=in_specs,
        out_specs=[row_spec] * len(dtypes),
        out_shape=[jax.ShapeDtypeStruct((b, t, d), dt) for dt in dtypes],
        compiler_params=_cparams(("arbitrary", "arbitrary")),
        name="rwkv_tokenwise_" + mode,
    )(*args)


def _scan_prep(r, kd, v, lw, kk, ic, reverse):
    c = r.shape[0]
    ti = _iota((c, c), 0)
    si = _iota((c, c), 1)
    tri = ((si >= ti) if reverse else (si <= ti)).astype(BF16)
    lw_hi, lw_lo = _split_bf16(lw)
    cum = _dot(tri, lw_hi) + _dot(tri, lw_lo)
    pc_log = cum[0:1] if reverse else cum[c - 1:c]
    p_inv = jnp.exp(-cum)
    p_rest = jnp.exp(pc_log - cum)
    b = kk * ic
    return {"rt": r * jnp.exp(cum), "at": -(kk * jnp.exp(cum - lw)),
            "bt": b * p_inv, "kt": kd * p_inv, "bp": b * p_rest, "kp": kd * p_rest,
            "v": v, "pc": jnp.exp(pc_log)}


def _scan_step(prep, states):
    c = prep[0]["v"].shape[0]
    n = 2 * c
    pp = len(states[0])
    chains = [(dr, p) for dr in range(2) for p in range(pp)]
    first = _iota((c, LANES), 1) < HEAD
    row = _iota((n, n), 0)
    col = _iota((n, n), 1)
    strict = (col < row, col > row)
    same = (row // SUB) == (col // SUB)

    def pair(name, dr, p):
        return prep[dr][name][:, p * LANES:(p + 1) * LANES]

    def stack(x):
        return jnp.concatenate([jnp.where(first, x, 0.0), jnp.where(first, 0.0, x)], axis=0)

    def ms(name, dr, p):
        return stack(pair(name, dr, p))

    def each(fn, *lists):
        return [fn(*xs) for xs in zip(*lists)]

    lhs = [jnp.concatenate([pair("at", dr, p), pair("rt", dr, p)], axis=0).astype(BF16)
           for dr, p in chains]
    rhs = [jnp.concatenate([ms("bt", dr, p), ms("kt", dr, p)], axis=0).astype(BF16) for dr, p in chains]
    at_ms = [ms("at", dr, p).astype(BF16) for dr, p in chains]
    v_ms = [ms("v", dr, p).astype(BF16) for dr, p in chains]
    gmat = each(_dot_nt, lhs, rhs)
    own = (row // c) == (col // c)
    tri = [own & m for m in strict]
    tri_d = [m & same for m in tri]
    tri_o = [m & jnp.logical_not(same) for m in tri]

    def twice(x):
        return jnp.concatenate([x, x], axis=0)

    nd = [jnp.where(tri_d[dr], twice(g[:c, :n]), 0.0) for (dr, _), g in zip(chains, gmat)]
    no = [jnp.where(tri_o[dr], twice(g[:c, :n]), 0.0) for (dr, _), g in zip(chains, gmat)]
    a_ak = [jnp.where(tri[dr], twice(g[:c, n:]), 0.0).astype(BF16) for (dr, _), g in zip(chains, gmat)]
    t_pl = _iota((c, 2 * n), 0)
    s_pl = _iota((c, 2 * n), 1) % c
    incl_pl = (s_pl <= t_pl, s_pl >= t_pl)
    a_r = [jnp.where(incl_pl[dr], g[c:], 0.0).astype(BF16)
           for (dr, _), g in zip(chains, gmat)]

    pw = nd
    tp = nd
    span = 2
    while span < SUB:
        pw = each(_bdot, pw, pw)
        tp = each(lambda t, q: t + q + _bdot(t, q), tp, pw)
        span *= 2
    m = each(lambda t, o: o + _bdot(t, o), tp, no)
    qp = m
    pw = m
    span = 2
    while span < c // SUB:
        pw = each(_bdot, pw, pw)
        qp = each(lambda t, q: t + q + _bdot(t, q), qp, pw)
        span *= 2
    tp = each(lambda t, q: t + q + _bdot(q, t), tp, qp)

    bk_t = [jnp.concatenate([ms("bp", dr, p), ms("kp", dr, p)], axis=0).T.astype(BF16)
            for dr, p in chains]
    diag = row == col
    pc_col = [jnp.sum(jnp.where(diag, prep[dr]["pc"][:, p * LANES:(p + 1) * LANES], 0.0),
                      axis=1, keepdims=True) for dr, p in chains]

    s_t = [states[dr][p].astype(BF16) for dr, p in chains]
    rhs_z = each(lambda l, a, st, v: _dot(jnp.concatenate([l, a], axis=1),
                                          jnp.concatenate([st, v], axis=0)), at_ms, a_ak, s_t, v_ms)
    z = each(lambda t, x: x + _bdot(t, x), tp, rhs_z)
    zv = each(lambda zz, v: jnp.concatenate([zz.astype(BF16), v], axis=0), z, v_ms)
    y_pl = each(lambda l, st, ar, w: _dot(l[c:], st) + _dot(ar, w), lhs, s_t, a_r, zv)
    y = [jnp.concatenate([yy for (d2, _), yy in zip(chains, y_pl) if d2 == dr], axis=1)
         for dr in range(2)]
    new_states = [[None] * pp for _ in range(2)]
    for (dr, p), bt_, w, pc in zip(chains, bk_t, zv, pc_col):
        new_states[dr][p] = states[dr][p] * pc + _dot(bt_, w)
    return y, new_states


def _scan_kernel(pp, nc, has_s0, want_y, want_state, *refs):
    refs = list(refs)
    ka_ref = refs[12]
    dirs = (refs[:6], refs[6:12])
    refs = refs[13:]
    if has_s0:
        s0_ref = refs.pop(0)
    if want_y:
        y_refs = (refs.pop(0), refs.pop(0))
    if want_state:
        st_ref = refs.pop(0)
    s_scr = refs.pop(0)
    c = pl.program_id(2)

    @pl.when(c == 0)
    def _():
        if has_s0:
            s_scr[...] = s0_ref[0]
        else:
            s_scr[...] = jnp.zeros_like(s_scr)

    per_step = dirs[0][0].shape[1] // CHUNK
    states = [[s_scr[dr, p] for p in range(pp)] for dr in range(2)]
    for j in range(per_step):
        prep = []
        rows = []
        for dr, (r_ref, k_ref, v_ref, kk_ref, ic_ref, lw_ref) in enumerate(dirs):
            q = j if dr == 0 else per_step - 1 - j
            sl = slice(q * CHUNK, (q + 1) * CHUNK)
            rows.append(sl)
            ic = ic_ref[0, sl, :].astype(F32)
            kd = k_ref[0, sl, :].astype(F32) * (1.0 + (ic - 1.0) * ka_ref[...])
            prep.append(_scan_prep(r_ref[0, sl, :].astype(F32), kd, v_ref[0, sl, :].astype(F32),
                                   lw_ref[0, sl, :], kk_ref[0, sl, :].astype(F32), ic,
                                   reverse=(dr == 1)))
        y, states = _scan_step(prep, states)
        if want_y:
            for dr in range(2):
                y_refs[dr][0, rows[dr], :] = y[dr].astype(BF16)
    for dr in range(2):
        for p in range(pp):
            s_scr[dr, p] = states[dr][p]

    if want_state:
        @pl.when(c == nc - 1)
        def _():
            st_ref[0] = s_scr[...]


def _scan(tok, ka, s0, want_y, want_state):
    r, k, v, kk, ic0, ic1, lw0, lw1 = tok
    b, t, d = r.shape
    rows = min(CHUNK * SCAN_CHUNKS, t)
    nc = t // rows
    npair = d // LANES
    pp = min(SCAN_PAIRS, npair)
    w = LANES * pp
    f_spec = pl.BlockSpec((1, rows, w), lambda bi, hi, ci: (bi, ci, hi))
    b_spec = pl.BlockSpec((1, rows, w), lambda bi, hi, ci: (bi, nc - 1 - ci, hi))
    st_spec = pl.BlockSpec((1, 2, pp, LANES, LANES), lambda bi, hi, ci: (bi, 0, hi, 0, 0))
    in_specs = [f_spec] * 6 + [b_spec] * 6 + [pl.BlockSpec((1, w), lambda bi, hi, ci: (0, hi))]
    args = [r, k, v, kk, ic0, lw0, r, k, v, kk, ic1, lw1, ka]
    if s0 is not None:
        in_specs.append(st_spec)
        args.append(s0)
    out_specs, out_shape = [], []
    if want_y:
        out_specs += [f_spec, b_spec]
        out_shape += [jax.ShapeDtypeStruct((b, t, d), BF16)] * 2
    if want_state:
        out_specs.append(st_spec)
        out_shape.append(jax.ShapeDtypeStruct((b, 2, npair, LANES, LANES), F32))
    return pl.pallas_call(
        functools.partial(_scan_kernel, pp, nc, s0 is not None, want_y, want_state),
        grid=(b, npair // pp, nc),
        in_specs=in_specs,
        out_specs=out_specs,
        out_shape=out_shape,
        scratch_shapes=[pltpu.VMEM((2, pp, LANES, LANES), F32)],
        compiler_params=_cparams(("arbitrary", "arbitrary", "arbitrary")),
        name="rwkv_scan_ctx" if s0 is None else "rwkv_scan_latent",
    )(*args)


def _readout_kernel(yf_ref, yb_ref, bon_ref, gate_ref, x_ref, mod_ref, lnw_ref, lnb_ref,
                    wo_ref, g1_ref, g2_ref, x_o, h_o):
    d = x_ref.shape[-1]
    y = yf_ref[0].astype(F32) + yb_ref[0].astype(F32)
    mean = _head_sum(y) * (1.0 / HEAD)
    yc = y - mean
    var = _head_sum(yc * yc) * (1.0 / HEAD)
    o = yc * lax.rsqrt(var + GN_EPS) * lnw_ref[...] + lnb_ref[...] + bon_ref[0]
    att = _dot((o * gate_ref[0]).astype(BF16), wo_ref[...])
    gt_a = mod_ref[0, :, 2 * d:3 * d]
    sh_f = mod_ref[0, :, 3 * d:4 * d]
    sc_f = mod_ref[0, :, 4 * d:5 * d]
    x1 = x_ref[0] + gt_a * _rms(att, g1_ref[...])
    x_o[0] = x1
    h_o[0] = (_rms(x1, g2_ref[...]) * (1.0 + sc_f) + sh_f).astype(BF16)


def _readout(yf, yb, bon, gate, x, mod, lnw, lnb, wo, g1, g2, tT):
    b, t, d = x.shape
    row_spec = pl.BlockSpec((1, tT, d), lambda bi, i: (bi, i, 0))
    vec = pl.BlockSpec((1, d), lambda bi, i: (0, 0))
    return pl.pallas_call(
        _readout_kernel,
        grid=(b, t // tT),
        in_specs=[row_spec] * 5 + [pl.BlockSpec((1, 1, 6 * d), lambda bi, i: (bi, 0, 0)),
                                   vec, vec, pl.BlockSpec((d, d), lambda bi, i: (0, 0)), vec, vec],
        out_specs=[row_spec, row_spec],
        out_shape=[jax.ShapeDtypeStruct((b, t, d), F32), jax.ShapeDtypeStruct((b, t, d), BF16)],
        compiler_params=_cparams(("arbitrary", "arbitrary")),
        name="rwkv_readout",
    )(yf, yb, bon, gate, x, mod, lnw, lnb, wo, g1, g2)


def _ffn_kernel(nf, h_ref, wg_ref, wu_ref, wd_ref, x_ref, mod_ref, g3_ref, gn_ref, mod2_ref,
                x_o, h_o, acc):
    f = pl.program_id(1)
    d = x_ref.shape[-1]

    @pl.when(f == 0)
    def _():
        acc[...] = jnp.zeros_like(acc)

    h = h_ref[...]
    act = _silu(_dot(h, wg_ref[...].astype(BF16))) * _dot(h, wu_ref[...].astype(BF16))
    acc[...] += _dot(act.astype(BF16), wd_ref[...].astype(BF16))

    @pl.when(f == nf - 1)
    def _():
        gt_f = mod_ref[0, :, 5 * d:6 * d]
        x2 = x_ref[...] + gt_f * _rms(acc[...], g3_ref[...])
        x_o[...] = x2
        sh = mod2_ref[0, :, 0:d]
        sc = mod2_ref[0, :, d:2 * d]
        h_o[...] = (_rms(x2, gn_ref[...]) * (1.0 + sc) + sh).astype(BF16)


def _ffn(h, w_gu, w_down, x, mod, g3, gn, mod2, rows_per_batch, tm, tf):
    n, d = x.shape
    ff = w_down.shape[0]
    nf = ff // tf
    per = rows_per_batch // tm
    row = pl.BlockSpec((tm, d), lambda i, f: (i, 0))
    vec = pl.BlockSpec((1, d), lambda i, f: (0, 0))
    modspec = pl.BlockSpec((1, 1, 6 * d), lambda i, f: (i // per, 0, 0))
    return pl.pallas_call(
        functools.partial(_ffn_kernel, nf),
        grid=(n // tm, nf),
        in_specs=[row,
                  pl.BlockSpec((d, tf), lambda i, f: (0, f)),
                  pl.BlockSpec((d, tf), lambda i, f: (0, nf + f)),
                  pl.BlockSpec((tf, d), lambda i, f: (f, 0)),
                  row, modspec, vec, vec, modspec],
        out_specs=[row, row],
        out_shape=[jax.ShapeDtypeStruct((n, d), F32), jax.ShapeDtypeStruct((n, d), BF16)],
        scratch_shapes=[pltpu.VMEM((tm, d), F32)],
        compiler_params=_cparams(("arbitrary", "arbitrary")),
        name="dense_swiglu",
    )(h, w_gu, w_gu, w_down, x, mod, g3, gn, mod2)


HALO = 16


def _conv_kernel(tT, nT, n_experts, h_ref, hp_ref, hn_ref, x_ref, mod_ref, win_ref, cw_ref, wo_ref,
                 g1_ref, g2_ref, wr_ref, x_o, h_o, te_o, gt_o):
    i = pl.program_id(1)
    d = x_ref.shape[-1]
    n = tT + 2 * HALO
    h_ext = jnp.concatenate([hp_ref[0], h_ref[0], hn_ref[0]], axis=0)
    proj = _dot(h_ext, win_ref[...])
    z = proj[:, d:2 * d] * proj[:, 2 * d:]
    row = _iota((n, d), 0)
    dead = ((row < HALO) & (i == 0)) | ((row >= HALO + tT) & (i == nT - 1))
    z = jnp.where(dead, 0.0, z)
    conv = (pltpu.roll(z, 1, 0) * cw_ref[0:1, :] + z * cw_ref[1:2, :]
            + pltpu.roll(z, n - 1, 0) * cw_ref[2:3, :])
    gated = (proj[HALO:HALO + tT, 0:d] * conv[HALO:HALO + tT]).astype(BF16)
    y = _dot(gated, wo_ref[...])
    gt_a = mod_ref[0, :, 2 * d:3 * d]
    sh_f = mod_ref[0, :, 3 * d:4 * d]
    sc_f = mod_ref[0, :, 4 * d:5 * d]
    x3 = x_ref[0] + gt_a * _rms(y, g1_ref[...])
    x_o[0] = x3
    h4 = _rms(x3, g2_ref[...]) * (1.0 + sc_f) + sh_f
    s_rows = d // LANES
    for c in range(s_rows):
        h_o[0, pl.ds(c, tT, stride=s_rows), :] = h4[:, c * LANES:(c + 1) * LANES]
    h_hi, h_lo = _split_bf16(h4)
    lg = _dot(h_hi, wr_ref[0]) + _dot(h_lo, wr_ref[0]) + _dot(h_hi, wr_ref[1])
    lg_t = lg.T
    lg8 = lg_t[:n_experts]
    eidx = _iota(lg8.shape, 0)
    m1 = jnp.max(lg8, axis=0, keepdims=True)
    i1 = jnp.min(jnp.where(lg8 == m1, eidx, n_experts), axis=0, keepdims=True)
    rest = jnp.where(eidx == i1, -jnp.inf, lg8)
    m2 = jnp.max(rest, axis=0, keepdims=True)
    i2 = jnp.min(jnp.where(rest == m2, eidx, n_experts), axis=0, keepdims=True)
    tw = jnp.exp(m2 - m1)
    te_o[0] = jnp.concatenate([i1, i2], axis=0)
    gates = jnp.concatenate([1.0 / (1.0 + tw), tw / (1.0 + tw),
                             jnp.zeros((LANES - TOP_K, tT), F32)], axis=0)
    gt_o[0] = gates.T


def _conv_layer(h, x, mod, w_in, conv_w, w_out, g1, g2, w_router, n_experts, tT):
    b, t, d = x.shape
    nT = t // tT
    hb = tT // HALO
    row_spec = pl.BlockSpec((1, tT, d), lambda bi, i: (bi, i, 0))
    vec = pl.BlockSpec((1, d), lambda bi, i: (0, 0))
    full = lambda a: pl.BlockSpec(a.shape, lambda bi, i, _n=a.ndim: (0,) * _n)
    return pl.pallas_call(
        functools.partial(_conv_kernel, tT, nT, n_experts),
        grid=(b, nT),
        in_specs=[row_spec,
                  pl.BlockSpec((1, HALO, d), lambda bi, i: (bi, jnp.maximum(i * hb - 1, 0), 0)),
                  pl.BlockSpec((1, HALO, d), lambda bi, i: (bi, jnp.minimum((i + 1) * hb, t // HALO - 1), 0)),
                  row_spec, pl.BlockSpec((1, 1, 6 * d), lambda bi, i: (bi, 0, 0)),
                  full(w_in), full(conv_w), full(w_out), vec, vec, full(w_router)],
        out_specs=[row_spec, pl.BlockSpec((1, tT * (d // LANES), LANES), lambda bi, i: (bi, i, 0)),
                   pl.BlockSpec((1, TOP_K, tT), lambda bi, i: (bi, 0, i)),
                   pl.BlockSpec((1, tT, LANES), lambda bi, i: (bi, i, 0))],
        out_shape=[jax.ShapeDtypeStruct((b, t, d), F32),
                   jax.ShapeDtypeStruct((b, t * (d // LANES), LANES), F32),
                   jax.ShapeDtypeStruct((b, TOP_K, t), jnp.int32),
                   jax.ShapeDtypeStruct((b, t, LANES), F32)],
        compiler_params=_cparams(("arbitrary", "arbitrary")),
        name="short_conv",
    )(h, h, h, x, mod, w_in, conv_w, w_out, g1, g2, w_router)


GATHER_ROWS = 512


def _gather_kernel(tg, nsteps, live_ref, tok_cur, tok_nxt, h_hbm, o_ref, buf, sem):
    j = pl.program_id(0)
    slot = j % 2
    s_rows = buf.shape[1] // tg

    def issue(tok_ref, s):
        for r in range(tg):
            src = pl.multiple_of(tok_ref[0, 0, r] * s_rows, s_rows)
            pltpu.make_async_copy(h_hbm.at[pl.ds(src, s_rows)],
                                  buf.at[s, pl.ds(r * s_rows, s_rows)], sem.at[s]).start(priority=r % 2)

    @pl.when((j == 0) & (live_ref[0] > 0))
    def _():
        issue(tok_cur, 0)

    @pl.when(j + 1 < nsteps)
    def _():
        @pl.when(live_ref[j + 1] > 0)
        def _():
            issue(tok_nxt, 1 - slot)

    @pl.when(live_ref[j] > 0)
    def _():
        pltpu.make_async_copy(h_hbm.at[pl.ds(0, tg * s_rows)], buf.at[slot], sem.at[slot]).wait()
        for c in range(s_rows):
            o_ref[:, c * LANES:(c + 1) * LANES] = buf[slot, pl.ds(c, tg, stride=s_rows), :].astype(BF16)

    @pl.when(live_ref[j] == 0)
    def _():
        o_ref[...] = jnp.zeros(o_ref.shape, BF16)


def _gather_rows(h, d, slot_tok, live, tg):
    s_rows = d // LANES
    nsteps = slot_tok.shape[0] // tg
    tok3 = slot_tok.reshape(nsteps, 1, tg)
    grid_spec = pltpu.PrefetchScalarGridSpec(
        num_scalar_prefetch=1,
        grid=(nsteps,),
        in_specs=[pl.BlockSpec((1, 1, tg), lambda j, lv: (j, 0, 0), memory_space=pltpu.SMEM),
                  pl.BlockSpec((1, 1, tg), lambda j, lv: (jnp.minimum(j + 1, nsteps - 1), 0, 0),
                               memory_space=pltpu.SMEM),
                  pl.BlockSpec(memory_space=pl.ANY)],
        out_specs=pl.BlockSpec((tg, d), lambda j, lv: (j, 0)),
        scratch_shapes=[pltpu.VMEM((2, tg * s_rows, LANES), F32), pltpu.SemaphoreType.DMA((2,))],
    )
    return pl.pallas_call(
        functools.partial(_gather_kernel, tg, nsteps),
        grid_spec=grid_spec,
        out_shape=jax.ShapeDtypeStruct((nsteps * tg, d), BF16),
        compiler_params=_cparams(("arbitrary",)),
        name="moe_dispatch",
    )(live, tok3, tok3, h)


MOE_SUB = 512


def _moe_kernel(nf, tm, e_ref, nv_ref, x_ref, wg_ref, wu_ref, wd_ref, y_o):
    i = pl.program_id(0)
    f = pl.program_id(1)
    nv = nv_ref[i]

    nsub = tm // MOE_SUB

    @pl.when(f == 0)
    def _():
        y_o[...] = jnp.zeros(y_o.shape, F32)

    def weights():
        return wg_ref[0].astype(BF16), wu_ref[0].astype(BF16), wd_ref[0].astype(BF16)

    def sub_block(s, wg, wu, wd):
        rows = slice(s * MOE_SUB, (s + 1) * MOE_SUB)
        h = x_ref[rows, :]
        act = _silu(_dot(h, wg)) * _dot(h, wu)
        y_o[rows, :] += _dot(act.astype(BF16), wd)

    for live in range(1, nsub + 1):
        @pl.when((nv > (live - 1) * MOE_SUB) & (nv <= live * MOE_SUB))
        def _():
            w = weights()
            for s in range(live):
                sub_block(s, *w)


def _moe_experts(xs, blk_e, blk_nv, w_gu, w_down, tm, tf):
    n, d = xs.shape
    ff = w_down.shape[1]
    nf = ff // tf
    nblk = n // tm

    def f_eff(i, f, nv_ref):
        return jnp.where(nv_ref[i] > 0, f, nf - 1)

    grid_spec = pltpu.PrefetchScalarGridSpec(
        num_scalar_prefetch=2,
        grid=(nblk, nf),
        in_specs=[pl.BlockSpec((tm, d), lambda i, f, e, nv: (i, 0)),
                  pl.BlockSpec((1, d, tf), lambda i, f, e, nv: (e[i], 0, f_eff(i, f, nv))),
                  pl.BlockSpec((1, d, tf), lambda i, f, e, nv: (e[i], 0, nf + f_eff(i, f, nv))),
                  pl.BlockSpec((1, tf, d), lambda i, f, e, nv: (e[i], f_eff(i, f, nv), 0))],
        out_specs=pl.BlockSpec((tm, d), lambda i, f, e, nv: (i, 0)),
    )
    return pl.pallas_call(
        functools.partial(_moe_kernel, nf, tm),
        grid_spec=grid_spec,
        out_shape=jax.ShapeDtypeStruct((n, d), F32),
        compiler_params=_cparams(("arbitrary", "arbitrary")),
        name="moe_experts",
    )(blk_e, blk_nv, xs, w_gu, w_gu, w_down)


def _combine_kernel(tT, nsteps, pos_cur, pos_nxt, ys_hbm, gt_ref, x_ref, mod_ref, g_ref, o_ref,
                    buf, sem):
    j = pl.program_id(0)
    slot = j % 2
    d = x_ref.shape[-1]

    def issue(pos_ref, s):
        for r in range(tT):
            for k in range(TOP_K):
                pltpu.make_async_copy(ys_hbm.at[pl.ds(pos_ref[0, k, r], 1)],
                                      buf.at[s, k, pl.ds(r, 1)], sem.at[s]).start(priority=k)

    @pl.when(j == 0)
    def _():
        issue(pos_cur, 0)

    @pl.when(j + 1 < nsteps)
    def _():
        issue(pos_nxt, 1 - slot)

    for k in range(TOP_K):
        pltpu.make_async_copy(ys_hbm.at[pl.ds(0, tT)], buf.at[slot, k], sem.at[slot]).wait()
    gates = gt_ref[...]
    y = buf[slot, 0] * gates[:, 0:1] + buf[slot, 1] * gates[:, 1:2]
    gt_f = mod_ref[0, :, 5 * d:6 * d]
    o_ref[...] = x_ref[...] + gt_f * _rms(y, g_ref[...])


def _combine(ys, pos, gates, x, mod, g, rows_per_batch, tT):
    n, d = x.shape
    nsteps = n // tT
    per = rows_per_batch // tT
    pos3 = pos.reshape(TOP_K, nsteps, tT).transpose(1, 0, 2)
    row_spec = pl.BlockSpec((tT, d), lambda j: (j, 0))
    smem = functools.partial(pl.BlockSpec, (1, TOP_K, tT), memory_space=pltpu.SMEM)
    return pl.pallas_call(
        functools.partial(_combine_kernel, tT, nsteps),
        grid=(nsteps,),
        in_specs=[smem(lambda j: (j, 0, 0)),
                  smem(lambda j: (jnp.minimum(j + 1, nsteps - 1), 0, 0)),
                  pl.BlockSpec(memory_space=pl.ANY),
                  pl.BlockSpec((tT, LANES), lambda j: (j, 0)),
                  row_spec, pl.BlockSpec((1, 1, 6 * d), lambda j: (j // per, 0, 0)),
                  pl.BlockSpec((1, d), lambda j: (0, 0))],
        out_specs=row_spec,
        out_shape=jax.ShapeDtypeStruct((n, d), F32),
        scratch_shapes=[pltpu.VMEM((2, TOP_K, tT, d), F32), pltpu.SemaphoreType.DMA((2,))],
        compiler_params=_cparams(("arbitrary",)),
        name="moe_combine",
    )(pos3, pos3, ys, gates, x, mod, g)


def _route(top_e, n_experts, tm):
    n = top_e.shape[1]
    nk = n * TOP_K
    i1, i2 = top_e[0], top_e[1]
    idx = jnp.arange(n_experts, dtype=jnp.int32)[:, None]
    oh = [(idx == i[None]).astype(jnp.int32) for i in (i1, i2)]
    cs = [jnp.cumsum(o, axis=1) for o in oh]
    first = cs[0][:, -1]
    counts = first + cs[1][:, -1]
    nblk_e = (counts + tm - 1) // tm
    start = jnp.cumsum(counts) - counts
    bend = jnp.cumsum(nblk_e)
    bstart = bend - nblk_e
    base = (bstart * tm)[:, None]
    pos = jnp.stack([jnp.sum(oh[0] * (base + cs[0] - oh[0]), axis=0),
                     jnp.sum(oh[1] * (base + first[:, None] + cs[1] - oh[1]), axis=0)])
    flat_e = jnp.concatenate([i1, i2])
    nblk = -(-nk // tm) + n_experts
    blk = jnp.arange(nblk, dtype=jnp.int32)
    blk_e = jnp.minimum(jnp.searchsorted(bend, blk, side="right"), n_experts - 1).astype(jnp.int32)
    blk_nv = jnp.clip(counts[blk_e] - (blk - bstart[blk_e]) * tm, 0, tm)
    blk_nv = jnp.where(blk < bend[-1], blk_nv, 0).astype(jnp.int32)
    order = jnp.argsort(flat_e, stable=True).astype(jnp.int32)
    order = jnp.concatenate([order, jnp.zeros((tm,), jnp.int32)])
    src0 = jnp.where(blk_nv > 0, start[blk_e] + (blk - bstart[blk_e]) * tm, 0)
    slot_pair = jnp.concatenate([lax.dynamic_slice_in_dim(order, src0[k], tm) for k in range(nblk)])
    slot_tok = jnp.where(slot_pair >= n, slot_pair - n, slot_pair)
    return slot_tok, pos, blk_e, blk_nv


def kernel(x, c, ctx, c_ctx, mod_w, mod_b, norm_g, rwkv_mu, rwkv_w_rkv, rwkv_w0, rwkv_w1, rwkv_w2,
           rwkv_a0, rwkv_a1, rwkv_a2, rwkv_g1, rwkv_g2, rwkv_k_k, rwkv_k_a, rwkv_r_k, rwkv_ln_w,
           rwkv_ln_b, rwkv_w_out, conv_w_in, conv_w, conv_w_out, ffn_w_gu, ffn_w_down,
           moe_router, moe_w_gu, moe_w_down):
    b, t, d = x.shape
    n_experts = moe_router.shape[-1]
    rows = 16
    cs = jnp.zeros((rows, d), F32).at[:b].set(c).at[b].set(c_ctx)
    mods = _modulation(cs, mod_w, mod_b)
    mod0 = mods[0, :b].reshape(b, 1, 6 * d)
    mod0c = mods[0, b].reshape(1, 1, 6 * d)
    mod1 = mods[1, :b].reshape(b, 1, 6 * d)

    def pad_lora(w):
        zr = jnp.zeros_like(w[0])
        return jnp.stack([jnp.concatenate([w[0], zr], 0), jnp.concatenate([zr, w[1]], 0)]).astype(BF16)

    lg = rwkv_g1.shape[-1]
    lgp = -(-lg // LANES) * LANES
    p = {
        "ng": norm_g[0, 0].reshape(1, d),
        "mu": rwkv_mu[0],
        "wr": rwkv_w_rkv[0, 0].astype(BF16), "wk": rwkv_w_rkv[0, 1].astype(BF16),
        "wv": rwkv_w_rkv[0, 2].astype(BF16),
        "w1": jnp.concatenate([rwkv_w1[0, 0], rwkv_w1[0, 1]], axis=1).astype(BF16),
        "w2": pad_lora(rwkv_w2[0]), "w0": rwkv_w0[0],
        "a1": jnp.concatenate([rwkv_a1[0, 0], rwkv_a1[0, 1]], axis=1).astype(BF16),
        "a2": pad_lora(rwkv_a2[0]), "a0": rwkv_a0[0],
        "k_k": rwkv_k_k[0].reshape(1, d), "k_a": rwkv_k_a[0].reshape(1, d),
        "g1": jnp.pad(rwkv_g1[0], ((0, 0), (0, lgp - lg))).astype(BF16),
        "g2": jnp.pad(rwkv_g2[0], ((0, lgp - lg), (0, 0))).astype(BF16),
        "r_k": rwkv_r_k[0].reshape(1, d),
    }

    tok_c = _tokenwise(ctx, mod0c, "seq", False, p, ctx.shape[1])
    s_ctx = _scan(tok_c, p["k_a"], None, want_y=False, want_state=True)[0]
    tt = min(ROW_TILE, t)
    tm = min(FFN_ROWS, t)
    tf = min(FFN_COLS, ffn_w_down.shape[1])
    tok_l = _tokenwise(x, mod0, "grid", True, p, min(2 * ROW_TILE, t))
    yf, yb = _scan(tok_l[:8], p["k_a"], s_ctx, want_y=True, want_state=False)
    x1, h2 = _readout(yf, yb, tok_l[9], tok_l[8], x, mod0,
                      rwkv_ln_w[0].reshape(1, d), rwkv_ln_b[0].reshape(1, d),
                      rwkv_w_out[0].astype(BF16), norm_g[0, 1].reshape(1, d),
                      norm_g[0, 2].reshape(1, d), min(4 * ROW_TILE, t))

    x2, h3 = _ffn(h2.reshape(b * t, d), ffn_w_gu[0], ffn_w_down[0],
                  x1.reshape(b * t, d), mod0, norm_g[0, 3].reshape(1, d),
                  norm_g[1, 0].reshape(1, d), mod1, t, tm, tf)

    w_router = jnp.stack(_split_bf16(jnp.pad(moe_router[0], ((0, 0), (0, LANES - n_experts)))))
    x3, h4, top_e, gates = _conv_layer(h3.reshape(b, t, d), x2.reshape(b, t, d), mod1,
                                       conv_w_in[0].astype(BF16), conv_w[0], conv_w_out[0].astype(BF16),
                                       norm_g[1, 1].reshape(1, d), norm_g[1, 2].reshape(1, d),
                                       w_router, n_experts, tt)

    tme = min(MOE_ROWS, b * t * TOP_K)
    slot_tok, pos, blk_e, blk_nv = _route(top_e.transpose(1, 0, 2).reshape(TOP_K, b * t),
                                          n_experts, tme)
    tg = min(GATHER_ROWS, tme)
    live = (blk_nv[:, None] > jnp.arange(0, tme, tg, dtype=jnp.int32)[None, :]).astype(jnp.int32)
    xs = _gather_rows(h4.reshape(-1, LANES), d, slot_tok, live.reshape(-1), tg)
    ys = _moe_experts(xs, blk_e, blk_nv, moe_w_gu[0], moe_w_down[0], tme,
                      min(MOE_COLS, moe_w_down.shape[2]))
    out = _combine(ys, pos, gates.reshape(b * t, LANES), x3.reshape(b * t, d), mod1,
                   norm_g[1, 3].reshape(1, d), t, tt)
    return out.reshape(b, t, d)
```

```python
import functools
import math

import jax
import jax.numpy as jnp
from jax import lax
from jax.experimental import pallas as pl
from jax.experimental.pallas import tpu as pltpu

F32 = jnp.float32
BF16 = jnp.bfloat16

HEAD = 64
LANES = 128
GRID_W = 64
CHUNK = 64
SUB = 16
SCAN_PAIRS = 8
SCAN_CHUNKS = 4
NORM_EPS = 1e-6
GN_EPS = 64e-5
DECAY_SCALE = math.exp(-0.5)
TOP_K = 2
VMEM_LIMIT = 56 * 1024 * 1024


ROW_TILE = 256
FFN_ROWS = 1024
FFN_COLS = 512
MOE_ROWS = 2560
MOE_COLS = 512


def _cparams(sem):
    return pltpu.CompilerParams(dimension_semantics=sem, vmem_limit_bytes=VMEM_LIMIT)


def _dot(a, b):
    return jnp.dot(a, b, preferred_element_type=F32)


def _dot_nt(a, b):
    return lax.dot_general(a, b, (((1,), (1,)), ((), ())), preferred_element_type=F32)


def _bdot(a, b):
    return _dot(a.astype(BF16), b.astype(BF16))


def _iota(shape, axis):
    return lax.broadcasted_iota(jnp.int32, shape, axis)


def _rms(x, g):
    return x * lax.rsqrt(jnp.mean(x * x, axis=-1, keepdims=True) + NORM_EPS) * g


def _silu(x):
    h = 0.5 * x
    return h * jnp.tanh(h) + h


def _split_bf16(x):
    hi = x.astype(BF16)
    lo = (x - hi.astype(F32)).astype(BF16)
    return hi, lo


def _head_sum(x):
    ones = (_iota((LANES, LANES), 0) // HEAD == _iota((LANES, LANES), 1) // HEAD).astype(BF16)
    xb = x.astype(BF16)
    outs = []
    for p in range(x.shape[-1] // LANES):
        outs.append(_dot(xb[:, p * LANES:(p + 1) * LANES], ones))
    return jnp.concatenate(outs, axis=1)


def _sigmoid(x):
    return 0.5 * jnp.tanh(0.5 * x) + 0.5


def _mod_kernel(c_ref, w_ref, b_ref, o_ref):
    s = _silu(c_ref[...])
    o_ref[0] = jnp.dot(s, w_ref[0], preferred_element_type=F32,
                       precision=lax.Precision.HIGHEST) + b_ref[0]


def _modulation(cs, mod_w, mod_b):
    depth, d, n = mod_w.shape
    rows = cs.shape[0]
    tn = 1536
    return pl.pallas_call(
        _mod_kernel,
        grid=(depth, n // tn),
        in_specs=[pl.BlockSpec((rows, d), lambda i, j: (0, 0)),
                  pl.BlockSpec((1, d, tn), lambda i, j: (i, 0, j)),
                  pl.BlockSpec((1, 1, tn), lambda i, j: (i, 0, j))],
        out_specs=pl.BlockSpec((1, rows, tn), lambda i, j: (i, 0, j)),
        out_shape=jax.ShapeDtypeStruct((depth, rows, n), F32),
        compiler_params=_cparams(("arbitrary", "arbitrary")),
        name="modulation",
    )(cs, mod_w, mod_b.reshape(depth, 1, n))


def _tok_kernel(mode, readout, tT, nT, *refs):
    refs = list(refs)
    x_ref = refs.pop(0)
    if mode == "grid":
        xp_ref = refs.pop(0)
        xn_ref = refs.pop(0)
    (mod_ref, ng_ref, mu_ref, wr_ref, wk_ref, wv_ref, w1_ref, w2_ref, w0_ref,
     a1_ref, a2_ref, a0_ref, kkp_ref, ka_ref) = refs[:14]
    refs = refs[14:]
    if readout:
        g1_ref, g2_ref, rk_ref = refs[:3]
        refs = refs[3:]
    r_o, k_o, v_o, kk_o, ic0_o, ic1_o, lw0_o, lw1_o = refs[:8]
    refs = refs[8:]

    d = x_ref.shape[-1]
    sh = mod_ref[0, :, 0:d]
    sc = mod_ref[0, :, d:2 * d]
    g = ng_ref[...]

    def norm_mod(xx):
        return _rms(xx, g) * (1.0 + sc) + sh

    h = norm_mod(x_ref[0])
    if mode == "grid":
        i = pl.program_id(1)
        q = d // 4
        hp = jnp.where(i > 0, norm_mod(xp_ref[0]), 0.0)
        hn = jnp.where(i < nT - 1, norm_mod(xn_ref[0]), 0.0)
        col = _iota((tT, q), 0) % GRID_W
        left = jnp.where(col == 0, 0.0, pltpu.roll(h[:, 0:q], 1, 0))
        right = jnp.where(col == GRID_W - 1, 0.0, pltpu.roll(h[:, q:2 * q], tT - 1, 0))
        up = jnp.concatenate([hp[:, 2 * q:3 * q], h[:tT - GRID_W, 2 * q:3 * q]], axis=0)
        down = jnp.concatenate([h[GRID_W:, 3 * q:], hn[:, 3 * q:]], axis=0)
        hs = jnp.concatenate([left, right, up, down], axis=1)
    else:
        half = d // 2
        row = _iota((tT, half), 0)
        prev = jnp.where(row == 0, 0.0, pltpu.roll(h[:, :half], 1, 0))
        nxt = jnp.where(row == tT - 1, 0.0, pltpu.roll(h[:, half:], tT - 1, 0))
        hs = jnp.concatenate([prev, nxt], axis=1)

    hb = h.astype(BF16)
    dxb = (hs - h).astype(BF16)

    def mix(n):
        return hb + dxb * mu_ref[n:n + 1, :].astype(BF16)

    r = _dot(mix(0), wr_ref[...])
    k = _dot(mix(2), wk_ref[...])
    v = _dot(mix(3), wv_ref[...])
    w1o = jnp.tanh(_dot(mix(1), w1_ref[...])).astype(BF16)
    a1o = _dot(mix(4), a1_ref[...]).astype(BF16)
    ics = []
    for z, (lw_o, ic_o) in enumerate(((lw0_o, ic0_o), (lw1_o, ic1_o))):
        w_pre = _dot(w1o, w2_ref[z]) + w0_ref[z:z + 1, :]
        lw_o[0] = (-0.5 * DECAY_SCALE) * jnp.tanh(0.5 * w_pre) - 0.5 * DECAY_SCALE
        ic = _sigmoid(_dot(a1o, a2_ref[z]) + a0_ref[z:z + 1, :])
        ic_o[0] = ic.astype(BF16)
        ics.append(ic)
    kk = k * kkp_ref[...]
    kk = kk * lax.rsqrt(jnp.maximum(_head_sum(kk * kk), 1e-24))
    r_o[0] = r.astype(BF16)
    k_o[0] = k.astype(BF16)
    v_o[0] = v.astype(BF16)
    kk_o[0] = kk.astype(BF16)
    if readout:
        gate_o, bon_o = refs
        gate = _dot(_sigmoid(_dot(mix(5), g1_ref[...])).astype(BF16), g2_ref[...])
        gate_o[0] = gate.astype(BF16)
        k_avg = k * (1.0 + (0.5 * (ics[0] + ics[1]) - 1.0) * ka_ref[...])
        bon_o[0] = (_head_sum(r * k_avg * rk_ref[...]) * v).astype(BF16)


def _tokenwise(x, mod, mode, readout, p, tT):
    b, t, d = x.shape
    nT = t // tT
    row_spec = pl.BlockSpec((1, tT, d), lambda bi, i: (bi, i, 0))
    in_specs = [row_spec]
    args = [x]
    if mode == "grid":
        hb = tT // GRID_W
        in_specs += [
            pl.BlockSpec((1, GRID_W, d), lambda bi, i: (bi, jnp.maximum(i * hb - 1, 0), 0)),
            pl.BlockSpec((1, GRID_W, d), lambda bi, i: (bi, jnp.minimum((i + 1) * hb, t // GRID_W - 1), 0)),
        ]
        args += [x, x]
    if mod.shape[0] == 1:
        in_specs.append(pl.BlockSpec((1, 1, mod.shape[-1]), lambda bi, i: (0, 0, 0)))
    else:
        in_specs.append(pl.BlockSpec((1, 1, mod.shape[-1]), lambda bi, i: (bi, 0, 0)))
    args.append(mod)
    consts = [p["ng"], p["mu"], p["wr"], p["wk"], p["wv"], p["w1"], p["w2"], p["w0"],
              p["a1"], p["a2"], p["a0"], p["k_k"], p["k_a"]]
    if readout:
        consts += [p["g1"], p["g2"], p["r_k"]]
    for a in consts:
        in_specs.append(pl.BlockSpec(a.shape, lambda bi, i, _n=a.ndim: (0,) * _n))
        args.append(a)
    dtypes = [BF16] * 6 + [F32] * 2 + ([BF16] * 2 if readout else [])
    return pl.pallas_call(
        functools.partial(_tok_kernel, mode, readout, tT, nT),
        grid=(b, nT),
        in_specs=in_specs,
        out_specs=[row_spec] * len(dtypes),
        out_shape=[jax.ShapeDtypeStruct((b, t, d), dt) for dt in dtypes],
        compiler_params=_cparams(("arbitrary", "arbitrary")),
        name="rwkv_tokenwise_" + mode,
    )(*args)


def _scan_prep(r, kd, v, lw, kk, ic, reverse):
    c = r.shape[0]
    ti = _iota((c, c), 0)
    si = _iota((c, c), 1)
    tri = ((si >= ti) if reverse else (si <= ti)).astype(BF16)
    lw_hi, lw_lo = _split_bf16(lw)
    cum = _dot(tri, lw_hi) + _dot(tri, lw_lo)
    pc_log = cum[0:1] if reverse else cum[c - 1:c]
    p_inv = jnp.exp(-cum)
    p_rest = jnp.exp(pc_log - cum)
    b = kk * ic
    return {"rt": r * jnp.exp(cum), "at": -(kk * jnp.exp(cum - lw)),
            "bt": b * p_inv, "kt": kd * p_inv, "bp": b * p_rest, "kp": kd * p_rest,
            "v": v, "pc": jnp.exp(pc_log)}


def _scan_step(prep, states):
    c = prep[0]["v"].shape[0]
    n = 2 * c
    pp = len(states[0])
    chains = [(dr, p) for dr in range(2) for p in range(pp)]
    first = _iota((c, LANES), 1) < HEAD
    row = _iota((n, n), 0)
    col = _iota((n, n), 1)
    strict = (col < row, col > row)
    same = (row // SUB) == (col // SUB)

    def pair(name, dr, p):
        return prep[dr][name][:, p * LANES:(p + 1) * LANES]

    def stack(x):
        return jnp.concatenate([jnp.where(first, x, 0.0), jnp.where(first, 0.0, x)], axis=0)

    def ms(name, dr, p):
        return stack(pair(name, dr, p))

    def each(fn, *lists):
        return [fn(*xs) for xs in zip(*lists)]

    lhs = [jnp.concatenate([pair("at", dr, p), pair("rt", dr, p)], axis=0).astype(BF16)
           for dr, p in chains]
    rhs = [jnp.concatenate([ms("bt", dr, p), ms("kt", dr, p)], axis=0).astype(BF16) for dr, p in chains]
    at_ms = [ms("at", dr, p).astype(BF16) for dr, p in chains]
    v_ms = [ms("v", dr, p).astype(BF16) for dr, p in chains]
    gmat = each(_dot_nt, lhs, rhs)
    own = (row // c) == (col // c)
    tri = [own & m for m in strict]
    tri_d = [m & same for m in tri]
    tri_o = [m & jnp.logical_not(same) for m in tri]

    def twice(x):
        return jnp.concatenate([x, x], axis=0)

    nd = [jnp.where(tri_d[dr], twice(g[:c, :n]), 0.0) for (dr, _), g in zip(chains, gmat)]
    no = [jnp.where(tri_o[dr], twice(g[:c, :n]), 0.0) for (dr, _), g in zip(chains, gmat)]
    a_ak = [jnp.where(tri[dr], twice(g[:c, n:]), 0.0).astype(BF16) for (dr, _), g in zip(chains, gmat)]
    t_pl = _iota((c, 2 * n), 0)
    s_pl = _iota((c, 2 * n), 1) % c
    incl_pl = (s_pl <= t_pl, s_pl >= t_pl)
    a_r = [jnp.where(incl_pl[dr], g[c:], 0.0).astype(BF16)
           for (dr, _), g in zip(chains, gmat)]

    pw = nd
    tp = nd
    span = 2
    while span < SUB:
        pw = each(_bdot, pw, pw)
        tp = each(lambda t, q: t + q + _bdot(t, q), tp, pw)
        span *= 2
    m = each(lambda t, o: o + _bdot(t, o), tp, no)
    qp = m
    pw = m
    span = 2
    while span < c // SUB:
        pw = each(_bdot, pw, pw)
        qp = each(lambda t, q: t + q + _bdot(t, q), qp, pw)
        span *= 2
    tp = each(lambda t, q: t + q + _bdot(q, t), tp, qp)

    bk_t = [jnp.concatenate([ms("bp", dr, p), ms("kp", dr, p)], axis=0).T.astype(BF16)
            for dr, p in chains]
    diag = row == col
    pc_col = [jnp.sum(jnp.where(diag, prep[dr]["pc"][:, p * LANES:(p + 1) * LANES], 0.0),
                      axis=1, keepdims=True) for dr, p in chains]

    s_t = [states[dr][p].astype(BF16) for dr, p in chains]
    rhs_z = each(lambda l, a, st, v: _dot(jnp.concatenate([l, a], axis=1),
                                          jnp.concatenate([st, v], axis=0)), at_ms, a_ak, s_t, v_ms)
    z = each(lambda t, x: x + _bdot(t, x), tp, rhs_z)
    zv = each(lambda zz, v: jnp.concatenate([zz.astype(BF16), v], axis=0), z, v_ms)
    y_pl = each(lambda l, st, ar, w: _dot(l[c:], st) + _dot(ar, w), lhs, s_t, a_r, zv)
    y = [jnp.concatenate([yy for (d2, _), yy in zip(chains, y_pl) if d2 == dr], axis=1)
         for dr in range(2)]
    new_states = [[None] * pp for _ in range(2)]
    for (dr, p), bt_, w, pc in zip(chains, bk_t, zv, pc_col):
        new_states[dr][p] = states[dr][p] * pc + _dot(bt_, w)
    return y, new_states


def _scan_kernel(pp, nc, has_s0, want_y, want_state, *refs):
    refs = list(refs)
    ka_ref = refs[12]
    dirs = (refs[:6], refs[6:12])
    refs = refs[13:]
    if has_s0:
        s0_ref = refs.pop(0)
    if want_y:
        y_refs = (refs.pop(0), refs.pop(0))
    if want_state:
        st_ref = refs.pop(0)
    s_scr = refs.pop(0)
    c = pl.program_id(2)

    @pl.when(c == 0)
    def _():
        if has_s0:
            s_scr[...] = s0_ref[0]
        else:
            s_scr[...] = jnp.zeros_like(s_scr)

    per_step = dirs[0][0].shape[1] // CHUNK
    states = [[s_scr[dr, p] for p in range(pp)] for dr in range(2)]
    for j in range(per_step):
        prep = []
        rows = []
        for dr, (r_ref, k_ref, v_ref, kk_ref, ic_ref, lw_ref) in enumerate(dirs):
            q = j if dr == 0 else per_step - 1 - j
            sl = slice(q * CHUNK, (q + 1) * CHUNK)
            rows.append(sl)
            ic = ic_ref[0, sl, :].astype(F32)
            kd = k_ref[0, sl, :].astype(F32) * (1.0 + (ic - 1.0) * ka_ref[...])
            prep.append(_scan_prep(r_ref[0, sl, :].astype(F32), kd, v_ref[0, sl, :].astype(F32),
                                   lw_ref[0, sl, :], kk_ref[0, sl, :].astype(F32), ic,
                                   reverse=(dr == 1)))
        y, states = _scan_step(prep, states)
        if want_y:
            for dr in range(2):
                y_refs[dr][0, rows[dr], :] = y[dr].astype(BF16)
    for dr in range(2):
        for p in range(pp):
            s_scr[dr, p] = states[dr][p]

    if want_state:
        @pl.when(c == nc - 1)
        def _():
            st_ref[0] = s_scr[...]


def _scan(tok, ka, s0, want_y, want_state):
    r, k, v, kk, ic0, ic1, lw0, lw1 = tok
    b, t, d = r.shape
    rows = min(CHUNK * SCAN_CHUNKS, t)
    nc = t // rows
    npair = d // LANES
    pp = min(SCAN_PAIRS, npair)
    w = LANES * pp
    f_spec = pl.BlockSpec((1, rows, w), lambda bi, hi, ci: (bi, ci, hi))
    b_spec = pl.BlockSpec((1, rows, w), lambda bi, hi, ci: (bi, nc - 1 - ci, hi))
    st_spec = pl.BlockSpec((1, 2, pp, LANES, LANES), lambda bi, hi, ci: (bi, 0, hi, 0, 0))
    in_specs = [f_spec] * 6 + [b_spec] * 6 + [pl.BlockSpec((1, w), lambda bi, hi, ci: (0, hi))]
    args = [r, k, v, kk, ic0, lw0, r, k, v, kk, ic1, lw1, ka]
    if s0 is not None:
        in_specs.append(st_spec)
        args.append(s0)
    out_specs, out_shape = [], []
    if want_y:
        out_specs += [f_spec, b_spec]
        out_shape += [jax.ShapeDtypeStruct((b, t, d), BF16)] * 2
    if want_state:
        out_specs.append(st_spec)
        out_shape.append(jax.ShapeDtypeStruct((b, 2, npair, LANES, LANES), F32))
    return pl.pallas_call(
        functools.partial(_scan_kernel, pp, nc, s0 is not None, want_y, want_state),
        grid=(b, npair // pp, nc),
        in_specs=in_specs,
        out_specs=out_specs,
        out_shape=out_shape,
        scratch_shapes=[pltpu.VMEM((2, pp, LANES, LANES), F32)],
        compiler_params=_cparams(("arbitrary", "arbitrary", "arbitrary")),
        name="rwkv_scan_ctx" if s0 is None else "rwkv_scan_latent",
    )(*args)


def _readout_kernel(yf_ref, yb_ref, bon_ref, gate_ref, x_ref, mod_ref, lnw_ref, lnb_ref,
                    wo_ref, g1_ref, g2_ref, x_o, h_o):
    d = x_ref.shape[-1]
    y = yf_ref[0].astype(F32) + yb_ref[0].astype(F32)
    mean = _head_sum(y) * (1.0 / HEAD)
    yc = y - mean
    var = _head_sum(yc * yc) * (1.0 / HEAD)
    o = yc * lax.rsqrt(var + GN_EPS) * lnw_ref[...] + lnb_ref[...] + bon_ref[0]
    att = _dot((o * gate_ref[0]).astype(BF16), wo_ref[...])
    gt_a = mod_ref[0, :, 2 * d:3 * d]
    sh_f = mod_ref[0, :, 3 * d:4 * d]
    sc_f = mod_ref[0, :, 4 * d:5 * d]
    x1 = x_ref[0] + gt_a * _rms(att, g1_ref[...])
    x_o[0] = x1
    h_o[0] = (_rms(x1, g2_ref[...]) * (1.0 + sc_f) + sh_f).astype(BF16)


def _readout(yf, yb, bon, gate, x, mod, lnw, lnb, wo, g1, g2, tT):
    b, t, d = x.shape
    row_spec = pl.BlockSpec((1, tT, d), lambda bi, i: (bi, i, 0))
    vec = pl.BlockSpec((1, d), lambda bi, i: (0, 0))
    return pl.pallas_call(
        _readout_kernel,
        grid=(b, t // tT),
        in_specs=[row_spec] * 5 + [pl.BlockSpec((1, 1, 6 * d), lambda bi, i: (bi, 0, 0)),
                                   vec, vec, pl.BlockSpec((d, d), lambda bi, i: (0, 0)), vec, vec],
        out_specs=[row_spec, row_spec],
        out_shape=[jax.ShapeDtypeStruct((b, t, d), F32), jax.ShapeDtypeStruct((b, t, d), BF16)],
        compiler_params=_cparams(("arbitrary", "arbitrary")),
        name="rwkv_readout",
    )(yf, yb, bon, gate, x, mod, lnw, lnb, wo, g1, g2)


def _ffn_kernel(nf, h_ref, wg_ref, wu_ref, wd_ref, x_ref, mod_ref, g3_ref, gn_ref, mod2_ref,
                x_o, h_o, acc):
    f = pl.program_id(1)
    d = x_ref.shape[-1]

    @pl.when(f == 0)
    def _():
        acc[...] = jnp.zeros_like(acc)

    h = h_ref[...]
    act = _silu(_dot(h, wg_ref[...])) * _dot(h, wu_ref[...])
    acc[...] += _dot(act.astype(BF16), wd_ref[...])

    @pl.when(f == nf - 1)
    def _():
        gt_f = mod_ref[0, :, 5 * d:6 * d]
        x2 = x_ref[...] + gt_f * _rms(acc[...], g3_ref[...])
        x_o[...] = x2
        sh = mod2_ref[0, :, 0:d]
        sc = mod2_ref[0, :, d:2 * d]
        h_o[...] = (_rms(x2, gn_ref[...]) * (1.0 + sc) + sh).astype(BF16)


def _ffn(h, w_gu, w_down, x, mod, g3, gn, mod2, rows_per_batch, tm, tf):
    n, d = x.shape
    ff = w_down.shape[0]
    nf = ff // tf
    per = rows_per_batch // tm
    row = pl.BlockSpec((tm, d), lambda i, f: (i, 0))
    vec = pl.BlockSpec((1, d), lambda i, f: (0, 0))
    modspec = pl.BlockSpec((1, 1, 6 * d), lambda i, f: (i // per, 0, 0))
    return pl.pallas_call(
        functools.partial(_ffn_kernel, nf),
        grid=(n // tm, nf),
        in_specs=[row,
                  pl.BlockSpec((d, tf), lambda i, f: (0, f)),
                  pl.BlockSpec((d, tf), lambda i, f: (0, nf + f)),
                  pl.BlockSpec((tf, d), lambda i, f: (f, 0)),
                  row, modspec, vec, vec, modspec],
        out_specs=[row, row],
        out_shape=[jax.ShapeDtypeStruct((n, d), F32), jax.ShapeDtypeStruct((n, d), BF16)],
        scratch_shapes=[pltpu.VMEM((tm, d), F32)],
        compiler_params=_cparams(("arbitrary", "arbitrary")),
        name="dense_swiglu",
    )(h, w_gu, w_gu, w_down, x, mod, g3, gn, mod2)


HALO = 16


def _conv_kernel(tT, nT, n_experts, h_ref, hp_ref, hn_ref, x_ref, mod_ref, win_ref, cw_ref, wo_ref,
                 g1_ref, g2_ref, wr_ref, x_o, h_o, te_o, gt_o):
    i = pl.program_id(1)
    d = x_ref.shape[-1]
    n = tT + 2 * HALO
    h_ext = jnp.concatenate([hp_ref[0], h_ref[0], hn_ref[0]], axis=0)
    proj = _dot(h_ext, win_ref[...])
    z = proj[:, d:2 * d] * proj[:, 2 * d:]
    row = _iota((n, d), 0)
    dead = ((row < HALO) & (i == 0)) | ((row >= HALO + tT) & (i == nT - 1))
    z = jnp.where(dead, 0.0, z)
    conv = (pltpu.roll(z, 1, 0) * cw_ref[0:1, :] + z * cw_ref[1:2, :]
            + pltpu.roll(z, n - 1, 0) * cw_ref[2:3, :])
    gated = (proj[HALO:HALO + tT, 0:d] * conv[HALO:HALO + tT]).astype(BF16)
    y = _dot(gated, wo_ref[...])
    gt_a = mod_ref[0, :, 2 * d:3 * d]
    sh_f = mod_ref[0, :, 3 * d:4 * d]
    sc_f = mod_ref[0, :, 4 * d:5 * d]
    x3 = x_ref[0] + gt_a * _rms(y, g1_ref[...])
    x_o[0] = x3
    h4 = _rms(x3, g2_ref[...]) * (1.0 + sc_f) + sh_f
    s_rows = d // LANES
    for c in range(s_rows):
        h_o[0, pl.ds(c, tT, stride=s_rows), :] = h4[:, c * LANES:(c + 1) * LANES]
    h_hi, h_lo = _split_bf16(h4)
    lg = _dot(h_hi, wr_ref[0]) + _dot(h_lo, wr_ref[0]) + _dot(h_hi, wr_ref[1])
    lg_t = lg.T
    lg8 = lg_t[:n_experts]
    eidx = _iota(lg8.shape, 0)
    m1 = jnp.max(lg8, axis=0, keepdims=True)
    i1 = jnp.min(jnp.where(lg8 == m1, eidx, n_experts), axis=0, keepdims=True)
    rest = jnp.where(eidx == i1, -jnp.inf, lg8)
    m2 = jnp.max(rest, axis=0, keepdims=True)
    i2 = jnp.min(jnp.where(rest == m2, eidx, n_experts), axis=0, keepdims=True)
    tw = jnp.exp(m2 - m1)
    te_o[0] = jnp.concatenate([i1, i2], axis=0)
    gates = jnp.concatenate([1.0 / (1.0 + tw), tw / (1.0 + tw),
                             jnp.zeros((LANES - TOP_K, tT), F32)], axis=0)
    gt_o[0] = gates.T


def _conv_layer(h, x, mod, w_in, conv_w, w_out, g1, g2, w_router, n_experts, tT):
    b, t, d = x.shape
    nT = t // tT
    hb = tT // HALO
    row_spec = pl.BlockSpec((1, tT, d), lambda bi, i: (bi, i, 0))
    vec = pl.BlockSpec((1, d), lambda bi, i: (0, 0))
    full = lambda a: pl.BlockSpec(a.shape, lambda bi, i, _n=a.ndim: (0,) * _n)
    return pl.pallas_call(
        functools.partial(_conv_kernel, tT, nT, n_experts),
        grid=(b, nT),
        in_specs=[row_spec,
                  pl.BlockSpec((1, HALO, d), lambda bi, i: (bi, jnp.maximum(i * hb - 1, 0), 0)),
                  pl.BlockSpec((1, HALO, d), lambda bi, i: (bi, jnp.minimum((i + 1) * hb, t // HALO - 1), 0)),
                  row_spec, pl.BlockSpec((1, 1, 6 * d), lambda bi, i: (bi, 0, 0)),
                  full(w_in), full(conv_w), full(w_out), vec, vec, full(w_router)],
        out_specs=[row_spec, pl.BlockSpec((1, tT * (d // LANES), LANES), lambda bi, i: (bi, i, 0)),
                   pl.BlockSpec((1, TOP_K, tT), lambda bi, i: (bi, 0, i)),
                   pl.BlockSpec((1, tT, LANES), lambda bi, i: (bi, i, 0))],
        out_shape=[jax.ShapeDtypeStruct((b, t, d), F32),
                   jax.ShapeDtypeStruct((b, t * (d // LANES), LANES), F32),
                   jax.ShapeDtypeStruct((b, TOP_K, t), jnp.int32),
                   jax.ShapeDtypeStruct((b, t, LANES), F32)],
        compiler_params=_cparams(("arbitrary", "arbitrary")),
        name="short_conv",
    )(h, h, h, x, mod, w_in, conv_w, w_out, g1, g2, w_router)


GATHER_ROWS = 512


def _gather_kernel(tg, nsteps, live_ref, tok_cur, tok_nxt, h_hbm, o_ref, buf, sem):
    j = pl.program_id(0)
    slot = j % 2
    s_rows = buf.shape[1] // tg

    def issue(tok_ref, s):
        for r in range(tg):
            src = pl.multiple_of(tok_ref[0, 0, r] * s_rows, s_rows)
            pltpu.make_async_copy(h_hbm.at[pl.ds(src, s_rows)],
                                  buf.at[s, pl.ds(r * s_rows, s_rows)], sem.at[s]).start(priority=r % 2)

    @pl.when((j == 0) & (live_ref[0] > 0))
    def _():
        issue(tok_cur, 0)

    @pl.when(j + 1 < nsteps)
    def _():
        @pl.when(live_ref[j + 1] > 0)
        def _():
            issue(tok_nxt, 1 - slot)

    @pl.when(live_ref[j] > 0)
    def _():
        pltpu.make_async_copy(h_hbm.at[pl.ds(0, tg * s_rows)], buf.at[slot], sem.at[slot]).wait()
        for c in range(s_rows):
            o_ref[:, c * LANES:(c + 1) * LANES] = buf[slot, pl.ds(c, tg, stride=s_rows), :].astype(BF16)

    @pl.when(live_ref[j] == 0)
    def _():
        o_ref[...] = jnp.zeros(o_ref.shape, BF16)


def _gather_rows(h, d, slot_tok, live, tg):
    s_rows = d // LANES
    nsteps = slot_tok.shape[0] // tg
    tok3 = slot_tok.reshape(nsteps, 1, tg)
    grid_spec = pltpu.PrefetchScalarGridSpec(
        num_scalar_prefetch=1,
        grid=(nsteps,),
        in_specs=[pl.BlockSpec((1, 1, tg), lambda j, lv: (j, 0, 0), memory_space=pltpu.SMEM),
                  pl.BlockSpec((1, 1, tg), lambda j, lv: (jnp.minimum(j + 1, nsteps - 1), 0, 0),
                               memory_space=pltpu.SMEM),
                  pl.BlockSpec(memory_space=pl.ANY)],
        out_specs=pl.BlockSpec((tg, d), lambda j, lv: (j, 0)),
        scratch_shapes=[pltpu.VMEM((2, tg * s_rows, LANES), F32), pltpu.SemaphoreType.DMA((2,))],
    )
    return pl.pallas_call(
        functools.partial(_gather_kernel, tg, nsteps),
        grid_spec=grid_spec,
        out_shape=jax.ShapeDtypeStruct((nsteps * tg, d), BF16),
        compiler_params=_cparams(("arbitrary",)),
        name="moe_dispatch",
    )(live, tok3, tok3, h)


MOE_SUB = 512


def _moe_kernel(nf, tm, e_ref, nv_ref, x_ref, wg_ref, wu_ref, wd_ref, y_o):
    i = pl.program_id(0)
    f = pl.program_id(1)
    nv = nv_ref[i]

    nsub = tm // MOE_SUB

    @pl.when(f == 0)
    def _():
        y_o[...] = jnp.zeros(y_o.shape, F32)

    def weights():
        return wg_ref[0].astype(BF16), wu_ref[0].astype(BF16), wd_ref[0].astype(BF16)

    def sub_block(s, wg, wu, wd):
        rows = slice(s * MOE_SUB, (s + 1) * MOE_SUB)
        h = x_ref[rows, :]
        act = _silu(_dot(h, wg)) * _dot(h, wu)
        y_o[rows, :] += _dot(act.astype(BF16), wd)

    for live in range(1, nsub + 1):
        @pl.when((nv > (live - 1) * MOE_SUB) & (nv <= live * MOE_SUB))
        def _():
            w = weights()
            for s in range(live):
                sub_block(s, *w)


def _moe_experts(xs, blk_e, blk_nv, w_gu, w_down, tm, tf):
    n, d = xs.shape
    ff = w_down.shape[1]
    nf = ff // tf
    nblk = n // tm

    def f_eff(i, f, nv_ref):
        return jnp.where(nv_ref[i] > 0, f, nf - 1)

    grid_spec = pltpu.PrefetchScalarGridSpec(
        num_scalar_prefetch=2,
        grid=(nblk, nf),
        in_specs=[pl.BlockSpec((tm, d), lambda i, f, e, nv: (i, 0)),
                  pl.BlockSpec((1, d, tf), lambda i, f, e, nv: (e[i], 0, f_eff(i, f, nv))),
                  pl.BlockSpec((1, d, tf), lambda i, f, e, nv: (e[i], 0, nf + f_eff(i, f, nv))),
                  pl.BlockSpec((1, tf, d), lambda i, f, e, nv: (e[i], f_eff(i, f, nv), 0))],
        out_specs=pl.BlockSpec((tm, d), lambda i, f, e, nv: (i, 0)),
    )
    return pl.pallas_call(
        functools.partial(_moe_kernel, nf, tm),
        grid_spec=grid_spec,
        out_shape=jax.ShapeDtypeStruct((n, d), F32),
        compiler_params=_cparams(("arbitrary", "arbitrary")),
        name="moe_experts",
    )(blk_e, blk_nv, xs, w_gu, w_gu, w_down)


def _combine_kernel(tT, nsteps, pos_cur, pos_nxt, ys_hbm, gt_ref, x_ref, mod_ref, g_ref, o_ref,
                    buf, sem):
    j = pl.program_id(0)
    slot = j % 2
    d = x_ref.shape[-1]

    def issue(pos_ref, s):
        for r in range(tT):
            for k in range(TOP_K):
                pltpu.make_async_copy(ys_hbm.at[pl.ds(pos_ref[0, k, r], 1)],
                                      buf.at[s, k, pl.ds(r, 1)], sem.at[s]).start(priority=k)

    @pl.when(j == 0)
    def _():
        issue(pos_cur, 0)

    @pl.when(j + 1 < nsteps)
    def _():
        issue(pos_nxt, 1 - slot)

    for k in range(TOP_K):
        pltpu.make_async_copy(ys_hbm.at[pl.ds(0, tT)], buf.at[slot, k], sem.at[slot]).wait()
    gates = gt_ref[...]
    y = buf[slot, 0] * gates[:, 0:1] + buf[slot, 1] * gates[:, 1:2]
    gt_f = mod_ref[0, :, 5 * d:6 * d]
    o_ref[...] = x_ref[...] + gt_f * _rms(y, g_ref[...])


def _combine(ys, pos, gates, x, mod, g, rows_per_batch, tT):
    n, d = x.shape
    nsteps = n // tT
    per = rows_per_batch // tT
    pos3 = pos.reshape(TOP_K, nsteps, tT).transpose(1, 0, 2)
    row_spec = pl.BlockSpec((tT, d), lambda j: (j, 0))
    smem = functools.partial(pl.BlockSpec, (1, TOP_K, tT), memory_space=pltpu.SMEM)
    return pl.pallas_call(
        functools.partial(_combine_kernel, tT, nsteps),
        grid=(nsteps,),
        in_specs=[smem(lambda j: (j, 0, 0)),
                  smem(lambda j: (jnp.minimum(j + 1, nsteps - 1), 0, 0)),
                  pl.BlockSpec(memory_space=pl.ANY),
                  pl.BlockSpec((tT, LANES), lambda j: (j, 0)),
                  row_spec, pl.BlockSpec((1, 1, 6 * d), lambda j: (j // per, 0, 0)),
                  pl.BlockSpec((1, d), lambda j: (0, 0))],
        out_specs=row_spec,
        out_shape=jax.ShapeDtypeStruct((n, d), F32),
        scratch_shapes=[pltpu.VMEM((2, TOP_K, tT, d), F32), pltpu.SemaphoreType.DMA((2,))],
        compiler_params=_cparams(("arbitrary",)),
        name="moe_combine",
    )(pos3, pos3, ys, gates, x, mod, g)


def _route(top_e, n_experts, tm):
    n = top_e.shape[1]
    nk = n * TOP_K
    i1, i2 = top_e[0], top_e[1]
    idx = jnp.arange(n_experts, dtype=jnp.int32)[:, None]
    oh = [(idx == i[None]).astype(jnp.int32) for i in (i1, i2)]
    cs = [jnp.cumsum(o, axis=1) for o in oh]
    first = cs[0][:, -1]
    counts = first + cs[1][:, -1]
    nblk_e = (counts + tm - 1) // tm
    start = jnp.cumsum(counts) - counts
    bend = jnp.cumsum(nblk_e)
    bstart = bend - nblk_e
    base = (bstart * tm)[:, None]
    pos = jnp.stack([jnp.sum(oh[0] * (base + cs[0] - oh[0]), axis=0),
                     jnp.sum(oh[1] * (base + first[:, None] + cs[1] - oh[1]), axis=0)])
    flat_e = jnp.concatenate([i1, i2])
    nblk = -(-nk // tm) + n_experts
    blk = jnp.arange(nblk, dtype=jnp.int32)
    blk_e = jnp.minimum(jnp.searchsorted(bend, blk, side="right"), n_experts - 1).astype(jnp.int32)
    blk_nv = jnp.clip(counts[blk_e] - (blk - bstart[blk_e]) * tm, 0, tm)
    blk_nv = jnp.where(blk < bend[-1], blk_nv, 0).astype(jnp.int32)
    order = jnp.argsort(flat_e, stable=True).astype(jnp.int32)
    order = jnp.concatenate([order, jnp.zeros((tm,), jnp.int32)])
    src0 = jnp.where(blk_nv > 0, start[blk_e] + (blk - bstart[blk_e]) * tm, 0)
    slot_pair = jnp.concatenate([lax.dynamic_slice_in_dim(order, src0[k], tm) for k in range(nblk)])
    slot_tok = jnp.where(slot_pair >= n, slot_pair - n, slot_pair)
    return slot_tok, pos, blk_e, blk_nv


def kernel(x, c, ctx, c_ctx, mod_w, mod_b, norm_g, rwkv_mu, rwkv_w_rkv, rwkv_w0, rwkv_w1, rwkv_w2,
           rwkv_a0, rwkv_a1, rwkv_a2, rwkv_g1, rwkv_g2, rwkv_k_k, rwkv_k_a, rwkv_r_k, rwkv_ln_w,
           rwkv_ln_b, rwkv_w_out, conv_w_in, conv_w, conv_w_out, ffn_w_gu, ffn_w_down,
           moe_router, moe_w_gu, moe_w_down):
    b, t, d = x.shape
    n_experts = moe_router.shape[-1]
    rows = 16
    cs = jnp.zeros((rows, d), F32).at[:b].set(c).at[b].set(c_ctx)
    mods = _modulation(cs, mod_w, mod_b)
    mod0 = mods[0, :b].reshape(b, 1, 6 * d)
    mod0c = mods[0, b].reshape(1, 1, 6 * d)
    mod1 = mods[1, :b].reshape(b, 1, 6 * d)

    def pad_lora(w):
        zr = jnp.zeros_like(w[0])
        return jnp.stack([jnp.concatenate([w[0], zr], 0), jnp.concatenate([zr, w[1]], 0)]).astype(BF16)

    lg = rwkv_g1.shape[-1]
    lgp = -(-lg // LANES) * LANES
    p = {
        "ng": norm_g[0, 0].reshape(1, d),
        "mu": rwkv_mu[0],
        "wr": rwkv_w_rkv[0, 0].astype(BF16), "wk": rwkv_w_rkv[0, 1].astype(BF16),
        "wv": rwkv_w_rkv[0, 2].astype(BF16),
        "w1": jnp.concatenate([rwkv_w1[0, 0], rwkv_w1[0, 1]], axis=1).astype(BF16),
        "w2": pad_lora(rwkv_w2[0]), "w0": rwkv_w0[0],
        "a1": jnp.concatenate([rwkv_a1[0, 0], rwkv_a1[0, 1]], axis=1).astype(BF16),
        "a2": pad_lora(rwkv_a2[0]), "a0": rwkv_a0[0],
        "k_k": rwkv_k_k[0].reshape(1, d), "k_a": rwkv_k_a[0].reshape(1, d),
        "g1": jnp.pad(rwkv_g1[0], ((0, 0), (0, lgp - lg))).astype(BF16),
        "g2": jnp.pad(rwkv_g2[0], ((0, lgp - lg), (0, 0))).astype(BF16),
        "r_k": rwkv_r_k[0].reshape(1, d),
    }

    tok_c = _tokenwise(ctx, mod0c, "seq", False, p, ctx.shape[1])
    s_ctx = _scan(tok_c, p["k_a"], None, want_y=False, want_state=True)[0]
    tt = min(ROW_TILE, t)
    tm = min(FFN_ROWS, t)
    tf = min(FFN_COLS, ffn_w_down.shape[1])
    tok_l = _tokenwise(x, mod0, "grid", True, p, min(2 * ROW_TILE, t))
    yf, yb = _scan(tok_l[:8], p["k_a"], s_ctx, want_y=True, want_state=False)
    x1, h2 = _readout(yf, yb, tok_l[9], tok_l[8], x, mod0,
                      rwkv_ln_w[0].reshape(1, d), rwkv_ln_b[0].reshape(1, d),
                      rwkv_w_out[0].astype(BF16), norm_g[0, 1].reshape(1, d),
                      norm_g[0, 2].reshape(1, d), min(4 * ROW_TILE, t))

    x2, h3 = _ffn(h2.reshape(b * t, d), ffn_w_gu[0].astype(BF16), ffn_w_down[0].astype(BF16),
                  x1.reshape(b * t, d), mod0, norm_g[0, 3].reshape(1, d),
                  norm_g[1, 0].reshape(1, d), mod1, t, tm, tf)

    w_router = jnp.stack(_split_bf16(jnp.pad(moe_router[0], ((0, 0), (0, LANES - n_experts)))))
    x3, h4, top_e, gates = _conv_layer(h3.reshape(b, t, d), x2.reshape(b, t, d), mod1,
                                       conv_w_in[0].astype(BF16), conv_w[0], conv_w_out[0].astype(BF16),
                                       norm_g[1, 1].reshape(1, d), norm_g[1, 2].reshape(1, d),
                                       w_router, n_experts, tt)

    tme = min(MOE_ROWS, b * t * TOP_K)
    slot_tok, pos, blk_e, blk_nv = _route(top_e.transpose(1, 0, 2).reshape(TOP_K, b * t),
                                          n_experts, tme)
    tg = min(GATHER_ROWS, tme)
    live = (blk_nv[:, None] > jnp.arange(0, tme, tg, dtype=jnp.int32)[None, :]).astype(jnp.int32)
    xs = _gather_rows(h4.reshape(-1, LANES), d, slot_tok, live.reshape(-1), tg)
    ys = _moe_experts(xs, blk_e, blk_nv, moe_w_gu[0], moe_w_down[0], tme,
                      min(MOE_COLS, moe_w_down.shape[2]))
    out = _combine(ys, pos, gates.reshape(b * t, LANES), x3.reshape(b * t, d), mod1,
                   norm_g[1, 3].reshape(1, d), t, min(2 * ROW_TILE, t))
    return out.reshape(b, t, d)
```
